```python
import math
import jax, jax.numpy as jnp
from jax import lax
import numpy as np

D_MODEL = 1024
BATCH = 8
SEQ = 2048
DEPTH = 2
DEC_BATCH = 128
DEC_SEQ = 4
PAST_LEN = 16384
PAGE_SIZE = 128

BR = 512
N_BRANCH = 4
ML_HEADS = 4
ML_DH = BR // ML_HEADS
GD_HEADS = 4
GD_DH = BR // GD_HEADS
CONV_W = 4
S5_GROUP = 16
S5_G = BR // S5_GROUP
S5_P = 64
LRU_BLOCKS = 8
LRU_BD = BR // LRU_BLOCKS
LRU_C = 8.0
PLE_DIM = 256
CHUNK = 64
EPS = 1e-6
IN_SIZES = (BR, BR, BR, ML_HEADS, ML_HEADS, BR, BR,
            3 * BR, GD_HEADS, GD_HEADS, BR,
            BR, BR,
            BR, BR,
            N_BRANCH * D_MODEL)
D_IN = sum(IN_SIZES)

kernel_name = 'hybrid_mlstm_gdn_s5_rglru_decode_step'

F32 = jnp.float32


def _split(h, sizes):
    idx = np.cumsum(np.array(sizes))[:-1].tolist()
    return jnp.split(h, idx, axis=-1)


def _rms(x, g):
    xf = x.astype(F32)
    return xf * lax.rsqrt(jnp.mean(xf * xf, -1, keepdims=True) + EPS) * g


def _head_rms(x, g):
    B, T, H, d = x.shape
    y = x * lax.rsqrt(jnp.mean(x * x, -1, keepdims=True) + EPS)
    return y.reshape(B, T, H * d) * g


def _l2n(x):
    return x * lax.rsqrt(jnp.sum(x * x, -1, keepdims=True) + EPS)


def _chunk_len(T):
    return math.gcd(T, CHUNK)


def _to_chunks(a, L):
    B, T = a.shape[:2]
    a = a.reshape((B, T // L, L) + a.shape[2:])
    return jnp.swapaxes(jnp.moveaxis(a, 1, 0), 2, 3)


def _from_chunks(a):
    a = jnp.moveaxis(jnp.swapaxes(a, 2, 3), 0, 1)
    return a.reshape((a.shape[0], -1) + a.shape[3:])


def _lin_combine(e1, e2):
    a1, b1 = e1
    a2, b2 = e2
    return a1 * a2, a2 * b1 + b2


def _cplx_combine(e1, e2):
    a1r, a1i, b1r, b1i = e1
    a2r, a2i, b2r, b2i = e2
    return (a2r * a1r - a2i * a1i, a2r * a1i + a2i * a1r,
            a2r * b1r - a2i * b1i + b2r, a2r * b1i + a2i * b1r + b2i)


def _causal_conv(x, buf, w, b=None):
    T = x.shape[1]
    xp = jnp.concatenate([buf.astype(F32), x], axis=1)
    y = sum(xp[:, j:j + T] * w[j] for j in range(CONV_W))
    if b is not None:
        y = y + b
    return y, xp[:, -(CONV_W - 1):]


def _mlstm(q, k, v, ig, logf, c0, n0, m0):
    B, T, H, d = q.shape
    L = _chunk_len(T)
    q = q * d ** -0.5
    causal = jnp.tril(jnp.ones((L, L), bool))

    def step(carry, inp):
        C, n, m = carry
        qc, kc, vc, ic, lfc = inp
        b = jnp.cumsum(lfc, -1)
        dmat = jnp.where(causal, b[..., :, None] - b[..., None, :] + ic[..., None, :], -jnp.inf)
        m_inter = b + m[..., None]
        m_t = jnp.maximum(m_inter, jnp.max(dmat, -1))
        s = jnp.einsum('bhtd,bhsd->bhts', qc, kc) * jnp.exp(dmat - m_t[..., None])
        inter = jnp.exp(m_inter - m_t)
        num = jnp.einsum('bhts,bhsd->bhtd', s, vc) + inter[..., None] * jnp.einsum('bhtk,bhkv->bhtv', qc, C)
        den = jnp.sum(s, -1) + inter * jnp.einsum('bhtk,bhk->bht', qc, n)
        h = num / jnp.maximum(jnp.abs(den), jnp.exp(-m_t))[..., None]
        bL = b[..., -1]
        g_s = bL[..., None] - b + ic
        m_new = jnp.maximum(bL + m, jnp.max(g_s, -1))
        ws = jnp.exp(g_s - m_new[..., None])
        sc = jnp.exp(bL + m - m_new)
        C = sc[..., None, None] * C + jnp.einsum('bhs,bhsk,bhsv->bhkv', ws, kc, vc)
        n = sc[..., None] * n + jnp.einsum('bhs,bhsk->bhk', ws, kc)
        return (C, n, m_new), h

    xs = (_to_chunks(q, L), _to_chunks(k, L), _to_chunks(v, L), _to_chunks(ig, L), _to_chunks(logf, L))
    (C, n, m), h = lax.scan(step, (c0.astype(F32), n0.astype(F32), m0.astype(F32)), xs)
    return _from_chunks(h), C, n, m


def _gated_delta(q, k, v, g, beta, s0):
    B, T, H, d = q.shape
    L = _chunk_len(T)
    q = _l2n(q) * d ** -0.5
    k = _l2n(k)
    causal = jnp.tril(jnp.ones((L, L), bool))
    strict = jnp.tril(jnp.ones((L, L), F32), -1)
    eye = jnp.eye(L, dtype=F32)

    def step(S, inp):
        qc, kc, vc, gc, bc = inp
        gam = jnp.cumsum(gc, -1)
        dec = jnp.exp(jnp.where(causal, gam[..., :, None] - gam[..., None, :], -jnp.inf))
        kb = kc * bc[..., None]
        A = jnp.einsum('bhtd,bhsd->bhts', kb, kc) * dec * strict
        rhs = jnp.concatenate([vc * bc[..., None], kb * jnp.exp(gam)[..., None]], -1)
        sol = lax.linalg.triangular_solve(eye + A, rhs, left_side=True, lower=True, unit_diagonal=True)
        u, w = sol[..., :d], sol[..., d:]
        v_new = u - jnp.einsum('bhtk,bhkv->bhtv', w, S)
        o = (jnp.einsum('bhtk,bhkv->bhtv', qc * jnp.exp(gam)[..., None], S)
             + jnp.einsum('bhts,bhsv->bhtv', jnp.einsum('bhtd,bhsd->bhts', qc, kc) * dec, v_new))
        gL = gam[..., -1]
        S = (jnp.exp(gL)[..., None, None] * S
             + jnp.einsum('bhsk,bhsv->bhkv', kc * jnp.exp(gL[..., None] - gam)[..., None], v_new))
        return S, o

    xs = (_to_chunks(q, L), _to_chunks(k, L), _to_chunks(v, L), _to_chunks(g, L), _to_chunks(beta, L))
    S, o = lax.scan(step, s0.astype(F32), xs)
    return _from_chunks(o), S


def _s5(u, x0r, x0i, a_re, a_im, log_dt, b_re, b_im, c_re, c_im, d_skip):
    T = u.shape[1]
    dt = jnp.exp(log_dt)[:, None]
    mag = jnp.exp(dt * a_re)
    ang = dt * a_im
    ab_r, ab_i = mag * jnp.cos(ang), mag * jnp.sin(ang)
    den = a_re * a_re + a_im * a_im
    nr, ni = ab_r - 1.0, ab_i
    f_r = (nr * a_re + ni * a_im) / den
    f_i = (ni * a_re - nr * a_im) / den
    bb_r = f_r[..., None] * b_re - f_i[..., None] * b_im
    bb_i = f_r[..., None] * b_im + f_i[..., None] * b_re
    bu_r = jnp.einsum('btgn,gpn->btgp', u, bb_r)
    bu_i = jnp.einsum('btgn,gpn->btgp', u, bb_i)
    x0r = x0r.astype(F32)
    x0i = x0i.astype(F32)
    bu_r = bu_r.at[:, 0].add(ab_r * x0r - ab_i * x0i)
    bu_i = bu_i.at[:, 0].add(ab_r * x0i + ab_i * x0r)
    G, P = ab_r.shape
    ar_t = jnp.broadcast_to(ab_r, (1, T, G, P))
    ai_t = jnp.broadcast_to(ab_i, (1, T, G, P))
    _, _, xr, xi = lax.associative_scan(_cplx_combine, (ar_t, ai_t, bu_r, bu_i), axis=1)
    y = jnp.einsum('btgp,gnp->btgn', xr, c_re) - jnp.einsum('btgp,gnp->btgn', xi, c_im) + d_skip * u
    return y, xr[:, -1], xi[:, -1]


def _rglru(x, h0, w_a, b_a, w_x, b_x, lam):
    B, T, _ = x.shape
    xb = x.reshape(B, T, LRU_BLOCKS, LRU_BD)
    r = jax.nn.sigmoid(jnp.einsum('btnd,nde->btne', xb, w_a).reshape(B, T, BR) + b_a)
    i = jax.nn.sigmoid(jnp.einsum('btnd,nde->btne', xb, w_x).reshape(B, T, BR) + b_x)
    log_a = -LRU_C * r * jax.nn.softplus(-lam)
    a = jnp.exp(log_a)
    bx = jnp.sqrt(-jnp.expm1(2.0 * log_a)) * (i * x)
    bx = bx.at[:, 0].add(a[:, 0] * h0.astype(F32))
    _, h = lax.associative_scan(_lin_combine, (a, bx), axis=1)
    return h, h[:, -1]


def _layer(x, pl, st, lp):
    (c0, n0, m0, s0, gconv0, s5r0, s5i0, lh0, lconv0) = st
    B, T, _ = x.shape
    xf = x.astype(F32)
    h = _rms(xf, lp['prenorm_g'])
    proj = h @ lp['w_in']
    (ml_q, ml_k, ml_v, ml_i, ml_f, ml_o, ml_z, gd_qkv, gd_a, gd_b, gd_z,
     s5_u, s5_z, lru_x, lru_z, gates) = _split(proj, IN_SIZES)
    hm, c1, n1, m1 = _mlstm(ml_q.reshape(B, T, ML_HEADS, ML_DH), ml_k.reshape(B, T, ML_HEADS, ML_DH),
                            ml_v.reshape(B, T, ML_HEADS, ML_DH), ml_i + lp['ml_bi'],
                            jax.nn.log_sigmoid(ml_f + lp['ml_bf']), c0, n0, m0)
    y_ml = _head_rms(hm, lp['ml_norm_g']) * jax.nn.sigmoid(ml_o) * jax.nn.silu(ml_z)
    qkv, gconv1 = _causal_conv(gd_qkv, gconv0, lp['gd_conv_w'])
    gq, gk, gv = jnp.split(jax.nn.silu(qkv), 3, axis=-1)
    g = -jnp.exp(lp['gd_a_log']) * jax.nn.softplus(gd_a + lp['gd_dt_bias'])
    hg, s1 = _gated_delta(gq.reshape(B, T, GD_HEADS, GD_DH), gk.reshape(B, T, GD_HEADS, GD_DH),
                          gv.reshape(B, T, GD_HEADS, GD_DH), g, jax.nn.sigmoid(gd_b), s0)
    y_gd = _head_rms(hg, lp['gd_norm_g']) * jax.nn.silu(gd_z)
    ys, s5r1, s5i1 = _s5(s5_u.reshape(B, T, S5_G, S5_GROUP), s5r0, s5i0, lp['s5_a_re'], lp['s5_a_im'],
                         lp['s5_log_dt'], lp['s5_b_re'], lp['s5_b_im'], lp['s5_c_re'], lp['s5_c_im'], lp['s5_d'])
    ys = jax.nn.gelu(ys.reshape(B, T, BR))
    ys = ys * jax.nn.sigmoid(ys @ lp['s5_glu_w'] + lp['s5_glu_b'])
    y_s5 = ys * jax.nn.silu(s5_z)
    xl, lconv1 = _causal_conv(lru_x, lconv0, lp['lru_conv_w'], lp['lru_conv_b'])
    hl, lh1 = _rglru(xl, lh0, lp['lru_wa'], lp['lru_ba'], lp['lru_wx'], lp['lru_bx'], lp['lru_lam'])
    y_lru = hl * jax.nn.silu(lru_z)
    ybr = jnp.stack([y_ml, y_gd, y_s5, y_lru], axis=2)
    pbr = jnp.einsum('btnc,ncd->btnd', ybr, lp['w_branch'])
    merged = jnp.sum(jax.nn.sigmoid(gates.reshape(B, T, N_BRANCH, D_MODEL)) * pbr, axis=2)
    out = merged @ lp['w_out']
    r = xf + _rms(out, lp['postnorm_g'])
    r = r + jax.nn.sigmoid(r @ lp['w_ple_gate']) * (pl.astype(F32) @ lp['w_ple'])
    return r.astype(x.dtype), (c1, n1, m1, s1, gconv1, s5r1, s5i1, lh1, lconv1)


def _zero_states(batch):
    z = lambda s: jnp.zeros((DEPTH, batch) + s, F32)
    return (z((ML_HEADS, ML_DH, ML_DH)), z((ML_HEADS, ML_DH)), z((ML_HEADS,)),
            z((GD_HEADS, GD_DH, GD_DH)), z((CONV_W - 1, 3 * BR)),
            z((S5_G, S5_P)), z((S5_G, S5_P)), z((BR,)), z((CONV_W - 1, BR)))


def _trunk(x, p, states, params):
    new = []
    for i in range(DEPTH):
        lp = {name: arr[i].astype(F32) for name, arr in params.items()}
        x, st = _layer(x, p[i], tuple(s[i] for s in states), lp)
        new.append(st)
    stacked = tuple(jnp.stack([n[j] for n in new]) for j in range(len(states)))
    return x, stacked


def setup_inputs(seed: int = 0) -> dict:
    key = jax.random.key(seed)
    ks = iter(jax.random.split(key, 64))
    nrm = lambda shape, scale: jax.random.normal(next(ks), shape, F32) * scale
    uni = lambda shape, lo, hi: jax.random.uniform(next(ks), shape, F32, lo, hi)
    Dp = DEPTH
    inp = {}
    inp['x_prompt'] = nrm((BATCH, SEQ, D_MODEL), 1.0)
    inp['x_sample'] = nrm((DEC_BATCH, DEC_SEQ, D_MODEL), 1.0)
    inp['state_mlstm_c'] = nrm((Dp, DEC_BATCH, ML_HEADS, ML_DH, ML_DH), 0.05)
    inp['state_mlstm_n'] = nrm((Dp, DEC_BATCH, ML_HEADS, ML_DH), 0.1)
    inp['state_mlstm_m'] = nrm((Dp, DEC_BATCH, ML_HEADS), 1.0)
    inp['state_gdn_s'] = nrm((Dp, DEC_BATCH, GD_HEADS, GD_DH, GD_DH), 0.05)
    inp['state_gdn_conv'] = nrm((Dp, DEC_BATCH, CONV_W - 1, 3 * BR), 1.0)
    inp['state_s5_re'] = nrm((Dp, DEC_BATCH, S5_G, S5_P), 0.1)
    inp['state_s5_im'] = nrm((Dp, DEC_BATCH, S5_G, S5_P), 0.1)
    inp['state_lru_h'] = nrm((Dp, DEC_BATCH, BR), 0.5)
    inp['state_lru_conv'] = nrm((Dp, DEC_BATCH, CONV_W - 1, BR), 1.0)
    inp['p_prompt'] = nrm((Dp, BATCH, SEQ, PLE_DIM), 1.0)
    inp['p_sample'] = nrm((Dp, DEC_BATCH, DEC_SEQ, PLE_DIM), 1.0)
    inp['prenorm_g'] = 1.0 + nrm((Dp, D_MODEL), 0.02)
    inp['postnorm_g'] = 1.0 + nrm((Dp, D_MODEL), 0.02)
    inp['w_in'] = nrm((Dp, D_MODEL, D_IN), D_MODEL ** -0.5)
    inp['ml_bi'] = nrm((Dp, ML_HEADS), 0.1)
    inp['ml_bf'] = jnp.linspace(3.0, 6.0, ML_HEADS, dtype=F32)[None] + nrm((Dp, ML_HEADS), 0.1)
    inp['ml_norm_g'] = 1.0 + nrm((Dp, BR), 0.02)
    inp['gd_conv_w'] = nrm((Dp, CONV_W, 3 * BR), CONV_W ** -0.5)
    inp['gd_a_log'] = jnp.log(uni((Dp, GD_HEADS), 1.0, 16.0))
    dt = jnp.exp(uni((Dp, GD_HEADS), math.log(1e-3), math.log(1e-1)))
    inp['gd_dt_bias'] = dt + jnp.log(-jnp.expm1(-dt))
    inp['gd_norm_g'] = 1.0 + nrm((Dp, BR), 0.02)
    n_idx = jnp.arange(S5_P, dtype=F32)
    inp['s5_a_re'] = -0.5 + nrm((Dp, S5_G, S5_P), 0.01)
    inp['s5_a_im'] = math.pi * n_idx + nrm((Dp, S5_G, S5_P), 0.01)
    inp['s5_log_dt'] = uni((Dp, S5_G), math.log(1e-3), math.log(1e-1))
    inp['s5_b_re'] = nrm((Dp, S5_G, S5_P, S5_GROUP), (2 * S5_GROUP) ** -0.5)
    inp['s5_b_im'] = nrm((Dp, S5_G, S5_P, S5_GROUP), (2 * S5_GROUP) ** -0.5)
    inp['s5_c_re'] = nrm((Dp, S5_G, S5_GROUP, S5_P), S5_P ** -0.5)
    inp['s5_c_im'] = nrm((Dp, S5_G, S5_GROUP, S5_P), S5_P ** -0.5)
    inp['s5_d'] = nrm((Dp, S5_G, S5_GROUP), 0.5)
    inp['s5_glu_w'] = nrm((Dp, BR, BR), BR ** -0.5)
    inp['s5_glu_b'] = nrm((Dp, BR), 0.01)
    inp['lru_conv_w'] = nrm((Dp, CONV_W, BR), CONV_W ** -0.5)
    inp['lru_conv_b'] = nrm((Dp, BR), 0.01)
    inp['lru_wa'] = nrm((Dp, LRU_BLOCKS, LRU_BD, LRU_BD), LRU_BD ** -0.5)
    inp['lru_ba'] = nrm((Dp, BR), 0.01)
    inp['lru_wx'] = nrm((Dp, LRU_BLOCKS, LRU_BD, LRU_BD), LRU_BD ** -0.5)
    inp['lru_bx'] = nrm((Dp, BR), 0.01)
    s = uni((Dp, BR), 0.9, 0.999) ** (1.0 / LRU_C)
    inp['lru_lam'] = jnp.log(s) - jnp.log1p(-s)
    inp['w_branch'] = nrm((Dp, N_BRANCH, BR, D_MODEL), BR ** -0.5)
    inp['w_out'] = nrm((Dp, D_MODEL, D_MODEL), D_MODEL ** -0.5)
    inp['w_ple'] = nrm((Dp, PLE_DIM, D_MODEL), PLE_DIM ** -0.5)
    inp['w_ple_gate'] = nrm((Dp, D_MODEL, D_MODEL), D_MODEL ** -0.5)
    return inp


def reference(x_prompt, x_sample, state_mlstm_c, state_mlstm_n, state_mlstm_m, state_gdn_s, state_gdn_conv,
              state_s5_re, state_s5_im, state_lru_h, state_lru_conv, p_prompt, p_sample,
              prenorm_g, postnorm_g, w_in, ml_bi, ml_bf, ml_norm_g, gd_conv_w, gd_a_log, gd_dt_bias, gd_norm_g,
              s5_a_re, s5_a_im, s5_log_dt, s5_b_re, s5_b_im, s5_c_re, s5_c_im, s5_d, s5_glu_w, s5_glu_b,
              lru_conv_w, lru_conv_b, lru_wa, lru_ba, lru_wx, lru_bx, lru_lam,
              w_branch, w_out, w_ple, w_ple_gate):
    params = dict(prenorm_g=prenorm_g, postnorm_g=postnorm_g, w_in=w_in, ml_bi=ml_bi, ml_bf=ml_bf,
                  ml_norm_g=ml_norm_g, gd_conv_w=gd_conv_w, gd_a_log=gd_a_log, gd_dt_bias=gd_dt_bias,
                  gd_norm_g=gd_norm_g, s5_a_re=s5_a_re, s5_a_im=s5_a_im, s5_log_dt=s5_log_dt,
                  s5_b_re=s5_b_re, s5_b_im=s5_b_im, s5_c_re=s5_c_re, s5_c_im=s5_c_im, s5_d=s5_d,
                  s5_glu_w=s5_glu_w, s5_glu_b=s5_glu_b, lru_conv_w=lru_conv_w, lru_conv_b=lru_conv_b,
                  lru_wa=lru_wa, lru_ba=lru_ba, lru_wx=lru_wx, lru_bx=lru_bx, lru_lam=lru_lam,
                  w_branch=w_branch, w_out=w_out, w_ple=w_ple, w_ple_gate=w_ple_gate)
    y_prompt, pr = _trunk(x_prompt, p_prompt, _zero_states(x_prompt.shape[0]), params)
    (pr_mlstm_c, pr_mlstm_n, pr_mlstm_m, pr_gdn_s, pr_gdn_conv, pr_s5_re, pr_s5_im, pr_lru_h, pr_lru_conv) = pr
    sample_states = (state_mlstm_c, state_mlstm_n, state_mlstm_m, state_gdn_s, state_gdn_conv,
                     state_s5_re, state_s5_im, state_lru_h, state_lru_conv)
    y_sample, sa = _trunk(x_sample, p_sample, sample_states, params)
    (sa_mlstm_c, sa_mlstm_n, sa_mlstm_m, sa_gdn_s, sa_gdn_conv, sa_s5_re, sa_s5_im, sa_lru_h, sa_lru_conv) = sa
    return (y_prompt, y_sample,
            pr_mlstm_c, pr_mlstm_n, pr_mlstm_m, pr_gdn_s, pr_gdn_conv, pr_s5_re, pr_s5_im, pr_lru_h, pr_lru_conv,
            sa_mlstm_c, sa_mlstm_n, sa_mlstm_m, sa_gdn_s, sa_gdn_conv, sa_s5_re, sa_s5_im, sa_lru_h, sa_lru_conv)
```

```python
import functools
import math

import jax
import jax.numpy as jnp
from jax import lax
from jax.experimental import pallas as pl
from jax.experimental.pallas import tpu as pltpu

F32 = jnp.float32
BF16 = jnp.bfloat16

D_MODEL = 1024
BR = 512
HEADS = 4
DH = 128
CONV_W = 4
S5_G = 32
S5_N = 16
S5_P = 64
S5_STATE = S5_G * S5_P
LRU_BLOCKS = 8
LRU_BD = 64
LRU_C = 8.0
PLE_DIM = 256
EPS = 1e-6
NEG = -1e30

LANE = 128
SEQ_BLOCK = 8
CARRY_ROWS = (CONV_W - 1) * SEQ_BLOCK

U_GD_QKV = 0
U_ML = 12
U_GATES = 32
U_GD_Z = 64
U_S5_U = 68
U_S5_Z = 72
U_LRU_X = 76
U_LRU_Z = 80
U_ML_IF = 84
U_GD_AB = 85
N_UNITS = 88
PROJ_COLS = N_UNITS * LANE

VMEM_LIMIT = 56 * 1024 * 1024


def _cparams(sem):
    return pltpu.CompilerParams(dimension_semantics=sem, vmem_limit_bytes=VMEM_LIMIT)


def _sigmoid(x):
    return 1.0 / (1.0 + jnp.exp(-x))


def _silu(x):
    return x * _sigmoid(x)


def _softplus(x):
    return jnp.maximum(x, 0.0) + jnp.log1p(jnp.exp(-jnp.abs(x)))


def _dot(a, b):
    return jnp.dot(a.astype(BF16), b.astype(BF16), preferred_element_type=F32)


def _dot_nt(a, b):
    return lax.dot_general(a.astype(BF16), b.astype(BF16), (((1,), (1,)), ((), ())),
                           preferred_element_type=F32)


def _dot_tn(a, b):
    return lax.dot_general(a.astype(BF16), b.astype(BF16), (((0,), (0,)), ((), ())),
                           preferred_element_type=F32)


def _cumsum_rows(x):
    n = x.shape[0]
    ri = lax.broadcasted_iota(jnp.int32, x.shape, 0)
    s = 1
    while s < n:
        x = x + jnp.where(ri >= s, pltpu.roll(x, s, axis=0), 0.0)
        s *= 2
    return x


def _row_from_col(col, eye):
    return jnp.sum(jnp.where(eye, col, 0.0), axis=0, keepdims=True)


def _head_rms(h, g):
    return h * lax.rsqrt(jnp.mean(h * h, axis=-1, keepdims=True) + EPS) * g


def _in_proj_kernel(x_ref, g_ref, w_ref, o_ref):
    x = x_ref[...]
    h = x * lax.rsqrt(jnp.mean(x * x, axis=-1, keepdims=True) + EPS) * g_ref[...]
    o_ref[...] = jnp.dot(h.astype(BF16), w_ref[...], preferred_element_type=F32)


def _in_proj(x, g, w, tm, tn):
    n = x.shape[0]
    return pl.pallas_call(
        _in_proj_kernel,
        grid=(PROJ_COLS // tn, n // tm),
        in_specs=[pl.BlockSpec((tm, D_MODEL), lambda j, i: (i, 0)),
                  pl.BlockSpec((1, D_MODEL), lambda j, i: (0, 0)),
                  pl.BlockSpec((D_MODEL, tn), lambda j, i: (0, j))],
        out_specs=pl.BlockSpec((tm, tn), lambda j, i: (i, j)),
        out_shape=jax.ShapeDtypeStruct((n, PROJ_COLS), F32),
        compiler_params=_cparams(("parallel", "parallel")),
        name="in_proj",
    )(x, g, w)


def _mlstm_kernel(q_ref, k_ref, v_ref, o_ref, z_ref, if_ref, ifb_ref, ng_ref,
                  c0_ref, n0_ref, m0_ref, y_ref, c_ref, n_ref, m_ref, *, L, TT, t_valid):
    h = pl.program_id(1)
    ti = pl.program_id(2)

    @pl.when(ti == 0)
    def _():
        c_ref[...] = c0_ref[...]
        n_ref[...] = n0_ref[...]
        m_ref[...] = m0_ref[...]

    ri = lax.broadcasted_iota(jnp.int32, (L, L), 0)
    ci = lax.broadcasted_iota(jnp.int32, (L, L), 1)
    causal = ci <= ri
    eye = ci == ri
    lane = lax.broadcasted_iota(jnp.int32, (L, LANE), 1)
    sel_i = lane == h
    sel_f = lane == HEADS + h
    tok_col = lax.broadcasted_iota(jnp.int32, (L, 1), 0)
    pick = lambda tile, sel: jnp.sum(jnp.where(sel, tile, 0.0), axis=1, keepdims=True)

    def seq_body(b, carry):
        srow = pl.ds(b, 1)
        for c in range(TT // L):
            rows = pl.ds(c * L * SEQ_BLOCK + b, L, stride=SEQ_BLOCK)
            valid = (ti * TT + c * L + tok_col) < t_valid
            ifv = if_ref[rows, :] + ifb_ref[...]
            logf = jnp.minimum(ifv, 0.0) - jnp.log1p(jnp.exp(-jnp.abs(ifv)))
            b_col = pick(_cumsum_rows(jnp.where(valid, logf, 0.0)), sel_f)
            i_col = jnp.where(valid, pick(ifv, sel_i), NEG)
            r_row = _row_from_col(i_col - b_col, eye)
            q = q_ref[rows, :] * (DH ** -0.5)
            k = k_ref[rows, :]
            v = v_ref[rows, :]
            m_prev = m_ref[srow, 0:1]
            cmat = c_ref[b, 0]
            nvec = n_ref[srow, :]
            dm = jnp.where(causal, b_col + r_row, NEG)
            m_inter = b_col + m_prev
            m_t = jnp.maximum(m_inter, jnp.max(dm, axis=1, keepdims=True))
            s = _dot_nt(q, k) * jnp.exp(dm - m_t)
            inter = jnp.exp(m_inter - m_t)
            num = _dot(s, v) + inter * _dot(q, cmat)
            den = jnp.sum(s, axis=1, keepdims=True) + inter * jnp.sum(q * nvec, axis=1, keepdims=True)
            hh = num / jnp.maximum(jnp.abs(den), jnp.exp(-m_t))
            b_last = b_col[L - 1:L, :]
            g_col = b_last - b_col + i_col
            m_new = jnp.maximum(b_last + m_prev, jnp.max(g_col, axis=0, keepdims=True))
            ws = jnp.exp(g_col - m_new)
            sc = jnp.exp(b_last + m_prev - m_new)
            kw = k * ws
            c_ref[b, 0] = sc * cmat + _dot_tn(kw, v)
            n_ref[srow, :] = sc * nvec + jnp.sum(kw, axis=0, keepdims=True)
            m_ref[srow, :] = jnp.broadcast_to(m_new, (1, LANE))
            yn = _head_rms(hh, ng_ref[...])
            y_ref[rows, :] = yn * _sigmoid(o_ref[rows, :]) * _silu(z_ref[rows, :])
        return carry

    lax.fori_loop(0, SEQ_BLOCK, seq_body, 0)


def _mlstm(proj, ifb, ng, c0, n0, m0, *, G, nT, TT, L, t_valid):
    R = TT * SEQ_BLOCK
    n = proj.shape[0]
    blk = lambda u: pl.BlockSpec((R, DH), lambda g, h, t, u=u: (g * nT + t, u + h))
    st4 = pl.BlockSpec((SEQ_BLOCK, 1, DH, DH), lambda g, h, t: (g, h, 0, 0))
    st2 = pl.BlockSpec((SEQ_BLOCK, LANE), lambda g, h, t: (g * HEADS + h, 0))
    kern = functools.partial(_mlstm_kernel, L=L, TT=TT, t_valid=t_valid)
    return pl.pallas_call(
        kern,
        grid=(G, HEADS, nT),
        in_specs=[blk(U_ML), blk(U_ML + 4), blk(U_ML + 8), blk(U_ML + 12), blk(U_ML + 16),
                  pl.BlockSpec((R, LANE), lambda g, h, t: (g * nT + t, U_ML_IF)),
                  pl.BlockSpec((1, LANE), lambda g, h, t: (0, 0)),
                  pl.BlockSpec((1, DH), lambda g, h, t: (0, h)),
                  st4, st2, st2],
        out_specs=[pl.BlockSpec((R, DH), lambda g, h, t: (g * nT + t, h)), st4, st2, st2],
        out_shape=[jax.ShapeDtypeStruct((n, BR), F32),
                   jax.ShapeDtypeStruct(c0.shape, F32),
                   jax.ShapeDtypeStruct(n0.shape, F32),
                   jax.ShapeDtypeStruct(m0.shape, F32)],
        compiler_params=_cparams(("arbitrary", "arbitrary", "arbitrary")),
        name="mlstm",
    )(proj, proj, proj, proj, proj, proj, ifb, ng, c0, n0, m0)


def _causal_conv_tile(x_ref, w_ref, b_ref, xp_ref, act_ref, cv0_ref, cv_ref, *, ti, R, width, tv_local,
                      last_tile, act_fn):
    @pl.when(ti == 0)
    def _():
        xp_ref[0:CARRY_ROWS, :] = cv0_ref[...]

    xp_ref[CARRY_ROWS:CARRY_ROWS + R, :] = x_ref[...]
    RB = 32 if R % 32 == 0 else SEQ_BLOCK

    def blk(i, carry):
        r0 = pl.multiple_of(i * RB, SEQ_BLOCK)
        cw = min(width, BR)
        for c0 in range(0, width, cw):
            cs = pl.ds(c0, cw)
            acc = xp_ref[pl.ds(r0, RB), cs] * w_ref[0:1, cs]
            for j in range(1, CONV_W):
                acc = acc + xp_ref[pl.ds(r0 + j * SEQ_BLOCK, RB), cs] * w_ref[j:j + 1, cs]
            if b_ref is not None:
                acc = acc + b_ref[:, cs]
            act_ref[pl.ds(r0, RB), cs] = act_fn(acc)
        return carry

    lax.fori_loop(0, R // RB, blk, 0)

    @pl.when(ti == last_tile)
    def _():
        cv_ref[...] = xp_ref[tv_local * SEQ_BLOCK:tv_local * SEQ_BLOCK + CARRY_ROWS, :]

    xp_ref[0:CARRY_ROWS, :] = xp_ref[R:R + CARRY_ROWS, :]


def _gdn_kernel(q_ref, k_ref, v_ref, z_ref, ab_ref, cwq_ref, cwk_ref, cwv_ref, alog_ref, dtb_ref, ng_ref,
                s0_ref, cvq0_ref, cvk0_ref, cvv0_ref,
                y_ref, s_ref, cvq_ref, cvk_ref, cvv_ref,
                xpq_ref, xpk_ref, xpv_ref, aq_ref, ak_ref, av_ref, *, L, TT, t_valid, nT):
    h = pl.program_id(1)
    ti = pl.program_id(2)
    R = TT * SEQ_BLOCK

    @pl.when(ti == 0)
    def _():
        s_ref[...] = s0_ref[...]

    for x_ref, cw_ref, xp_ref, a_ref, cv0_ref, cv_ref in (
            (q_ref, cwq_ref, xpq_ref, aq_ref, cvq0_ref, cvq_ref),
            (k_ref, cwk_ref, xpk_ref, ak_ref, cvk0_ref, cvk_ref),
            (v_ref, cwv_ref, xpv_ref, av_ref, cvv0_ref, cvv_ref)):
        _causal_conv_tile(x_ref, cw_ref, None, xp_ref, a_ref, cv0_ref, cv_ref, ti=ti, R=R, width=DH,
                          tv_local=t_valid - (nT - 1) * TT, last_tile=nT - 1, act_fn=_silu)

    ri = lax.broadcasted_iota(jnp.int32, (L, L), 0)
    ci = lax.broadcasted_iota(jnp.int32, (L, L), 1)
    causal = ci <= ri
    strict = ci < ri
    eye = ci == ri
    lane = lax.broadcasted_iota(jnp.int32, (L, LANE), 1)
    sel_a = lane == h
    sel_b = lane == HEADS + h
    tok_col = lax.broadcasted_iota(jnp.int32, (L, 1), 0)
    pick = lambda tile, sel: jnp.sum(jnp.where(sel, tile, 0.0), axis=1, keepdims=True)
    n_double = int(math.log2(L)) - 1

    def seq_body(b, carry):
        for c in range(TT // L):
            rows = pl.ds(c * L * SEQ_BLOCK + b, L, stride=SEQ_BLOCK)
            valid = (ti * TT + c * L + tok_col) < t_valid
            abv = ab_ref[rows, :]
            g_all = -jnp.exp(alog_ref[...]) * _softplus(abv + dtb_ref[...])
            gam = pick(_cumsum_rows(jnp.where(valid, g_all, 0.0)), sel_a)
            beta = jnp.where(valid, pick(_sigmoid(abv), sel_b), 0.0)
            dec = jnp.where(causal, jnp.exp(gam - _row_from_col(gam, eye)), 0.0)
            q = aq_ref[rows, :]
            k = ak_ref[rows, :]
            v = av_ref[rows, :]
            q = q * lax.rsqrt(jnp.sum(q * q, axis=-1, keepdims=True) + EPS) * (DH ** -0.5)
            k = k * lax.rsqrt(jnp.sum(k * k, axis=-1, keepdims=True) + EPS)
            kbeta = k * beta
            amat = jnp.where(strict, _dot_nt(kbeta, k) * dec, 0.0)
            x = -amat
            p = amat
            for _ in range(n_double):
                p = _dot(p, p)
                x = x + p + _dot(x, p)
            eg = jnp.exp(gam)
            rhs = jnp.concatenate([v * beta, kbeta * eg], axis=1)
            sol = rhs + _dot(x, rhs)
            u = sol[:, :DH]
            w = sol[:, DH:]
            smat = s_ref[b, 0]
            v_new = u - _dot(w, smat)
            o = _dot(q * eg, smat) + _dot(_dot_nt(q, k) * dec, v_new)
            g_last = gam[L - 1:L, :]
            s_ref[b, 0] = jnp.exp(g_last) * smat + _dot_tn(k * jnp.exp(g_last - gam), v_new)
            y_ref[rows, :] = _head_rms(o, ng_ref[...]) * _silu(z_ref[rows, :])
        return carry

    lax.fori_loop(0, SEQ_BLOCK, seq_body, 0)


def _gdn(proj, cw, alog, dtb, ng, s0, cv0, *, G, nT, TT, L, t_valid):
    R = TT * SEQ_BLOCK
    n = proj.shape[0]
    upb = BR // LANE
    blk = lambda u: pl.BlockSpec((R, DH), lambda g, h, t, u=u: (g * nT + t, u + h))
    st4 = pl.BlockSpec((SEQ_BLOCK, 1, DH, DH), lambda g, h, t: (g, h, 0, 0))
    cvs = lambda j: pl.BlockSpec((CARRY_ROWS, DH), lambda g, h, t, j=j: (g, j * upb + h))
    cws = lambda j: pl.BlockSpec((CONV_W, DH), lambda g, h, t, j=j: (0, j * upb + h))
    one = pl.BlockSpec((1, LANE), lambda g, h, t: (0, 0))
    kern = functools.partial(_gdn_kernel, L=L, TT=TT, t_valid=t_valid, nT=nT)
    cv_shape = jax.ShapeDtypeStruct((cv0.shape[0], BR), F32)
    return pl.pallas_call(
        kern,
        grid=(G, HEADS, nT),
        in_specs=[blk(U_GD_QKV), blk(U_GD_QKV + 4), blk(U_GD_QKV + 8), blk(U_GD_Z),
                  pl.BlockSpec((R, LANE), lambda g, h, t: (g * nT + t, U_GD_AB)),
                  cws(0), cws(1), cws(2), one, one,
                  pl.BlockSpec((1, DH), lambda g, h, t: (0, h)),
                  st4, cvs(0), cvs(1), cvs(2)],
        out_specs=[pl.BlockSpec((R, DH), lambda g, h, t: (g * nT + t, h)), st4, cvs(0), cvs(0), cvs(0)],
        out_shape=[jax.ShapeDtypeStruct((n, BR), F32), jax.ShapeDtypeStruct(s0.shape, F32),
                   cv_shape, cv_shape, cv_shape],
        scratch_shapes=[pltpu.VMEM((R + CARRY_ROWS, DH), F32)] * 3 + [pltpu.VMEM((R, DH), F32)] * 3,
        compiler_params=_cparams(("arbitrary", "arbitrary", "arbitrary")),
        name="gdn",
    )(proj, proj, proj, proj, proj, cw, cw, cw, alog, dtb, ng, s0, cv0, cv0, cv0)


def _s5_prep_kernel(are_ref, aim_ref, ldt_ref, bre_ref, bim_ref, abr_ref, abi_ref, bbr_ref, bbi_ref):
    a_re = are_ref[...]
    a_im = aim_ref[...]
    dt = jnp.exp(ldt_ref[...])
    mag = jnp.exp(dt * a_re)
    ang = dt * a_im
    ab_r = mag * jnp.cos(ang)
    ab_i = mag * jnp.sin(ang)
    den = a_re * a_re + a_im * a_im
    nr = ab_r - 1.0
    ni = ab_i
    f_r = (nr * a_re + ni * a_im) / den
    f_i = (ni * a_re - nr * a_im) / den
    abr_ref[...] = ab_r
    abi_ref[...] = ab_i
    f_r = f_r[0:1, :]
    f_i = f_i[0:1, :]
    bbr_ref[...] = f_r * bre_ref[...] - f_i * bim_ref[...]
    bbi_ref[...] = f_r * bim_ref[...] + f_i * bre_ref[...]


def _s5_prep(a_re, a_im, log_dt, b_re, b_im):
    rep = lambda a: jnp.broadcast_to(a.reshape(1, S5_STATE), (SEQ_BLOCK, S5_STATE))
    ldt = rep(jnp.broadcast_to(log_dt[:, None], (S5_G, S5_P)))
    bt = lambda b: jnp.transpose(b, (2, 0, 1)).reshape(S5_N, S5_STATE)
    shp = lambda r: jax.ShapeDtypeStruct((r, S5_STATE), F32)
    abr, abi, bbr, bbi = pl.pallas_call(
        _s5_prep_kernel,
        out_shape=[shp(SEQ_BLOCK), shp(SEQ_BLOCK), shp(S5_N), shp(S5_N)],
        name="s5_prep",
    )(rep(a_re), rep(a_im), ldt, bt(b_re), bt(b_im))
    return abr, abi, bbr, bbi


def _gelu_tanh(x):
    return 0.5 * x * (1.0 + jnp.tanh(math.sqrt(2.0 / math.pi) * (x + 0.044715 * (x * x * x))))


def _s5_kernel(u_ref, z_ref, abr_ref, abi_ref, wbr_ref, wbi_ref, wcr_ref, wci_ref, d_ref, gw_ref, gb_ref,
               x0r_ref, x0i_ref, y_ref, xr_ref, xi_ref, hr_ref, hi_ref, ys_ref, *, TT, n_steps):
    ti = pl.program_id(1)
    NB = BR // LANE
    SB = S5_STATE // NB

    @pl.when(ti == 0)
    def _():
        xr_ref[...] = x0r_ref[...]
        xi_ref[...] = x0i_ref[...]

    for kb in range(NB):
        ub = u_ref[:, kb * LANE:(kb + 1) * LANE].astype(BF16)
        hr_ref[:, kb * SB:(kb + 1) * SB] = jnp.dot(ub, wbr_ref[kb], preferred_element_type=F32)
        hi_ref[:, kb * SB:(kb + 1) * SB] = jnp.dot(ub, wbi_ref[kb], preferred_element_type=F32)

    for kb in range(NB):
        sl = pl.ds(kb * SB, SB)
        ar = abr_ref[:, sl]
        ai = abi_ref[:, sl]

        def step(t, carry):
            xr, xi = carry
            rows = pl.ds(pl.multiple_of(t * SEQ_BLOCK, SEQ_BLOCK), SEQ_BLOCK)
            nxr = ar * xr - ai * xi + hr_ref[rows, sl]
            nxi = ar * xi + ai * xr + hi_ref[rows, sl]
            hr_ref[rows, sl] = nxr
            hi_ref[rows, sl] = nxi
            return nxr, nxi

        xr, xi = lax.fori_loop(0, n_steps, step, (xr_ref[:, sl], xi_ref[:, sl]))
        xr_ref[:, sl] = xr
        xi_ref[:, sl] = xi

    for kb in range(NB):
        sl = pl.ds(kb * SB, SB)
        cs = pl.ds(kb * LANE, LANE)
        yk = (jnp.dot(hr_ref[:, sl].astype(BF16), wcr_ref[kb], preferred_element_type=F32)
              - jnp.dot(hi_ref[:, sl].astype(BF16), wci_ref[kb], preferred_element_type=F32)
              + d_ref[:, cs] * u_ref[:, cs])
        ys_ref[:, cs] = _gelu_tanh(yk)
    ys = ys_ref[...]
    glu = ys * _sigmoid(jnp.dot(ys.astype(BF16), gw_ref[...], preferred_element_type=F32) + gb_ref[...])
    y_ref[...] = glu * _silu(z_ref[...])


def _s5(proj, abr, abi, wbr, wbi, wcr, wci, d, gw, gb, x0r, x0i, *, G, nT, TT, t_valid):
    assert nT == 1 or t_valid == nT * TT
    R = TT * SEQ_BLOCK
    n = proj.shape[0]
    full = lambda a: pl.BlockSpec(a.shape, lambda g, t, nd=a.ndim: (0,) * nd)
    st = pl.BlockSpec((SEQ_BLOCK, S5_STATE), lambda g, t: (g, 0))
    kern = functools.partial(_s5_kernel, TT=TT, n_steps=t_valid - (nT - 1) * TT)
    return pl.pallas_call(
        kern,
        grid=(G, nT),
        in_specs=[pl.BlockSpec((R, BR), lambda g, t: (g * nT + t, U_S5_U // 4)),
                  pl.BlockSpec((R, BR), lambda g, t: (g * nT + t, U_S5_Z // 4)),
                  full(abr), full(abi), full(wbr), full(wbi), full(wcr), full(wci), full(d), full(gw), full(gb),
                  st, st],
        out_specs=[pl.BlockSpec((R, BR), lambda g, t: (g * nT + t, 0)), st, st],
        out_shape=[jax.ShapeDtypeStruct((n, BR), F32),
                   jax.ShapeDtypeStruct(x0r.shape, F32),
                   jax.ShapeDtypeStruct(x0i.shape, F32)],
        scratch_shapes=[pltpu.VMEM((R, S5_STATE), F32), pltpu.VMEM((R, S5_STATE), F32),
                        pltpu.VMEM((R, BR), F32)],
        compiler_params=_cparams(("arbitrary", "arbitrary")),
        name="s5",
    )(proj, proj, abr, abi, wbr, wbi, wcr, wci, d, gw, gb, x0r, x0i)


def _lru_kernel(x_ref, z_ref, cw_ref, cb_ref, wa_ref, ba_ref, wx_ref, bx_ref, lam_ref, h0_ref, cv0_ref,
                y_ref, h_ref, cv_ref, xp_ref, xl_ref, a_ref, *, TT, n_steps, nT):
    ti = pl.program_id(1)
    R = TT * SEQ_BLOCK

    @pl.when(ti == 0)
    def _():
        h_ref[...] = h0_ref[...]

    _causal_conv_tile(x_ref, cw_ref, cb_ref, xp_ref, xl_ref, cv0_ref, cv_ref, ti=ti, R=R, width=BR,
                      tv_local=n_steps, last_tile=nT - 1, act_fn=lambda a: a)

    xl = xl_ref[...]
    xb = xl.astype(BF16)
    r = _sigmoid(jnp.dot(xb, wa_ref[...], preferred_element_type=F32) + ba_ref[...])
    i = _sigmoid(jnp.dot(xb, wx_ref[...], preferred_element_type=F32) + bx_ref[...])
    log_a = -LRU_C * r * _softplus(-lam_ref[...])
    a_ref[...] = jnp.exp(log_a)
    th = jnp.tanh(log_a)
    xl_ref[...] = jnp.sqrt(-2.0 * th / (1.0 - th)) * (i * xl)

    def step(t, h):
        rows = pl.ds(pl.multiple_of(t * SEQ_BLOCK, SEQ_BLOCK), SEQ_BLOCK)
        hn = a_ref[rows, :] * h + xl_ref[rows, :]
        xl_ref[rows, :] = hn
        return hn

    h_ref[...] = lax.fori_loop(0, n_steps, step, h_ref[...])
    y_ref[...] = xl_ref[...] * _silu(z_ref[...])


def _lru(proj, cw, cb, wa, ba, wx, bx, lam, h0, cv0, *, G, nT, TT, t_valid):
    assert nT == 1 or t_valid == nT * TT
    R = TT * SEQ_BLOCK
    n = proj.shape[0]
    full = lambda a: pl.BlockSpec(a.shape, lambda g, t, nd=a.ndim: (0,) * nd)
    st = pl.BlockSpec((SEQ_BLOCK, BR), lambda g, t: (g, 0))
    cvs = pl.BlockSpec((CARRY_ROWS, BR), lambda g, t: (g, 0))
    kern = functools.partial(_lru_kernel, TT=TT, n_steps=t_valid - (nT - 1) * TT, nT=nT)
    return pl.pallas_call(
        kern,
        grid=(G, nT),
        in_specs=[pl.BlockSpec((R, BR), lambda g, t: (g * nT + t, U_LRU_X // 4)),
                  pl.BlockSpec((R, BR), lambda g, t: (g * nT + t, U_LRU_Z // 4)),
                  full(cw), full(cb), full(wa), full(ba), full(wx), full(bx), full(lam), st, cvs],
        out_specs=[pl.BlockSpec((R, BR), lambda g, t: (g * nT + t, 0)), st, cvs],
        out_shape=[jax.ShapeDtypeStruct((n, BR), F32),
                   jax.ShapeDtypeStruct(h0.shape, F32),
                   jax.ShapeDtypeStruct(cv0.shape, F32)],
        scratch_shapes=[pltpu.VMEM((R + CARRY_ROWS, BR), F32), pltpu.VMEM((R, BR), F32),
                        pltpu.VMEM((R, BR), F32)],
        compiler_params=_cparams(("arbitrary", "arbitrary")),
        name="lru",
    )(proj, proj, cw, cb, wa, ba, wx, bx, lam, h0, cv0)


def _merge_kernel(yml_ref, ygd_ref, ys5_ref, ylru_ref, gates_ref, x_ref, p_ref,
                  wbr_ref, wout_ref, wpg_ref, wple_ref, pg_ref, o_ref):
    merged = None
    for nb, y_ref in enumerate((yml_ref, ygd_ref, ys5_ref, ylru_ref)):
        pbr = jnp.dot(y_ref[...].astype(BF16), wbr_ref[nb], preferred_element_type=F32)
        term = _sigmoid(gates_ref[:, nb * D_MODEL:(nb + 1) * D_MODEL]) * pbr
        merged = term if merged is None else merged + term
    out = jnp.dot(merged.astype(BF16), wout_ref[...], preferred_element_type=F32)
    r = x_ref[...] + out * lax.rsqrt(jnp.mean(out * out, axis=-1, keepdims=True) + EPS) * pg_ref[...]
    gate = _sigmoid(jnp.dot(r.astype(BF16), wpg_ref[...], preferred_element_type=F32))
    o_ref[...] = r + gate * jnp.dot(p_ref[...].astype(BF16), wple_ref[...], preferred_element_type=F32)


def _merge(ys, proj, x, p, wbr, wout, wpg, wple, pg, tm):
    n = x.shape[0]
    row = lambda c: pl.BlockSpec((tm, c), lambda i: (i, 0))
    full = lambda a: pl.BlockSpec(a.shape, lambda i, nd=a.ndim: (0,) * nd)
    return pl.pallas_call(
        _merge_kernel,
        grid=(n // tm,),
        in_specs=[row(BR), row(BR), row(BR), row(BR),
                  pl.BlockSpec((tm, 4 * D_MODEL), lambda i: (i, U_GATES // 32)),
                  row(D_MODEL), row(PLE_DIM), full(wbr), full(wout), full(wpg), full(wple), full(pg)],
        out_specs=row(D_MODEL),
        out_shape=jax.ShapeDtypeStruct((n, D_MODEL), F32),
        compiler_params=_cparams(("parallel",)),
        name="merge",
    )(*ys, proj, x, p, wbr, wout, wpg, wple, pg)


def _permute_w_in(w_in):
    sizes = (BR, BR, BR, HEADS, HEADS, BR, BR, 3 * BR, HEADS, HEADS, BR, BR, BR, BR, BR, 4 * D_MODEL)
    offs = [0]
    for s in sizes:
        offs.append(offs[-1] + s)
    seg = lambda i: w_in[:, :, offs[i]:offs[i + 1]]
    (ml_q, ml_k, ml_v, ml_i, ml_f, ml_o, ml_z, gd_qkv, gd_a, gd_b, gd_z,
     s5_u, s5_z, lru_x, lru_z, gates) = [seg(i) for i in range(len(sizes))]
    zeros = lambda c: jnp.zeros(w_in.shape[:2] + (c,), w_in.dtype)
    cols = [gd_qkv, ml_q, ml_k, ml_v, ml_o, ml_z, gates, gd_z, s5_u, s5_z, lru_x, lru_z,
            ml_i, ml_f, zeros(LANE - 2 * HEADS), gd_a, gd_b, zeros(LANE - 2 * HEADS),
            zeros((N_UNITS - U_GD_AB - 1) * LANE)]
    return jnp.concatenate(cols, axis=-1).astype(BF16)


def _pad_lanes(*vecs):
    v = jnp.concatenate(vecs)
    return jnp.pad(v, (0, LANE - v.shape[0])).reshape(1, LANE)


def _block_diag(blocks):
    n, a, b = blocks.shape
    eye = jnp.eye(n, dtype=blocks.dtype)
    return jnp.einsum('ij,iab->iajb', eye, blocks).reshape(n * a, n * b)


def _to_rows(x, G):
    B, T, C = x.shape
    return x.reshape(G, SEQ_BLOCK, T, C).transpose(0, 2, 1, 3).reshape(G * T * SEQ_BLOCK, C)


def _from_rows(y, G, T):
    C = y.shape[-1]
    return y.reshape(G, T, SEQ_BLOCK, C).transpose(0, 2, 1, 3).reshape(G * SEQ_BLOCK, T, C)


def _group(x, p, states, params, *, T, t_valid, TT, L, tm_proj, tm_merge):
    B = x.shape[0]
    G = B // SEQ_BLOCK
    nT = T // TT
    depth = p.shape[0]
    xr = _to_rows(x, G)
    tm_proj = min(tm_proj, xr.shape[0])
    tm_merge = min(tm_merge, xr.shape[0])
    new_states = []
    for li in range(depth):
        lp = {k: v[li] for k, v in params.items()}
        if states is None:
            c0 = jnp.zeros((B, HEADS, DH, DH), F32)
            n0 = jnp.zeros((B * HEADS, LANE), F32)
            m0 = jnp.zeros((B * HEADS, LANE), F32)
            s0 = jnp.zeros((B, HEADS, DH, DH), F32)
            gcv0 = jnp.zeros((G * CARRY_ROWS, 3 * BR), F32)
            x0r = jnp.zeros((B, S5_STATE), F32)
            x0i = jnp.zeros((B, S5_STATE), F32)
            h0 = jnp.zeros((B, BR), F32)
            lcv0 = jnp.zeros((G * CARRY_ROWS, BR), F32)
        else:
            (c0, n0, m0, s0, gcv0, x0r, x0i, h0, lcv0) = [s[li] for s in states]
            n0 = n0.reshape(G, SEQ_BLOCK, HEADS, DH).transpose(0, 2, 1, 3).reshape(B * HEADS, DH)
            m0 = m0.reshape(G, SEQ_BLOCK, HEADS).transpose(0, 2, 1).reshape(B * HEADS, 1)
            m0 = jnp.broadcast_to(m0, (B * HEADS, LANE))
            gcv0 = _to_rows(gcv0, G)
            x0r = x0r.reshape(B, S5_STATE)
            x0i = x0i.reshape(B, S5_STATE)
            lcv0 = _to_rows(lcv0, G)
        pr = _to_rows(p[li], G)

        proj = _in_proj(xr, lp['prenorm_g'], lp['w_in_p'], tm_proj, PROJ_COLS // 4)

        y_ml, c1, n1, m1 = _mlstm(proj, lp['ml_ifb'], lp['ml_norm_g'], c0, n0, m0,
                                  G=G, nT=nT, TT=TT, L=L, t_valid=t_valid)
        y_gd, s1, cvq, cvk, cvv = _gdn(proj, lp['gd_conv_w'], lp['gd_alog'], lp['gd_dtb'], lp['gd_norm_g'], s0,
                                       gcv0, G=G, nT=nT, TT=TT, L=L, t_valid=t_valid)
        gcv1 = jnp.concatenate([cvq, cvk, cvv], axis=-1)
        y_s5, x1r, x1i = _s5(proj, lp['s5_abr'], lp['s5_abi'], lp['s5_wbr'], lp['s5_wbi'], lp['s5_wcr'],
                             lp['s5_wci'], lp['s5_d'], lp['s5_glu_w'], lp['s5_glu_b'], x0r, x0i,
                             G=G, nT=nT, TT=TT, t_valid=t_valid)
        y_lru, h1, lcv1 = _lru(proj, lp['lru_conv_w'], lp['lru_conv_b'], lp['lru_wa'], lp['lru_ba'],
                               lp['lru_wx'], lp['lru_bx'], lp['lru_lam'], h0, lcv0,
                               G=G, nT=nT, TT=TT, t_valid=t_valid)
        xr = _merge((y_ml, y_gd, y_s5, y_lru), proj, xr, pr, lp['w_branch'], lp['w_out'], lp['w_ple_gate'],
                    lp['w_ple'], lp['postnorm_g'], tm_merge)

        n1 = n1.reshape(G, HEADS, SEQ_BLOCK, DH).transpose(0, 2, 1, 3)
        m1 = m1[:, 0].reshape(G, HEADS, SEQ_BLOCK).transpose(0, 2, 1)
        new_states.append((c1, n1.reshape(B, HEADS, DH), m1.reshape(B, HEADS), s1,
                           _from_rows(gcv1, G, CONV_W - 1), x1r.reshape(B, S5_G, S5_P),
                           x1i.reshape(B, S5_G, S5_P), h1, _from_rows(lcv1, G, CONV_W - 1)))
    y = _from_rows(xr, G, T)
    stacked = tuple(jnp.stack([ns[j] for ns in new_states]) for j in range(9))
    return y, stacked


def _prepare_params(prenorm_g, postnorm_g, w_in, ml_bi, ml_bf, ml_norm_g, gd_conv_w, gd_a_log, gd_dt_bias,
                    gd_norm_g, s5_a_re, s5_a_im, s5_log_dt, s5_b_re, s5_b_im, s5_c_re, s5_c_im, s5_d,
                    s5_glu_w, s5_glu_b, lru_conv_w, lru_conv_b, lru_wa, lru_ba, lru_wx, lru_bx, lru_lam,
                    w_branch, w_out, w_ple, w_ple_gate):
    depth = w_in.shape[0]
    row = lambda a: a.reshape(depth, 1, -1)
    per_layer = lambda f, *a: jnp.stack([f(*[x[i] for x in a]) for i in range(depth)])
    prm = dict(
        prenorm_g=row(prenorm_g), postnorm_g=row(postnorm_g), w_in_p=_permute_w_in(w_in),
        ml_ifb=per_layer(_pad_lanes, ml_bi, ml_bf), ml_norm_g=row(ml_norm_g),
        gd_conv_w=gd_conv_w, gd_norm_g=row(gd_norm_g),
        gd_alog=per_layer(lambda a: _pad_lanes(a), gd_a_log),
        gd_dtb=per_layer(lambda a: _pad_lanes(a), gd_dt_bias),
        s5_d=row(s5_d), s5_glu_w=s5_glu_w.astype(BF16), s5_glu_b=row(s5_glu_b),
        lru_conv_w=lru_conv_w, lru_conv_b=row(lru_conv_b),
        lru_wa=per_layer(_block_diag, lru_wa).astype(BF16), lru_ba=row(lru_ba),
        lru_wx=per_layer(_block_diag, lru_wx).astype(BF16), lru_bx=row(lru_bx), lru_lam=row(lru_lam),
        w_branch=w_branch.astype(BF16), w_out=w_out.astype(BF16), w_ple=w_ple.astype(BF16),
        w_ple_gate=w_ple_gate.astype(BF16),
    )
    abr, abi, wbr, wbi, wcr, wci = [], [], [], [], [], []
    nb = BR // LANE
    gpb = S5_G // nb
    for i in range(depth):
        a_r, a_i, bbr, bbi = _s5_prep(s5_a_re[i], s5_a_im[i], s5_log_dt[i], s5_b_re[i], s5_b_im[i])
        abr.append(a_r)
        abi.append(a_i)
        in_blocks = lambda bb: jnp.stack([_block_diag(bb.reshape(S5_N, S5_G, S5_P).transpose(1, 0, 2)
                                                      [k * gpb:(k + 1) * gpb]) for k in range(nb)])
        out_blocks = lambda c: jnp.stack([_block_diag(jnp.transpose(c, (0, 2, 1))[k * gpb:(k + 1) * gpb])
                                          for k in range(nb)])
        wbr.append(in_blocks(bbr).astype(BF16))
        wbi.append(in_blocks(bbi).astype(BF16))
        wcr.append(out_blocks(s5_c_re[i]).astype(BF16))
        wci.append(out_blocks(s5_c_im[i]).astype(BF16))
    prm.update(s5_abr=jnp.stack(abr), s5_abi=jnp.stack(abi), s5_wbr=jnp.stack(wbr), s5_wbi=jnp.stack(wbi),
               s5_wcr=jnp.stack(wcr), s5_wci=jnp.stack(wci))
    return prm


def kernel(x_prompt, x_sample, state_mlstm_c, state_mlstm_n, state_mlstm_m, state_gdn_s, state_gdn_conv, state_s5_re, state_s5_im, state_lru_h, state_lru_conv, p_prompt, p_sample, prenorm_g, postnorm_g, w_in, ml_bi, ml_bf, ml_norm_g, gd_conv_w, gd_a_log, gd_dt_bias, gd_norm_g, s5_a_re, s5_a_im, s5_log_dt, s5_b_re, s5_b_im, s5_c_re, s5_c_im, s5_d, s5_glu_w, s5_glu_b, lru_conv_w, lru_conv_b, lru_wa, lru_ba, lru_wx, lru_bx, lru_lam, w_branch, w_out, w_ple, w_ple_gate):
    prm = _prepare_params(prenorm_g, postnorm_g, w_in, ml_bi, ml_bf, ml_norm_g, gd_conv_w, gd_a_log,
                          gd_dt_bias, gd_norm_g, s5_a_re, s5_a_im, s5_log_dt, s5_b_re, s5_b_im, s5_c_re,
                          s5_c_im, s5_d, s5_glu_w, s5_glu_b, lru_conv_w, lru_conv_b, lru_wa, lru_ba, lru_wx,
                          lru_bx, lru_lam, w_branch, w_out, w_ple, w_ple_gate)

    t_p = x_prompt.shape[1]
    tt_p = math.gcd(t_p, 64)
    y_prompt, pr = _group(x_prompt, p_prompt, None, prm, T=t_p, t_valid=t_p, TT=tt_p, L=tt_p,
                          tm_proj=min(512, t_p * SEQ_BLOCK), tm_merge=min(256, t_p * SEQ_BLOCK))

    t_s = x_sample.shape[1]
    t_pad = -(-t_s // SEQ_BLOCK) * SEQ_BLOCK
    pad_t = lambda a, ax: jnp.pad(a, [(0, t_pad - t_s) if i == ax else (0, 0) for i in range(a.ndim)])
    sample_states = (state_mlstm_c, state_mlstm_n, state_mlstm_m, state_gdn_s, state_gdn_conv,
                     state_s5_re, state_s5_im, state_lru_h, state_lru_conv)
    y_s, sa = _group(pad_t(x_sample, 1), pad_t(p_sample, 2), sample_states, prm, T=t_pad, t_valid=t_s,
                     TT=t_pad, L=t_pad, tm_proj=512, tm_merge=256)
    y_sample = y_s[:, :t_s]
    return (y_prompt, y_sample) + pr + sa
```

```python
import functools
import math

import jax
import jax.numpy as jnp
from jax import lax
from jax.experimental import pallas as pl
from jax.experimental.pallas import tpu as pltpu

F32 = jnp.float32
BF16 = jnp.bfloat16

D_MODEL = 1024
BR = 512
HEADS = 4
DH = 128
CONV_W = 4
S5_G = 32
S5_N = 16
S5_P = 64
S5_STATE = S5_G * S5_P
LRU_BLOCKS = 8
LRU_BD = 64
LRU_C = 8.0
PLE_DIM = 256
EPS = 1e-6
NEG = -1e30

LANE = 128
SEQ_BLOCK = 8
CARRY_ROWS = (CONV_W - 1) * SEQ_BLOCK

U_GD_QKV = 0
U_ML = 12
U_GATES = 32
U_GD_Z = 64
U_S5_U = 68
U_S5_Z = 72
U_LRU_X = 76
U_LRU_Z = 80
U_ML_IF = 84
U_GD_AB = 85
N_UNITS = 88
PROJ_COLS = N_UNITS * LANE

VMEM_LIMIT = 56 * 1024 * 1024


def _cparams(sem):
    return pltpu.CompilerParams(dimension_semantics=sem, vmem_limit_bytes=VMEM_LIMIT)


def _sigmoid(x):
    return 1.0 / (1.0 + jnp.exp(-x))


def _silu(x):
    return x * _sigmoid(x)


def _softplus(x):
    return jnp.maximum(x, 0.0) + jnp.log1p(jnp.exp(-jnp.abs(x)))


def _dot(a, b):
    return jnp.dot(a.astype(BF16), b.astype(BF16), preferred_element_type=F32)


def _dot_nt(a, b):
    return lax.dot_general(a.astype(BF16), b.astype(BF16), (((1,), (1,)), ((), ())),
                           preferred_element_type=F32)


def _dot_tn(a, b):
    return lax.dot_general(a.astype(BF16), b.astype(BF16), (((0,), (0,)), ((), ())),
                           preferred_element_type=F32)


def _cumsum_rows(x):
    n = x.shape[0]
    ri = lax.broadcasted_iota(jnp.int32, x.shape, 0)
    s = 1
    while s < n:
        x = x + jnp.where(ri >= s, pltpu.roll(x, s, axis=0), 0.0)
        s *= 2
    return x


def _row_from_col(col, eye):
    return jnp.sum(jnp.where(eye, col, 0.0), axis=0, keepdims=True)


def _head_rms(h, g):
    return h * lax.rsqrt(jnp.mean(h * h, axis=-1, keepdims=True) + EPS) * g


def _in_proj_kernel(x_ref, g_ref, w_ref, o_ref):
    x = x_ref[...]
    h = x * lax.rsqrt(jnp.mean(x * x, axis=-1, keepdims=True) + EPS) * g_ref[...]
    o_ref[...] = jnp.dot(h.astype(BF16), w_ref[...], preferred_element_type=F32)


def _in_proj(x, g, w, tm, tn):
    n = x.shape[0]
    return pl.pallas_call(
        _in_proj_kernel,
        grid=(PROJ_COLS // tn, n // tm),
        in_specs=[pl.BlockSpec((tm, D_MODEL), lambda j, i: (i, 0)),
                  pl.BlockSpec((1, D_MODEL), lambda j, i: (0, 0)),
                  pl.BlockSpec((D_MODEL, tn), lambda j, i: (0, j))],
        out_specs=pl.BlockSpec((tm, tn), lambda j, i: (i, j)),
        out_shape=jax.ShapeDtypeStruct((n, PROJ_COLS), F32),
        compiler_params=_cparams(("parallel", "parallel")),
        name="in_proj",
    )(x, g, w)


def _mlstm_kernel(q_ref, k_ref, v_ref, o_ref, z_ref, if_ref, ifb_ref, ng_ref,
                  c0_ref, n0_ref, m0_ref, y_ref, c_ref, n_ref, m_ref, *, L, TT, t_valid):
    h = pl.program_id(1)
    ti = pl.program_id(2)

    @pl.when(ti == 0)
    def _():
        c_ref[...] = c0_ref[...]
        n_ref[...] = n0_ref[...]
        m_ref[...] = m0_ref[...]

    ri = lax.broadcasted_iota(jnp.int32, (L, L), 0)
    ci = lax.broadcasted_iota(jnp.int32, (L, L), 1)
    causal = ci <= ri
    eye = ci == ri
    lane = lax.broadcasted_iota(jnp.int32, (L, LANE), 1)
    sel_i = lane == h
    sel_f = lane == HEADS + h
    tok_col = lax.broadcasted_iota(jnp.int32, (L, 1), 0)
    pick = lambda tile, sel: jnp.sum(jnp.where(sel, tile, 0.0), axis=1, keepdims=True)

    seqs = range(SEQ_BLOCK)
    for c in range(TT // L):
        valid = (ti * TT + c * L + tok_col) < t_valid
        rows, st = [], []
        for b in seqs:
            r = pl.ds(c * L * SEQ_BLOCK + b, L, stride=SEQ_BLOCK)
            rows.append(r)
            ifv = if_ref[r, :] + ifb_ref[...]
            logf = jnp.minimum(ifv, 0.0) - jnp.log1p(jnp.exp(-jnp.abs(ifv)))
            b_col = pick(_cumsum_rows(jnp.where(valid, logf, 0.0)), sel_f)
            i_col = jnp.where(valid, pick(ifv, sel_i), NEG)
            q = q_ref[r, :] * (DH ** -0.5)
            k = k_ref[r, :]
            v = v_ref[r, :]
            m_prev = m_ref[pl.ds(b, 1), 0:1]
            b_last = b_col[L - 1:L, :]
            g_col = b_last - b_col + i_col
            m_new = jnp.maximum(b_last + m_prev, jnp.max(g_col, axis=0, keepdims=True))
            kw = k * jnp.exp(g_col - m_new)
            st.append(dict(b_col=b_col, i_col=i_col, q=q, v=v, kw=kw, m_prev=m_prev, m_new=m_new,
                           sc=jnp.exp(b_last + m_prev - m_new)))
        for b in seqs:
            s_ = st[b]
            cmat = c_ref[b, 0]
            s_['qk'] = _dot_nt(s_['q'], k_ref[rows[b], :])
            s_['qc'] = _dot(s_['q'], cmat)
            c_ref[b, 0] = s_['sc'] * cmat + _dot_tn(s_['kw'], s_['v'])
        for b in seqs:
            s_ = st[b]
            srow = pl.ds(b, 1)
            nvec = n_ref[srow, :]
            dm = jnp.where(causal, s_['b_col'] + _row_from_col(s_['i_col'] - s_['b_col'], eye), NEG)
            m_inter = s_['b_col'] + s_['m_prev']
            m_t = jnp.maximum(m_inter, jnp.max(dm, axis=1, keepdims=True))
            s = s_['qk'] * jnp.exp(dm - m_t)
            inter = jnp.exp(m_inter - m_t)
            s_['num'] = _dot(s, s_['v']) + inter * s_['qc']
            den = jnp.sum(s, axis=1, keepdims=True) + inter * jnp.sum(s_['q'] * nvec, axis=1, keepdims=True)
            s_['den'] = jnp.maximum(jnp.abs(den), jnp.exp(-m_t))
            n_ref[srow, :] = s_['sc'] * nvec + jnp.sum(s_['kw'], axis=0, keepdims=True)
            m_ref[srow, :] = jnp.broadcast_to(s_['m_new'], (1, LANE))
        for b in seqs:
            s_ = st[b]
            yn = _head_rms(s_['num'] / s_['den'], ng_ref[...])
            y_ref[rows[b], :] = yn * _sigmoid(o_ref[rows[b], :]) * _silu(z_ref[rows[b], :])


def _mlstm(proj, ifb, ng, c0, n0, m0, *, G, nT, TT, L, t_valid):
    R = TT * SEQ_BLOCK
    n = proj.shape[0]
    blk = lambda u: pl.BlockSpec((R, DH), lambda g, h, t, u=u: (g * nT + t, u + h))
    st4 = pl.BlockSpec((SEQ_BLOCK, 1, DH, DH), lambda g, h, t: (g, h, 0, 0))
    st2 = pl.BlockSpec((SEQ_BLOCK, LANE), lambda g, h, t: (g * HEADS + h, 0))
    kern = functools.partial(_mlstm_kernel, L=L, TT=TT, t_valid=t_valid)
    return pl.pallas_call(
        kern,
        grid=(G, HEADS, nT),
        in_specs=[blk(U_ML), blk(U_ML + 4), blk(U_ML + 8), blk(U_ML + 12), blk(U_ML + 16),
                  pl.BlockSpec((R, LANE), lambda g, h, t: (g * nT + t, U_ML_IF)),
                  pl.BlockSpec((1, LANE), lambda g, h, t: (0, 0)),
                  pl.BlockSpec((1, DH), lambda g, h, t: (0, h)),
                  st4, st2, st2],
        out_specs=[pl.BlockSpec((R, DH), lambda g, h, t: (g * nT + t, h)), st4, st2, st2],
        out_shape=[jax.ShapeDtypeStruct((n, BR), F32),
                   jax.ShapeDtypeStruct(c0.shape, F32),
                   jax.ShapeDtypeStruct(n0.shape, F32),
                   jax.ShapeDtypeStruct(m0.shape, F32)],
        compiler_params=_cparams(("arbitrary", "arbitrary", "arbitrary")),
        name="mlstm",
    )(proj, proj, proj, proj, proj, proj, ifb, ng, c0, n0, m0)


def _causal_conv_tile(x_ref, w_ref, b_ref, xp_ref, act_ref, cv0_ref, cv_ref, *, ti, R, width, tv_local,
                      last_tile, act_fn):
    @pl.when(ti == 0)
    def _():
        xp_ref[0:CARRY_ROWS, :] = cv0_ref[...]

    xp_ref[CARRY_ROWS:CARRY_ROWS + R, :] = x_ref[...]
    RB = 32 if R % 32 == 0 else SEQ_BLOCK

    def blk(i, carry):
        r0 = pl.multiple_of(i * RB, SEQ_BLOCK)
        cw = min(width, BR)
        for c0 in range(0, width, cw):
            cs = pl.ds(c0, cw)
            acc = xp_ref[pl.ds(r0, RB), cs] * w_ref[0:1, cs]
            for j in range(1, CONV_W):
                acc = acc + xp_ref[pl.ds(r0 + j * SEQ_BLOCK, RB), cs] * w_ref[j:j + 1, cs]
            if b_ref is not None:
                acc = acc + b_ref[:, cs]
            act_ref[pl.ds(r0, RB), cs] = act_fn(acc)
        return carry

    lax.fori_loop(0, R // RB, blk, 0)

    @pl.when(ti == last_tile)
    def _():
        cv_ref[...] = xp_ref[tv_local * SEQ_BLOCK:tv_local * SEQ_BLOCK + CARRY_ROWS, :]

    xp_ref[0:CARRY_ROWS, :] = xp_ref[R:R + CARRY_ROWS, :]


def _gdn_kernel(q_ref, k_ref, v_ref, z_ref, ab_ref, cwq_ref, cwk_ref, cwv_ref, alog_ref, dtb_ref, ng_ref,
                s0_ref, cvq0_ref, cvk0_ref, cvv0_ref,
                y_ref, s_ref, cvq_ref, cvk_ref, cvv_ref,
                xpq_ref, xpk_ref, xpv_ref, aq_ref, ak_ref, av_ref, *, L, TT, t_valid, nT):
    h = pl.program_id(1)
    ti = pl.program_id(2)
    R = TT * SEQ_BLOCK

    @pl.when(ti == 0)
    def _():
        s_ref[...] = s0_ref[...]

    for x_ref, cw_ref, xp_ref, a_ref, cv0_ref, cv_ref in (
            (q_ref, cwq_ref, xpq_ref, aq_ref, cvq0_ref, cvq_ref),
            (k_ref, cwk_ref, xpk_ref, ak_ref, cvk0_ref, cvk_ref),
            (v_ref, cwv_ref, xpv_ref, av_ref, cvv0_ref, cvv_ref)):
        _causal_conv_tile(x_ref, cw_ref, None, xp_ref, a_ref, cv0_ref, cv_ref, ti=ti, R=R, width=DH,
                          tv_local=t_valid - (nT - 1) * TT, last_tile=nT - 1, act_fn=_silu)

    ri = lax.broadcasted_iota(jnp.int32, (L, L), 0)
    ci = lax.broadcasted_iota(jnp.int32, (L, L), 1)
    causal = ci <= ri
    strict = ci < ri
    eye = ci == ri
    lane = lax.broadcasted_iota(jnp.int32, (L, LANE), 1)
    sel_a = lane == h
    sel_b = lane == HEADS + h
    tok_col = lax.broadcasted_iota(jnp.int32, (L, 1), 0)
    pick = lambda tile, sel: jnp.sum(jnp.where(sel, tile, 0.0), axis=1, keepdims=True)
    n_double = int(math.log2(L)) - 1

    seqs = range(SEQ_BLOCK)
    for c in range(TT // L):
        valid = (ti * TT + c * L + tok_col) < t_valid
        rows, st = [], []
        for b in seqs:
            r = pl.ds(c * L * SEQ_BLOCK + b, L, stride=SEQ_BLOCK)
            rows.append(r)
            abv = ab_ref[r, :]
            g_all = -jnp.exp(alog_ref[...]) * _softplus(abv + dtb_ref[...])
            gam = pick(_cumsum_rows(jnp.where(valid, g_all, 0.0)), sel_a)
            beta = jnp.where(valid, pick(_sigmoid(abv), sel_b), 0.0)
            dec = jnp.where(causal, jnp.exp(gam - _row_from_col(gam, eye)), 0.0)
            q = aq_ref[r, :]
            k = ak_ref[r, :]
            q = q * lax.rsqrt(jnp.sum(q * q, axis=-1, keepdims=True) + EPS) * (DH ** -0.5)
            k = k * lax.rsqrt(jnp.sum(k * k, axis=-1, keepdims=True) + EPS)
            eg = jnp.exp(gam)
            g_last = gam[L - 1:L, :]
            kbeta = k * beta
            st.append(dict(dec=dec, q=q, k=k, kbeta=kbeta, eg=eg, g_last=g_last,
                           kd=k * jnp.exp(g_last - gam),
                           rhs=jnp.concatenate([av_ref[r, :] * beta, kbeta * eg], axis=1)))
        for b in seqs:
            s_ = st[b]
            amat = jnp.where(strict, _dot_nt(s_['kbeta'], s_['k']) * s_['dec'], 0.0)
            s_['qk'] = _dot_nt(s_['q'], s_['k']) * s_['dec']
            s_['o'] = _dot(s_['q'] * s_['eg'], s_ref[b, 0])
            s_['x'] = -amat
            s_['p'] = amat
        for b in seqs:
            st[b]['p'] = _dot(st[b]['p'], st[b]['p'])
        for i in range(n_double):
            for b in seqs:
                s_ = st[b]
                s_['xp'] = _dot(s_['x'], s_['p'])
                if i + 1 < n_double:
                    s_['p2'] = _dot(s_['p'], s_['p'])
            for b in seqs:
                s_ = st[b]
                s_['x'] = s_['x'] + s_['p'] + s_['xp']
                if i + 1 < n_double:
                    s_['p'] = s_['p2']
        for b in seqs:
            s_ = st[b]
            s_['sol'] = s_['rhs'] + _dot(s_['x'], s_['rhs'])
        for b in seqs:
            s_ = st[b]
            s_['v_new'] = s_['sol'][:, :DH] - _dot(s_['sol'][:, DH:], s_ref[b, 0])
        for b in seqs:
            s_ = st[b]
            o = s_['o'] + _dot(s_['qk'], s_['v_new'])
            s_ref[b, 0] = jnp.exp(s_['g_last']) * s_ref[b, 0] + _dot_tn(s_['kd'], s_['v_new'])
            y_ref[rows[b], :] = _head_rms(o, ng_ref[...]) * _silu(z_ref[rows[b], :])


def _gdn(proj, cw, alog, dtb, ng, s0, cv0, *, G, nT, TT, L, t_valid):
    R = TT * SEQ_BLOCK
    n = proj.shape[0]
    upb = BR // LANE
    blk = lambda u: pl.BlockSpec((R, DH), lambda g, h, t, u=u: (g * nT + t, u + h))
    st4 = pl.BlockSpec((SEQ_BLOCK, 1, DH, DH), lambda g, h, t: (g, h, 0, 0))
    cvs = lambda j: pl.BlockSpec((CARRY_ROWS, DH), lambda g, h, t, j=j: (g, j * upb + h))
    cws = lambda j: pl.BlockSpec((CONV_W, DH), lambda g, h, t, j=j: (0, j * upb + h))
    one = pl.BlockSpec((1, LANE), lambda g, h, t: (0, 0))
    kern = functools.partial(_gdn_kernel, L=L, TT=TT, t_valid=t_valid, nT=nT)
    cv_shape = jax.ShapeDtypeStruct((cv0.shape[0], BR), F32)
    return pl.pallas_call(
        kern,
        grid=(G, HEADS, nT),
        in_specs=[blk(U_GD_QKV), blk(U_GD_QKV + 4), blk(U_GD_QKV + 8), blk(U_GD_Z),
                  pl.BlockSpec((R, LANE), lambda g, h, t: (g * nT + t, U_GD_AB)),
                  cws(0), cws(1), cws(2), one, one,
                  pl.BlockSpec((1, DH), lambda g, h, t: (0, h)),
                  st4, cvs(0), cvs(1), cvs(2)],
        out_specs=[pl.BlockSpec((R, DH), lambda g, h, t: (g * nT + t, h)), st4, cvs(0), cvs(0), cvs(0)],
        out_shape=[jax.ShapeDtypeStruct((n, BR), F32), jax.ShapeDtypeStruct(s0.shape, F32),
                   cv_shape, cv_shape, cv_shape],
        scratch_shapes=[pltpu.VMEM((R + CARRY_ROWS, DH), F32)] * 3 + [pltpu.VMEM((R, DH), F32)] * 3,
        compiler_params=_cparams(("arbitrary", "arbitrary", "arbitrary")),
        name="gdn",
    )(proj, proj, proj, proj, proj, cw, cw, cw, alog, dtb, ng, s0, cv0, cv0, cv0)


def _s5_prep_kernel(are_ref, aim_ref, ldt_ref, bre_ref, bim_ref, abr_ref, abi_ref, bbr_ref, bbi_ref):
    a_re = are_ref[...]
    a_im = aim_ref[...]
    dt = jnp.exp(ldt_ref[...])
    mag = jnp.exp(dt * a_re)
    ang = dt * a_im
    ab_r = mag * jnp.cos(ang)
    ab_i = mag * jnp.sin(ang)
    den = a_re * a_re + a_im * a_im
    nr = ab_r - 1.0
    ni = ab_i
    f_r = (nr * a_re + ni * a_im) / den
    f_i = (ni * a_re - nr * a_im) / den
    abr_ref[...] = ab_r
    abi_ref[...] = ab_i
    f_r = f_r[0:1, :]
    f_i = f_i[0:1, :]
    bbr_ref[...] = f_r * bre_ref[...] - f_i * bim_ref[...]
    bbi_ref[...] = f_r * bim_ref[...] + f_i * bre_ref[...]


def _s5_prep(a_re, a_im, log_dt, b_re, b_im):
    rep = lambda a: jnp.broadcast_to(a.reshape(1, S5_STATE), (SEQ_BLOCK, S5_STATE))
    ldt = rep(jnp.broadcast_to(log_dt[:, None], (S5_G, S5_P)))
    bt = lambda b: jnp.transpose(b, (2, 0, 1)).reshape(S5_N, S5_STATE)
    shp = lambda r: jax.ShapeDtypeStruct((r, S5_STATE), F32)
    abr, abi, bbr, bbi = pl.pallas_call(
        _s5_prep_kernel,
        out_shape=[shp(SEQ_BLOCK), shp(SEQ_BLOCK), shp(S5_N), shp(S5_N)],
        name="s5_prep",
    )(rep(a_re), rep(a_im), ldt, bt(b_re), bt(b_im))
    return abr, abi, bbr, bbi


def _gelu_tanh(x):
    return 0.5 * x * (1.0 + jnp.tanh(math.sqrt(2.0 / math.pi) * (x + 0.044715 * (x * x * x))))


def _s5_kernel(u_ref, z_ref, abr_ref, abi_ref, wbr_ref, wbi_ref, wcr_ref, wci_ref, d_ref, gw_ref, gb_ref,
               x0r_ref, x0i_ref, y_ref, xr_ref, xi_ref, hr_ref, hi_ref, ys_ref, *, TT, n_steps):
    ti = pl.program_id(1)
    NB = BR // LANE
    SB = S5_STATE // NB

    @pl.when(ti == 0)
    def _():
        xr_ref[...] = x0r_ref[...]
        xi_ref[...] = x0i_ref[...]

    for kb in range(NB):
        ub = u_ref[:, kb * LANE:(kb + 1) * LANE].astype(BF16)
        hr_ref[:, kb * SB:(kb + 1) * SB] = jnp.dot(ub, wbr_ref[kb], preferred_element_type=F32)
        hi_ref[:, kb * SB:(kb + 1) * SB] = jnp.dot(ub, wbi_ref[kb], preferred_element_type=F32)

    for kb in range(NB):
        sl = pl.ds(kb * SB, SB)
        ar = abr_ref[:, sl]
        ai = abi_ref[:, sl]

        def step(t, carry):
            xr, xi = carry
            rows = pl.ds(pl.multiple_of(t * SEQ_BLOCK, SEQ_BLOCK), SEQ_BLOCK)
            nxr = ar * xr - ai * xi + hr_ref[rows, sl]
            nxi = ar * xi + ai * xr + hi_ref[rows, sl]
            hr_ref[rows, sl] = nxr
            hi_ref[rows, sl] = nxi
            return nxr, nxi

        xr, xi = lax.fori_loop(0, n_steps, step, (xr_ref[:, sl], xi_ref[:, sl]))
        xr_ref[:, sl] = xr
        xi_ref[:, sl] = xi

    for kb in range(NB):
        sl = pl.ds(kb * SB, SB)
        cs = pl.ds(kb * LANE, LANE)
        yk = (jnp.dot(hr_ref[:, sl].astype(BF16), wcr_ref[kb], preferred_element_type=F32)
              - jnp.dot(hi_ref[:, sl].astype(BF16), wci_ref[kb], preferred_element_type=F32)
              + d_ref[:, cs] * u_ref[:, cs])
        ys_ref[:, cs] = _gelu_tanh(yk)
    ys = ys_ref[...]
    glu = ys * _sigmoid(jnp.dot(ys.astype(BF16), gw_ref[...], preferred_element_type=F32) + gb_ref[...])
    y_ref[...] = glu * _silu(z_ref[...])


def _s5(proj, abr, abi, wbr, wbi, wcr, wci, d, gw, gb, x0r, x0i, *, G, nT, TT, t_valid):
    assert nT == 1 or t_valid == nT * TT
    R = TT * SEQ_BLOCK
    n = proj.shape[0]
    full = lambda a: pl.BlockSpec(a.shape, lambda g, t, nd=a.ndim: (0,) * nd)
    st = pl.BlockSpec((SEQ_BLOCK, S5_STATE), lambda g, t: (g, 0))
    kern = functools.partial(_s5_kernel, TT=TT, n_steps=t_valid - (nT - 1) * TT)
    return pl.pallas_call(
        kern,
        grid=(G, nT),
        in_specs=[pl.BlockSpec((R, BR), lambda g, t: (g * nT + t, U_S5_U // 4)),
                  pl.BlockSpec((R, BR), lambda g, t: (g * nT + t, U_S5_Z // 4)),
                  full(abr), full(abi), full(wbr), full(wbi), full(wcr), full(wci), full(d), full(gw), full(gb),
                  st, st],
        out_specs=[pl.BlockSpec((R, BR), lambda g, t: (g * nT + t, 0)), st, st],
        out_shape=[jax.ShapeDtypeStruct((n, BR), F32),
                   jax.ShapeDtypeStruct(x0r.shape, F32),
                   jax.ShapeDtypeStruct(x0i.shape, F32)],
        scratch_shapes=[pltpu.VMEM((R, S5_STATE), F32), pltpu.VMEM((R, S5_STATE), F32),
                        pltpu.VMEM((R, BR), F32)],
        compiler_params=_cparams(("arbitrary", "arbitrary")),
        name="s5",
    )(proj, proj, abr, abi, wbr, wbi, wcr, wci, d, gw, gb, x0r, x0i)


def _lru_kernel(x_ref, z_ref, cw_ref, cb_ref, wa_ref, ba_ref, wx_ref, bx_ref, lam_ref, h0_ref, cv0_ref,
                y_ref, h_ref, cv_ref, xp_ref, xl_ref, a_ref, *, TT, n_steps, nT):
    ti = pl.program_id(1)
    R = TT * SEQ_BLOCK

    @pl.when(ti == 0)
    def _():
        h_ref[...] = h0_ref[...]

    _causal_conv_tile(x_ref, cw_ref, cb_ref, xp_ref, xl_ref, cv0_ref, cv_ref, ti=ti, R=R, width=BR,
                      tv_local=n_steps, last_tile=nT - 1, act_fn=lambda a: a)

    xl = xl_ref[...]
    xb = xl.astype(BF16)
    r = _sigmoid(jnp.dot(xb, wa_ref[...], preferred_element_type=F32) + ba_ref[...])
    i = _sigmoid(jnp.dot(xb, wx_ref[...], preferred_element_type=F32) + bx_ref[...])
    log_a = -LRU_C * r * _softplus(-lam_ref[...])
    a_ref[...] = jnp.exp(log_a)
    th = jnp.tanh(log_a)
    xl_ref[...] = jnp.sqrt(-2.0 * th / (1.0 - th)) * (i * xl)

    def step(t, h):
        rows = pl.ds(pl.multiple_of(t * SEQ_BLOCK, SEQ_BLOCK), SEQ_BLOCK)
        hn = a_ref[rows, :] * h + xl_ref[rows, :]
        xl_ref[rows, :] = hn
        return hn

    h_ref[...] = lax.fori_loop(0, n_steps, step, h_ref[...])
    y_ref[...] = xl_ref[...] * _silu(z_ref[...])


def _lru(proj, cw, cb, wa, ba, wx, bx, lam, h0, cv0, *, G, nT, TT, t_valid):
    assert nT == 1 or t_valid == nT * TT
    R = TT * SEQ_BLOCK
    n = proj.shape[0]
    full = lambda a: pl.BlockSpec(a.shape, lambda g, t, nd=a.ndim: (0,) * nd)
    st = pl.BlockSpec((SEQ_BLOCK, BR), lambda g, t: (g, 0))
    cvs = pl.BlockSpec((CARRY_ROWS, BR), lambda g, t: (g, 0))
    kern = functools.partial(_lru_kernel, TT=TT, n_steps=t_valid - (nT - 1) * TT, nT=nT)
    return pl.pallas_call(
        kern,
        grid=(G, nT),
        in_specs=[pl.BlockSpec((R, BR), lambda g, t: (g * nT + t, U_LRU_X // 4)),
                  pl.BlockSpec((R, BR), lambda g, t: (g * nT + t, U_LRU_Z // 4)),
                  full(cw), full(cb), full(wa), full(ba), full(wx), full(bx), full(lam), st, cvs],
        out_specs=[pl.BlockSpec((R, BR), lambda g, t: (g * nT + t, 0)), st, cvs],
        out_shape=[jax.ShapeDtypeStruct((n, BR), F32),
                   jax.ShapeDtypeStruct(h0.shape, F32),
                   jax.ShapeDtypeStruct(cv0.shape, F32)],
        scratch_shapes=[pltpu.VMEM((R + CARRY_ROWS, BR), F32), pltpu.VMEM((R, BR), F32),
                        pltpu.VMEM((R, BR), F32)],
        compiler_params=_cparams(("arbitrary", "arbitrary")),
        name="lru",
    )(proj, proj, cw, cb, wa, ba, wx, bx, lam, h0, cv0)


def _merge_kernel(yml_ref, ygd_ref, ys5_ref, ylru_ref, gates_ref, x_ref, p_ref,
                  wbr_ref, wout_ref, wpg_ref, wple_ref, pg_ref, o_ref):
    merged = None
    for nb, y_ref in enumerate((yml_ref, ygd_ref, ys5_ref, ylru_ref)):
        pbr = jnp.dot(y_ref[...].astype(BF16), wbr_ref[nb], preferred_element_type=F32)
        term = _sigmoid(gates_ref[:, nb * D_MODEL:(nb + 1) * D_MODEL]) * pbr
        merged = term if merged is None else merged + term
    out = jnp.dot(merged.astype(BF16), wout_ref[...], preferred_element_type=F32)
    r = x_ref[...] + out * lax.rsqrt(jnp.mean(out * out, axis=-1, keepdims=True) + EPS) * pg_ref[...]
    gate = _sigmoid(jnp.dot(r.astype(BF16), wpg_ref[...], preferred_element_type=F32))
    o_ref[...] = r + gate * jnp.dot(p_ref[...].astype(BF16), wple_ref[...], preferred_element_type=F32)


def _merge(ys, proj, x, p, wbr, wout, wpg, wple, pg, tm):
    n = x.shape[0]
    row = lambda c: pl.BlockSpec((tm, c), lambda i: (i, 0))
    full = lambda a: pl.BlockSpec(a.shape, lambda i, nd=a.ndim: (0,) * nd)
    return pl.pallas_call(
        _merge_kernel,
        grid=(n // tm,),
        in_specs=[row(BR), row(BR), row(BR), row(BR),
                  pl.BlockSpec((tm, 4 * D_MODEL), lambda i: (i, U_GATES // 32)),
                  row(D_MODEL), row(PLE_DIM), full(wbr), full(wout), full(wpg), full(wple), full(pg)],
        out_specs=row(D_MODEL),
        out_shape=jax.ShapeDtypeStruct((n, D_MODEL), F32),
        compiler_params=_cparams(("parallel",)),
        name="merge",
    )(*ys, proj, x, p, wbr, wout, wpg, wple, pg)


def _permute_w_in(w_in):
    sizes = (BR, BR, BR, HEADS, HEADS, BR, BR, 3 * BR, HEADS, HEADS, BR, BR, BR, BR, BR, 4 * D_MODEL)
    offs = [0]
    for s in sizes:
        offs.append(offs[-1] + s)
    seg = lambda i: w_in[:, :, offs[i]:offs[i + 1]]
    (ml_q, ml_k, ml_v, ml_i, ml_f, ml_o, ml_z, gd_qkv, gd_a, gd_b, gd_z,
     s5_u, s5_z, lru_x, lru_z, gates) = [seg(i) for i in range(len(sizes))]
    zeros = lambda c: jnp.zeros(w_in.shape[:2] + (c,), w_in.dtype)
    cols = [gd_qkv, ml_q, ml_k, ml_v, ml_o, ml_z, gates, gd_z, s5_u, s5_z, lru_x, lru_z,
            ml_i, ml_f, zeros(LANE - 2 * HEADS), gd_a, gd_b, zeros(LANE - 2 * HEADS),
            zeros((N_UNITS - U_GD_AB - 1) * LANE)]
    return jnp.concatenate(cols, axis=-1).astype(BF16)


def _pad_lanes(*vecs):
    v = jnp.concatenate(vecs)
    return jnp.pad(v, (0, LANE - v.shape[0])).reshape(1, LANE)


def _block_diag(blocks):
    n, a, b = blocks.shape
    eye = jnp.eye(n, dtype=blocks.dtype)
    return jnp.einsum('ij,iab->iajb', eye, blocks).reshape(n * a, n * b)


def _to_rows(x, G):
    B, T, C = x.shape
    return x.reshape(G, SEQ_BLOCK, T, C).transpose(0, 2, 1, 3).reshape(G * T * SEQ_BLOCK, C)


def _from_rows(y, G, T):
    C = y.shape[-1]
    return y.reshape(G, T, SEQ_BLOCK, C).transpose(0, 2, 1, 3).reshape(G * SEQ_BLOCK, T, C)


def _group(x, p, states, params, *, T, t_valid, TT, L, tm_proj, tm_merge):
    B = x.shape[0]
    G = B // SEQ_BLOCK
    nT = T // TT
    depth = p.shape[0]
    xr = _to_rows(x, G)
    tm_proj = min(tm_proj, xr.shape[0])
    tm_merge = min(tm_merge, xr.shape[0])
    new_states = []
    for li in range(depth):
        lp = {k: v[li] for k, v in params.items()}
        if states is None:
            c0 = jnp.zeros((B, HEADS, DH, DH), F32)
            n0 = jnp.zeros((B * HEADS, LANE), F32)
            m0 = jnp.zeros((B * HEADS, LANE), F32)
            s0 = jnp.zeros((B, HEADS, DH, DH), F32)
            gcv0 = jnp.zeros((G * CARRY_ROWS, 3 * BR), F32)
            x0r = jnp.zeros((B, S5_STATE), F32)
            x0i = jnp.zeros((B, S5_STATE), F32)
            h0 = jnp.zeros((B, BR), F32)
            lcv0 = jnp.zeros((G * CARRY_ROWS, BR), F32)
        else:
            (c0, n0, m0, s0, gcv0, x0r, x0i, h0, lcv0) = [s[li] for s in states]
            n0 = n0.reshape(G, SEQ_BLOCK, HEADS, DH).transpose(0, 2, 1, 3).reshape(B * HEADS, DH)
            m0 = m0.reshape(G, SEQ_BLOCK, HEADS).transpose(0, 2, 1).reshape(B * HEADS, 1)
            m0 = jnp.broadcast_to(m0, (B * HEADS, LANE))
            gcv0 = _to_rows(gcv0, G)
            x0r = x0r.reshape(B, S5_STATE)
            x0i = x0i.reshape(B, S5_STATE)
            lcv0 = _to_rows(lcv0, G)
        pr = _to_rows(p[li], G)

        proj = _in_proj(xr, lp['prenorm_g'], lp['w_in_p'], tm_proj, PROJ_COLS // 4)

        y_ml, c1, n1, m1 = _mlstm(proj, lp['ml_ifb'], lp['ml_norm_g'], c0, n0, m0,
                                  G=G, nT=nT, TT=TT, L=L, t_valid=t_valid)
        y_gd, s1, cvq, cvk, cvv = _gdn(proj, lp['gd_conv_w'], lp['gd_alog'], lp['gd_dtb'], lp['gd_norm_g'], s0,
                                       gcv0, G=G, nT=nT, TT=TT, L=L, t_valid=t_valid)
        gcv1 = jnp.concatenate([cvq, cvk, cvv], axis=-1)
        y_s5, x1r, x1i = _s5(proj, lp['s5_abr'], lp['s5_abi'], lp['s5_wbr'], lp['s5_wbi'], lp['s5_wcr'],
                             lp['s5_wci'], lp['s5_d'], lp['s5_glu_w'], lp['s5_glu_b'], x0r, x0i,
                             G=G, nT=nT, TT=TT, t_valid=t_valid)
        y_lru, h1, lcv1 = _lru(proj, lp['lru_conv_w'], lp['lru_conv_b'], lp['lru_wa'], lp['lru_ba'],
                               lp['lru_wx'], lp['lru_bx'], lp['lru_lam'], h0, lcv0,
                               G=G, nT=nT, TT=TT, t_valid=t_valid)
        xr = _merge((y_ml, y_gd, y_s5, y_lru), proj, xr, pr, lp['w_branch'], lp['w_out'], lp['w_ple_gate'],
                    lp['w_ple'], lp['postnorm_g'], tm_merge)

        n1 = n1.reshape(G, HEADS, SEQ_BLOCK, DH).transpose(0, 2, 1, 3)
        m1 = m1[:, 0].reshape(G, HEADS, SEQ_BLOCK).transpose(0, 2, 1)
        new_states.append((c1, n1.reshape(B, HEADS, DH), m1.reshape(B, HEADS), s1,
                           _from_rows(gcv1, G, CONV_W - 1), x1r.reshape(B, S5_G, S5_P),
                           x1i.reshape(B, S5_G, S5_P), h1, _from_rows(lcv1, G, CONV_W - 1)))
    y = _from_rows(xr, G, T)
    stacked = tuple(jnp.stack([ns[j] for ns in new_states]) for j in range(9))
    return y, stacked


def _prepare_params(prenorm_g, postnorm_g, w_in, ml_bi, ml_bf, ml_norm_g, gd_conv_w, gd_a_log, gd_dt_bias,
                    gd_norm_g, s5_a_re, s5_a_im, s5_log_dt, s5_b_re, s5_b_im, s5_c_re, s5_c_im, s5_d,
                    s5_glu_w, s5_glu_b, lru_conv_w, lru_conv_b, lru_wa, lru_ba, lru_wx, lru_bx, lru_lam,
                    w_branch, w_out, w_ple, w_ple_gate):
    depth = w_in.shape[0]
    row = lambda a: a.reshape(depth, 1, -1)
    per_layer = lambda f, *a: jnp.stack([f(*[x[i] for x in a]) for i in range(depth)])
    prm = dict(
        prenorm_g=row(prenorm_g), postnorm_g=row(postnorm_g), w_in_p=_permute_w_in(w_in),
        ml_ifb=per_layer(_pad_lanes, ml_bi, ml_bf), ml_norm_g=row(ml_norm_g),
        gd_conv_w=gd_conv_w, gd_norm_g=row(gd_norm_g),
        gd_alog=per_layer(lambda a: _pad_lanes(a), gd_a_log),
        gd_dtb=per_layer(lambda a: _pad_lanes(a), gd_dt_bias),
        s5_d=row(s5_d), s5_glu_w=s5_glu_w.astype(BF16), s5_glu_b=row(s5_glu_b),
        lru_conv_w=lru_conv_w, lru_conv_b=row(lru_conv_b),
        lru_wa=per_layer(_block_diag, lru_wa).astype(BF16), lru_ba=row(lru_ba),
        lru_wx=per_layer(_block_diag, lru_wx).astype(BF16), lru_bx=row(lru_bx), lru_lam=row(lru_lam),
        w_branch=w_branch.astype(BF16), w_out=w_out.astype(BF16), w_ple=w_ple.astype(BF16),
        w_ple_gate=w_ple_gate.astype(BF16),
    )
    abr, abi, wbr, wbi, wcr, wci = [], [], [], [], [], []
    nb = BR // LANE
    gpb = S5_G // nb
    for i in range(depth):
        a_r, a_i, bbr, bbi = _s5_prep(s5_a_re[i], s5_a_im[i], s5_log_dt[i], s5_b_re[i], s5_b_im[i])
        abr.append(a_r)
        abi.append(a_i)
        in_blocks = lambda bb: jnp.stack([_block_diag(bb.reshape(S5_N, S5_G, S5_P).transpose(1, 0, 2)
                                                      [k * gpb:(k + 1) * gpb]) for k in range(nb)])
        out_blocks = lambda c: jnp.stack([_block_diag(jnp.transpose(c, (0, 2, 1))[k * gpb:(k + 1) * gpb])
                                          for k in range(nb)])
        wbr.append(in_blocks(bbr).astype(BF16))
        wbi.append(in_blocks(bbi).astype(BF16))
        wcr.append(out_blocks(s5_c_re[i]).astype(BF16))
        wci.append(out_blocks(s5_c_im[i]).astype(BF16))
    prm.update(s5_abr=jnp.stack(abr), s5_abi=jnp.stack(abi), s5_wbr=jnp.stack(wbr), s5_wbi=jnp.stack(wbi),
               s5_wcr=jnp.stack(wcr), s5_wci=jnp.stack(wci))
    return prm


def kernel(x_prompt, x_sample, state_mlstm_c, state_mlstm_n, state_mlstm_m, state_gdn_s, state_gdn_conv, state_s5_re, state_s5_im, state_lru_h, state_lru_conv, p_prompt, p_sample, prenorm_g, postnorm_g, w_in, ml_bi, ml_bf, ml_norm_g, gd_conv_w, gd_a_log, gd_dt_bias, gd_norm_g, s5_a_re, s5_a_im, s5_log_dt, s5_b_re, s5_b_im, s5_c_re, s5_c_im, s5_d, s5_glu_w, s5_glu_b, lru_conv_w, lru_conv_b, lru_wa, lru_ba, lru_wx, lru_bx, lru_lam, w_branch, w_out, w_ple, w_ple_gate):
    prm = _prepare_params(prenorm_g, postnorm_g, w_in, ml_bi, ml_bf, ml_norm_g, gd_conv_w, gd_a_log,
                          gd_dt_bias, gd_norm_g, s5_a_re, s5_a_im, s5_log_dt, s5_b_re, s5_b_im, s5_c_re,
                          s5_c_im, s5_d, s5_glu_w, s5_glu_b, lru_conv_w, lru_conv_b, lru_wa, lru_ba, lru_wx,
                          lru_bx, lru_lam, w_branch, w_out, w_ple, w_ple_gate)

    t_p = x_prompt.shape[1]
    tt_p = math.gcd(t_p, 64)
    y_prompt, pr = _group(x_prompt, p_prompt, None, prm, T=t_p, t_valid=t_p, TT=tt_p, L=tt_p,
                          tm_proj=min(512, t_p * SEQ_BLOCK), tm_merge=min(256, t_p * SEQ_BLOCK))

    t_s = x_sample.shape[1]
    t_pad = -(-t_s // SEQ_BLOCK) * SEQ_BLOCK
    pad_t = lambda a, ax: jnp.pad(a, [(0, t_pad - t_s) if i == ax else (0, 0) for i in range(a.ndim)])
    sample_states = (state_mlstm_c, state_mlstm_n, state_mlstm_m, state_gdn_s, state_gdn_conv,
                     state_s5_re, state_s5_im, state_lru_h, state_lru_conv)
    y_s, sa = _group(pad_t(x_sample, 1), pad_t(p_sample, 2), sample_states, prm, T=t_pad, t_valid=t_s,
                     TT=t_pad, L=t_pad, tm_proj=512, tm_merge=256)
    y_sample = y_s[:, :t_s]
    return (y_prompt, y_sample) + pr + sa
```

```python
import functools
import math

import jax
import jax.numpy as jnp
from jax import lax
from jax.experimental import pallas as pl
from jax.experimental.pallas import tpu as pltpu

F32 = jnp.float32
BF16 = jnp.bfloat16

D_MODEL = 1024
BR = 512
HEADS = 4
DH = 128
CONV_W = 4
S5_G = 32
S5_N = 16
S5_P = 64
S5_STATE = S5_G * S5_P
LRU_BLOCKS = 8
LRU_BD = 64
LRU_C = 8.0
PLE_DIM = 256
EPS = 1e-6
NEG = -1e30

LANE = 128
SEQ_BLOCK = 8
CARRY_ROWS = (CONV_W - 1) * SEQ_BLOCK
PROMPT_CHUNK = 128
GDN_CHUNK = 64

U_GD_QKV = 0
U_ML = 12
U_GATES = 32
U_GD_Z = 64
U_S5_U = 68
U_S5_Z = 72
U_LRU_X = 76
U_LRU_Z = 80
U_ML_IF = 84
U_GD_AB = 85
N_UNITS = 88
PROJ_COLS = N_UNITS * LANE

VMEM_LIMIT = 56 * 1024 * 1024


def _cparams(sem):
    return pltpu.CompilerParams(dimension_semantics=sem, vmem_limit_bytes=VMEM_LIMIT)


def _sigmoid(x):
    return 1.0 / (1.0 + jnp.exp(-x))


def _silu(x):
    return x * _sigmoid(x)


def _softplus(x):
    return jnp.maximum(x, 0.0) + jnp.log1p(jnp.exp(-jnp.abs(x)))


def _dot(a, b):
    return jnp.dot(a.astype(BF16), b.astype(BF16), preferred_element_type=F32)


def _dot_nt(a, b):
    return lax.dot_general(a.astype(BF16), b.astype(BF16), (((1,), (1,)), ((), ())),
                           preferred_element_type=F32)


def _dot_tn(a, b):
    return lax.dot_general(a.astype(BF16), b.astype(BF16), (((0,), (0,)), ((), ())),
                           preferred_element_type=F32)


def _cumsum_rows(x):
    n = x.shape[0]
    ri = lax.broadcasted_iota(jnp.int32, x.shape, 0)
    s = 1
    while s < n:
        x = x + jnp.where(ri >= s, pltpu.roll(x, s, axis=0), 0.0)
        s *= 2
    return x


def _row_from_col(col, eye):
    return jnp.sum(jnp.where(eye, col, 0.0), axis=0, keepdims=True)


def _head_rms(h, g):
    return h * lax.rsqrt(jnp.mean(h * h, axis=-1, keepdims=True) + EPS) * g


def _in_proj_kernel(x_ref, g_ref, w_ref, o_ref):
    x = x_ref[...]
    h = x * lax.rsqrt(jnp.mean(x * x, axis=-1, keepdims=True) + EPS) * g_ref[...]
    o_ref[...] = jnp.dot(h.astype(BF16), w_ref[...], preferred_element_type=F32)


def _in_proj(x, g, w_all, li, tm, tn):
    n = x.shape[0]
    return pl.pallas_call(
        _in_proj_kernel,
        grid=(PROJ_COLS // tn, n // tm),
        in_specs=[pl.BlockSpec((tm, D_MODEL), lambda j, i: (i, 0)),
                  pl.BlockSpec((1, D_MODEL), lambda j, i: (0, 0)),
                  pl.BlockSpec((None, D_MODEL, tn), lambda j, i: (li, 0, j))],
        out_specs=pl.BlockSpec((tm, tn), lambda j, i: (i, j)),
        out_shape=jax.ShapeDtypeStruct((n, PROJ_COLS), F32),
        compiler_params=_cparams(("parallel", "parallel")),
        name="in_proj",
    )(x, g, w_all)


def _token_cumsum(x, tok_in_chunk, L):
    s = 1
    while s < L:
        x = x + jnp.where(tok_in_chunk >= s, pltpu.roll(x, s * SEQ_BLOCK, axis=0), 0.0)
        s *= 2
    return x


def _tile_token_ids(ti, TT, L):
    R = TT * SEQ_BLOCK
    t_local = lax.broadcasted_iota(jnp.int32, (R, LANE), 0) // SEQ_BLOCK
    return ti * TT + t_local, t_local % L


def _token_scan(x, op, fill):
    s = 1
    while s < x.shape[0]:
        shifted = jnp.concatenate([jnp.full((s,) + x.shape[1:], fill, x.dtype), x[:-s]], axis=0)
        x = op(x, shifted)
        s *= 2
    return x


ML_R, ML_M, ML_INTER, ML_ENEG, ML_WS = (j * HEADS for j in range(5))


def _mlstm_gate_tile(if_ref, ifb_ref, m_ref, sc_ref, gt_ref, *, ti, T, t_valid):
    shape = (T, SEQ_BLOCK, LANE)
    lane = lax.broadcasted_iota(jnp.int32, shape, 2)
    valid = (ti * T + lax.broadcasted_iota(jnp.int32, shape, 0)) < t_valid
    rot = lambda x, k: pltpu.roll(x.reshape(T * SEQ_BLOCK, LANE), k, axis=1).reshape(shape)
    ifv = (if_ref[...] + ifb_ref[...]).reshape(shape)
    logf = jnp.minimum(ifv, 0.0) - jnp.log1p(jnp.exp(-jnp.abs(ifv)))
    bcum = rot(_token_scan(jnp.where(valid, logf, 0.0), jnp.add, 0.0), LANE - HEADS)
    ig = jnp.where(valid, ifv, NEG)
    r = ig - bcum
    m_prev = m_ref[...]
    big_m = jnp.maximum(m_prev[None], _token_scan(r, jnp.maximum, NEG))
    inter = jnp.exp(m_prev[None] - big_m)
    eneg = jnp.exp(-(bcum + big_m))
    b_last = bcum[T - 1]
    g = b_last[None] - bcum + ig
    m_new = jnp.maximum(b_last + m_prev, jnp.max(g, axis=0))
    ws = jnp.exp(g - m_new[None])
    head_lanes = lane[0] < HEADS
    sc_ref[...] = jnp.where(head_lanes, jnp.exp(b_last + m_prev - m_new), 0.0)
    m_ref[...] = jnp.where(head_lanes, m_new, 0.0)
    packed = jnp.where(lane < ML_M, r,
                       jnp.where(lane < ML_INTER, rot(big_m, ML_M),
                                 jnp.where(lane < ML_ENEG, rot(inter, ML_INTER),
                                           jnp.where(lane < ML_WS, rot(eneg, ML_ENEG), rot(ws, ML_WS)))))
    gt_ref[...] = packed.reshape(T * SEQ_BLOCK, LANE)


def _mlstm_kernel(*refs, L, t_valid, has_init, has_prev):
    q_ref, k_ref, v_ref, o_ref, z_ref, if_ref, ifb_ref, ng_ref = refs[:8]
    n_in = 8 + (3 if has_init else 0) + (1 if has_prev else 0)
    y_ref, c_ref, n_ref, m_ref, gt_ref, sc_ref, rt_ref = refs[n_in:]
    ti = pl.program_id(1)
    h = pl.program_id(2)
    seqs = range(SEQ_BLOCK)
    rows = [pl.ds(b, L, stride=SEQ_BLOCK) for b in seqs]
    use_transpose = L == LANE

    @pl.when((ti == 0) & (h == 0))
    def _():
        if has_init:
            c_ref[...] = refs[8][...]
            n_ref[...] = refs[9][...]
            m_ref[...] = refs[10][...]
        else:
            c_ref[...] = jnp.zeros(c_ref.shape, F32)
            n_ref[...] = jnp.zeros(n_ref.shape, F32)
            m_ref[...] = jnp.zeros(m_ref.shape, F32)

    @pl.when(h == 0)
    def _():
        _mlstm_gate_tile(if_ref, ifb_ref, m_ref, sc_ref, gt_ref, ti=ti, T=L, t_valid=t_valid)
        if use_transpose:
            for b in seqs:
                rt_ref[b * SEQ_BLOCK:(b + 1) * SEQ_BLOCK, :] = gt_ref[rows[b], :].T[:SEQ_BLOCK, :]

    ri = lax.broadcasted_iota(jnp.int32, (L, L), 0)
    ci = lax.broadcasted_iota(jnp.int32, (L, L), 1)
    causal = ci <= ri
    eye = ci == ri
    to_lane0 = jnp.where(h == 0, 0, LANE - h)
    sc_all = pltpu.roll(sc_ref[...], to_lane0, axis=1)
    col = lambda tile, off: tile[:, off:off + 1]

    st = []
    for b in seqs:
        gq = pltpu.roll(gt_ref[rows[b], :], to_lane0, axis=1)
        q = q_ref[rows[b], :] * (DH ** -0.5)
        v = v_ref[rows[b], :]
        kw = k_ref[rows[b], :] * col(gq, ML_WS)
        cmat = c_ref[b, h]
        sc = sc_all[b:b + 1, 0:1]
        st.append(dict(gq=gq, q=q, v=v, kw=kw, sc=sc, qk=_dot_nt(q, k_ref[rows[b], :]), qc=_dot(q, cmat)))
        c_ref[b, h] = sc * cmat + _dot_tn(kw, v)
    for b in seqs:
        s_ = st[b]
        if use_transpose:
            r_row = rt_ref[pl.ds(b * SEQ_BLOCK + h, 1), :]
        else:
            r_row = _row_from_col(col(s_['gq'], ML_R), eye)
        s_['s'] = s_['qk'] * jnp.where(causal, jnp.exp(r_row - col(s_['gq'], ML_M)), 0.0)
    for b in seqs:
        s_ = st[b]
        srow = pl.ds(h * SEQ_BLOCK + b, 1)
        nvec = n_ref[srow, :]
        inter = col(s_['gq'], ML_INTER)
        s_['num'] = _dot(s_['s'], s_['v']) + inter * s_['qc']
        den = (jnp.sum(s_['s'], axis=1, keepdims=True)
               + inter * jnp.sum(s_['q'] * nvec, axis=1, keepdims=True))
        s_['den'] = jnp.maximum(jnp.abs(den), col(s_['gq'], ML_ENEG))
        n_ref[srow, :] = s_['sc'] * nvec + jnp.sum(s_['kw'], axis=0, keepdims=True)
    for b in seqs:
        s_ = st[b]
        yn = _head_rms(s_['num'] / s_['den'], ng_ref[...])
        y_ref[rows[b], :] = yn * _sigmoid(o_ref[rows[b], :]) * _silu(z_ref[rows[b], :])


def _state_specs(li, n_layers, B):
    spec = pl.BlockSpec((None, SEQ_BLOCK, HEADS, DH, DH), lambda g, t, h: (li, g, 0, 0, 0))
    shape = jax.ShapeDtypeStruct((n_layers, B, HEADS, DH, DH), F32)
    return spec, shape


def _mlstm(proj, ifb, ng, init, prev_c, *, li, n_layers, B, G, nT, TT, L, t_valid):
    assert TT == L
    R = TT * SEQ_BLOCK
    n = proj.shape[0]
    blk = lambda u: pl.BlockSpec((R, DH), lambda g, t, h, u=u: (g * nT + t, u + h))
    st4, c_shape = _state_specs(li, n_layers, B)
    n_spec = pl.BlockSpec((SEQ_BLOCK * HEADS, LANE), lambda g, t, h: (g, 0))
    m_spec = pl.BlockSpec((SEQ_BLOCK, LANE), lambda g, t, h: (g, 0))
    in_specs = [blk(U_ML), blk(U_ML + 4), blk(U_ML + 8), blk(U_ML + 12), blk(U_ML + 16),
                pl.BlockSpec((R, LANE), lambda g, t, h: (g * nT + t, U_ML_IF)),
                pl.BlockSpec((1, LANE), lambda g, t, h: (0, 0)),
                pl.BlockSpec((1, DH), lambda g, t, h: (0, h))]
    args = [proj] * 6 + [ifb, ng]
    if init is not None:
        in_specs += [st4, n_spec, m_spec]
        args += list(init)
    aliases = {}
    if prev_c is not None:
        aliases = {len(args): 1}
        in_specs.append(pl.BlockSpec(memory_space=pl.ANY))
        args.append(prev_c)
    kern = functools.partial(_mlstm_kernel, L=L, t_valid=t_valid, has_init=init is not None,
                             has_prev=prev_c is not None)
    return pl.pallas_call(
        kern,
        grid=(G, nT, HEADS),
        in_specs=in_specs,
        out_specs=[pl.BlockSpec((R, DH), lambda g, t, h: (g * nT + t, h)), st4, n_spec, m_spec],
        out_shape=[jax.ShapeDtypeStruct((n, BR), F32), c_shape,
                   jax.ShapeDtypeStruct((B * HEADS, LANE), F32), jax.ShapeDtypeStruct((B, LANE), F32)],
        scratch_shapes=[pltpu.VMEM((R, LANE), F32), pltpu.VMEM((SEQ_BLOCK, LANE), F32),
                        pltpu.VMEM((SEQ_BLOCK * SEQ_BLOCK, LANE), F32)],
        input_output_aliases=aliases,
        compiler_params=_cparams(("arbitrary", "arbitrary", "arbitrary")),
        name="mlstm",
    )(*args)


def _causal_conv_tile(x_ref, w_ref, b_ref, xp_ref, act_ref, cv0_ref, cv_ref, *, ti, R, width, tv_local,
                      act_fn, carry_ref=None):
    @pl.when(ti == 0)
    def _():
        xp_ref[0:CARRY_ROWS, :] = cv0_ref[...]

    if carry_ref is not None:
        @pl.when(ti > 0)
        def _():
            xp_ref[0:CARRY_ROWS, :] = carry_ref[...]

    xp_ref[CARRY_ROWS:CARRY_ROWS + R, :] = x_ref[...]
    RB = 32 if R % 32 == 0 else SEQ_BLOCK

    def blk(i, carry):
        r0 = pl.multiple_of(i * RB, SEQ_BLOCK)
        cw = min(width, BR)
        for c0 in range(0, width, cw):
            cs = pl.ds(c0, cw)
            acc = xp_ref[pl.ds(r0, RB), cs] * w_ref[0:1, cs]
            for j in range(1, CONV_W):
                acc = acc + xp_ref[pl.ds(r0 + j * SEQ_BLOCK, RB), cs] * w_ref[j:j + 1, cs]
            if b_ref is not None:
                acc = acc + b_ref[:, cs]
            act_ref[pl.ds(r0, RB), cs] = act_fn(acc)
        return carry

    lax.fori_loop(0, R // RB, blk, 0)

    cv_ref[...] = xp_ref[tv_local * SEQ_BLOCK:tv_local * SEQ_BLOCK + CARRY_ROWS, :]
    if carry_ref is not None:
        carry_ref[...] = xp_ref[R:R + CARRY_ROWS, :]
    else:
        xp_ref[0:CARRY_ROWS, :] = xp_ref[R:R + CARRY_ROWS, :]


def _gdn_kernel(*refs, L, TT, t_valid, nT, has_init, has_prev):
    (q_ref, k_ref, v_ref, z_ref, ab_ref, cwq_ref, cwk_ref, cwv_ref, alog_ref, dtb_ref, ng_ref,
     cvq0_ref, cvk0_ref, cvv0_ref) = refs[:14]
    n_in = 14 + (1 if has_init else 0) + (1 if has_prev else 0)
    (y_ref, s_ref, cvq_ref, cvk_ref, cvv_ref,
     xpq_ref, xpk_ref, xpv_ref, aq_ref, ak_ref, av_ref, carry_ref, gb_ref) = refs[n_in:]
    ti = pl.program_id(1)
    h = pl.program_id(2)
    R = TT * SEQ_BLOCK

    @pl.when((ti == 0) & (h == 0))
    def _():
        if has_init:
            s_ref[...] = refs[14][...]
        else:
            s_ref[...] = jnp.zeros(s_ref.shape, F32)

    for j, (x_ref, cw_ref, xp_ref, a_ref, cv0_ref, cv_ref) in enumerate((
            (q_ref, cwq_ref, xpq_ref, aq_ref, cvq0_ref, cvq_ref),
            (k_ref, cwk_ref, xpk_ref, ak_ref, cvk0_ref, cvk_ref),
            (v_ref, cwv_ref, xpv_ref, av_ref, cvv0_ref, cvv_ref))):
        cv_head = cv_ref.at[:, pl.ds(pl.multiple_of(h * DH, DH), DH)]
        _causal_conv_tile(x_ref, cw_ref, None, xp_ref, a_ref, cv0_ref, cv_head, ti=ti, R=R, width=DH,
                          tv_local=t_valid - (nT - 1) * TT, act_fn=_silu, carry_ref=carry_ref.at[j, h])

    @pl.when(h == 0)
    def _():
        tok, tok_in_chunk = _tile_token_ids(ti, TT, L)
        valid = tok < t_valid
        abv = ab_ref[...]
        g_all = -jnp.exp(alog_ref[...]) * _softplus(abv + dtb_ref[...])
        gam = _token_cumsum(jnp.where(valid, g_all, 0.0), tok_in_chunk, L)
        lane = lax.broadcasted_iota(jnp.int32, abv.shape, 1)
        gb_ref[...] = jnp.where(lane < HEADS, gam, jnp.where(valid, _sigmoid(abv), 0.0))

    ri = lax.broadcasted_iota(jnp.int32, (L, L), 0)
    ci = lax.broadcasted_iota(jnp.int32, (L, L), 1)
    causal = ci <= ri
    strict = ci < ri
    eye = ci == ri
    lane = lax.broadcasted_iota(jnp.int32, (L, LANE), 1)
    sel_a = lane == h
    sel_b = lane == HEADS + h
    pick = lambda tile, sel: jnp.sum(jnp.where(sel, tile, 0.0), axis=1, keepdims=True)
    n_double = int(math.log2(L)) - 1

    seqs = range(SEQ_BLOCK)
    for c in range(TT // L):
        rows, st = [], []
        for b in seqs:
            r = pl.ds(c * L * SEQ_BLOCK + b, L, stride=SEQ_BLOCK)
            rows.append(r)
            gbv = gb_ref[r, :]
            gam = pick(gbv, sel_a)
            beta = pick(gbv, sel_b)
            dec = jnp.where(causal, jnp.exp(gam - _row_from_col(gam, eye)), 0.0)
            q = aq_ref[r, :]
            k = ak_ref[r, :]
            q = q * lax.rsqrt(jnp.sum(q * q, axis=-1, keepdims=True) + EPS) * (DH ** -0.5)
            k = k * lax.rsqrt(jnp.sum(k * k, axis=-1, keepdims=True) + EPS)
            eg = jnp.exp(gam)
            g_last = gam[L - 1:L, :]
            kbeta = k * beta
            st.append(dict(dec=dec, q=q, k=k, kbeta=kbeta, eg=eg, g_last=g_last,
                           kd=k * jnp.exp(g_last - gam),
                           rhs=jnp.concatenate([av_ref[r, :] * beta, kbeta * eg], axis=1)))
        for b in seqs:
            s_ = st[b]
            amat = jnp.where(strict, _dot_nt(s_['kbeta'], s_['k']) * s_['dec'], 0.0)
            s_['qk'] = _dot_nt(s_['q'], s_['k']) * s_['dec']
            s_['o'] = _dot(s_['q'] * s_['eg'], s_ref[b, h])
            s_['x'] = -amat
            s_['p'] = amat
        for b in seqs:
            st[b]['p'] = _dot(st[b]['p'], st[b]['p'])
        for i in range(n_double):
            for b in seqs:
                s_ = st[b]
                s_['xp'] = _dot(s_['x'], s_['p'])
                if i + 1 < n_double:
                    s_['p2'] = _dot(s_['p'], s_['p'])
            for b in seqs:
                s_ = st[b]
                s_['x'] = s_['x'] + s_['p'] + s_['xp']
                if i + 1 < n_double:
                    s_['p'] = s_['p2']
        for b in seqs:
            s_ = st[b]
            s_['sol'] = s_['rhs'] + _dot(s_['x'], s_['rhs'])
        for b in seqs:
            s_ = st[b]
            s_['v_new'] = s_['sol'][:, :DH] - _dot(s_['sol'][:, DH:], s_ref[b, h])
        for b in seqs:
            s_ = st[b]
            o = s_['o'] + _dot(s_['qk'], s_['v_new'])
            s_ref[b, h] = jnp.exp(s_['g_last']) * s_ref[b, h] + _dot_tn(s_['kd'], s_['v_new'])
            y_ref[rows[b], :] = _head_rms(o, ng_ref[...]) * _silu(z_ref[rows[b], :])


def _gdn(proj, cw, alog, dtb, ng, cv0, s_init, prev_s, *, li, n_layers, B, G, nT, TT, L, t_valid):
    R = TT * SEQ_BLOCK
    n = proj.shape[0]
    upb = BR // LANE
    blk = lambda u: pl.BlockSpec((R, DH), lambda g, t, h, u=u: (g * nT + t, u + h))
    st4, s_shape = _state_specs(li, n_layers, B)
    cvs = lambda j: pl.BlockSpec((CARRY_ROWS, DH), lambda g, t, h, j=j: (g, j * upb + h))
    cws = lambda j: pl.BlockSpec((CONV_W, DH), lambda g, t, h, j=j: (0, j * upb + h))
    one = pl.BlockSpec((1, LANE), lambda g, t, h: (0, 0))
    cvo = pl.BlockSpec((CARRY_ROWS, BR), lambda g, t, h: (g, 0))
    in_specs = [blk(U_GD_QKV), blk(U_GD_QKV + 4), blk(U_GD_QKV + 8), blk(U_GD_Z),
                pl.BlockSpec((R, LANE), lambda g, t, h: (g * nT + t, U_GD_AB)),
                cws(0), cws(1), cws(2), one, one,
                pl.BlockSpec((1, DH), lambda g, t, h: (0, h)),
                cvs(0), cvs(1), cvs(2)]
    args = [proj] * 5 + [cw, cw, cw, alog, dtb, ng, cv0, cv0, cv0]
    if s_init is not None:
        in_specs.append(st4)
        args.append(s_init)
    aliases = {}
    if prev_s is not None:
        aliases = {len(args): 1}
        in_specs.append(pl.BlockSpec(memory_space=pl.ANY))
        args.append(prev_s)
    kern = functools.partial(_gdn_kernel, L=L, TT=TT, t_valid=t_valid, nT=nT, has_init=s_init is not None,
                             has_prev=prev_s is not None)
    cv_shape = jax.ShapeDtypeStruct((cv0.shape[0], BR), F32)
    return pl.pallas_call(
        kern,
        grid=(G, nT, HEADS),
        in_specs=in_specs,
        out_specs=[pl.BlockSpec((R, DH), lambda g, t, h: (g * nT + t, h)), st4, cvo, cvo, cvo],
        out_shape=[jax.ShapeDtypeStruct((n, BR), F32), s_shape, cv_shape, cv_shape, cv_shape],
        scratch_shapes=([pltpu.VMEM((R + CARRY_ROWS, DH), F32)] * 3 + [pltpu.VMEM((R, DH), F32)] * 3
                        + [pltpu.VMEM((3, HEADS, CARRY_ROWS, DH), F32), pltpu.VMEM((R, LANE), F32)]),
        input_output_aliases=aliases,
        compiler_params=_cparams(("arbitrary", "arbitrary", "arbitrary")),
        name="gdn",
    )(*args)


def _s5_prep_kernel(are_ref, aim_ref, ldt_ref, bre_ref, bim_ref, abr_ref, abi_ref, bbr_ref, bbi_ref):
    a_re = are_ref[...]
    a_im = aim_ref[...]
    dt = jnp.exp(ldt_ref[...])
    mag = jnp.exp(dt * a_re)
    ang = dt * a_im
    ab_r = mag * jnp.cos(ang)
    ab_i = mag * jnp.sin(ang)
    den = a_re * a_re + a_im * a_im
    nr = ab_r - 1.0
    ni = ab_i
    f_r = (nr * a_re + ni * a_im) / den
    f_i = (ni * a_re - nr * a_im) / den
    abr_ref[...] = ab_r
    abi_ref[...] = ab_i
    f_r = f_r[0:1, :]
    f_i = f_i[0:1, :]
    bbr_ref[...] = f_r * bre_ref[...] - f_i * bim_ref[...]
    bbi_ref[...] = f_r * bim_ref[...] + f_i * bre_ref[...]


def _s5_prep(a_re, a_im, log_dt, b_re, b_im):
    rep = lambda a: jnp.broadcast_to(a.reshape(1, S5_STATE), (SEQ_BLOCK, S5_STATE))
    ldt = rep(jnp.broadcast_to(log_dt[:, None], (S5_G, S5_P)))
    bt = lambda b: jnp.transpose(b, (2, 0, 1)).reshape(S5_N, S5_STATE)
    shp = lambda r: jax.ShapeDtypeStruct((r, S5_STATE), F32)
    abr, abi, bbr, bbi = pl.pallas_call(
        _s5_prep_kernel,
        out_shape=[shp(SEQ_BLOCK), shp(SEQ_BLOCK), shp(S5_N), shp(S5_N)],
        name="s5_prep",
    )(rep(a_re), rep(a_im), ldt, bt(b_re), bt(b_im))
    return abr, abi, bbr, bbi


def _gelu_tanh(x):
    return 0.5 * x * (1.0 + jnp.tanh(math.sqrt(2.0 / math.pi) * (x + 0.044715 * (x * x * x))))


def _s5_kernel(u_ref, z_ref, abr_ref, abi_ref, wbr_ref, wbi_ref, wcr_ref, wci_ref, d_ref, gw_ref, gb_ref,
               x0r_ref, x0i_ref, y_ref, xr_ref, xi_ref, hr_ref, hi_ref, ys_ref, *, TT, n_steps):
    ti = pl.program_id(1)
    NB = BR // LANE
    SB = S5_STATE // NB

    @pl.when(ti == 0)
    def _():
        xr_ref[...] = x0r_ref[...]
        xi_ref[...] = x0i_ref[...]

    for kb in range(NB):
        ub = u_ref[:, kb * LANE:(kb + 1) * LANE].astype(BF16)
        hr_ref[:, kb * SB:(kb + 1) * SB] = jnp.dot(ub, wbr_ref[kb], preferred_element_type=F32)
        hi_ref[:, kb * SB:(kb + 1) * SB] = jnp.dot(ub, wbi_ref[kb], preferred_element_type=F32)

    for kb in range(NB):
        sl = pl.ds(kb * SB, SB)
        ar = abr_ref[:, sl]
        ai = abi_ref[:, sl]

        def step(t, carry):
            xr, xi = carry
            rows = pl.ds(pl.multiple_of(t * SEQ_BLOCK, SEQ_BLOCK), SEQ_BLOCK)
            nxr = ar * xr - ai * xi + hr_ref[rows, sl]
            nxi = ar * xi + ai * xr + hi_ref[rows, sl]
            hr_ref[rows, sl] = nxr
            hi_ref[rows, sl] = nxi
            return nxr, nxi

        xr, xi = lax.fori_loop(0, n_steps, step, (xr_ref[:, sl], xi_ref[:, sl]))
        xr_ref[:, sl] = xr
        xi_ref[:, sl] = xi

    for kb in range(NB):
        sl = pl.ds(kb * SB, SB)
        cs = pl.ds(kb * LANE, LANE)
        yk = (jnp.dot(hr_ref[:, sl].astype(BF16), wcr_ref[kb], preferred_element_type=F32)
              - jnp.dot(hi_ref[:, sl].astype(BF16), wci_ref[kb], preferred_element_type=F32)
              + d_ref[:, cs] * u_ref[:, cs])
        ys_ref[:, cs] = _gelu_tanh(yk)
    ys = ys_ref[...]
    glu = ys * _sigmoid(jnp.dot(ys.astype(BF16), gw_ref[...], preferred_element_type=F32) + gb_ref[...])
    y_ref[...] = glu * _silu(z_ref[...])


def _s5(proj, abr, abi, wbr, wbi, wcr, wci, d, gw, gb, x0r, x0i, *, G, nT, TT, t_valid):
    assert nT == 1 or t_valid == nT * TT
    R = TT * SEQ_BLOCK
    n = proj.shape[0]
    full = lambda a: pl.BlockSpec(a.shape, lambda g, t, nd=a.ndim: (0,) * nd)
    st = pl.BlockSpec((SEQ_BLOCK, S5_STATE), lambda g, t: (g, 0))
    kern = functools.partial(_s5_kernel, TT=TT, n_steps=t_valid - (nT - 1) * TT)
    return pl.pallas_call(
        kern,
        grid=(G, nT),
        in_specs=[pl.BlockSpec((R, BR), lambda g, t: (g * nT + t, U_S5_U // 4)),
                  pl.BlockSpec((R, BR), lambda g, t: (g * nT + t, U_S5_Z // 4)),
                  full(abr), full(abi), full(wbr), full(wbi), full(wcr), full(wci), full(d), full(gw), full(gb),
                  st, st],
        out_specs=[pl.BlockSpec((R, BR), lambda g, t: (g * nT + t, 0)), st, st],
        out_shape=[jax.ShapeDtypeStruct((n, BR), F32),
                   jax.ShapeDtypeStruct(x0r.shape, F32),
                   jax.ShapeDtypeStruct(x0i.shape, F32)],
        scratch_shapes=[pltpu.VMEM((R, S5_STATE), F32), pltpu.VMEM((R, S5_STATE), F32),
                        pltpu.VMEM((R, BR), F32)],
        compiler_params=_cparams(("arbitrary", "arbitrary")),
        name="s5",
    )(proj, proj, abr, abi, wbr, wbi, wcr, wci, d, gw, gb, x0r, x0i)


def _lru_kernel(x_ref, z_ref, cw_ref, cb_ref, wa_ref, ba_ref, wx_ref, bx_ref, lam_ref, h0_ref, cv0_ref,
                y_ref, h_ref, cv_ref, xp_ref, xl_ref, a_ref, *, TT, n_steps, nT):
    ti = pl.program_id(1)
    R = TT * SEQ_BLOCK

    @pl.when(ti == 0)
    def _():
        h_ref[...] = h0_ref[...]

    _causal_conv_tile(x_ref, cw_ref, cb_ref, xp_ref, xl_ref, cv0_ref, cv_ref, ti=ti, R=R, width=BR,
                      tv_local=n_steps, act_fn=lambda a: a)

    xl = xl_ref[...]
    xb = xl.astype(BF16)
    r = _sigmoid(jnp.dot(xb, wa_ref[...], preferred_element_type=F32) + ba_ref[...])
    i = _sigmoid(jnp.dot(xb, wx_ref[...], preferred_element_type=F32) + bx_ref[...])
    log_a = -LRU_C * r * _softplus(-lam_ref[...])
    a_ref[...] = jnp.exp(log_a)
    th = jnp.tanh(log_a)
    xl_ref[...] = jnp.sqrt(-2.0 * th / (1.0 - th)) * (i * xl)

    def step(t, h):
        rows = pl.ds(pl.multiple_of(t * SEQ_BLOCK, SEQ_BLOCK), SEQ_BLOCK)
        hn = a_ref[rows, :] * h + xl_ref[rows, :]
        xl_ref[rows, :] = hn
        return hn

    h_ref[...] = lax.fori_loop(0, n_steps, step, h_ref[...])
    y_ref[...] = xl_ref[...] * _silu(z_ref[...])


def _lru(proj, cw, cb, wa, ba, wx, bx, lam, h0, cv0, *, G, nT, TT, t_valid):
    assert nT == 1 or t_valid == nT * TT
    R = TT * SEQ_BLOCK
    n = proj.shape[0]
    full = lambda a: pl.BlockSpec(a.shape, lambda g, t, nd=a.ndim: (0,) * nd)
    st = pl.BlockSpec((SEQ_BLOCK, BR), lambda g, t: (g, 0))
    cvs = pl.BlockSpec((CARRY_ROWS, BR), lambda g, t: (g, 0))
    kern = functools.partial(_lru_kernel, TT=TT, n_steps=t_valid - (nT - 1) * TT, nT=nT)
    return pl.pallas_call(
        kern,
        grid=(G, nT),
        in_specs=[pl.BlockSpec((R, BR), lambda g, t: (g * nT + t, U_LRU_X // 4)),
                  pl.BlockSpec((R, BR), lambda g, t: (g * nT + t, U_LRU_Z // 4)),
                  full(cw), full(cb), full(wa), full(ba), full(wx), full(bx), full(lam), st, cvs],
        out_specs=[pl.BlockSpec((R, BR), lambda g, t: (g * nT + t, 0)), st, cvs],
        out_shape=[jax.ShapeDtypeStruct((n, BR), F32),
                   jax.ShapeDtypeStruct(h0.shape, F32),
                   jax.ShapeDtypeStruct(cv0.shape, F32)],
        scratch_shapes=[pltpu.VMEM((R + CARRY_ROWS, BR), F32), pltpu.VMEM((R, BR), F32),
                        pltpu.VMEM((R, BR), F32)],
        compiler_params=_cparams(("arbitrary", "arbitrary")),
        name="lru",
    )(proj, proj, cw, cb, wa, ba, wx, bx, lam, h0, cv0)


def _merge_kernel(yml_ref, ygd_ref, ys5_ref, ylru_ref, gates_ref, x_ref, p_ref,
                  wbr_ref, wout_ref, wpg_ref, wple_ref, pg_ref, o_ref):
    merged = None
    for nb, y_ref in enumerate((yml_ref, ygd_ref, ys5_ref, ylru_ref)):
        pbr = jnp.dot(y_ref[...].astype(BF16), wbr_ref[nb], preferred_element_type=F32)
        term = _sigmoid(gates_ref[:, nb * D_MODEL:(nb + 1) * D_MODEL]) * pbr
        merged = term if merged is None else merged + term
    out = jnp.dot(merged.astype(BF16), wout_ref[...], preferred_element_type=F32)
    r = x_ref[...] + out * lax.rsqrt(jnp.mean(out * out, axis=-1, keepdims=True) + EPS) * pg_ref[...]
    gate = _sigmoid(jnp.dot(r.astype(BF16), wpg_ref[...], preferred_element_type=F32))
    o_ref[...] = r + gate * jnp.dot(p_ref[...].astype(BF16), wple_ref[...], preferred_element_type=F32)


def _merge(ys, proj, x, p, wbr, wout, wpg, wple, pg, tm):
    n = x.shape[0]
    row = lambda c: pl.BlockSpec((tm, c), lambda i: (i, 0))
    full = lambda a: pl.BlockSpec(a.shape, lambda i, nd=a.ndim: (0,) * nd)
    return pl.pallas_call(
        _merge_kernel,
        grid=(n // tm,),
        in_specs=[row(BR), row(BR), row(BR), row(BR),
                  pl.BlockSpec((tm, 4 * D_MODEL), lambda i: (i, U_GATES // 32)),
                  row(D_MODEL), row(PLE_DIM), full(wbr), full(wout), full(wpg), full(wple), full(pg)],
        out_specs=row(D_MODEL),
        out_shape=jax.ShapeDtypeStruct((n, D_MODEL), F32),
        compiler_params=_cparams(("parallel",)),
        name="merge",
    )(*ys, proj, x, p, wbr, wout, wpg, wple, pg)


def _permute_w_in(w_in):
    sizes = (BR, BR, BR, HEADS, HEADS, BR, BR, 3 * BR, HEADS, HEADS, BR, BR, BR, BR, BR, 4 * D_MODEL)
    offs = [0]
    for s in sizes:
        offs.append(offs[-1] + s)
    seg = lambda i: w_in[:, :, offs[i]:offs[i + 1]]
    (ml_q, ml_k, ml_v, ml_i, ml_f, ml_o, ml_z, gd_qkv, gd_a, gd_b, gd_z,
     s5_u, s5_z, lru_x, lru_z, gates) = [seg(i) for i in range(len(sizes))]
    zeros = lambda c: jnp.zeros(w_in.shape[:2] + (c,), w_in.dtype)
    cols = [gd_qkv, ml_q, ml_k, ml_v, ml_o, ml_z, gates, gd_z, s5_u, s5_z, lru_x, lru_z,
            ml_i, ml_f, zeros(LANE - 2 * HEADS), gd_a, gd_b, zeros(LANE - 2 * HEADS),
            zeros((N_UNITS - U_GD_AB - 1) * LANE)]
    return jnp.concatenate(cols, axis=-1).astype(BF16)


def _pad_lanes(*vecs):
    v = jnp.concatenate(vecs)
    return jnp.pad(v, (0, LANE - v.shape[0])).reshape(1, LANE)


def _block_diag(blocks):
    n, a, b = blocks.shape
    eye = jnp.eye(n, dtype=blocks.dtype)
    return jnp.einsum('ij,iab->iajb', eye, blocks).reshape(n * a, n * b)


def _to_rows(x, G):
    B, T, C = x.shape
    return x.reshape(G, SEQ_BLOCK, T, C).transpose(0, 2, 1, 3).reshape(G * T * SEQ_BLOCK, C)


def _from_rows(y, G, T):
    C = y.shape[-1]
    return y.reshape(G, T, SEQ_BLOCK, C).transpose(0, 2, 1, 3).reshape(G * SEQ_BLOCK, T, C)


def _group(x, p, states, params, *, T, t_valid, TT, L, tm_proj, tm_merge):
    B = x.shape[0]
    G = B // SEQ_BLOCK
    nT = T // TT
    depth = p.shape[0]
    xr = _to_rows(x, G)
    tm_proj = min(tm_proj, xr.shape[0])
    tm_merge = min(tm_merge, xr.shape[0])
    new_states = []
    c_all = s_all = None
    mat = dict(n_layers=depth, B=B, G=G, nT=nT, TT=TT, L=L, t_valid=t_valid)
    for li in range(depth):
        lp = {k: v[li] for k, v in params.items() if k != 'w_in_p'}
        if states is None:
            ml_init = s_init = None
            gcv0 = jnp.zeros((G * CARRY_ROWS, 3 * BR), F32)
            x0r = jnp.zeros((B, S5_STATE), F32)
            x0i = jnp.zeros((B, S5_STATE), F32)
            h0 = jnp.zeros((B, BR), F32)
            lcv0 = jnp.zeros((G * CARRY_ROWS, BR), F32)
        else:
            (n0, m0, gcv0, x0r, x0i, h0, lcv0) = [states[j][li] for j in (1, 2, 4, 5, 6, 7, 8)]
            n0 = n0.reshape(G, SEQ_BLOCK, HEADS, DH).transpose(0, 2, 1, 3).reshape(B * HEADS, DH)
            ml_init = (states[0], n0, jnp.pad(m0, ((0, 0), (0, LANE - HEADS))))
            s_init = states[3]
            gcv0 = _to_rows(gcv0, G)
            x0r = x0r.reshape(B, S5_STATE)
            x0i = x0i.reshape(B, S5_STATE)
            lcv0 = _to_rows(lcv0, G)
        pr = _to_rows(p[li], G)

        proj = _in_proj(xr, lp['prenorm_g'], params['w_in_p'], li, tm_proj, PROJ_COLS // 4)

        y_ml, c_all, n1, m1 = _mlstm(proj, lp['ml_ifb'], lp['ml_norm_g'], ml_init, c_all, li=li, **mat)
        y_gd, s_all, cvq, cvk, cvv = _gdn(proj, lp['gd_conv_w'], lp['gd_alog'], lp['gd_dtb'], lp['gd_norm_g'],
                                          gcv0, s_init, s_all, li=li, **dict(mat, L=min(L, GDN_CHUNK)))
        gcv1 = jnp.concatenate([cvq, cvk, cvv], axis=-1)
        y_s5, x1r, x1i = _s5(proj, lp['s5_abr'], lp['s5_abi'], lp['s5_wbr'], lp['s5_wbi'], lp['s5_wcr'],
                             lp['s5_wci'], lp['s5_d'], lp['s5_glu_w'], lp['s5_glu_b'], x0r, x0i,
                             G=G, nT=nT, TT=TT, t_valid=t_valid)
        y_lru, h1, lcv1 = _lru(proj, lp['lru_conv_w'], lp['lru_conv_b'], lp['lru_wa'], lp['lru_ba'],
                               lp['lru_wx'], lp['lru_bx'], lp['lru_lam'], h0, lcv0,
                               G=G, nT=nT, TT=TT, t_valid=t_valid)
        xr = _merge((y_ml, y_gd, y_s5, y_lru), proj, xr, pr, lp['w_branch'], lp['w_out'], lp['w_ple_gate'],
                    lp['w_ple'], lp['postnorm_g'], tm_merge)

        n1 = n1.reshape(G, HEADS, SEQ_BLOCK, DH).transpose(0, 2, 1, 3)
        new_states.append((n1.reshape(B, HEADS, DH), m1[:, :HEADS],
                           _from_rows(gcv1, G, CONV_W - 1), x1r.reshape(B, S5_G, S5_P),
                           x1i.reshape(B, S5_G, S5_P), h1, _from_rows(lcv1, G, CONV_W - 1)))
    y = _from_rows(xr, G, T)
    n_s, m_s, gcv_s, xr_s, xi_s, h_s, lcv_s = (jnp.stack([ns[j] for ns in new_states]) for j in range(7))
    return y, (c_all, n_s, m_s, s_all, gcv_s, xr_s, xi_s, h_s, lcv_s)


def _prepare_params(prenorm_g, postnorm_g, w_in, ml_bi, ml_bf, ml_norm_g, gd_conv_w, gd_a_log, gd_dt_bias,
                    gd_norm_g, s5_a_re, s5_a_im, s5_log_dt, s5_b_re, s5_b_im, s5_c_re, s5_c_im, s5_d,
                    s5_glu_w, s5_glu_b, lru_conv_w, lru_conv_b, lru_wa, lru_ba, lru_wx, lru_bx, lru_lam,
                    w_branch, w_out, w_ple, w_ple_gate):
    depth = w_in.shape[0]
    row = lambda a: a.reshape(depth, 1, -1)
    per_layer = lambda f, *a: jnp.stack([f(*[x[i] for x in a]) for i in range(depth)])
    prm = dict(
        prenorm_g=row(prenorm_g), postnorm_g=row(postnorm_g), w_in_p=_permute_w_in(w_in),
        ml_ifb=per_layer(_pad_lanes, ml_bi, ml_bf), ml_norm_g=row(ml_norm_g),
        gd_conv_w=gd_conv_w, gd_norm_g=row(gd_norm_g),
        gd_alog=per_layer(lambda a: _pad_lanes(a), gd_a_log),
        gd_dtb=per_layer(lambda a: _pad_lanes(a), gd_dt_bias),
        s5_d=row(s5_d), s5_glu_w=s5_glu_w.astype(BF16), s5_glu_b=row(s5_glu_b),
        lru_conv_w=lru_conv_w, lru_conv_b=row(lru_conv_b),
        lru_wa=per_layer(_block_diag, lru_wa).astype(BF16), lru_ba=row(lru_ba),
        lru_wx=per_layer(_block_diag, lru_wx).astype(BF16), lru_bx=row(lru_bx), lru_lam=row(lru_lam),
        w_branch=w_branch.astype(BF16), w_out=w_out.astype(BF16), w_ple=w_ple.astype(BF16),
        w_ple_gate=w_ple_gate.astype(BF16),
    )
    abr, abi, wbr, wbi, wcr, wci = [], [], [], [], [], []
    nb = BR // LANE
    gpb = S5_G // nb
    for i in range(depth):
        a_r, a_i, bbr, bbi = _s5_prep(s5_a_re[i], s5_a_im[i], s5_log_dt[i], s5_b_re[i], s5_b_im[i])
        abr.append(a_r)
        abi.append(a_i)
        in_blocks = lambda bb: jnp.stack([_block_diag(bb.reshape(S5_N, S5_G, S5_P).transpose(1, 0, 2)
                                                      [k * gpb:(k + 1) * gpb]) for k in range(nb)])
        out_blocks = lambda c: jnp.stack([_block_diag(jnp.transpose(c, (0, 2, 1))[k * gpb:(k + 1) * gpb])
                                          for k in range(nb)])
        wbr.append(in_blocks(bbr).astype(BF16))
        wbi.append(in_blocks(bbi).astype(BF16))
        wcr.append(out_blocks(s5_c_re[i]).astype(BF16))
        wci.append(out_blocks(s5_c_im[i]).astype(BF16))
    prm.update(s5_abr=jnp.stack(abr), s5_abi=jnp.stack(abi), s5_wbr=jnp.stack(wbr), s5_wbi=jnp.stack(wbi),
               s5_wcr=jnp.stack(wcr), s5_wci=jnp.stack(wci))
    return prm


def kernel(x_prompt, x_sample, state_mlstm_c, state_mlstm_n, state_mlstm_m, state_gdn_s, state_gdn_conv, state_s5_re, state_s5_im, state_lru_h, state_lru_conv, p_prompt, p_sample, prenorm_g, postnorm_g, w_in, ml_bi, ml_bf, ml_norm_g, gd_conv_w, gd_a_log, gd_dt_bias, gd_norm_g, s5_a_re, s5_a_im, s5_log_dt, s5_b_re, s5_b_im, s5_c_re, s5_c_im, s5_d, s5_glu_w, s5_glu_b, lru_conv_w, lru_conv_b, lru_wa, lru_ba, lru_wx, lru_bx, lru_lam, w_branch, w_out, w_ple, w_ple_gate):
    prm = _prepare_params(prenorm_g, postnorm_g, w_in, ml_bi, ml_bf, ml_norm_g, gd_conv_w, gd_a_log,
                          gd_dt_bias, gd_norm_g, s5_a_re, s5_a_im, s5_log_dt, s5_b_re, s5_b_im, s5_c_re,
                          s5_c_im, s5_d, s5_glu_w, s5_glu_b, lru_conv_w, lru_conv_b, lru_wa, lru_ba, lru_wx,
                          lru_bx, lru_lam, w_branch, w_out, w_ple, w_ple_gate)

    t_p = x_prompt.shape[1]
    tt_p = math.gcd(t_p, PROMPT_CHUNK)
    y_prompt, pr = _group(x_prompt, p_prompt, None, prm, T=t_p, t_valid=t_p, TT=tt_p, L=tt_p,
                          tm_proj=min(512, t_p * SEQ_BLOCK), tm_merge=min(256, t_p * SEQ_BLOCK))

    t_s = x_sample.shape[1]
    t_pad = -(-t_s // SEQ_BLOCK) * SEQ_BLOCK
    pad_t = lambda a, ax: jnp.pad(a, [(0, t_pad - t_s) if i == ax else (0, 0) for i in range(a.ndim)])
    sample_states = (state_mlstm_c, state_mlstm_n, state_mlstm_m, state_gdn_s, state_gdn_conv,
                     state_s5_re, state_s5_im, state_lru_h, state_lru_conv)
    y_s, sa = _group(pad_t(x_sample, 1), pad_t(p_sample, 2), sample_states, prm, T=t_pad, t_valid=t_s,
                     TT=t_pad, L=t_pad, tm_proj=512, tm_merge=256)
    y_sample = y_s[:, :t_s]
    return (y_prompt, y_sample) + pr + sa
```

```python
import functools
import math

import jax
import jax.numpy as jnp
from jax import lax
from jax.experimental import pallas as pl
from jax.experimental.pallas import tpu as pltpu

F32 = jnp.float32
BF16 = jnp.bfloat16

D_MODEL = 1024
BR = 512
HEADS = 4
DH = 128
CONV_W = 4
S5_G = 32
S5_N = 16
S5_P = 64
S5_STATE = S5_G * S5_P
LRU_BLOCKS = 8
LRU_BD = 64
LRU_C = 8.0
PLE_DIM = 256
EPS = 1e-6
NEG = -1e30

LANE = 128
SEQ_BLOCK = 8
CARRY_ROWS = (CONV_W - 1) * SEQ_BLOCK
PROMPT_CHUNK = 128
GDN_CHUNK = 64
SAMPLE_GROUPS_PER_STEP = 2

U_GD_QKV = 0
U_ML = 12
U_GATES = 32
U_GD_Z = 64
U_S5_U = 68
U_S5_Z = 72
U_LRU_X = 76
U_LRU_Z = 80
U_ML_IF = 84
U_GD_AB = 85
N_UNITS = 88
PROJ_COLS = N_UNITS * LANE

VMEM_LIMIT = 56 * 1024 * 1024


def _cparams(sem):
    return pltpu.CompilerParams(dimension_semantics=sem, vmem_limit_bytes=VMEM_LIMIT)


def _sigmoid(x):
    return 1.0 / (1.0 + jnp.exp(-x))


def _silu(x):
    return x * _sigmoid(x)


def _softplus(x):
    return jnp.maximum(x, 0.0) + jnp.log1p(jnp.exp(-jnp.abs(x)))


def _dot(a, b):
    return jnp.dot(a.astype(BF16), b.astype(BF16), preferred_element_type=F32)


def _dot_nt(a, b):
    return lax.dot_general(a.astype(BF16), b.astype(BF16), (((1,), (1,)), ((), ())),
                           preferred_element_type=F32)


def _dot_tn(a, b):
    return lax.dot_general(a.astype(BF16), b.astype(BF16), (((0,), (0,)), ((), ())),
                           preferred_element_type=F32)


def _cumsum_rows(x):
    n = x.shape[0]
    ri = lax.broadcasted_iota(jnp.int32, x.shape, 0)
    s = 1
    while s < n:
        x = x + jnp.where(ri >= s, pltpu.roll(x, s, axis=0), 0.0)
        s *= 2
    return x


def _row_from_col(col, eye):
    return jnp.sum(jnp.where(eye, col, 0.0), axis=0, keepdims=True)


def _head_rms(h, g):
    return h * lax.rsqrt(jnp.mean(h * h, axis=-1, keepdims=True) + EPS) * g


def _in_proj_kernel(x_ref, g_ref, w_ref, o_ref):
    x = x_ref[...]
    h = x * lax.rsqrt(jnp.mean(x * x, axis=-1, keepdims=True) + EPS) * g_ref[...]
    o_ref[...] = jnp.dot(h.astype(BF16), w_ref[...], preferred_element_type=F32)


def _in_proj(x, g, w_all, li, tm, tn):
    n = x.shape[0]
    return pl.pallas_call(
        _in_proj_kernel,
        grid=(PROJ_COLS // tn, n // tm),
        in_specs=[pl.BlockSpec((tm, D_MODEL), lambda j, i: (i, 0)),
                  pl.BlockSpec((1, D_MODEL), lambda j, i: (0, 0)),
                  pl.BlockSpec((None, D_MODEL, tn), lambda j, i: (li, 0, j))],
        out_specs=pl.BlockSpec((tm, tn), lambda j, i: (i, j)),
        out_shape=jax.ShapeDtypeStruct((n, PROJ_COLS), F32),
        compiler_params=_cparams(("parallel", "parallel")),
        name="in_proj",
    )(x, g, w_all)


def _token_cumsum(x, tok_in_chunk, L):
    s = 1
    while s < L:
        x = x + jnp.where(tok_in_chunk >= s, pltpu.roll(x, s * SEQ_BLOCK, axis=0), 0.0)
        s *= 2
    return x


def _tile_token_ids(ti, TT, L):
    R = TT * SEQ_BLOCK
    t_local = lax.broadcasted_iota(jnp.int32, (R, LANE), 0) // SEQ_BLOCK
    return ti * TT + t_local, t_local % L


def _token_scan(x, op, fill):
    s = 1
    while s < x.shape[0]:
        shifted = jnp.concatenate([jnp.full((s,) + x.shape[1:], fill, x.dtype), x[:-s]], axis=0)
        x = op(x, shifted)
        s *= 2
    return x


ML_R, ML_M, ML_INTER, ML_ENEG, ML_WS = (j * HEADS for j in range(5))


def _mlstm_gate_tile(if_ref, ifb_ref, m_ref, sc_ref, gt_ref, *, ti, T, t_valid):
    shape = (T, SEQ_BLOCK, LANE)
    lane = lax.broadcasted_iota(jnp.int32, shape, 2)
    valid = (ti * T + lax.broadcasted_iota(jnp.int32, shape, 0)) < t_valid
    rot = lambda x, k: pltpu.roll(x.reshape(T * SEQ_BLOCK, LANE), k, axis=1).reshape(shape)
    ifv = (if_ref[...] + ifb_ref[...]).reshape(shape)
    logf = jnp.minimum(ifv, 0.0) - jnp.log1p(jnp.exp(-jnp.abs(ifv)))
    bcum = rot(_token_scan(jnp.where(valid, logf, 0.0), jnp.add, 0.0), LANE - HEADS)
    ig = jnp.where(valid, ifv, NEG)
    r = ig - bcum
    m_prev = m_ref[...]
    big_m = jnp.maximum(m_prev[None], _token_scan(r, jnp.maximum, NEG))
    inter = jnp.exp(m_prev[None] - big_m)
    eneg = jnp.exp(-(bcum + big_m))
    b_last = bcum[T - 1]
    g = b_last[None] - bcum + ig
    m_new = jnp.maximum(b_last + m_prev, jnp.max(g, axis=0))
    ws = jnp.exp(g - m_new[None])
    head_lanes = lane[0] < HEADS
    sc_ref[...] = jnp.where(head_lanes, jnp.exp(b_last + m_prev - m_new), 0.0)
    m_ref[...] = jnp.where(head_lanes, m_new, 0.0)
    packed = jnp.where(lane < ML_M, r,
                       jnp.where(lane < ML_INTER, rot(big_m, ML_M),
                                 jnp.where(lane < ML_ENEG, rot(inter, ML_INTER),
                                           jnp.where(lane < ML_WS, rot(eneg, ML_ENEG), rot(ws, ML_WS)))))
    gt_ref[...] = packed.reshape(T * SEQ_BLOCK, LANE)


def _own_layer(state_ref, li, has_prev, first_step):
    if has_prev:
        return state_ref

    @pl.when(first_step)
    def _():
        for other in range(state_ref.shape[0]):
            if other != li:
                state_ref[other] = jnp.zeros(state_ref.shape[1:], F32)

    return state_ref.at[li]


def _mlstm_kernel(*refs, L, gps, t_valid, li, has_init, has_prev, single_tile):
    q_ref, k_ref, v_ref, o_ref, z_ref, if_ref, ifb_ref, ng_ref = refs[:8]
    n_in = 8 + (3 if has_init else 0) + (1 if has_prev else 0)
    y_ref, c_ref, n_ref, m_ref, gt_ref, sc_ref, rt_ref = refs[n_in:]
    ti = pl.program_id(1)
    h = pl.program_id(2)
    R = L * SEQ_BLOCK
    seqs = range(gps * SEQ_BLOCK)
    rows = [pl.ds((j // SEQ_BLOCK) * R + j % SEQ_BLOCK, L, stride=SEQ_BLOCK) for j in seqs]
    n_rows = [pl.ds((j // SEQ_BLOCK) * SEQ_BLOCK * HEADS + h * SEQ_BLOCK + j % SEQ_BLOCK, 1) for j in seqs]
    use_transpose = L == LANE
    c_ref = _own_layer(c_ref, li, has_prev, (ti == 0) & (h == 0))
    c_src = refs[8] if (has_init and single_tile) else c_ref

    @pl.when((ti == 0) & (h == 0))
    def _():
        if has_init:
            if not single_tile:
                c_ref[...] = refs[8][...]
            n_ref[...] = refs[9][...]
            m_ref[...] = refs[10][...]
        else:
            c_ref[...] = jnp.zeros(c_ref.shape, F32)
            n_ref[...] = jnp.zeros(n_ref.shape, F32)
            m_ref[...] = jnp.zeros(m_ref.shape, F32)

    @pl.when(h == 0)
    def _():
        for gi in range(gps):
            grp = pl.ds(gi * SEQ_BLOCK, SEQ_BLOCK)
            tile = pl.ds(gi * R, R)
            _mlstm_gate_tile(if_ref.at[tile], ifb_ref, m_ref.at[grp], sc_ref.at[grp], gt_ref.at[tile],
                             ti=ti, T=L, t_valid=t_valid)
        if use_transpose:
            for b in seqs:
                rt_ref[b * SEQ_BLOCK:(b + 1) * SEQ_BLOCK, :] = gt_ref[rows[b], :].T[:SEQ_BLOCK, :]

    ri = lax.broadcasted_iota(jnp.int32, (L, L), 0)
    ci = lax.broadcasted_iota(jnp.int32, (L, L), 1)
    causal = ci <= ri
    eye = ci == ri
    to_lane0 = jnp.where(h == 0, 0, LANE - h)
    sc_all = pltpu.roll(sc_ref[...], to_lane0, axis=1)
    col = lambda tile, off: tile[:, off:off + 1]

    st = []
    for b in seqs:
        gq = pltpu.roll(gt_ref[rows[b], :], to_lane0, axis=1)
        q = q_ref[rows[b], :] * (DH ** -0.5)
        v = v_ref[rows[b], :]
        kw = k_ref[rows[b], :] * col(gq, ML_WS)
        cmat = c_src[b, h]
        sc = sc_all[b:b + 1, 0:1]
        st.append(dict(gq=gq, q=q, v=v, kw=kw, sc=sc, qk=_dot_nt(q, k_ref[rows[b], :]), qc=_dot(q, cmat)))
        c_ref[b, h] = sc * cmat + _dot_tn(kw, v)
    for b in seqs:
        s_ = st[b]
        if use_transpose:
            r_row = rt_ref[pl.ds(b * SEQ_BLOCK + h, 1), :]
        else:
            r_row = _row_from_col(col(s_['gq'], ML_R), eye)
        s_['s'] = s_['qk'] * jnp.where(causal, jnp.exp(r_row - col(s_['gq'], ML_M)), 0.0)
    for b in seqs:
        s_ = st[b]
        srow = n_rows[b]
        nvec = n_ref[srow, :]
        inter = col(s_['gq'], ML_INTER)
        s_['num'] = _dot(s_['s'], s_['v']) + inter * s_['qc']
        den = (jnp.sum(s_['s'], axis=1, keepdims=True)
               + inter * jnp.sum(s_['q'] * nvec, axis=1, keepdims=True))
        s_['den'] = jnp.maximum(jnp.abs(den), col(s_['gq'], ML_ENEG))
        n_ref[srow, :] = s_['sc'] * nvec + jnp.sum(s_['kw'], axis=0, keepdims=True)
    for b in seqs:
        s_ = st[b]
        yn = _head_rms(s_['num'] / s_['den'], ng_ref[...])
        y_ref[rows[b], :] = yn * _sigmoid(o_ref[rows[b], :]) * _silu(z_ref[rows[b], :])


def _state_specs(li, n_layers, B, has_prev, gps):
    nseq = gps * SEQ_BLOCK
    one = pl.BlockSpec((None, nseq, HEADS, DH, DH), lambda g, t, h: (li, g, 0, 0, 0))
    every = pl.BlockSpec((n_layers, nseq, HEADS, DH, DH), lambda g, t, h: (0, g, 0, 0, 0))
    shape = jax.ShapeDtypeStruct((n_layers, B, HEADS, DH, DH), F32)
    return one, (one if has_prev else every), shape


def _mlstm(proj, ifb, ng, init, prev_c, *, li, n_layers, B, G, nT, TT, L, t_valid, gps):
    assert TT == L and G % gps == 0 and (gps == 1 or nT == 1)
    R = TT * SEQ_BLOCK * gps
    n = proj.shape[0]
    blk = lambda u: pl.BlockSpec((R, DH), lambda g, t, h, u=u: (g * nT + t, u + h))
    st4, c_out_spec, c_shape = _state_specs(li, n_layers, B, prev_c is not None, gps)
    n_spec = pl.BlockSpec((gps * SEQ_BLOCK * HEADS, LANE), lambda g, t, h: (g, 0))
    m_spec = pl.BlockSpec((gps * SEQ_BLOCK, LANE), lambda g, t, h: (g, 0))
    in_specs = [blk(U_ML), blk(U_ML + 4), blk(U_ML + 8), blk(U_ML + 12), blk(U_ML + 16),
                pl.BlockSpec((R, LANE), lambda g, t, h: (g * nT + t, U_ML_IF)),
                pl.BlockSpec((1, LANE), lambda g, t, h: (0, 0)),
                pl.BlockSpec((1, DH), lambda g, t, h: (0, h))]
    args = [proj] * 6 + [ifb, ng]
    if init is not None:
        in_specs += [st4, n_spec, m_spec]
        args += list(init)
    aliases = {}
    if prev_c is not None:
        aliases = {len(args): 1}
        in_specs.append(pl.BlockSpec(memory_space=pl.ANY))
        args.append(prev_c)
    kern = functools.partial(_mlstm_kernel, L=L, gps=gps, t_valid=t_valid, li=li, has_init=init is not None,
                             has_prev=prev_c is not None, single_tile=nT == 1)
    return pl.pallas_call(
        kern,
        grid=(G // gps, nT, HEADS),
        in_specs=in_specs,
        out_specs=[pl.BlockSpec((R, DH), lambda g, t, h: (g * nT + t, h)), c_out_spec, n_spec, m_spec],
        out_shape=[jax.ShapeDtypeStruct((n, BR), F32), c_shape,
                   jax.ShapeDtypeStruct((B * HEADS, LANE), F32), jax.ShapeDtypeStruct((B, LANE), F32)],
        scratch_shapes=[pltpu.VMEM((R, LANE), F32), pltpu.VMEM((gps * SEQ_BLOCK, LANE), F32),
                        pltpu.VMEM((gps * SEQ_BLOCK * SEQ_BLOCK, LANE), F32)],
        input_output_aliases=aliases,
        compiler_params=_cparams(("arbitrary", "arbitrary", "arbitrary")),
        name="mlstm",
    )(*args)


def _causal_conv_tile(x_ref, w_ref, b_ref, xp_ref, act_ref, cv0_ref, cv_ref, *, ti, R, width, tv_local,
                      act_fn, carry_ref=None):
    @pl.when(ti == 0)
    def _():
        xp_ref[0:CARRY_ROWS, :] = cv0_ref[...]

    if carry_ref is not None:
        @pl.when(ti > 0)
        def _():
            xp_ref[0:CARRY_ROWS, :] = carry_ref[...]

    xp_ref[CARRY_ROWS:CARRY_ROWS + R, :] = x_ref[...]
    RB = next(rb for rb in (128, 32, SEQ_BLOCK) if R % rb == 0 and rb * min(width, BR) <= 128 * LANE)

    def blk(i, carry):
        r0 = pl.multiple_of(i * RB, SEQ_BLOCK)
        cw = min(width, BR)
        for c0 in range(0, width, cw):
            cs = pl.ds(c0, cw)
            acc = xp_ref[pl.ds(r0, RB), cs] * w_ref[0:1, cs]
            for j in range(1, CONV_W):
                acc = acc + xp_ref[pl.ds(r0 + j * SEQ_BLOCK, RB), cs] * w_ref[j:j + 1, cs]
            if b_ref is not None:
                acc = acc + b_ref[:, cs]
            act_ref[pl.ds(r0, RB), cs] = act_fn(acc)
        return carry

    lax.fori_loop(0, R // RB, blk, 0)

    cv_ref[...] = xp_ref[tv_local * SEQ_BLOCK:tv_local * SEQ_BLOCK + CARRY_ROWS, :]
    if carry_ref is not None:
        carry_ref[...] = xp_ref[R:R + CARRY_ROWS, :]
    else:
        xp_ref[0:CARRY_ROWS, :] = xp_ref[R:R + CARRY_ROWS, :]


def _gdn_kernel(*refs, L, TT, gps, t_valid, nT, li, has_init, has_prev):
    (q_ref, k_ref, v_ref, z_ref, ab_ref, cwq_ref, cwk_ref, cwv_ref, alog_ref, dtb_ref, ng_ref,
     cvq0_ref, cvk0_ref, cvv0_ref) = refs[:14]
    n_in = 14 + (1 if has_init else 0) + (1 if has_prev else 0)
    (y_ref, s_ref, cvq_ref, cvk_ref, cvv_ref,
     xpq_ref, xpk_ref, xpv_ref, aq_ref, ak_ref, av_ref, carry_ref, gb_ref) = refs[n_in:]
    ti = pl.program_id(1)
    h = pl.program_id(2)
    R = TT * SEQ_BLOCK
    s_ref = _own_layer(s_ref, li, has_prev, (ti == 0) & (h == 0))
    single_chunk = nT == 1 and TT == L
    s_src = refs[14] if (has_init and single_chunk) else s_ref

    @pl.when((ti == 0) & (h == 0))
    def _():
        if has_init and single_chunk:
            pass
        elif has_init:
            s_ref[...] = refs[14][...]
        else:
            s_ref[...] = jnp.zeros(s_ref.shape, F32)

    head_lanes = pl.ds(pl.multiple_of(h * DH, DH), DH)
    for j, (x_ref, cw_ref, xp_ref, a_ref, cv0_ref, cv_ref) in enumerate((
            (q_ref, cwq_ref, xpq_ref, aq_ref, cvq0_ref, cvq_ref),
            (k_ref, cwk_ref, xpk_ref, ak_ref, cvk0_ref, cvk_ref),
            (v_ref, cwv_ref, xpv_ref, av_ref, cvv0_ref, cvv_ref))):
        for gi in range(gps):
            tile = pl.ds(gi * R, R)
            cvr = pl.ds(gi * CARRY_ROWS, CARRY_ROWS)
            _causal_conv_tile(x_ref.at[tile], cw_ref, None, xp_ref.at[pl.ds(gi * (R + CARRY_ROWS), R + CARRY_ROWS)],
                              a_ref.at[tile], cv0_ref.at[cvr], cv_ref.at[cvr, head_lanes], ti=ti, R=R, width=DH,
                              tv_local=t_valid - (nT - 1) * TT, act_fn=_silu, carry_ref=carry_ref.at[j, h, cvr])

    @pl.when(h == 0)
    def _():
        tok, tok_in_chunk = _tile_token_ids(ti, TT, L)
        valid = tok < t_valid
        lane = lax.broadcasted_iota(jnp.int32, (R, LANE), 1)
        for gi in range(gps):
            tile = pl.ds(gi * R, R)
            abv = ab_ref[tile, :]
            g_all = -jnp.exp(alog_ref[...]) * _softplus(abv + dtb_ref[...])
            gam = _token_cumsum(jnp.where(valid, g_all, 0.0), tok_in_chunk, L)
            gb_ref[tile, :] = jnp.where(lane < HEADS, gam, jnp.where(valid, _sigmoid(abv), 0.0))

    ri = lax.broadcasted_iota(jnp.int32, (L, L), 0)
    ci = lax.broadcasted_iota(jnp.int32, (L, L), 1)
    causal = ci <= ri
    strict = ci < ri
    eye = ci == ri
    lane = lax.broadcasted_iota(jnp.int32, (L, LANE), 1)
    sel_a = lane == h
    sel_b = lane == HEADS + h
    pick = lambda tile, sel: jnp.sum(jnp.where(sel, tile, 0.0), axis=1, keepdims=True)
    n_double = int(math.log2(L)) - 1

    seqs = range(gps * SEQ_BLOCK)
    for c in range(TT // L):
        rows, st = [], []
        for b in seqs:
            r = pl.ds((b // SEQ_BLOCK) * R + c * L * SEQ_BLOCK + b % SEQ_BLOCK, L, stride=SEQ_BLOCK)
            rows.append(r)
            gbv = gb_ref[r, :]
            gam = pick(gbv, sel_a)
            beta = pick(gbv, sel_b)
            dec = jnp.where(causal, jnp.exp(gam - _row_from_col(gam, eye)), 0.0)
            q = aq_ref[r, :]
            k = ak_ref[r, :]
            q = q * lax.rsqrt(jnp.sum(q * q, axis=-1, keepdims=True) + EPS) * (DH ** -0.5)
            k = k * lax.rsqrt(jnp.sum(k * k, axis=-1, keepdims=True) + EPS)
            eg = jnp.exp(gam)
            g_last = gam[L - 1:L, :]
            kbeta = k * beta
            st.append(dict(dec=dec, q=q, k=k, kbeta=kbeta, eg=eg, g_last=g_last,
                           kd=k * jnp.exp(g_last - gam),
                           rhs=jnp.concatenate([av_ref[r, :] * beta, kbeta * eg], axis=1)))
        for b in seqs:
            s_ = st[b]
            amat = jnp.where(strict, _dot_nt(s_['kbeta'], s_['k']) * s_['dec'], 0.0)
            s_['qk'] = _dot_nt(s_['q'], s_['k']) * s_['dec']
            s_['o'] = _dot(s_['q'] * s_['eg'], s_src[b, h])
            s_['x'] = -amat
            s_['p'] = amat
        for b in seqs:
            st[b]['p'] = _dot(st[b]['p'], st[b]['p'])
        for i in range(n_double):
            for b in seqs:
                s_ = st[b]
                s_['xp'] = _dot(s_['x'], s_['p'])
                if i + 1 < n_double:
                    s_['p2'] = _dot(s_['p'], s_['p'])
            for b in seqs:
                s_ = st[b]
                s_['x'] = s_['x'] + s_['p'] + s_['xp']
                if i + 1 < n_double:
                    s_['p'] = s_['p2']
        for b in seqs:
            s_ = st[b]
            s_['sol'] = s_['rhs'] + _dot(s_['x'], s_['rhs'])
        for b in seqs:
            s_ = st[b]
            s_['v_new'] = s_['sol'][:, :DH] - _dot(s_['sol'][:, DH:], s_src[b, h])
        for b in seqs:
            s_ = st[b]
            o = s_['o'] + _dot(s_['qk'], s_['v_new'])
            s_ref[b, h] = jnp.exp(s_['g_last']) * s_src[b, h] + _dot_tn(s_['kd'], s_['v_new'])
            y_ref[rows[b], :] = _head_rms(o, ng_ref[...]) * _silu(z_ref[rows[b], :])


def _gdn(proj, cw, alog, dtb, ng, cv0, s_init, prev_s, *, li, n_layers, B, G, nT, TT, L, t_valid, gps):
    assert G % gps == 0 and (gps == 1 or nT == 1)
    R = TT * SEQ_BLOCK * gps
    n = proj.shape[0]
    upb = BR // LANE
    blk = lambda u: pl.BlockSpec((R, DH), lambda g, t, h, u=u: (g * nT + t, u + h))
    st4, s_out_spec, s_shape = _state_specs(li, n_layers, B, prev_s is not None, gps)
    cvs = lambda j: pl.BlockSpec((gps * CARRY_ROWS, DH), lambda g, t, h, j=j: (g, j * upb + h))
    cws = lambda j: pl.BlockSpec((CONV_W, DH), lambda g, t, h, j=j: (0, j * upb + h))
    one = pl.BlockSpec((1, LANE), lambda g, t, h: (0, 0))
    cvo = pl.BlockSpec((gps * CARRY_ROWS, BR), lambda g, t, h: (g, 0))
    in_specs = [blk(U_GD_QKV), blk(U_GD_QKV + 4), blk(U_GD_QKV + 8), blk(U_GD_Z),
                pl.BlockSpec((R, LANE), lambda g, t, h: (g * nT + t, U_GD_AB)),
                cws(0), cws(1), cws(2), one, one,
                pl.BlockSpec((1, DH), lambda g, t, h: (0, h)),
                cvs(0), cvs(1), cvs(2)]
    args = [proj] * 5 + [cw, cw, cw, alog, dtb, ng, cv0, cv0, cv0]
    if s_init is not None:
        in_specs.append(st4)
        args.append(s_init)
    aliases = {}
    if prev_s is not None:
        aliases = {len(args): 1}
        in_specs.append(pl.BlockSpec(memory_space=pl.ANY))
        args.append(prev_s)
    kern = functools.partial(_gdn_kernel, L=L, TT=TT, gps=gps, t_valid=t_valid, nT=nT, li=li,
                             has_init=s_init is not None, has_prev=prev_s is not None)
    cv_shape = jax.ShapeDtypeStruct((cv0.shape[0], BR), F32)
    return pl.pallas_call(
        kern,
        grid=(G // gps, nT, HEADS),
        in_specs=in_specs,
        out_specs=[pl.BlockSpec((R, DH), lambda g, t, h: (g * nT + t, h)), s_out_spec, cvo, cvo, cvo],
        out_shape=[jax.ShapeDtypeStruct((n, BR), F32), s_shape, cv_shape, cv_shape, cv_shape],
        scratch_shapes=([pltpu.VMEM((R + gps * CARRY_ROWS, DH), F32)] * 3 + [pltpu.VMEM((R, DH), F32)] * 3
                        + [pltpu.VMEM((3, HEADS, gps * CARRY_ROWS, DH), F32), pltpu.VMEM((R, LANE), F32)]),
        input_output_aliases=aliases,
        compiler_params=_cparams(("arbitrary", "arbitrary", "arbitrary")),
        name="gdn",
    )(*args)


def _s5_prep_kernel(are_ref, aim_ref, ldt_ref, bre_ref, bim_ref, abr_ref, abi_ref, bbr_ref, bbi_ref):
    a_re = are_ref[...]
    a_im = aim_ref[...]
    dt = jnp.exp(ldt_ref[...])
    mag = jnp.exp(dt * a_re)
    ang = dt * a_im
    ab_r = mag * jnp.cos(ang)
    ab_i = mag * jnp.sin(ang)
    den = a_re * a_re + a_im * a_im
    nr = ab_r - 1.0
    ni = ab_i
    f_r = (nr * a_re + ni * a_im) / den
    f_i = (ni * a_re - nr * a_im) / den
    abr_ref[...] = ab_r
    abi_ref[...] = ab_i
    f_r = f_r[0:1, :]
    f_i = f_i[0:1, :]
    bbr_ref[...] = f_r * bre_ref[...] - f_i * bim_ref[...]
    bbi_ref[...] = f_r * bim_ref[...] + f_i * bre_ref[...]


def _s5_prep(a_re, a_im, log_dt, b_re, b_im):
    rep = lambda a: jnp.broadcast_to(a.reshape(1, S5_STATE), (SEQ_BLOCK, S5_STATE))
    ldt = rep(jnp.broadcast_to(log_dt[:, None], (S5_G, S5_P)))
    bt = lambda b: jnp.transpose(b, (2, 0, 1)).reshape(S5_N, S5_STATE)
    shp = lambda r: jax.ShapeDtypeStruct((r, S5_STATE), F32)
    abr, abi, bbr, bbi = pl.pallas_call(
        _s5_prep_kernel,
        out_shape=[shp(SEQ_BLOCK), shp(SEQ_BLOCK), shp(S5_N), shp(S5_N)],
        name="s5_prep",
    )(rep(a_re), rep(a_im), ldt, bt(b_re), bt(b_im))
    return abr, abi, bbr, bbi


def _gelu_tanh(x):
    return 0.5 * x * (1.0 + jnp.tanh(math.sqrt(2.0 / math.pi) * (x + 0.044715 * (x * x * x))))


def _s5_kernel(u_ref, z_ref, abr_ref, abi_ref, wbr_ref, wbi_ref, wcr_ref, wci_ref, d_ref, gw_ref, gb_ref,
               x0r_ref, x0i_ref, y_ref, xr_ref, xi_ref, hr_ref, hi_ref, ys_ref, *, TT, n_steps):
    ti = pl.program_id(1)
    NB = BR // LANE
    SB = S5_STATE // NB

    @pl.when(ti == 0)
    def _():
        xr_ref[...] = x0r_ref[...]
        xi_ref[...] = x0i_ref[...]

    for kb in range(NB):
        ub = u_ref[:, kb * LANE:(kb + 1) * LANE].astype(BF16)
        hr_ref[:, kb * SB:(kb + 1) * SB] = jnp.dot(ub, wbr_ref[kb], preferred_element_type=F32)
        hi_ref[:, kb * SB:(kb + 1) * SB] = jnp.dot(ub, wbi_ref[kb], preferred_element_type=F32)

    for kb in range(NB):
        sl = pl.ds(kb * SB, SB)
        ar = abr_ref[:, sl]
        ai = abi_ref[:, sl]

        def step(t, carry):
            xr, xi = carry
            rows = pl.ds(pl.multiple_of(t * SEQ_BLOCK, SEQ_BLOCK), SEQ_BLOCK)
            nxr = ar * xr - ai * xi + hr_ref[rows, sl]
            nxi = ar * xi + ai * xr + hi_ref[rows, sl]
            hr_ref[rows, sl] = nxr
            hi_ref[rows, sl] = nxi
            return nxr, nxi

        xr, xi = lax.fori_loop(0, n_steps, step, (xr_ref[:, sl], xi_ref[:, sl]))
        xr_ref[:, sl] = xr
        xi_ref[:, sl] = xi

    for kb in range(NB):
        sl = pl.ds(kb * SB, SB)
        cs = pl.ds(kb * LANE, LANE)
        yk = (jnp.dot(hr_ref[:, sl].astype(BF16), wcr_ref[kb], preferred_element_type=F32)
              - jnp.dot(hi_ref[:, sl].astype(BF16), wci_ref[kb], preferred_element_type=F32)
              + d_ref[:, cs] * u_ref[:, cs])
        ys_ref[:, cs] = _gelu_tanh(yk)
    ys = ys_ref[...]
    glu = ys * _sigmoid(jnp.dot(ys.astype(BF16), gw_ref[...], preferred_element_type=F32) + gb_ref[...])
    y_ref[...] = glu * _silu(z_ref[...])


def _s5(proj, abr, abi, wbr, wbi, wcr, wci, d, gw, gb, x0r, x0i, *, G, nT, TT, t_valid):
    assert nT == 1 or t_valid == nT * TT
    R = TT * SEQ_BLOCK
    n = proj.shape[0]
    full = lambda a: pl.BlockSpec(a.shape, lambda g, t, nd=a.ndim: (0,) * nd)
    st = pl.BlockSpec((SEQ_BLOCK, S5_STATE), lambda g, t: (g, 0))
    kern = functools.partial(_s5_kernel, TT=TT, n_steps=t_valid - (nT - 1) * TT)
    return pl.pallas_call(
        kern,
        grid=(G, nT),
        in_specs=[pl.BlockSpec((R, BR), lambda g, t: (g * nT + t, U_S5_U // 4)),
                  pl.BlockSpec((R, BR), lambda g, t: (g * nT + t, U_S5_Z // 4)),
                  full(abr), full(abi), full(wbr), full(wbi), full(wcr), full(wci), full(d), full(gw), full(gb),
                  st, st],
        out_specs=[pl.BlockSpec((R, BR), lambda g, t: (g * nT + t, 0)), st, st],
        out_shape=[jax.ShapeDtypeStruct((n, BR), F32),
                   jax.ShapeDtypeStruct(x0r.shape, F32),
                   jax.ShapeDtypeStruct(x0i.shape, F32)],
        scratch_shapes=[pltpu.VMEM((R, S5_STATE), F32), pltpu.VMEM((R, S5_STATE), F32),
                        pltpu.VMEM((R, BR), F32)],
        compiler_params=_cparams(("arbitrary", "arbitrary")),
        name="s5",
    )(proj, proj, abr, abi, wbr, wbi, wcr, wci, d, gw, gb, x0r, x0i)


def _lru_kernel(x_ref, z_ref, cw_ref, cb_ref, wa_ref, ba_ref, wx_ref, bx_ref, lam_ref, h0_ref, cv0_ref,
                y_ref, h_ref, cv_ref, xp_ref, xl_ref, a_ref, *, TT, n_steps, nT):
    ti = pl.program_id(1)
    R = TT * SEQ_BLOCK

    @pl.when(ti == 0)
    def _():
        h_ref[...] = h0_ref[...]

    _causal_conv_tile(x_ref, cw_ref, cb_ref, xp_ref, xl_ref, cv0_ref, cv_ref, ti=ti, R=R, width=BR,
                      tv_local=n_steps, act_fn=lambda a: a)

    xl = xl_ref[...]
    xb = xl.astype(BF16)
    r = _sigmoid(jnp.dot(xb, wa_ref[...], preferred_element_type=F32) + ba_ref[...])
    i = _sigmoid(jnp.dot(xb, wx_ref[...], preferred_element_type=F32) + bx_ref[...])
    log_a = -LRU_C * r * _softplus(-lam_ref[...])
    a_ref[...] = jnp.exp(log_a)
    th = jnp.tanh(log_a)
    xl_ref[...] = jnp.sqrt(-2.0 * th / (1.0 - th)) * (i * xl)

    def step(t, h):
        rows = pl.ds(pl.multiple_of(t * SEQ_BLOCK, SEQ_BLOCK), SEQ_BLOCK)
        hn = a_ref[rows, :] * h + xl_ref[rows, :]
        xl_ref[rows, :] = hn
        return hn

    h_ref[...] = lax.fori_loop(0, n_steps, step, h_ref[...])
    y_ref[...] = xl_ref[...] * _silu(z_ref[...])


def _lru(proj, cw, cb, wa, ba, wx, bx, lam, h0, cv0, *, G, nT, TT, t_valid):
    assert nT == 1 or t_valid == nT * TT
    R = TT * SEQ_BLOCK
    n = proj.shape[0]
    full = lambda a: pl.BlockSpec(a.shape, lambda g, t, nd=a.ndim: (0,) * nd)
    st = pl.BlockSpec((SEQ_BLOCK, BR), lambda g, t: (g, 0))
    cvs = pl.BlockSpec((CARRY_ROWS, BR), lambda g, t: (g, 0))
    kern = functools.partial(_lru_kernel, TT=TT, n_steps=t_valid - (nT - 1) * TT, nT=nT)
    return pl.pallas_call(
        kern,
        grid=(G, nT),
        in_specs=[pl.BlockSpec((R, BR), lambda g, t: (g * nT + t, U_LRU_X // 4)),
                  pl.BlockSpec((R, BR), lambda g, t: (g * nT + t, U_LRU_Z // 4)),
                  full(cw), full(cb), full(wa), full(ba), full(wx), full(bx), full(lam), st, cvs],
        out_specs=[pl.BlockSpec((R, BR), lambda g, t: (g * nT + t, 0)), st, cvs],
        out_shape=[jax.ShapeDtypeStruct((n, BR), F32),
                   jax.ShapeDtypeStruct(h0.shape, F32),
                   jax.ShapeDtypeStruct(cv0.shape, F32)],
        scratch_shapes=[pltpu.VMEM((R + CARRY_ROWS, BR), F32), pltpu.VMEM((R, BR), F32),
                        pltpu.VMEM((R, BR), F32)],
        compiler_params=_cparams(("arbitrary", "arbitrary")),
        name="lru",
    )(proj, proj, cw, cb, wa, ba, wx, bx, lam, h0, cv0)


def _merge_kernel(yml_ref, ygd_ref, ys5_ref, ylru_ref, gates_ref, x_ref, p_ref,
                  wbr_ref, wout_ref, wpg_ref, wple_ref, pg_ref, o_ref):
    merged = None
    for nb, y_ref in enumerate((yml_ref, ygd_ref, ys5_ref, ylru_ref)):
        pbr = jnp.dot(y_ref[...].astype(BF16), wbr_ref[nb], preferred_element_type=F32)
        term = _sigmoid(gates_ref[:, nb * D_MODEL:(nb + 1) * D_MODEL]) * pbr
        merged = term if merged is None else merged + term
    out = jnp.dot(merged.astype(BF16), wout_ref[...], preferred_element_type=F32)
    r = x_ref[...] + out * lax.rsqrt(jnp.mean(out * out, axis=-1, keepdims=True) + EPS) * pg_ref[...]
    gate = _sigmoid(jnp.dot(r.astype(BF16), wpg_ref[...], preferred_element_type=F32))
    o_ref[...] = r + gate * jnp.dot(p_ref[...].astype(BF16), wple_ref[...], preferred_element_type=F32)


def _merge(ys, proj, x, p, wbr, wout, wpg, wple, pg, tm):
    n = x.shape[0]
    row = lambda c: pl.BlockSpec((tm, c), lambda i: (i, 0))
    full = lambda a: pl.BlockSpec(a.shape, lambda i, nd=a.ndim: (0,) * nd)
    return pl.pallas_call(
        _merge_kernel,
        grid=(n // tm,),
        in_specs=[row(BR), row(BR), row(BR), row(BR),
                  pl.BlockSpec((tm, 4 * D_MODEL), lambda i: (i, U_GATES // 32)),
                  row(D_MODEL), row(PLE_DIM), full(wbr), full(wout), full(wpg), full(wple), full(pg)],
        out_specs=row(D_MODEL),
        out_shape=jax.ShapeDtypeStruct((n, D_MODEL), F32),
        compiler_params=_cparams(("parallel",)),
        name="merge",
    )(*ys, proj, x, p, wbr, wout, wpg, wple, pg)


def _permute_w_in(w_in):
    sizes = (BR, BR, BR, HEADS, HEADS, BR, BR, 3 * BR, HEADS, HEADS, BR, BR, BR, BR, BR, 4 * D_MODEL)
    offs = [0]
    for s in sizes:
        offs.append(offs[-1] + s)
    seg = lambda i: w_in[:, :, offs[i]:offs[i + 1]]
    (ml_q, ml_k, ml_v, ml_i, ml_f, ml_o, ml_z, gd_qkv, gd_a, gd_b, gd_z,
     s5_u, s5_z, lru_x, lru_z, gates) = [seg(i) for i in range(len(sizes))]
    zeros = lambda c: jnp.zeros(w_in.shape[:2] + (c,), w_in.dtype)
    cols = [gd_qkv, ml_q, ml_k, ml_v, ml_o, ml_z, gates, gd_z, s5_u, s5_z, lru_x, lru_z,
            ml_i, ml_f, zeros(LANE - 2 * HEADS), gd_a, gd_b, zeros(LANE - 2 * HEADS),
            zeros((N_UNITS - U_GD_AB - 1) * LANE)]
    return jnp.concatenate(cols, axis=-1).astype(BF16)


def _pad_lanes(*vecs):
    v = jnp.concatenate(vecs)
    return jnp.pad(v, (0, LANE - v.shape[0])).reshape(1, LANE)


def _block_diag(blocks):
    n, a, b = blocks.shape
    eye = jnp.eye(n, dtype=blocks.dtype)
    return jnp.einsum('ij,iab->iajb', eye, blocks).reshape(n * a, n * b)


def _to_rows(x, G):
    B, T, C = x.shape
    return x.reshape(G, SEQ_BLOCK, T, C).transpose(0, 2, 1, 3).reshape(G * T * SEQ_BLOCK, C)


def _from_rows(y, G, T):
    C = y.shape[-1]
    return y.reshape(G, T, SEQ_BLOCK, C).transpose(0, 2, 1, 3).reshape(G * SEQ_BLOCK, T, C)


def _group(x, p, states, params, *, T, t_valid, TT, L, tm_proj, tm_merge, gps=1):
    B = x.shape[0]
    G = B // SEQ_BLOCK
    nT = T // TT
    depth = p.shape[0]
    xr = _to_rows(x, G)
    tm_proj = min(tm_proj, xr.shape[0])
    tm_merge = min(tm_merge, xr.shape[0])
    new_states = []
    c_all = s_all = None
    gps = math.gcd(gps, G)
    mat = dict(n_layers=depth, B=B, G=G, nT=nT, TT=TT, L=L, t_valid=t_valid, gps=gps)
    for li in range(depth):
        lp = {k: v[li] for k, v in params.items() if k != 'w_in_p'}
        if states is None:
            ml_init = s_init = None
            gcv0 = jnp.zeros((G * CARRY_ROWS, 3 * BR), F32)
            x0r = jnp.zeros((B, S5_STATE), F32)
            x0i = jnp.zeros((B, S5_STATE), F32)
            h0 = jnp.zeros((B, BR), F32)
            lcv0 = jnp.zeros((G * CARRY_ROWS, BR), F32)
        else:
            (n0, m0, gcv0, x0r, x0i, h0, lcv0) = [states[j][li] for j in (1, 2, 4, 5, 6, 7, 8)]
            n0 = n0.reshape(G, SEQ_BLOCK, HEADS, DH).transpose(0, 2, 1, 3).reshape(B * HEADS, DH)
            ml_init = (states[0], n0, jnp.pad(m0, ((0, 0), (0, LANE - HEADS))))
            s_init = states[3]
            gcv0 = _to_rows(gcv0, G)
            x0r = x0r.reshape(B, S5_STATE)
            x0i = x0i.reshape(B, S5_STATE)
            lcv0 = _to_rows(lcv0, G)
        pr = _to_rows(p[li], G)

        proj = _in_proj(xr, lp['prenorm_g'], params['w_in_p'], li, tm_proj, PROJ_COLS // 4)

        y_ml, c_all, n1, m1 = _mlstm(proj, lp['ml_ifb'], lp['ml_norm_g'], ml_init, c_all, li=li, **mat)
        y_gd, s_all, cvq, cvk, cvv = _gdn(proj, lp['gd_conv_w'], lp['gd_alog'], lp['gd_dtb'], lp['gd_norm_g'],
                                          gcv0, s_init, s_all, li=li, **dict(mat, L=min(L, GDN_CHUNK)))
        gcv1 = jnp.concatenate([cvq, cvk, cvv], axis=-1)
        y_s5, x1r, x1i = _s5(proj, lp['s5_abr'], lp['s5_abi'], lp['s5_wbr'], lp['s5_wbi'], lp['s5_wcr'],
                             lp['s5_wci'], lp['s5_d'], lp['s5_glu_w'], lp['s5_glu_b'], x0r, x0i,
                             G=G, nT=nT, TT=TT, t_valid=t_valid)
        y_lru, h1, lcv1 = _lru(proj, lp['lru_conv_w'], lp['lru_conv_b'], lp['lru_wa'], lp['lru_ba'],
                               lp['lru_wx'], lp['lru_bx'], lp['lru_lam'], h0, lcv0,
                               G=G, nT=nT, TT=TT, t_valid=t_valid)
        xr = _merge((y_ml, y_gd, y_s5, y_lru), proj, xr, pr, lp['w_branch'], lp['w_out'], lp['w_ple_gate'],
                    lp['w_ple'], lp['postnorm_g'], tm_merge)

        n1 = n1.reshape(G, HEADS, SEQ_BLOCK, DH).transpose(0, 2, 1, 3)
        new_states.append((n1.reshape(B, HEADS, DH), m1[:, :HEADS],
                           _from_rows(gcv1, G, CONV_W - 1), x1r.reshape(B, S5_G, S5_P),
                           x1i.reshape(B, S5_G, S5_P), h1, _from_rows(lcv1, G, CONV_W - 1)))
    y = _from_rows(xr, G, T)
    n_s, m_s, gcv_s, xr_s, xi_s, h_s, lcv_s = (jnp.stack([ns[j] for ns in new_states]) for j in range(7))
    return y, (c_all, n_s, m_s, s_all, gcv_s, xr_s, xi_s, h_s, lcv_s)


def _prepare_params(prenorm_g, postnorm_g, w_in, ml_bi, ml_bf, ml_norm_g, gd_conv_w, gd_a_log, gd_dt_bias,
                    gd_norm_g, s5_a_re, s5_a_im, s5_log_dt, s5_b_re, s5_b_im, s5_c_re, s5_c_im, s5_d,
                    s5_glu_w, s5_glu_b, lru_conv_w, lru_conv_b, lru_wa, lru_ba, lru_wx, lru_bx, lru_lam,
                    w_branch, w_out, w_ple, w_ple_gate):
    depth = w_in.shape[0]
    row = lambda a: a.reshape(depth, 1, -1)
    per_layer = lambda f, *a: jnp.stack([f(*[x[i] for x in a]) for i in range(depth)])
    prm = dict(
        prenorm_g=row(prenorm_g), postnorm_g=row(postnorm_g), w_in_p=_permute_w_in(w_in),
        ml_ifb=per_layer(_pad_lanes, ml_bi, ml_bf), ml_norm_g=row(ml_norm_g),
        gd_conv_w=gd_conv_w, gd_norm_g=row(gd_norm_g),
        gd_alog=per_layer(lambda a: _pad_lanes(a), gd_a_log),
        gd_dtb=per_layer(lambda a: _pad_lanes(a), gd_dt_bias),
        s5_d=row(s5_d), s5_glu_w=s5_glu_w.astype(BF16), s5_glu_b=row(s5_glu_b),
        lru_conv_w=lru_conv_w, lru_conv_b=row(lru_conv_b),
        lru_wa=per_layer(_block_diag, lru_wa).astype(BF16), lru_ba=row(lru_ba),
        lru_wx=per_layer(_block_diag, lru_wx).astype(BF16), lru_bx=row(lru_bx), lru_lam=row(lru_lam),
        w_branch=w_branch.astype(BF16), w_out=w_out.astype(BF16), w_ple=w_ple.astype(BF16),
        w_ple_gate=w_ple_gate.astype(BF16),
    )
    abr, abi, wbr, wbi, wcr, wci = [], [], [], [], [], []
    nb = BR // LANE
    gpb = S5_G // nb
    for i in range(depth):
        a_r, a_i, bbr, bbi = _s5_prep(s5_a_re[i], s5_a_im[i], s5_log_dt[i], s5_b_re[i], s5_b_im[i])
        abr.append(a_r)
        abi.append(a_i)
        in_blocks = lambda bb: jnp.stack([_block_diag(bb.reshape(S5_N, S5_G, S5_P).transpose(1, 0, 2)
                                                      [k * gpb:(k + 1) * gpb]) for k in range(nb)])
        out_blocks = lambda c: jnp.stack([_block_diag(jnp.transpose(c, (0, 2, 1))[k * gpb:(k + 1) * gpb])
                                          for k in range(nb)])
        wbr.append(in_blocks(bbr).astype(BF16))
        wbi.append(in_blocks(bbi).astype(BF16))
        wcr.append(out_blocks(s5_c_re[i]).astype(BF16))
        wci.append(out_blocks(s5_c_im[i]).astype(BF16))
    prm.update(s5_abr=jnp.stack(abr), s5_abi=jnp.stack(abi), s5_wbr=jnp.stack(wbr), s5_wbi=jnp.stack(wbi),
               s5_wcr=jnp.stack(wcr), s5_wci=jnp.stack(wci))
    return prm


def kernel(x_prompt, x_sample, state_mlstm_c, state_mlstm_n, state_mlstm_m, state_gdn_s, state_gdn_conv, state_s5_re, state_s5_im, state_lru_h, state_lru_conv, p_prompt, p_sample, prenorm_g, postnorm_g, w_in, ml_bi, ml_bf, ml_norm_g, gd_conv_w, gd_a_log, gd_dt_bias, gd_norm_g, s5_a_re, s5_a_im, s5_log_dt, s5_b_re, s5_b_im, s5_c_re, s5_c_im, s5_d, s5_glu_w, s5_glu_b, lru_conv_w, lru_conv_b, lru_wa, lru_ba, lru_wx, lru_bx, lru_lam, w_branch, w_out, w_ple, w_ple_gate):
    prm = _prepare_params(prenorm_g, postnorm_g, w_in, ml_bi, ml_bf, ml_norm_g, gd_conv_w, gd_a_log,
                          gd_dt_bias, gd_norm_g, s5_a_re, s5_a_im, s5_log_dt, s5_b_re, s5_b_im, s5_c_re,
                          s5_c_im, s5_d, s5_glu_w, s5_glu_b, lru_conv_w, lru_conv_b, lru_wa, lru_ba, lru_wx,
                          lru_bx, lru_lam, w_branch, w_out, w_ple, w_ple_gate)

    t_p = x_prompt.shape[1]
    tt_p = math.gcd(t_p, PROMPT_CHUNK)
    y_prompt, pr = _group(x_prompt, p_prompt, None, prm, T=t_p, t_valid=t_p, TT=tt_p, L=tt_p,
                          tm_proj=min(512, t_p * SEQ_BLOCK), tm_merge=min(256, t_p * SEQ_BLOCK))

    t_s = x_sample.shape[1]
    t_pad = -(-t_s // SEQ_BLOCK) * SEQ_BLOCK
    pad_t = lambda a, ax: jnp.pad(a, [(0, t_pad - t_s) if i == ax else (0, 0) for i in range(a.ndim)])
    sample_states = (state_mlstm_c, state_mlstm_n, state_mlstm_m, state_gdn_s, state_gdn_conv,
                     state_s5_re, state_s5_im, state_lru_h, state_lru_conv)
    y_s, sa = _group(pad_t(x_sample, 1), pad_t(p_sample, 2), sample_states, prm, T=t_pad, t_valid=t_s,
                     TT=t_pad, L=t_pad, tm_proj=512, tm_merge=256, gps=SAMPLE_GROUPS_PER_STEP)
    y_sample = y_s[:, :t_s]
    return (y_prompt, y_sample) + pr + sa
```

```python
import functools
import math

import jax
import jax.numpy as jnp
from jax import lax
from jax.experimental import pallas as pl
from jax.experimental.pallas import tpu as pltpu

F32 = jnp.float32
BF16 = jnp.bfloat16

D_MODEL = 1024
BR = 512
HEADS = 4
DH = 128
CONV_W = 4
S5_G = 32
S5_N = 16
S5_P = 64
S5_STATE = S5_G * S5_P
LRU_BLOCKS = 8
LRU_BD = 64
LRU_C = 8.0
PLE_DIM = 256
EPS = 1e-6
NEG = -1e30

LANE = 128
SEQ_BLOCK = 8
CARRY_ROWS = (CONV_W - 1) * SEQ_BLOCK
PROMPT_CHUNK = 128
GDN_CHUNK = 64
SAMPLE_GROUPS_PER_STEP = 2
TM_PROJ = 512
TM_MERGE = 512

U_GD_QKV = 0
U_ML = 12
U_GATES = 32
U_GD_Z = 64
U_S5_U = 68
U_S5_Z = 72
U_LRU_X = 76
U_LRU_Z = 80
U_ML_IF = 84
U_GD_AB = 85
N_UNITS = 86
PROJ_COLS = N_UNITS * LANE
PROJ_COL_TILES = 2

VMEM_LIMIT = 56 * 1024 * 1024


def _cparams(sem):
    return pltpu.CompilerParams(dimension_semantics=sem, vmem_limit_bytes=VMEM_LIMIT)


def _sigmoid(x):
    return 1.0 / (1.0 + jnp.exp(-x))


def _silu(x):
    return x * _sigmoid(x)


def _softplus(x):
    return jnp.maximum(x, 0.0) + jnp.log1p(jnp.exp(-jnp.abs(x)))


def _dot(a, b):
    return jnp.dot(a.astype(BF16), b.astype(BF16), preferred_element_type=F32)


def _dot_nt(a, b):
    return lax.dot_general(a.astype(BF16), b.astype(BF16), (((1,), (1,)), ((), ())),
                           preferred_element_type=F32)


def _dot_tn(a, b):
    return lax.dot_general(a.astype(BF16), b.astype(BF16), (((0,), (0,)), ((), ())),
                           preferred_element_type=F32)


def _cumsum_rows(x):
    n = x.shape[0]
    ri = lax.broadcasted_iota(jnp.int32, x.shape, 0)
    s = 1
    while s < n:
        x = x + jnp.where(ri >= s, pltpu.roll(x, s, axis=0), 0.0)
        s *= 2
    return x


def _row_from_col(col, eye):
    return jnp.sum(jnp.where(eye, col, 0.0), axis=0, keepdims=True)


def _head_rms(h, g):
    return h * lax.rsqrt(jnp.mean(h * h, axis=-1, keepdims=True) + EPS) * g


def _in_proj_kernel(x_ref, g_ref, w_ref, o_ref):
    x = x_ref[...]
    h = x * lax.rsqrt(jnp.mean(x * x, axis=-1, keepdims=True) + EPS) * g_ref[...]
    o_ref[...] = jnp.dot(h.astype(BF16), w_ref[...], preferred_element_type=F32)


def _in_proj(x, g, w_all, li, tm, tn):
    n = x.shape[0]
    return pl.pallas_call(
        _in_proj_kernel,
        grid=(PROJ_COLS // tn, n // tm),
        in_specs=[pl.BlockSpec((tm, D_MODEL), lambda j, i: (i, 0)),
                  pl.BlockSpec((1, D_MODEL), lambda j, i: (0, 0)),
                  pl.BlockSpec((None, D_MODEL, tn), lambda j, i: (li, 0, j))],
        out_specs=pl.BlockSpec((tm, tn), lambda j, i: (i, j)),
        out_shape=jax.ShapeDtypeStruct((n, PROJ_COLS), F32),
        compiler_params=_cparams(("parallel", "parallel")),
        name="in_proj",
    )(x, g, w_all)


def _token_cumsum(x, tok_in_chunk, L):
    s = 1
    while s < L:
        x = x + jnp.where(tok_in_chunk >= s, pltpu.roll(x, s * SEQ_BLOCK, axis=0), 0.0)
        s *= 2
    return x


def _tile_token_ids(ti, TT, L):
    R = TT * SEQ_BLOCK
    t_local = lax.broadcasted_iota(jnp.int32, (R, LANE), 0) // SEQ_BLOCK
    return ti * TT + t_local, t_local % L


def _token_scan(x, op, fill):
    s = 1
    while s < x.shape[0]:
        shifted = jnp.concatenate([jnp.full((s,) + x.shape[1:], fill, x.dtype), x[:-s]], axis=0)
        x = op(x, shifted)
        s *= 2
    return x


ML_R, ML_M, ML_INTER, ML_ENEG, ML_WS = (j * HEADS for j in range(5))


def _mlstm_gate_tile(if_ref, ifb_ref, m_ref, sc_ref, gt_ref, *, ti, T, t_valid):
    shape = (T, SEQ_BLOCK, LANE)
    lane = lax.broadcasted_iota(jnp.int32, shape, 2)
    valid = (ti * T + lax.broadcasted_iota(jnp.int32, shape, 0)) < t_valid
    rot = lambda x, k: pltpu.roll(x.reshape(T * SEQ_BLOCK, LANE), k, axis=1).reshape(shape)
    ifv = (if_ref[...] + ifb_ref[...]).reshape(shape)
    logf = jnp.minimum(ifv, 0.0) - jnp.log1p(jnp.exp(-jnp.abs(ifv)))
    bcum = rot(_token_scan(jnp.where(valid, logf, 0.0), jnp.add, 0.0), LANE - HEADS)
    ig = jnp.where(valid, ifv, NEG)
    r = ig - bcum
    m_prev = m_ref[...]
    big_m = jnp.maximum(m_prev[None], _token_scan(r, jnp.maximum, NEG))
    inter = jnp.exp(m_prev[None] - big_m)
    eneg = jnp.exp(-(bcum + big_m))
    b_last = bcum[T - 1]
    g = b_last[None] - bcum + ig
    m_new = jnp.maximum(b_last + m_prev, jnp.max(g, axis=0))
    ws = jnp.exp(g - m_new[None])
    head_lanes = lane[0] < HEADS
    sc_ref[...] = jnp.where(head_lanes, jnp.exp(b_last + m_prev - m_new), 0.0)
    m_ref[...] = jnp.where(head_lanes, m_new, 0.0)
    packed = jnp.where(lane < ML_M, r,
                       jnp.where(lane < ML_INTER, rot(big_m, ML_M),
                                 jnp.where(lane < ML_ENEG, rot(inter, ML_INTER),
                                           jnp.where(lane < ML_WS, rot(eneg, ML_ENEG), rot(ws, ML_WS)))))
    gt_ref[...] = packed.reshape(T * SEQ_BLOCK, LANE)


def _own_layer(state_ref, li, has_prev, first_step):
    if has_prev:
        return state_ref

    @pl.when(first_step)
    def _():
        for other in range(state_ref.shape[0]):
            if other != li:
                state_ref[other] = jnp.zeros(state_ref.shape[1:], F32)

    return state_ref.at[li]


def _mlstm_kernel(*refs, L, gps, t_valid, li, has_init, has_prev, single_tile):
    q_ref, k_ref, v_ref, o_ref, z_ref, if_ref, ifb_ref, ng_ref = refs[:8]
    n_in = 8 + (3 if has_init else 0) + (1 if has_prev else 0)
    y_ref, c_ref, n_ref, m_ref, gt_ref, sc_ref, rt_ref = refs[n_in:]
    ti = pl.program_id(1)
    h = pl.program_id(2)
    R = L * SEQ_BLOCK
    seqs = range(gps * SEQ_BLOCK)
    rows = [pl.ds((j // SEQ_BLOCK) * R + j % SEQ_BLOCK, L, stride=SEQ_BLOCK) for j in seqs]
    n_rows = [pl.ds((j // SEQ_BLOCK) * SEQ_BLOCK * HEADS + h * SEQ_BLOCK + j % SEQ_BLOCK, 1) for j in seqs]
    use_transpose = L == LANE
    c_ref = _own_layer(c_ref, li, has_prev, (ti == 0) & (h == 0))
    c_src = refs[8] if (has_init and single_tile) else c_ref

    @pl.when((ti == 0) & (h == 0))
    def _():
        if has_init:
            if not single_tile:
                c_ref[...] = refs[8][...]
            n_ref[...] = refs[9][...]
            m_ref[...] = refs[10][...]
        else:
            c_ref[...] = jnp.zeros(c_ref.shape, F32)
            n_ref[...] = jnp.zeros(n_ref.shape, F32)
            m_ref[...] = jnp.zeros(m_ref.shape, F32)

    @pl.when(h == 0)
    def _():
        for gi in range(gps):
            grp = pl.ds(gi * SEQ_BLOCK, SEQ_BLOCK)
            tile = pl.ds(gi * R, R)
            _mlstm_gate_tile(if_ref.at[tile], ifb_ref, m_ref.at[grp], sc_ref.at[grp], gt_ref.at[tile],
                             ti=ti, T=L, t_valid=t_valid)
        if use_transpose:
            for b in seqs:
                rt_ref[b * SEQ_BLOCK:(b + 1) * SEQ_BLOCK, :] = gt_ref[rows[b], :].T[:SEQ_BLOCK, :]

    ri = lax.broadcasted_iota(jnp.int32, (L, L), 0)
    ci = lax.broadcasted_iota(jnp.int32, (L, L), 1)
    causal = ci <= ri
    eye = ci == ri
    to_lane0 = jnp.where(h == 0, 0, LANE - h)
    sc_all = pltpu.roll(sc_ref[...], to_lane0, axis=1)
    col = lambda tile, off: tile[:, off:off + 1]

    st = []
    for b in seqs:
        gq = pltpu.roll(gt_ref[rows[b], :], to_lane0, axis=1)
        q = q_ref[rows[b], :] * (DH ** -0.5)
        v = v_ref[rows[b], :]
        kw = k_ref[rows[b], :] * col(gq, ML_WS)
        cmat = c_src[b, h]
        sc = sc_all[b:b + 1, 0:1]
        st.append(dict(gq=gq, q=q, v=v, kw=kw, sc=sc, qk=_dot_nt(q, k_ref[rows[b], :]), qc=_dot(q, cmat)))
        c_ref[b, h] = sc * cmat + _dot_tn(kw, v)
    for b in seqs:
        s_ = st[b]
        if use_transpose:
            r_row = rt_ref[pl.ds(b * SEQ_BLOCK + h, 1), :]
        else:
            r_row = _row_from_col(col(s_['gq'], ML_R), eye)
        s_['s'] = s_['qk'] * jnp.where(causal, jnp.exp(r_row - col(s_['gq'], ML_M)), 0.0)
    for b in seqs:
        s_ = st[b]
        srow = n_rows[b]
        nvec = n_ref[srow, :]
        inter = col(s_['gq'], ML_INTER)
        s_['num'] = _dot(s_['s'], s_['v']) + inter * s_['qc']
        den = (jnp.sum(s_['s'], axis=1, keepdims=True)
               + inter * jnp.sum(s_['q'] * nvec, axis=1, keepdims=True))
        s_['den'] = jnp.maximum(jnp.abs(den), col(s_['gq'], ML_ENEG))
        n_ref[srow, :] = s_['sc'] * nvec + jnp.sum(s_['kw'], axis=0, keepdims=True)
    for b in seqs:
        s_ = st[b]
        yn = _head_rms(s_['num'] / s_['den'], ng_ref[...])
        y_ref[rows[b], :] = yn * _sigmoid(o_ref[rows[b], :]) * _silu(z_ref[rows[b], :])


def _state_specs(li, n_layers, B, has_prev, gps):
    nseq = gps * SEQ_BLOCK
    one = pl.BlockSpec((None, nseq, HEADS, DH, DH), lambda g, t, h: (li, g, 0, 0, 0))
    every = pl.BlockSpec((n_layers, nseq, HEADS, DH, DH), lambda g, t, h: (0, g, 0, 0, 0))
    shape = jax.ShapeDtypeStruct((n_layers, B, HEADS, DH, DH), F32)
    return one, (one if has_prev else every), shape


def _mlstm(proj, ifb, ng, init, prev_c, *, li, n_layers, B, G, nT, TT, L, t_valid, gps):
    assert TT == L and G % gps == 0 and (gps == 1 or nT == 1)
    R = TT * SEQ_BLOCK * gps
    n = proj.shape[0]
    blk = lambda u: pl.BlockSpec((R, DH), lambda g, t, h, u=u: (g * nT + t, u + h))
    st4, c_out_spec, c_shape = _state_specs(li, n_layers, B, prev_c is not None, gps)
    n_spec = pl.BlockSpec((gps * SEQ_BLOCK * HEADS, LANE), lambda g, t, h: (g, 0))
    m_spec = pl.BlockSpec((gps * SEQ_BLOCK, LANE), lambda g, t, h: (g, 0))
    in_specs = [blk(U_ML), blk(U_ML + 4), blk(U_ML + 8), blk(U_ML + 12), blk(U_ML + 16),
                pl.BlockSpec((R, LANE), lambda g, t, h: (g * nT + t, U_ML_IF)),
                pl.BlockSpec((1, LANE), lambda g, t, h: (0, 0)),
                pl.BlockSpec((1, DH), lambda g, t, h: (0, h))]
    args = [proj] * 6 + [ifb, ng]
    if init is not None:
        in_specs += [st4, n_spec, m_spec]
        args += list(init)
    aliases = {}
    if prev_c is not None:
        aliases = {len(args): 1}
        in_specs.append(pl.BlockSpec(memory_space=pl.ANY))
        args.append(prev_c)
    kern = functools.partial(_mlstm_kernel, L=L, gps=gps, t_valid=t_valid, li=li, has_init=init is not None,
                             has_prev=prev_c is not None, single_tile=nT == 1)
    return pl.pallas_call(
        kern,
        grid=(G // gps, nT, HEADS),
        in_specs=in_specs,
        out_specs=[pl.BlockSpec((R, DH), lambda g, t, h: (g * nT + t, h)), c_out_spec, n_spec, m_spec],
        out_shape=[jax.ShapeDtypeStruct((n, BR), F32), c_shape,
                   jax.ShapeDtypeStruct((B * HEADS, LANE), F32), jax.ShapeDtypeStruct((B, LANE), F32)],
        scratch_shapes=[pltpu.VMEM((R, LANE), F32), pltpu.VMEM((gps * SEQ_BLOCK, LANE), F32),
                        pltpu.VMEM((gps * SEQ_BLOCK * SEQ_BLOCK, LANE), F32)],
        input_output_aliases=aliases,
        compiler_params=_cparams(("arbitrary", "arbitrary", "arbitrary")),
        name="mlstm",
    )(*args)


def _causal_conv_tile(x_ref, w_ref, b_ref, xp_ref, act_ref, cv0_ref, cv_ref, *, ti, R, width, tv_local,
                      act_fn, carry_ref=None):
    @pl.when(ti == 0)
    def _():
        xp_ref[0:CARRY_ROWS, :] = cv0_ref[...]

    if carry_ref is not None:
        @pl.when(ti > 0)
        def _():
            xp_ref[0:CARRY_ROWS, :] = carry_ref[...]

    xp_ref[CARRY_ROWS:CARRY_ROWS + R, :] = x_ref[...]
    RB = next(rb for rb in (128, 32, SEQ_BLOCK) if R % rb == 0 and rb * min(width, BR) <= 128 * LANE)

    def blk(i, carry):
        r0 = pl.multiple_of(i * RB, SEQ_BLOCK)
        cw = min(width, BR)
        for c0 in range(0, width, cw):
            cs = pl.ds(c0, cw)
            acc = xp_ref[pl.ds(r0, RB), cs] * w_ref[0:1, cs]
            for j in range(1, CONV_W):
                acc = acc + xp_ref[pl.ds(r0 + j * SEQ_BLOCK, RB), cs] * w_ref[j:j + 1, cs]
            if b_ref is not None:
                acc = acc + b_ref[:, cs]
            act_ref[pl.ds(r0, RB), cs] = act_fn(acc)
        return carry

    lax.fori_loop(0, R // RB, blk, 0)

    cv_ref[...] = xp_ref[tv_local * SEQ_BLOCK:tv_local * SEQ_BLOCK + CARRY_ROWS, :]
    if carry_ref is not None:
        carry_ref[...] = xp_ref[R:R + CARRY_ROWS, :]
    else:
        xp_ref[0:CARRY_ROWS, :] = xp_ref[R:R + CARRY_ROWS, :]


def _gdn_kernel(*refs, L, TT, gps, t_valid, nT, li, has_init, has_prev):
    (q_ref, k_ref, v_ref, z_ref, ab_ref, cwq_ref, cwk_ref, cwv_ref, alog_ref, dtb_ref, ng_ref,
     cvq0_ref, cvk0_ref, cvv0_ref) = refs[:14]
    n_in = 14 + (1 if has_init else 0) + (1 if has_prev else 0)
    (y_ref, s_ref, cvq_ref, cvk_ref, cvv_ref,
     xpq_ref, xpk_ref, xpv_ref, aq_ref, ak_ref, av_ref, carry_ref, gb_ref) = refs[n_in:]
    ti = pl.program_id(1)
    h = pl.program_id(2)
    R = TT * SEQ_BLOCK
    s_ref = _own_layer(s_ref, li, has_prev, (ti == 0) & (h == 0))
    single_chunk = nT == 1 and TT == L
    s_src = refs[14] if (has_init and single_chunk) else s_ref

    @pl.when((ti == 0) & (h == 0))
    def _():
        if has_init and single_chunk:
            pass
        elif has_init:
            s_ref[...] = refs[14][...]
        else:
            s_ref[...] = jnp.zeros(s_ref.shape, F32)

    head_lanes = pl.ds(pl.multiple_of(h * DH, DH), DH)
    for j, (x_ref, cw_ref, xp_ref, a_ref, cv0_ref, cv_ref) in enumerate((
            (q_ref, cwq_ref, xpq_ref, aq_ref, cvq0_ref, cvq_ref),
            (k_ref, cwk_ref, xpk_ref, ak_ref, cvk0_ref, cvk_ref),
            (v_ref, cwv_ref, xpv_ref, av_ref, cvv0_ref, cvv_ref))):
        for gi in range(gps):
            tile = pl.ds(gi * R, R)
            cvr = pl.ds(gi * CARRY_ROWS, CARRY_ROWS)
            _causal_conv_tile(x_ref.at[tile], cw_ref, None, xp_ref.at[pl.ds(gi * (R + CARRY_ROWS), R + CARRY_ROWS)],
                              a_ref.at[tile], cv0_ref.at[cvr], cv_ref.at[cvr, head_lanes], ti=ti, R=R, width=DH,
                              tv_local=t_valid - (nT - 1) * TT, act_fn=_silu, carry_ref=carry_ref.at[j, h, cvr])

    @pl.when(h == 0)
    def _():
        tok, tok_in_chunk = _tile_token_ids(ti, TT, L)
        valid = tok < t_valid
        lane = lax.broadcasted_iota(jnp.int32, (R, LANE), 1)
        for gi in range(gps):
            tile = pl.ds(gi * R, R)
            abv = ab_ref[tile, :]
            g_all = -jnp.exp(alog_ref[...]) * _softplus(abv + dtb_ref[...])
            gam = _token_cumsum(jnp.where(valid, g_all, 0.0), tok_in_chunk, L)
            gb_ref[tile, :] = jnp.where(lane < HEADS, gam, jnp.where(valid, _sigmoid(abv), 0.0))

    ri = lax.broadcasted_iota(jnp.int32, (L, L), 0)
    ci = lax.broadcasted_iota(jnp.int32, (L, L), 1)
    causal = ci <= ri
    strict = ci < ri
    eye = ci == ri
    lane = lax.broadcasted_iota(jnp.int32, (L, LANE), 1)
    sel_a = lane == h
    sel_b = lane == HEADS + h
    pick = lambda tile, sel: jnp.sum(jnp.where(sel, tile, 0.0), axis=1, keepdims=True)
    n_double = int(math.log2(L)) - 1

    seqs = range(gps * SEQ_BLOCK)
    for c in range(TT // L):
        rows, st = [], []
        for b in seqs:
            r = pl.ds((b // SEQ_BLOCK) * R + c * L * SEQ_BLOCK + b % SEQ_BLOCK, L, stride=SEQ_BLOCK)
            rows.append(r)
            gbv = gb_ref[r, :]
            gam = pick(gbv, sel_a)
            beta = pick(gbv, sel_b)
            dec = jnp.where(causal, jnp.exp(gam - _row_from_col(gam, eye)), 0.0)
            q = aq_ref[r, :]
            k = ak_ref[r, :]
            q = q * lax.rsqrt(jnp.sum(q * q, axis=-1, keepdims=True) + EPS) * (DH ** -0.5)
            k = k * lax.rsqrt(jnp.sum(k * k, axis=-1, keepdims=True) + EPS)
            eg = jnp.exp(gam)
            g_last = gam[L - 1:L, :]
            kbeta = k * beta
            st.append(dict(dec=dec, q=q, k=k, kbeta=kbeta, eg=eg, g_last=g_last,
                           kd=k * jnp.exp(g_last - gam),
                           rhs=jnp.concatenate([av_ref[r, :] * beta, kbeta * eg], axis=1)))
        for b in seqs:
            s_ = st[b]
            amat = jnp.where(strict, _dot_nt(s_['kbeta'], s_['k']) * s_['dec'], 0.0)
            s_['qk'] = _dot_nt(s_['q'], s_['k']) * s_['dec']
            s_['o'] = _dot(s_['q'] * s_['eg'], s_src[b, h])
            s_['x'] = -amat
            s_['p'] = amat
        for b in seqs:
            st[b]['p'] = _dot(st[b]['p'], st[b]['p'])
        for i in range(n_double):
            for b in seqs:
                s_ = st[b]
                s_['xp'] = _dot(s_['x'], s_['p'])
                if i + 1 < n_double:
                    s_['p2'] = _dot(s_['p'], s_['p'])
            for b in seqs:
                s_ = st[b]
                s_['x'] = s_['x'] + s_['p'] + s_['xp']
                if i + 1 < n_double:
                    s_['p'] = s_['p2']
        for b in seqs:
            s_ = st[b]
            s_['sol'] = s_['rhs'] + _dot(s_['x'], s_['rhs'])
        for b in seqs:
            s_ = st[b]
            s_['v_new'] = s_['sol'][:, :DH] - _dot(s_['sol'][:, DH:], s_src[b, h])
        for b in seqs:
            s_ = st[b]
            o = s_['o'] + _dot(s_['qk'], s_['v_new'])
            s_ref[b, h] = jnp.exp(s_['g_last']) * s_src[b, h] + _dot_tn(s_['kd'], s_['v_new'])
            y_ref[rows[b], :] = _head_rms(o, ng_ref[...]) * _silu(z_ref[rows[b], :])


def _gdn(proj, cw, alog, dtb, ng, cv0, s_init, prev_s, *, li, n_layers, B, G, nT, TT, L, t_valid, gps):
    assert G % gps == 0 and (gps == 1 or nT == 1)
    R = TT * SEQ_BLOCK * gps
    n = proj.shape[0]
    upb = BR // LANE
    blk = lambda u: pl.BlockSpec((R, DH), lambda g, t, h, u=u: (g * nT + t, u + h))
    st4, s_out_spec, s_shape = _state_specs(li, n_layers, B, prev_s is not None, gps)
    cvs = lambda j: pl.BlockSpec((gps * CARRY_ROWS, DH), lambda g, t, h, j=j: (g, j * upb + h))
    cws = lambda j: pl.BlockSpec((CONV_W, DH), lambda g, t, h, j=j: (0, j * upb + h))
    one = pl.BlockSpec((1, LANE), lambda g, t, h: (0, 0))
    cvo = pl.BlockSpec((gps * CARRY_ROWS, BR), lambda g, t, h: (g, 0))
    in_specs = [blk(U_GD_QKV), blk(U_GD_QKV + 4), blk(U_GD_QKV + 8), blk(U_GD_Z),
                pl.BlockSpec((R, LANE), lambda g, t, h: (g * nT + t, U_GD_AB)),
                cws(0), cws(1), cws(2), one, one,
                pl.BlockSpec((1, DH), lambda g, t, h: (0, h)),
                cvs(0), cvs(1), cvs(2)]
    args = [proj] * 5 + [cw, cw, cw, alog, dtb, ng, cv0, cv0, cv0]
    if s_init is not None:
        in_specs.append(st4)
        args.append(s_init)
    aliases = {}
    if prev_s is not None:
        aliases = {len(args): 1}
        in_specs.append(pl.BlockSpec(memory_space=pl.ANY))
        args.append(prev_s)
    kern = functools.partial(_gdn_kernel, L=L, TT=TT, gps=gps, t_valid=t_valid, nT=nT, li=li,
                             has_init=s_init is not None, has_prev=prev_s is not None)
    cv_shape = jax.ShapeDtypeStruct((cv0.shape[0], BR), F32)
    return pl.pallas_call(
        kern,
        grid=(G // gps, nT, HEADS),
        in_specs=in_specs,
        out_specs=[pl.BlockSpec((R, DH), lambda g, t, h: (g * nT + t, h)), s_out_spec, cvo, cvo, cvo],
        out_shape=[jax.ShapeDtypeStruct((n, BR), F32), s_shape, cv_shape, cv_shape, cv_shape],
        scratch_shapes=([pltpu.VMEM((R + gps * CARRY_ROWS, DH), F32)] * 3 + [pltpu.VMEM((R, DH), F32)] * 3
                        + [pltpu.VMEM((3, HEADS, gps * CARRY_ROWS, DH), F32), pltpu.VMEM((R, LANE), F32)]),
        input_output_aliases=aliases,
        compiler_params=_cparams(("arbitrary", "arbitrary", "arbitrary")),
        name="gdn",
    )(*args)


def _s5_prep_kernel(are_ref, aim_ref, ldt_ref, bre_ref, bim_ref, abr_ref, abi_ref, bbr_ref, bbi_ref):
    a_re = are_ref[...]
    a_im = aim_ref[...]
    dt = jnp.exp(ldt_ref[...])
    mag = jnp.exp(dt * a_re)
    ang = dt * a_im
    ab_r = mag * jnp.cos(ang)
    ab_i = mag * jnp.sin(ang)
    den = a_re * a_re + a_im * a_im
    nr = ab_r - 1.0
    ni = ab_i
    f_r = (nr * a_re + ni * a_im) / den
    f_i = (ni * a_re - nr * a_im) / den
    abr_ref[...] = ab_r
    abi_ref[...] = ab_i
    f_r = f_r[0:1, :]
    f_i = f_i[0:1, :]
    bbr_ref[...] = f_r * bre_ref[...] - f_i * bim_ref[...]
    bbi_ref[...] = f_r * bim_ref[...] + f_i * bre_ref[...]


def _s5_prep(a_re, a_im, log_dt, b_re, b_im):
    rep = lambda a: jnp.broadcast_to(a.reshape(1, S5_STATE), (SEQ_BLOCK, S5_STATE))
    ldt = rep(jnp.broadcast_to(log_dt[:, None], (S5_G, S5_P)))
    bt = lambda b: jnp.transpose(b, (2, 0, 1)).reshape(S5_N, S5_STATE)
    shp = lambda r: jax.ShapeDtypeStruct((r, S5_STATE), F32)
    abr, abi, bbr, bbi = pl.pallas_call(
        _s5_prep_kernel,
        out_shape=[shp(SEQ_BLOCK), shp(SEQ_BLOCK), shp(S5_N), shp(S5_N)],
        name="s5_prep",
    )(rep(a_re), rep(a_im), ldt, bt(b_re), bt(b_im))
    return abr, abi, bbr, bbi


def _gelu_tanh(x):
    return 0.5 * x * (1.0 + jnp.tanh(math.sqrt(2.0 / math.pi) * (x + 0.044715 * (x * x * x))))


def _s5_kernel(u_ref, z_ref, abr_ref, abi_ref, wbr_ref, wbi_ref, wcr_ref, wci_ref, d_ref, gw_ref, gb_ref,
               x0r_ref, x0i_ref, y_ref, xr_ref, xi_ref, hr_ref, hi_ref, ys_ref, *, TT, n_steps):
    ti = pl.program_id(1)
    NB = BR // LANE
    SB = S5_STATE // NB

    @pl.when(ti == 0)
    def _():
        xr_ref[...] = x0r_ref[...]
        xi_ref[...] = x0i_ref[...]

    for kb in range(NB):
        ub = u_ref[:, kb * LANE:(kb + 1) * LANE].astype(BF16)
        hr_ref[:, kb * SB:(kb + 1) * SB] = jnp.dot(ub, wbr_ref[kb], preferred_element_type=F32)
        hi_ref[:, kb * SB:(kb + 1) * SB] = jnp.dot(ub, wbi_ref[kb], preferred_element_type=F32)

    for kb in range(NB):
        sl = pl.ds(kb * SB, SB)
        ar = abr_ref[:, sl]
        ai = abi_ref[:, sl]

        def step(t, carry):
            xr, xi = carry
            rows = pl.ds(pl.multiple_of(t * SEQ_BLOCK, SEQ_BLOCK), SEQ_BLOCK)
            nxr = ar * xr - ai * xi + hr_ref[rows, sl]
            nxi = ar * xi + ai * xr + hi_ref[rows, sl]
            hr_ref[rows, sl] = nxr
            hi_ref[rows, sl] = nxi
            return nxr, nxi

        xr, xi = lax.fori_loop(0, n_steps, step, (xr_ref[:, sl], xi_ref[:, sl]))
        xr_ref[:, sl] = xr
        xi_ref[:, sl] = xi

    for kb in range(NB):
        sl = pl.ds(kb * SB, SB)
        cs = pl.ds(kb * LANE, LANE)
        yk = (jnp.dot(hr_ref[:, sl].astype(BF16), wcr_ref[kb], preferred_element_type=F32)
              - jnp.dot(hi_ref[:, sl].astype(BF16), wci_ref[kb], preferred_element_type=F32)
              + d_ref[:, cs] * u_ref[:, cs])
        ys_ref[:, cs] = _gelu_tanh(yk)
    ys = ys_ref[...]
    glu = ys * _sigmoid(jnp.dot(ys.astype(BF16), gw_ref[...], preferred_element_type=F32) + gb_ref[...])
    y_ref[...] = glu * _silu(z_ref[...])


def _s5(proj, abr, abi, wbr, wbi, wcr, wci, d, gw, gb, x0r, x0i, *, G, nT, TT, t_valid):
    assert nT == 1 or t_valid == nT * TT
    R = TT * SEQ_BLOCK
    n = proj.shape[0]
    full = lambda a: pl.BlockSpec(a.shape, lambda g, t, nd=a.ndim: (0,) * nd)
    st = pl.BlockSpec((SEQ_BLOCK, S5_STATE), lambda g, t: (g, 0))
    kern = functools.partial(_s5_kernel, TT=TT, n_steps=t_valid - (nT - 1) * TT)
    return pl.pallas_call(
        kern,
        grid=(G, nT),
        in_specs=[pl.BlockSpec((R, BR), lambda g, t: (g * nT + t, U_S5_U // 4)),
                  pl.BlockSpec((R, BR), lambda g, t: (g * nT + t, U_S5_Z // 4)),
                  full(abr), full(abi), full(wbr), full(wbi), full(wcr), full(wci), full(d), full(gw), full(gb),
                  st, st],
        out_specs=[pl.BlockSpec((R, BR), lambda g, t: (g * nT + t, 0)), st, st],
        out_shape=[jax.ShapeDtypeStruct((n, BR), F32),
                   jax.ShapeDtypeStruct(x0r.shape, F32),
                   jax.ShapeDtypeStruct(x0i.shape, F32)],
        scratch_shapes=[pltpu.VMEM((R, S5_STATE), F32), pltpu.VMEM((R, S5_STATE), F32),
                        pltpu.VMEM((R, BR), F32)],
        compiler_params=_cparams(("arbitrary", "arbitrary")),
        name="s5",
    )(proj, proj, abr, abi, wbr, wbi, wcr, wci, d, gw, gb, x0r, x0i)


def _lru_kernel(x_ref, z_ref, cw_ref, cb_ref, wa_ref, ba_ref, wx_ref, bx_ref, lam_ref, h0_ref, cv0_ref,
                y_ref, h_ref, cv_ref, xp_ref, xl_ref, a_ref, *, TT, n_steps, nT):
    ti = pl.program_id(1)
    R = TT * SEQ_BLOCK

    @pl.when(ti == 0)
    def _():
        h_ref[...] = h0_ref[...]

    _causal_conv_tile(x_ref, cw_ref, cb_ref, xp_ref, xl_ref, cv0_ref, cv_ref, ti=ti, R=R, width=BR,
                      tv_local=n_steps, act_fn=lambda a: a)

    xl = xl_ref[...]
    xb = xl.astype(BF16)
    r = _sigmoid(jnp.dot(xb, wa_ref[...], preferred_element_type=F32) + ba_ref[...])
    i = _sigmoid(jnp.dot(xb, wx_ref[...], preferred_element_type=F32) + bx_ref[...])
    log_a = -LRU_C * r * _softplus(-lam_ref[...])
    a = jnp.exp(log_a)
    a_ref[...] = a
    xl_ref[...] = jnp.sqrt(1.0 - a * a) * (i * xl)

    def step(t, h):
        rows = pl.ds(pl.multiple_of(t * SEQ_BLOCK, SEQ_BLOCK), SEQ_BLOCK)
        hn = a_ref[rows, :] * h + xl_ref[rows, :]
        xl_ref[rows, :] = hn
        return hn

    h_ref[...] = lax.fori_loop(0, n_steps, step, h_ref[...])
    y_ref[...] = xl_ref[...] * _silu(z_ref[...])


def _lru(proj, cw, cb, wa, ba, wx, bx, lam, h0, cv0, *, G, nT, TT, t_valid):
    assert nT == 1 or t_valid == nT * TT
    R = TT * SEQ_BLOCK
    n = proj.shape[0]
    full = lambda a: pl.BlockSpec(a.shape, lambda g, t, nd=a.ndim: (0,) * nd)
    st = pl.BlockSpec((SEQ_BLOCK, BR), lambda g, t: (g, 0))
    cvs = pl.BlockSpec((CARRY_ROWS, BR), lambda g, t: (g, 0))
    kern = functools.partial(_lru_kernel, TT=TT, n_steps=t_valid - (nT - 1) * TT, nT=nT)
    return pl.pallas_call(
        kern,
        grid=(G, nT),
        in_specs=[pl.BlockSpec((R, BR), lambda g, t: (g * nT + t, U_LRU_X // 4)),
                  pl.BlockSpec((R, BR), lambda g, t: (g * nT + t, U_LRU_Z // 4)),
                  full(cw), full(cb), full(wa), full(ba), full(wx), full(bx), full(lam), st, cvs],
        out_specs=[pl.BlockSpec((R, BR), lambda g, t: (g * nT + t, 0)), st, cvs],
        out_shape=[jax.ShapeDtypeStruct((n, BR), F32),
                   jax.ShapeDtypeStruct(h0.shape, F32),
                   jax.ShapeDtypeStruct(cv0.shape, F32)],
        scratch_shapes=[pltpu.VMEM((R + CARRY_ROWS, BR), F32), pltpu.VMEM((R, BR), F32),
                        pltpu.VMEM((R, BR), F32)],
        compiler_params=_cparams(("arbitrary", "arbitrary")),
        name="lru",
    )(proj, proj, cw, cb, wa, ba, wx, bx, lam, h0, cv0)


def _merge_kernel(yml_ref, ygd_ref, ys5_ref, ylru_ref, gates_ref, x_ref, p_ref,
                  wbr_ref, wout_ref, wpg_ref, wple_ref, pg_ref, o_ref, *, p_btd, out_btd):
    tm = x_ref.shape[0]
    if p_btd:
        p = jnp.swapaxes(p_ref[...], 0, 1).reshape(tm, PLE_DIM)
    else:
        p = p_ref[...]
    merged = None
    for nb, y_ref in enumerate((yml_ref, ygd_ref, ys5_ref, ylru_ref)):
        pbr = jnp.dot(y_ref[...].astype(BF16), wbr_ref[nb], preferred_element_type=F32)
        term = _sigmoid(gates_ref[:, nb * D_MODEL:(nb + 1) * D_MODEL]) * pbr
        merged = term if merged is None else merged + term
    out = jnp.dot(merged.astype(BF16), wout_ref[...], preferred_element_type=F32)
    r = x_ref[...] + out * lax.rsqrt(jnp.mean(out * out, axis=-1, keepdims=True) + EPS) * pg_ref[...]
    gate = _sigmoid(jnp.dot(r.astype(BF16), wpg_ref[...], preferred_element_type=F32))
    res = r + gate * jnp.dot(p.astype(BF16), wple_ref[...], preferred_element_type=F32)
    if out_btd:
        o_ref[...] = jnp.swapaxes(res.reshape(tm // SEQ_BLOCK, SEQ_BLOCK, D_MODEL), 0, 1)
    else:
        o_ref[...] = res


def _merge(ys, proj, x, p, wbr, wout, wpg, wple, pg, tm, *, p_layer=None, out_btd=False):
    n = x.shape[0]
    tt = tm // SEQ_BLOCK
    row = lambda c: pl.BlockSpec((tm, c), lambda i: (i, 0))
    full = lambda a: pl.BlockSpec(a.shape, lambda i, nd=a.ndim: (0,) * nd, pipeline_mode=pl.Buffered(1))
    p_spec = row(PLE_DIM) if p_layer is None else pl.BlockSpec((None, SEQ_BLOCK, tt, PLE_DIM),
                                                               lambda i: (p_layer, 0, i, 0))
    if out_btd:
        out_spec = pl.BlockSpec((SEQ_BLOCK, tt, D_MODEL), lambda i: (0, i, 0))
        out_shape = jax.ShapeDtypeStruct((SEQ_BLOCK, n // SEQ_BLOCK, D_MODEL), F32)
    else:
        out_spec, out_shape = row(D_MODEL), jax.ShapeDtypeStruct((n, D_MODEL), F32)
    return pl.pallas_call(
        functools.partial(_merge_kernel, p_btd=p_layer is not None, out_btd=out_btd),
        grid=(n // tm,),
        in_specs=[row(BR), row(BR), row(BR), row(BR),
                  pl.BlockSpec((tm, 4 * D_MODEL), lambda i: (i, U_GATES // 32)),
                  row(D_MODEL), p_spec, full(wbr), full(wout), full(wpg), full(wple), full(pg)],
        out_specs=out_spec,
        out_shape=out_shape,
        compiler_params=_cparams(("parallel",)),
        name="merge",
    )(*ys, proj, x, p, wbr, wout, wpg, wple, pg)


def _permute_w_in(w_in):
    sizes = (BR, BR, BR, HEADS, HEADS, BR, BR, 3 * BR, HEADS, HEADS, BR, BR, BR, BR, BR, 4 * D_MODEL)
    offs = [0]
    for s in sizes:
        offs.append(offs[-1] + s)
    seg = lambda i: w_in[:, :, offs[i]:offs[i + 1]]
    (ml_q, ml_k, ml_v, ml_i, ml_f, ml_o, ml_z, gd_qkv, gd_a, gd_b, gd_z,
     s5_u, s5_z, lru_x, lru_z, gates) = [seg(i) for i in range(len(sizes))]
    zeros = lambda c: jnp.zeros(w_in.shape[:2] + (c,), w_in.dtype)
    cols = [gd_qkv, ml_q, ml_k, ml_v, ml_o, ml_z, gates, gd_z, s5_u, s5_z, lru_x, lru_z,
            ml_i, ml_f, zeros(LANE - 2 * HEADS), gd_a, gd_b, zeros(LANE - 2 * HEADS),
            zeros((N_UNITS - U_GD_AB - 1) * LANE)]
    return jnp.concatenate(cols, axis=-1).astype(BF16)


def _pad_lanes(*vecs):
    v = jnp.concatenate(vecs)
    return jnp.pad(v, (0, LANE - v.shape[0])).reshape(1, LANE)


def _block_diag(blocks):
    n, a, b = blocks.shape
    eye = jnp.eye(n, dtype=blocks.dtype)
    return jnp.einsum('ij,iab->iajb', eye, blocks).reshape(n * a, n * b)


def _to_rows(x, G):
    B, T, C = x.shape
    return x.reshape(G, SEQ_BLOCK, T, C).transpose(0, 2, 1, 3).reshape(G * T * SEQ_BLOCK, C)


def _from_rows(y, G, T):
    C = y.shape[-1]
    return y.reshape(G, T, SEQ_BLOCK, C).transpose(0, 2, 1, 3).reshape(G * SEQ_BLOCK, T, C)


def _group(x, p, states, params, *, T, t_valid, TT, L, tm_proj, tm_merge, gps=1):
    B = x.shape[0]
    G = B // SEQ_BLOCK
    nT = T // TT
    depth = p.shape[0]
    xr = _to_rows(x, G)
    tm_proj = min(tm_proj, xr.shape[0])
    tm_merge = min(tm_merge, xr.shape[0])
    new_states = []
    c_all = s_all = None
    gps = math.gcd(gps, G)
    mat = dict(n_layers=depth, B=B, G=G, nT=nT, TT=TT, L=L, t_valid=t_valid, gps=gps)
    for li in range(depth):
        lp = {k: v[li] for k, v in params.items() if k != 'w_in_p'}
        if states is None:
            ml_init = s_init = None
            gcv0 = jnp.zeros((G * CARRY_ROWS, 3 * BR), F32)
            x0r = jnp.zeros((B, S5_STATE), F32)
            x0i = jnp.zeros((B, S5_STATE), F32)
            h0 = jnp.zeros((B, BR), F32)
            lcv0 = jnp.zeros((G * CARRY_ROWS, BR), F32)
        else:
            (n0, m0, gcv0, x0r, x0i, h0, lcv0) = [states[j][li] for j in (1, 2, 4, 5, 6, 7, 8)]
            n0 = n0.reshape(G, SEQ_BLOCK, HEADS, DH).transpose(0, 2, 1, 3).reshape(B * HEADS, DH)
            ml_init = (states[0], n0, jnp.pad(m0, ((0, 0), (0, LANE - HEADS))))
            s_init = states[3]
            gcv0 = _to_rows(gcv0, G)
            x0r = x0r.reshape(B, S5_STATE)
            x0i = x0i.reshape(B, S5_STATE)
            lcv0 = _to_rows(lcv0, G)
        in_place = G == 1

        proj = _in_proj(xr, lp['prenorm_g'], params['w_in_p'], li, tm_proj, PROJ_COLS // PROJ_COL_TILES)

        y_ml, c_all, n1, m1 = _mlstm(proj, lp['ml_ifb'], lp['ml_norm_g'], ml_init, c_all, li=li, **mat)
        y_gd, s_all, cvq, cvk, cvv = _gdn(proj, lp['gd_conv_w'], lp['gd_alog'], lp['gd_dtb'], lp['gd_norm_g'],
                                          gcv0, s_init, s_all, li=li, **dict(mat, L=min(L, GDN_CHUNK)))
        gcv1 = jnp.concatenate([cvq, cvk, cvv], axis=-1)
        y_s5, x1r, x1i = _s5(proj, lp['s5_abr'], lp['s5_abi'], lp['s5_wbr'], lp['s5_wbi'], lp['s5_wcr'],
                             lp['s5_wci'], lp['s5_d'], lp['s5_glu_w'], lp['s5_glu_b'], x0r, x0i,
                             G=G, nT=nT, TT=TT, t_valid=t_valid)
        y_lru, h1, lcv1 = _lru(proj, lp['lru_conv_w'], lp['lru_conv_b'], lp['lru_wa'], lp['lru_ba'],
                               lp['lru_wx'], lp['lru_bx'], lp['lru_lam'], h0, lcv0,
                               G=G, nT=nT, TT=TT, t_valid=t_valid)
        xr = _merge((y_ml, y_gd, y_s5, y_lru), proj, xr, p if in_place else _to_rows(p[li], G),
                    lp['w_branch'], lp['w_out'], lp['w_ple_gate'], lp['w_ple'], lp['postnorm_g'], tm_merge,
                    p_layer=li if in_place else None, out_btd=in_place and li == depth - 1)

        n1 = n1.reshape(G, HEADS, SEQ_BLOCK, DH).transpose(0, 2, 1, 3)
        new_states.append((n1.reshape(B, HEADS, DH), m1[:, :HEADS],
                           _from_rows(gcv1, G, CONV_W - 1), x1r.reshape(B, S5_G, S5_P),
                           x1i.reshape(B, S5_G, S5_P), h1, _from_rows(lcv1, G, CONV_W - 1)))
    y = xr if G == 1 else _from_rows(xr, G, T)
    n_s, m_s, gcv_s, xr_s, xi_s, h_s, lcv_s = (jnp.stack([ns[j] for ns in new_states]) for j in range(7))
    return y, (c_all, n_s, m_s, s_all, gcv_s, xr_s, xi_s, h_s, lcv_s)


def _prepare_params(prenorm_g, postnorm_g, w_in, ml_bi, ml_bf, ml_norm_g, gd_conv_w, gd_a_log, gd_dt_bias,
                    gd_norm_g, s5_a_re, s5_a_im, s5_log_dt, s5_b_re, s5_b_im, s5_c_re, s5_c_im, s5_d,
                    s5_glu_w, s5_glu_b, lru_conv_w, lru_conv_b, lru_wa, lru_ba, lru_wx, lru_bx, lru_lam,
                    w_branch, w_out, w_ple, w_ple_gate):
    depth = w_in.shape[0]
    row = lambda a: a.reshape(depth, 1, -1)
    per_layer = lambda f, *a: jnp.stack([f(*[x[i] for x in a]) for i in range(depth)])
    prm = dict(
        prenorm_g=row(prenorm_g), postnorm_g=row(postnorm_g), w_in_p=_permute_w_in(w_in),
        ml_ifb=per_layer(_pad_lanes, ml_bi, ml_bf), ml_norm_g=row(ml_norm_g),
        gd_conv_w=gd_conv_w, gd_norm_g=row(gd_norm_g),
        gd_alog=per_layer(lambda a: _pad_lanes(a), gd_a_log),
        gd_dtb=per_layer(lambda a: _pad_lanes(a), gd_dt_bias),
        s5_d=row(s5_d), s5_glu_w=s5_glu_w.astype(BF16), s5_glu_b=row(s5_glu_b),
        lru_conv_w=lru_conv_w, lru_conv_b=row(lru_conv_b),
        lru_wa=per_layer(_block_diag, lru_wa).astype(BF16), lru_ba=row(lru_ba),
        lru_wx=per_layer(_block_diag, lru_wx).astype(BF16), lru_bx=row(lru_bx), lru_lam=row(lru_lam),
        w_branch=w_branch.astype(BF16), w_out=w_out.astype(BF16), w_ple=w_ple.astype(BF16),
        w_ple_gate=w_ple_gate.astype(BF16),
    )
    abr, abi, wbr, wbi, wcr, wci = [], [], [], [], [], []
    nb = BR // LANE
    gpb = S5_G // nb
    for i in range(depth):
        a_r, a_i, bbr, bbi = _s5_prep(s5_a_re[i], s5_a_im[i], s5_log_dt[i], s5_b_re[i], s5_b_im[i])
        abr.append(a_r)
        abi.append(a_i)
        in_blocks = lambda bb: jnp.stack([_block_diag(bb.reshape(S5_N, S5_G, S5_P).transpose(1, 0, 2)
                                                      [k * gpb:(k + 1) * gpb]) for k in range(nb)])
        out_blocks = lambda c: jnp.stack([_block_diag(jnp.transpose(c, (0, 2, 1))[k * gpb:(k + 1) * gpb])
                                          for k in range(nb)])
        wbr.append(in_blocks(bbr).astype(BF16))
        wbi.append(in_blocks(bbi).astype(BF16))
        wcr.append(out_blocks(s5_c_re[i]).astype(BF16))
        wci.append(out_blocks(s5_c_im[i]).astype(BF16))
    prm.update(s5_abr=jnp.stack(abr), s5_abi=jnp.stack(abi), s5_wbr=jnp.stack(wbr), s5_wbi=jnp.stack(wbi),
               s5_wcr=jnp.stack(wcr), s5_wci=jnp.stack(wci))
    return prm


def kernel(x_prompt, x_sample, state_mlstm_c, state_mlstm_n, state_mlstm_m, state_gdn_s, state_gdn_conv, state_s5_re, state_s5_im, state_lru_h, state_lru_conv, p_prompt, p_sample, prenorm_g, postnorm_g, w_in, ml_bi, ml_bf, ml_norm_g, gd_conv_w, gd_a_log, gd_dt_bias, gd_norm_g, s5_a_re, s5_a_im, s5_log_dt, s5_b_re, s5_b_im, s5_c_re, s5_c_im, s5_d, s5_glu_w, s5_glu_b, lru_conv_w, lru_conv_b, lru_wa, lru_ba, lru_wx, lru_bx, lru_lam, w_branch, w_out, w_ple, w_ple_gate):
    prm = _prepare_params(prenorm_g, postnorm_g, w_in, ml_bi, ml_bf, ml_norm_g, gd_conv_w, gd_a_log,
                          gd_dt_bias, gd_norm_g, s5_a_re, s5_a_im, s5_log_dt, s5_b_re, s5_b_im, s5_c_re,
                          s5_c_im, s5_d, s5_glu_w, s5_glu_b, lru_conv_w, lru_conv_b, lru_wa, lru_ba, lru_wx,
                          lru_bx, lru_lam, w_branch, w_out, w_ple, w_ple_gate)

    t_p = x_prompt.shape[1]
    tt_p = math.gcd(t_p, PROMPT_CHUNK)
    y_prompt, pr = _group(x_prompt, p_prompt, None, prm, T=t_p, t_valid=t_p, TT=tt_p, L=tt_p,
                          tm_proj=TM_PROJ, tm_merge=TM_MERGE)

    t_s = x_sample.shape[1]
    t_pad = -(-t_s // SEQ_BLOCK) * SEQ_BLOCK
    pad_t = lambda a, ax: jnp.pad(a, [(0, t_pad - t_s) if i == ax else (0, 0) for i in range(a.ndim)])
    sample_states = (state_mlstm_c, state_mlstm_n, state_mlstm_m, state_gdn_s, state_gdn_conv,
                     state_s5_re, state_s5_im, state_lru_h, state_lru_conv)
    y_s, sa = _group(pad_t(x_sample, 1), pad_t(p_sample, 2), sample_states, prm, T=t_pad, t_valid=t_s,
                     TT=t_pad, L=t_pad, tm_proj=TM_PROJ, tm_merge=TM_MERGE, gps=SAMPLE_GROUPS_PER_STEP)
    y_sample = y_s[:, :t_s]
    return (y_prompt, y_sample) + pr + sa
```

```python
import functools
import math

import jax
import jax.numpy as jnp
from jax import lax
from jax.experimental import pallas as pl
from jax.experimental.pallas import tpu as pltpu

F32 = jnp.float32
BF16 = jnp.bfloat16

D_MODEL = 1024
BR = 512
HEADS = 4
DH = 128
CONV_W = 4
S5_G = 32
S5_N = 16
S5_P = 64
S5_STATE = S5_G * S5_P
LRU_BLOCKS = 8
LRU_BD = 64
LRU_C = 8.0
PLE_DIM = 256
EPS = 1e-6
NEG = -1e30

LANE = 128
SEQ_BLOCK = 8
CARRY_ROWS = (CONV_W - 1) * SEQ_BLOCK
PROMPT_CHUNK = 128
GDN_CHUNK = 64
SAMPLE_GROUPS_PER_STEP = 2
TM_PROJ = 512
TM_MERGE = 512

U_GD_QKV = 0
U_ML = 12
U_GATES = 32
U_GD_Z = 64
U_S5_U = 68
U_S5_Z = 72
U_LRU_X = 76
U_LRU_Z = 80
U_ML_IF = 84
U_GD_AB = 85
N_UNITS = 86
PROJ_COLS = N_UNITS * LANE
PROJ_COL_TILES = 2

VMEM_LIMIT = 56 * 1024 * 1024


def _cparams(sem):
    return pltpu.CompilerParams(dimension_semantics=sem, vmem_limit_bytes=VMEM_LIMIT)


def _sigmoid(x):
    return 0.5 * jnp.tanh(0.5 * x) + 0.5


def _silu(x):
    return x * _sigmoid(x)


def _softplus(x):
    return jnp.maximum(x, 0.0) + jnp.log1p(jnp.exp(-jnp.abs(x)))


def _dot(a, b):
    return jnp.dot(a.astype(BF16), b.astype(BF16), preferred_element_type=F32)


def _dot_nt(a, b):
    return lax.dot_general(a.astype(BF16), b.astype(BF16), (((1,), (1,)), ((), ())),
                           preferred_element_type=F32)


def _dot_tn(a, b):
    return lax.dot_general(a.astype(BF16), b.astype(BF16), (((0,), (0,)), ((), ())),
                           preferred_element_type=F32)


def _cumsum_rows(x):
    n = x.shape[0]
    ri = lax.broadcasted_iota(jnp.int32, x.shape, 0)
    s = 1
    while s < n:
        x = x + jnp.where(ri >= s, pltpu.roll(x, s, axis=0), 0.0)
        s *= 2
    return x


def _row_from_col(col, eye):
    return jnp.sum(jnp.where(eye, col, 0.0), axis=0, keepdims=True)


def _head_rms(h, g):
    return h * lax.rsqrt(jnp.mean(h * h, axis=-1, keepdims=True) + EPS) * g


def _in_proj_kernel(x_ref, g_ref, w_ref, o_ref):
    x = x_ref[...]
    h = x * lax.rsqrt(jnp.mean(x * x, axis=-1, keepdims=True) + EPS) * g_ref[...]
    o_ref[...] = jnp.dot(h.astype(BF16), w_ref[...], preferred_element_type=F32)


def _in_proj(x, g, w_all, li, tm, tn):
    n = x.shape[0]
    return pl.pallas_call(
        _in_proj_kernel,
        grid=(PROJ_COLS // tn, n // tm),
        in_specs=[pl.BlockSpec((tm, D_MODEL), lambda j, i: (i, 0)),
                  pl.BlockSpec((1, D_MODEL), lambda j, i: (0, 0)),
                  pl.BlockSpec((None, D_MODEL, tn), lambda j, i: (li, 0, j))],
        out_specs=pl.BlockSpec((tm, tn), lambda j, i: (i, j)),
        out_shape=jax.ShapeDtypeStruct((n, PROJ_COLS), F32),
        compiler_params=_cparams(("parallel", "parallel")),
        name="in_proj",
    )(x, g, w_all)


def _token_cumsum(x, tok_in_chunk, L):
    s = 1
    while s < L:
        x = x + jnp.where(tok_in_chunk >= s, pltpu.roll(x, s * SEQ_BLOCK, axis=0), 0.0)
        s *= 2
    return x


def _tile_token_ids(ti, TT, L):
    R = TT * SEQ_BLOCK
    t_local = lax.broadcasted_iota(jnp.int32, (R, LANE), 0) // SEQ_BLOCK
    return ti * TT + t_local, t_local % L


def _token_scan(x, op, fill):
    s = 1
    while s < x.shape[0]:
        shifted = jnp.concatenate([jnp.full((s,) + x.shape[1:], fill, x.dtype), x[:-s]], axis=0)
        x = op(x, shifted)
        s *= 2
    return x


ML_R, ML_M, ML_INTER, ML_ENEG, ML_WS = (j * HEADS for j in range(5))


def _mlstm_gate_tile(if_ref, ifb_ref, m_ref, sc_ref, gt_ref, *, ti, T, t_valid):
    shape = (T, SEQ_BLOCK, LANE)
    lane = lax.broadcasted_iota(jnp.int32, shape, 2)
    valid = (ti * T + lax.broadcasted_iota(jnp.int32, shape, 0)) < t_valid
    rot = lambda x, k: pltpu.roll(x.reshape(T * SEQ_BLOCK, LANE), k, axis=1).reshape(shape)
    ifv = (if_ref[...] + ifb_ref[...]).reshape(shape)
    logf = jnp.minimum(ifv, 0.0) - jnp.log1p(jnp.exp(-jnp.abs(ifv)))
    bcum = rot(_token_scan(jnp.where(valid, logf, 0.0), jnp.add, 0.0), LANE - HEADS)
    ig = jnp.where(valid, ifv, NEG)
    r = ig - bcum
    m_prev = m_ref[...]
    big_m = jnp.maximum(m_prev[None], _token_scan(r, jnp.maximum, NEG))
    inter = jnp.exp(m_prev[None] - big_m)
    eneg = jnp.exp(-(bcum + big_m))
    b_last = bcum[T - 1]
    g = b_last[None] - bcum + ig
    m_new = jnp.maximum(b_last + m_prev, jnp.max(g, axis=0))
    ws = jnp.exp(g - m_new[None])
    head_lanes = lane[0] < HEADS
    sc_ref[...] = jnp.where(head_lanes, jnp.exp(b_last + m_prev - m_new), 0.0)
    m_ref[...] = jnp.where(head_lanes, m_new, 0.0)
    packed = jnp.where(lane < ML_M, r,
                       jnp.where(lane < ML_INTER, rot(big_m, ML_M),
                                 jnp.where(lane < ML_ENEG, rot(inter, ML_INTER),
                                           jnp.where(lane < ML_WS, rot(eneg, ML_ENEG), rot(ws, ML_WS)))))
    gt_ref[...] = packed.reshape(T * SEQ_BLOCK, LANE)


def _own_layer(state_ref, li, has_prev, first_step):
    if has_prev:
        return state_ref

    @pl.when(first_step)
    def _():
        for other in range(state_ref.shape[0]):
            if other != li:
                state_ref[other] = jnp.zeros(state_ref.shape[1:], F32)

    return state_ref.at[li]


def _mlstm_kernel(*refs, L, gps, t_valid, li, has_init, has_prev, single_tile):
    q_ref, k_ref, v_ref, o_ref, z_ref, if_ref, ifb_ref, ng_ref = refs[:8]
    n_in = 8 + (3 if has_init else 0) + (1 if has_prev else 0)
    y_ref, c_ref, n_ref, m_ref, gt_ref, sc_ref, rt_ref = refs[n_in:]
    ti = pl.program_id(1)
    h = pl.program_id(2)
    R = L * SEQ_BLOCK
    seqs = range(gps * SEQ_BLOCK)
    rows = [pl.ds((j // SEQ_BLOCK) * R + j % SEQ_BLOCK, L, stride=SEQ_BLOCK) for j in seqs]
    n_rows = [pl.ds((j // SEQ_BLOCK) * SEQ_BLOCK * HEADS + h * SEQ_BLOCK + j % SEQ_BLOCK, 1) for j in seqs]
    use_transpose = L == LANE
    c_ref = _own_layer(c_ref, li, has_prev, (ti == 0) & (h == 0))
    c_src = refs[8] if (has_init and single_tile) else c_ref

    @pl.when((ti == 0) & (h == 0))
    def _():
        if has_init:
            if not single_tile:
                c_ref[...] = refs[8][...]
            n_ref[...] = refs[9][...]
            m_ref[...] = refs[10][...]
        else:
            c_ref[...] = jnp.zeros(c_ref.shape, F32)
            n_ref[...] = jnp.zeros(n_ref.shape, F32)
            m_ref[...] = jnp.zeros(m_ref.shape, F32)

    @pl.when(h == 0)
    def _():
        for gi in range(gps):
            grp = pl.ds(gi * SEQ_BLOCK, SEQ_BLOCK)
            tile = pl.ds(gi * R, R)
            _mlstm_gate_tile(if_ref.at[tile], ifb_ref, m_ref.at[grp], sc_ref.at[grp], gt_ref.at[tile],
                             ti=ti, T=L, t_valid=t_valid)
        if use_transpose:
            for b in seqs:
                rt_ref[b * SEQ_BLOCK:(b + 1) * SEQ_BLOCK, :] = gt_ref[rows[b], :].T[:SEQ_BLOCK, :]

    ri = lax.broadcasted_iota(jnp.int32, (L, L), 0)
    ci = lax.broadcasted_iota(jnp.int32, (L, L), 1)
    causal = ci <= ri
    eye = ci == ri
    to_lane0 = jnp.where(h == 0, 0, LANE - h)
    sc_all = pltpu.roll(sc_ref[...], to_lane0, axis=1)
    col = lambda tile, off: tile[:, off:off + 1]

    st = []
    for b in seqs:
        gq = pltpu.roll(gt_ref[rows[b], :], to_lane0, axis=1)
        q = q_ref[rows[b], :] * (DH ** -0.5)
        v = v_ref[rows[b], :]
        kw = k_ref[rows[b], :] * col(gq, ML_WS)
        cmat = c_src[b, h]
        sc = sc_all[b:b + 1, 0:1]
        st.append(dict(gq=gq, q=q, v=v, kw=kw, sc=sc, qk=_dot_nt(q, k_ref[rows[b], :]), qc=_dot(q, cmat)))
        c_ref[b, h] = sc * cmat + _dot_tn(kw, v)
    for b in seqs:
        s_ = st[b]
        if use_transpose:
            r_row = rt_ref[pl.ds(b * SEQ_BLOCK + h, 1), :]
        else:
            r_row = _row_from_col(col(s_['gq'], ML_R), eye)
        s_['s'] = s_['qk'] * jnp.where(causal, jnp.exp(r_row - col(s_['gq'], ML_M)), 0.0)
    for b in seqs:
        s_ = st[b]
        srow = n_rows[b]
        nvec = n_ref[srow, :]
        inter = col(s_['gq'], ML_INTER)
        s_['num'] = _dot(s_['s'], s_['v']) + inter * s_['qc']
        den = (jnp.sum(s_['s'], axis=1, keepdims=True)
               + inter * jnp.sum(s_['q'] * nvec, axis=1, keepdims=True))
        s_['den'] = jnp.maximum(jnp.abs(den), col(s_['gq'], ML_ENEG))
        n_ref[srow, :] = s_['sc'] * nvec + jnp.sum(s_['kw'], axis=0, keepdims=True)
    for b in seqs:
        s_ = st[b]
        yn = _head_rms(s_['num'] / s_['den'], ng_ref[...])
        y_ref[rows[b], :] = yn * _sigmoid(o_ref[rows[b], :]) * _silu(z_ref[rows[b], :])


def _state_specs(li, n_layers, B, has_prev, gps):
    nseq = gps * SEQ_BLOCK
    one = pl.BlockSpec((None, nseq, HEADS, DH, DH), lambda g, t, h: (li, g, 0, 0, 0))
    every = pl.BlockSpec((n_layers, nseq, HEADS, DH, DH), lambda g, t, h: (0, g, 0, 0, 0))
    shape = jax.ShapeDtypeStruct((n_layers, B, HEADS, DH, DH), F32)
    return one, (one if has_prev else every), shape


def _mlstm(proj, ifb, ng, init, prev_c, *, li, n_layers, B, G, nT, TT, L, t_valid, gps):
    assert TT == L and G % gps == 0 and (gps == 1 or nT == 1)
    R = TT * SEQ_BLOCK * gps
    n = proj.shape[0]
    blk = lambda u: pl.BlockSpec((R, DH), lambda g, t, h, u=u: (g * nT + t, u + h))
    st4, c_out_spec, c_shape = _state_specs(li, n_layers, B, prev_c is not None, gps)
    n_spec = pl.BlockSpec((gps * SEQ_BLOCK * HEADS, LANE), lambda g, t, h: (g, 0))
    m_spec = pl.BlockSpec((gps * SEQ_BLOCK, LANE), lambda g, t, h: (g, 0))
    in_specs = [blk(U_ML), blk(U_ML + 4), blk(U_ML + 8), blk(U_ML + 12), blk(U_ML + 16),
                pl.BlockSpec((R, LANE), lambda g, t, h: (g * nT + t, U_ML_IF)),
                pl.BlockSpec((1, LANE), lambda g, t, h: (0, 0)),
                pl.BlockSpec((1, DH), lambda g, t, h: (0, h))]
    args = [proj] * 6 + [ifb, ng]
    if init is not None:
        in_specs += [st4, n_spec, m_spec]
        args += list(init)
    aliases = {}
    if prev_c is not None:
        aliases = {len(args): 1}
        in_specs.append(pl.BlockSpec(memory_space=pl.ANY))
        args.append(prev_c)
    kern = functools.partial(_mlstm_kernel, L=L, gps=gps, t_valid=t_valid, li=li, has_init=init is not None,
                             has_prev=prev_c is not None, single_tile=nT == 1)
    return pl.pallas_call(
        kern,
        grid=(G // gps, nT, HEADS),
        in_specs=in_specs,
        out_specs=[pl.BlockSpec((R, DH), lambda g, t, h: (g * nT + t, h)), c_out_spec, n_spec, m_spec],
        out_shape=[jax.ShapeDtypeStruct((n, BR), F32), c_shape,
                   jax.ShapeDtypeStruct((B * HEADS, LANE), F32), jax.ShapeDtypeStruct((B, LANE), F32)],
        scratch_shapes=[pltpu.VMEM((R, LANE), F32), pltpu.VMEM((gps * SEQ_BLOCK, LANE), F32),
                        pltpu.VMEM((gps * SEQ_BLOCK * SEQ_BLOCK, LANE), F32)],
        input_output_aliases=aliases,
        compiler_params=_cparams(("arbitrary", "arbitrary", "arbitrary")),
        name="mlstm",
    )(*args)


def _causal_conv_tile(x_ref, w_ref, b_ref, xp_ref, act_ref, cv0_ref, cv_ref, *, ti, R, width, tv_local,
                      act_fn, carry_ref=None):
    @pl.when(ti == 0)
    def _():
        xp_ref[0:CARRY_ROWS, :] = cv0_ref[...]

    if carry_ref is not None:
        @pl.when(ti > 0)
        def _():
            xp_ref[0:CARRY_ROWS, :] = carry_ref[...]

    xp_ref[CARRY_ROWS:CARRY_ROWS + R, :] = x_ref[...]
    RB = next(rb for rb in (128, 32, SEQ_BLOCK) if R % rb == 0 and rb * min(width, BR) <= 128 * LANE)

    def blk(i, carry):
        r0 = pl.multiple_of(i * RB, SEQ_BLOCK)
        cw = min(width, BR)
        for c0 in range(0, width, cw):
            cs = pl.ds(c0, cw)
            acc = xp_ref[pl.ds(r0, RB), cs] * w_ref[0:1, cs]
            for j in range(1, CONV_W):
                acc = acc + xp_ref[pl.ds(r0 + j * SEQ_BLOCK, RB), cs] * w_ref[j:j + 1, cs]
            if b_ref is not None:
                acc = acc + b_ref[:, cs]
            act_ref[pl.ds(r0, RB), cs] = act_fn(acc)
        return carry

    lax.fori_loop(0, R // RB, blk, 0)

    cv_ref[...] = xp_ref[tv_local * SEQ_BLOCK:tv_local * SEQ_BLOCK + CARRY_ROWS, :]
    if carry_ref is not None:
        carry_ref[...] = xp_ref[R:R + CARRY_ROWS, :]
    else:
        xp_ref[0:CARRY_ROWS, :] = xp_ref[R:R + CARRY_ROWS, :]


def _gdn_kernel(*refs, L, TT, gps, t_valid, nT, li, has_init, has_prev):
    (q_ref, k_ref, v_ref, z_ref, ab_ref, cwq_ref, cwk_ref, cwv_ref, alog_ref, dtb_ref, ng_ref,
     cvq0_ref, cvk0_ref, cvv0_ref) = refs[:14]
    n_in = 14 + (1 if has_init else 0) + (1 if has_prev else 0)
    (y_ref, s_ref, cvq_ref, cvk_ref, cvv_ref,
     xpq_ref, xpk_ref, xpv_ref, aq_ref, ak_ref, av_ref, carry_ref, gb_ref) = refs[n_in:]
    ti = pl.program_id(1)
    h = pl.program_id(2)
    R = TT * SEQ_BLOCK
    s_ref = _own_layer(s_ref, li, has_prev, (ti == 0) & (h == 0))
    single_chunk = nT == 1 and TT == L
    s_src = refs[14] if (has_init and single_chunk) else s_ref

    @pl.when((ti == 0) & (h == 0))
    def _():
        if has_init and single_chunk:
            pass
        elif has_init:
            s_ref[...] = refs[14][...]
        else:
            s_ref[...] = jnp.zeros(s_ref.shape, F32)

    head_lanes = pl.ds(pl.multiple_of(h * DH, DH), DH)
    for j, (x_ref, cw_ref, xp_ref, a_ref, cv0_ref, cv_ref) in enumerate((
            (q_ref, cwq_ref, xpq_ref, aq_ref, cvq0_ref, cvq_ref),
            (k_ref, cwk_ref, xpk_ref, ak_ref, cvk0_ref, cvk_ref),
            (v_ref, cwv_ref, xpv_ref, av_ref, cvv0_ref, cvv_ref))):
        for gi in range(gps):
            tile = pl.ds(gi * R, R)
            cvr = pl.ds(gi * CARRY_ROWS, CARRY_ROWS)
            _causal_conv_tile(x_ref.at[tile], cw_ref, None, xp_ref.at[pl.ds(gi * (R + CARRY_ROWS), R + CARRY_ROWS)],
                              a_ref.at[tile], cv0_ref.at[cvr], cv_ref.at[cvr, head_lanes], ti=ti, R=R, width=DH,
                              tv_local=t_valid - (nT - 1) * TT, act_fn=_silu, carry_ref=carry_ref.at[j, h, cvr])

    @pl.when(h == 0)
    def _():
        tok, tok_in_chunk = _tile_token_ids(ti, TT, L)
        valid = tok < t_valid
        lane = lax.broadcasted_iota(jnp.int32, (R, LANE), 1)
        for gi in range(gps):
            tile = pl.ds(gi * R, R)
            abv = ab_ref[tile, :]
            g_all = -jnp.exp(alog_ref[...]) * _softplus(abv + dtb_ref[...])
            gam = _token_cumsum(jnp.where(valid, g_all, 0.0), tok_in_chunk, L)
            gb_ref[tile, :] = jnp.where(lane < HEADS, gam, jnp.where(valid, _sigmoid(abv), 0.0))

    ri = lax.broadcasted_iota(jnp.int32, (L, L), 0)
    ci = lax.broadcasted_iota(jnp.int32, (L, L), 1)
    causal = ci <= ri
    strict = ci < ri
    eye = ci == ri
    lane = lax.broadcasted_iota(jnp.int32, (L, LANE), 1)
    sel_a = lane == h
    sel_b = lane == HEADS + h
    pick = lambda tile, sel: jnp.sum(jnp.where(sel, tile, 0.0), axis=1, keepdims=True)
    n_double = int(math.log2(L)) - 1

    seqs = range(gps * SEQ_BLOCK)
    for c in range(TT // L):
        rows, st = [], []
        for b in seqs:
            r = pl.ds((b // SEQ_BLOCK) * R + c * L * SEQ_BLOCK + b % SEQ_BLOCK, L, stride=SEQ_BLOCK)
            rows.append(r)
            gbv = gb_ref[r, :]
            gam = pick(gbv, sel_a)
            beta = pick(gbv, sel_b)
            dec = jnp.where(causal, jnp.exp(gam - _row_from_col(gam, eye)), 0.0)
            q = aq_ref[r, :]
            k = ak_ref[r, :]
            q = q * lax.rsqrt(jnp.sum(q * q, axis=-1, keepdims=True) + EPS) * (DH ** -0.5)
            k = k * lax.rsqrt(jnp.sum(k * k, axis=-1, keepdims=True) + EPS)
            eg = jnp.exp(gam)
            g_last = gam[L - 1:L, :]
            kbeta = k * beta
            st.append(dict(dec=dec, q=q, k=k, kbeta=kbeta, eg=eg, g_last=g_last,
                           kd=k * jnp.exp(g_last - gam),
                           rhs=jnp.concatenate([av_ref[r, :] * beta, kbeta * eg], axis=1)))
        for b in seqs:
            s_ = st[b]
            amat = jnp.where(strict, _dot_nt(s_['kbeta'], s_['k']) * s_['dec'], 0.0)
            s_['qk'] = _dot_nt(s_['q'], s_['k']) * s_['dec']
            s_['o'] = _dot(s_['q'] * s_['eg'], s_src[b, h])
            s_['x'] = -amat
            s_['p'] = amat
        for b in seqs:
            st[b]['p'] = _dot(st[b]['p'], st[b]['p'])
        for i in range(n_double):
            for b in seqs:
                s_ = st[b]
                s_['xp'] = _dot(s_['x'], s_['p'])
                if i + 1 < n_double:
                    s_['p2'] = _dot(s_['p'], s_['p'])
            for b in seqs:
                s_ = st[b]
                s_['x'] = s_['x'] + s_['p'] + s_['xp']
                if i + 1 < n_double:
                    s_['p'] = s_['p2']
        for b in seqs:
            s_ = st[b]
            s_['sol'] = s_['rhs'] + _dot(s_['x'], s_['rhs'])
        for b in seqs:
            s_ = st[b]
            s_['v_new'] = s_['sol'][:, :DH] - _dot(s_['sol'][:, DH:], s_src[b, h])
        for b in seqs:
            s_ = st[b]
            o = s_['o'] + _dot(s_['qk'], s_['v_new'])
            s_ref[b, h] = jnp.exp(s_['g_last']) * s_src[b, h] + _dot_tn(s_['kd'], s_['v_new'])
            y_ref[rows[b], :] = _head_rms(o, ng_ref[...]) * _silu(z_ref[rows[b], :])


def _gdn(proj, cw, alog, dtb, ng, cv0, s_init, prev_s, *, li, n_layers, B, G, nT, TT, L, t_valid, gps):
    assert G % gps == 0 and (gps == 1 or nT == 1)
    R = TT * SEQ_BLOCK * gps
    n = proj.shape[0]
    upb = BR // LANE
    blk = lambda u: pl.BlockSpec((R, DH), lambda g, t, h, u=u: (g * nT + t, u + h))
    st4, s_out_spec, s_shape = _state_specs(li, n_layers, B, prev_s is not None, gps)
    cvs = lambda j: pl.BlockSpec((gps * CARRY_ROWS, DH), lambda g, t, h, j=j: (g, j * upb + h))
    cws = lambda j: pl.BlockSpec((CONV_W, DH), lambda g, t, h, j=j: (0, j * upb + h))
    one = pl.BlockSpec((1, LANE), lambda g, t, h: (0, 0))
    cvo = pl.BlockSpec((gps * CARRY_ROWS, BR), lambda g, t, h: (g, 0))
    in_specs = [blk(U_GD_QKV), blk(U_GD_QKV + 4), blk(U_GD_QKV + 8), blk(U_GD_Z),
                pl.BlockSpec((R, LANE), lambda g, t, h: (g * nT + t, U_GD_AB)),
                cws(0), cws(1), cws(2), one, one,
                pl.BlockSpec((1, DH), lambda g, t, h: (0, h)),
                cvs(0), cvs(1), cvs(2)]
    args = [proj] * 5 + [cw, cw, cw, alog, dtb, ng, cv0, cv0, cv0]
    if s_init is not None:
        in_specs.append(st4)
        args.append(s_init)
    aliases = {}
    if prev_s is not None:
        aliases = {len(args): 1}
        in_specs.append(pl.BlockSpec(memory_space=pl.ANY))
        args.append(prev_s)
    kern = functools.partial(_gdn_kernel, L=L, TT=TT, gps=gps, t_valid=t_valid, nT=nT, li=li,
                             has_init=s_init is not None, has_prev=prev_s is not None)
    cv_shape = jax.ShapeDtypeStruct((cv0.shape[0], BR), F32)
    return pl.pallas_call(
        kern,
        grid=(G // gps, nT, HEADS),
        in_specs=in_specs,
        out_specs=[pl.BlockSpec((R, DH), lambda g, t, h: (g * nT + t, h)), s_out_spec, cvo, cvo, cvo],
        out_shape=[jax.ShapeDtypeStruct((n, BR), F32), s_shape, cv_shape, cv_shape, cv_shape],
        scratch_shapes=([pltpu.VMEM((R + gps * CARRY_ROWS, DH), F32)] * 3 + [pltpu.VMEM((R, DH), F32)] * 3
                        + [pltpu.VMEM((3, HEADS, gps * CARRY_ROWS, DH), F32), pltpu.VMEM((R, LANE), F32)]),
        input_output_aliases=aliases,
        compiler_params=_cparams(("arbitrary", "arbitrary", "arbitrary")),
        name="gdn",
    )(*args)


def _s5_prep_kernel(are_ref, aim_ref, ldt_ref, bre_ref, bim_ref, abr_ref, abi_ref, bbr_ref, bbi_ref):
    a_re = are_ref[...]
    a_im = aim_ref[...]
    dt = jnp.exp(ldt_ref[...])
    mag = jnp.exp(dt * a_re)
    ang = dt * a_im
    ab_r = mag * jnp.cos(ang)
    ab_i = mag * jnp.sin(ang)
    den = a_re * a_re + a_im * a_im
    nr = ab_r - 1.0
    ni = ab_i
    f_r = (nr * a_re + ni * a_im) / den
    f_i = (ni * a_re - nr * a_im) / den
    abr_ref[...] = ab_r
    abi_ref[...] = ab_i
    f_r = f_r[0:1, :]
    f_i = f_i[0:1, :]
    bbr_ref[...] = f_r * bre_ref[...] - f_i * bim_ref[...]
    bbi_ref[...] = f_r * bim_ref[...] + f_i * bre_ref[...]


def _s5_prep(a_re, a_im, log_dt, b_re, b_im):
    rep = lambda a: jnp.broadcast_to(a.reshape(1, S5_STATE), (SEQ_BLOCK, S5_STATE))
    ldt = rep(jnp.broadcast_to(log_dt[:, None], (S5_G, S5_P)))
    bt = lambda b: jnp.transpose(b, (2, 0, 1)).reshape(S5_N, S5_STATE)
    shp = lambda r: jax.ShapeDtypeStruct((r, S5_STATE), F32)
    abr, abi, bbr, bbi = pl.pallas_call(
        _s5_prep_kernel,
        out_shape=[shp(SEQ_BLOCK), shp(SEQ_BLOCK), shp(S5_N), shp(S5_N)],
        name="s5_prep",
    )(rep(a_re), rep(a_im), ldt, bt(b_re), bt(b_im))
    return abr, abi, bbr, bbi


def _gelu_tanh(x):
    return 0.5 * x * (1.0 + jnp.tanh(math.sqrt(2.0 / math.pi) * (x + 0.044715 * (x * x * x))))


def _s5_kernel(u_ref, z_ref, abr_ref, abi_ref, wbr_ref, wbi_ref, wcr_ref, wci_ref, d_ref, gw_ref, gb_ref,
               x0r_ref, x0i_ref, y_ref, xr_ref, xi_ref, hr_ref, hi_ref, ys_ref, *, TT, n_steps):
    ti = pl.program_id(1)
    NB = BR // LANE
    SB = S5_STATE // NB

    @pl.when(ti == 0)
    def _():
        xr_ref[...] = x0r_ref[...]
        xi_ref[...] = x0i_ref[...]

    for kb in range(NB):
        ub = u_ref[:, kb * LANE:(kb + 1) * LANE].astype(BF16)
        hr_ref[:, kb * SB:(kb + 1) * SB] = jnp.dot(ub, wbr_ref[kb], preferred_element_type=F32)
        hi_ref[:, kb * SB:(kb + 1) * SB] = jnp.dot(ub, wbi_ref[kb], preferred_element_type=F32)

    for kb in range(NB):
        sl = pl.ds(kb * SB, SB)
        ar = abr_ref[:, sl]
        ai = abi_ref[:, sl]

        def step(t, carry):
            xr, xi = carry
            rows = pl.ds(pl.multiple_of(t * SEQ_BLOCK, SEQ_BLOCK), SEQ_BLOCK)
            nxr = ar * xr - ai * xi + hr_ref[rows, sl]
            nxi = ar * xi + ai * xr + hi_ref[rows, sl]
            hr_ref[rows, sl] = nxr
            hi_ref[rows, sl] = nxi
            return nxr, nxi

        xr, xi = lax.fori_loop(0, n_steps, step, (xr_ref[:, sl], xi_ref[:, sl]))
        xr_ref[:, sl] = xr
        xi_ref[:, sl] = xi

    for kb in range(NB):
        sl = pl.ds(kb * SB, SB)
        cs = pl.ds(kb * LANE, LANE)
        yk = (jnp.dot(hr_ref[:, sl].astype(BF16), wcr_ref[kb], preferred_element_type=F32)
              - jnp.dot(hi_ref[:, sl].astype(BF16), wci_ref[kb], preferred_element_type=F32)
              + d_ref[:, cs] * u_ref[:, cs])
        ys_ref[:, cs] = _gelu_tanh(yk)
    ys = ys_ref[...]
    glu = ys * _sigmoid(jnp.dot(ys.astype(BF16), gw_ref[...], preferred_element_type=F32) + gb_ref[...])
    y_ref[...] = glu * _silu(z_ref[...])


def _s5(proj, abr, abi, wbr, wbi, wcr, wci, d, gw, gb, x0r, x0i, *, G, nT, TT, t_valid):
    assert nT == 1 or t_valid == nT * TT
    R = TT * SEQ_BLOCK
    n = proj.shape[0]
    full = lambda a: pl.BlockSpec(a.shape, lambda g, t, nd=a.ndim: (0,) * nd)
    st = pl.BlockSpec((SEQ_BLOCK, S5_STATE), lambda g, t: (g, 0))
    kern = functools.partial(_s5_kernel, TT=TT, n_steps=t_valid - (nT - 1) * TT)
    return pl.pallas_call(
        kern,
        grid=(G, nT),
        in_specs=[pl.BlockSpec((R, BR), lambda g, t: (g * nT + t, U_S5_U // 4)),
                  pl.BlockSpec((R, BR), lambda g, t: (g * nT + t, U_S5_Z // 4)),
                  full(abr), full(abi), full(wbr), full(wbi), full(wcr), full(wci), full(d), full(gw), full(gb),
                  st, st],
        out_specs=[pl.BlockSpec((R, BR), lambda g, t: (g * nT + t, 0)), st, st],
        out_shape=[jax.ShapeDtypeStruct((n, BR), F32),
                   jax.ShapeDtypeStruct(x0r.shape, F32),
                   jax.ShapeDtypeStruct(x0i.shape, F32)],
        scratch_shapes=[pltpu.VMEM((R, S5_STATE), F32), pltpu.VMEM((R, S5_STATE), F32),
                        pltpu.VMEM((R, BR), F32)],
        compiler_params=_cparams(("arbitrary", "arbitrary")),
        name="s5",
    )(proj, proj, abr, abi, wbr, wbi, wcr, wci, d, gw, gb, x0r, x0i)


def _lru_kernel(x_ref, z_ref, cw_ref, cb_ref, wa_ref, ba_ref, wx_ref, bx_ref, lam_ref, h0_ref, cv0_ref,
                y_ref, h_ref, cv_ref, xp_ref, xl_ref, a_ref, *, TT, n_steps, nT):
    ti = pl.program_id(1)
    R = TT * SEQ_BLOCK

    @pl.when(ti == 0)
    def _():
        h_ref[...] = h0_ref[...]

    _causal_conv_tile(x_ref, cw_ref, cb_ref, xp_ref, xl_ref, cv0_ref, cv_ref, ti=ti, R=R, width=BR,
                      tv_local=n_steps, act_fn=lambda a: a)

    xl = xl_ref[...]
    xb = xl.astype(BF16)
    r = _sigmoid(jnp.dot(xb, wa_ref[...], preferred_element_type=F32) + ba_ref[...])
    i = _sigmoid(jnp.dot(xb, wx_ref[...], preferred_element_type=F32) + bx_ref[...])
    log_a = -LRU_C * r * _softplus(-lam_ref[...])
    a = jnp.exp(log_a)
    a_ref[...] = a
    xl_ref[...] = jnp.sqrt(1.0 - a * a) * (i * xl)

    def step(t, h):
        rows = pl.ds(pl.multiple_of(t * SEQ_BLOCK, SEQ_BLOCK), SEQ_BLOCK)
        hn = a_ref[rows, :] * h + xl_ref[rows, :]
        xl_ref[rows, :] = hn
        return hn

    h_ref[...] = lax.fori_loop(0, n_steps, step, h_ref[...])
    y_ref[...] = xl_ref[...] * _silu(z_ref[...])


def _lru(proj, cw, cb, wa, ba, wx, bx, lam, h0, cv0, *, G, nT, TT, t_valid):
    assert nT == 1 or t_valid == nT * TT
    R = TT * SEQ_BLOCK
    n = proj.shape[0]
    full = lambda a: pl.BlockSpec(a.shape, lambda g, t, nd=a.ndim: (0,) * nd)
    st = pl.BlockSpec((SEQ_BLOCK, BR), lambda g, t: (g, 0))
    cvs = pl.BlockSpec((CARRY_ROWS, BR), lambda g, t: (g, 0))
    kern = functools.partial(_lru_kernel, TT=TT, n_steps=t_valid - (nT - 1) * TT, nT=nT)
    return pl.pallas_call(
        kern,
        grid=(G, nT),
        in_specs=[pl.BlockSpec((R, BR), lambda g, t: (g * nT + t, U_LRU_X // 4)),
                  pl.BlockSpec((R, BR), lambda g, t: (g * nT + t, U_LRU_Z // 4)),
                  full(cw), full(cb), full(wa), full(ba), full(wx), full(bx), full(lam), st, cvs],
        out_specs=[pl.BlockSpec((R, BR), lambda g, t: (g * nT + t, 0)), st, cvs],
        out_shape=[jax.ShapeDtypeStruct((n, BR), F32),
                   jax.ShapeDtypeStruct(h0.shape, F32),
                   jax.ShapeDtypeStruct(cv0.shape, F32)],
        scratch_shapes=[pltpu.VMEM((R + CARRY_ROWS, BR), F32), pltpu.VMEM((R, BR), F32),
                        pltpu.VMEM((R, BR), F32)],
        compiler_params=_cparams(("arbitrary", "arbitrary")),
        name="lru",
    )(proj, proj, cw, cb, wa, ba, wx, bx, lam, h0, cv0)


def _merge_kernel(yml_ref, ygd_ref, ys5_ref, ylru_ref, gates_ref, x_ref, p_ref,
                  wbr_ref, wout_ref, wpg_ref, wple_ref, pg_ref, o_ref, *, p_btd, out_btd):
    tm = x_ref.shape[0]
    if p_btd:
        p = jnp.swapaxes(p_ref[...], 0, 1).reshape(tm, PLE_DIM)
    else:
        p = p_ref[...]
    merged = None
    for nb, y_ref in enumerate((yml_ref, ygd_ref, ys5_ref, ylru_ref)):
        pbr = jnp.dot(y_ref[...].astype(BF16), wbr_ref[nb], preferred_element_type=F32)
        term = _sigmoid(gates_ref[:, nb * D_MODEL:(nb + 1) * D_MODEL]) * pbr
        merged = term if merged is None else merged + term
    out = jnp.dot(merged.astype(BF16), wout_ref[...], preferred_element_type=F32)
    r = x_ref[...] + out * lax.rsqrt(jnp.mean(out * out, axis=-1, keepdims=True) + EPS) * pg_ref[...]
    gate = _sigmoid(jnp.dot(r.astype(BF16), wpg_ref[...], preferred_element_type=F32))
    res = r + gate * jnp.dot(p.astype(BF16), wple_ref[...], preferred_element_type=F32)
    if out_btd:
        o_ref[...] = jnp.swapaxes(res.reshape(tm // SEQ_BLOCK, SEQ_BLOCK, D_MODEL), 0, 1)
    else:
        o_ref[...] = res


def _merge(ys, proj, x, p, wbr, wout, wpg, wple, pg, tm, *, p_layer=None, out_btd=False):
    n = x.shape[0]
    tt = tm // SEQ_BLOCK
    row = lambda c: pl.BlockSpec((tm, c), lambda i: (i, 0))
    full = lambda a: pl.BlockSpec(a.shape, lambda i, nd=a.ndim: (0,) * nd, pipeline_mode=pl.Buffered(1))
    p_spec = row(PLE_DIM) if p_layer is None else pl.BlockSpec((None, SEQ_BLOCK, tt, PLE_DIM),
                                                               lambda i: (p_layer, 0, i, 0))
    if out_btd:
        out_spec = pl.BlockSpec((SEQ_BLOCK, tt, D_MODEL), lambda i: (0, i, 0))
        out_shape = jax.ShapeDtypeStruct((SEQ_BLOCK, n // SEQ_BLOCK, D_MODEL), F32)
    else:
        out_spec, out_shape = row(D_MODEL), jax.ShapeDtypeStruct((n, D_MODEL), F32)
    return pl.pallas_call(
        functools.partial(_merge_kernel, p_btd=p_layer is not None, out_btd=out_btd),
        grid=(n // tm,),
        in_specs=[row(BR), row(BR), row(BR), row(BR),
                  pl.BlockSpec((tm, 4 * D_MODEL), lambda i: (i, U_GATES // 32)),
                  row(D_MODEL), p_spec, full(wbr), full(wout), full(wpg), full(wple), full(pg)],
        out_specs=out_spec,
        out_shape=out_shape,
        compiler_params=_cparams(("parallel",)),
        name="merge",
    )(*ys, proj, x, p, wbr, wout, wpg, wple, pg)


def _w_perm_kernel(tbl_ref, lo_ref, hi_ref, o_ref):
    u = pl.program_id(1)
    shift = tbl_ref[u, 1]
    valid = tbl_ref[u, 2]
    win = jnp.concatenate([lo_ref[...], hi_ref[...]], axis=1)
    rolled = pltpu.roll(win, jnp.where(shift == 0, 0, 2 * LANE - shift), axis=1)[:, :LANE]
    lane = lax.broadcasted_iota(jnp.int32, rolled.shape, 1)
    o_ref[...] = jnp.where(lane < valid, rolled, 0.0).astype(BF16)


def _permute_w_in(w_in):
    sizes = (BR, BR, BR, HEADS, HEADS, BR, BR, 3 * BR, HEADS, HEADS, BR, BR, BR, BR, BR, 4 * D_MODEL)
    offs = [0]
    for s in sizes:
        offs.append(offs[-1] + s)
    (ml_q, _, _, ml_i, _, ml_o, _, gd_qkv, gd_a, _, gd_z, _, _, _, _, gates) = offs[:-1]
    runs = [(U_GD_QKV, gd_qkv, 12, LANE), (U_ML, ml_q, 12, LANE), (U_ML + 12, ml_o, 8, LANE),
            (U_GATES, gates, 32, LANE), (U_GD_Z, gd_z, 20, LANE),
            (U_ML_IF, ml_i, 1, 2 * HEADS), (U_GD_AB, gd_a, 1, 2 * HEADS)]
    depth, d_model, d_in = w_in.shape
    last_src = (d_in - 1) // LANE
    tbl = [None] * N_UNITS
    for u0, src, count, valid in runs:
        for j in range(count):
            off = src + j * LANE
            tbl[u0 + j] = (off // LANE, off % LANE, valid)
    assert all(t is not None for t in tbl)
    tbl = jnp.asarray(tbl, jnp.int32)
    grid_spec = pltpu.PrefetchScalarGridSpec(
        num_scalar_prefetch=1,
        grid=(depth, N_UNITS),
        in_specs=[pl.BlockSpec((None, d_model, LANE), lambda l, u, t: (l, 0, t[u, 0])),
                  pl.BlockSpec((None, d_model, LANE), lambda l, u, t: (l, 0, jnp.minimum(t[u, 0] + 1, last_src)))],
        out_specs=pl.BlockSpec((None, d_model, LANE), lambda l, u, t: (l, 0, u)),
    )
    return pl.pallas_call(
        _w_perm_kernel,
        grid_spec=grid_spec,
        out_shape=jax.ShapeDtypeStruct((depth, d_model, PROJ_COLS), BF16),
        compiler_params=_cparams(("arbitrary", "arbitrary")),
        name="w_perm",
    )(tbl, w_in, w_in)


def _pad_lanes(*vecs):
    v = jnp.concatenate(vecs)
    return jnp.pad(v, (0, LANE - v.shape[0])).reshape(1, LANE)


def _block_diag(blocks):
    n, a, b = blocks.shape
    eye = jnp.eye(n, dtype=blocks.dtype)
    return jnp.einsum('ij,iab->iajb', eye, blocks).reshape(n * a, n * b)


def _to_rows(x, G):
    B, T, C = x.shape
    return x.reshape(G, SEQ_BLOCK, T, C).transpose(0, 2, 1, 3).reshape(G * T * SEQ_BLOCK, C)


def _from_rows(y, G, T):
    C = y.shape[-1]
    return y.reshape(G, T, SEQ_BLOCK, C).transpose(0, 2, 1, 3).reshape(G * SEQ_BLOCK, T, C)


def _group(x, p, states, params, *, T, t_valid, TT, L, tm_proj, tm_merge, gps=1):
    B = x.shape[0]
    G = B // SEQ_BLOCK
    nT = T // TT
    depth = p.shape[0]
    xr = _to_rows(x, G)
    tm_proj = min(tm_proj, xr.shape[0])
    tm_merge = min(tm_merge, xr.shape[0])
    new_states = []
    c_all = s_all = None
    gps = math.gcd(gps, G)
    mat = dict(n_layers=depth, B=B, G=G, nT=nT, TT=TT, L=L, t_valid=t_valid, gps=gps)
    for li in range(depth):
        lp = {k: v[li] for k, v in params.items() if k != 'w_in_p'}
        if states is None:
            ml_init = s_init = None
            gcv0 = jnp.zeros((G * CARRY_ROWS, 3 * BR), F32)
            x0r = jnp.zeros((B, S5_STATE), F32)
            x0i = jnp.zeros((B, S5_STATE), F32)
            h0 = jnp.zeros((B, BR), F32)
            lcv0 = jnp.zeros((G * CARRY_ROWS, BR), F32)
        else:
            (n0, m0, gcv0, x0r, x0i, h0, lcv0) = [states[j][li] for j in (1, 2, 4, 5, 6, 7, 8)]
            n0 = n0.reshape(G, SEQ_BLOCK, HEADS, DH).transpose(0, 2, 1, 3).reshape(B * HEADS, DH)
            ml_init = (states[0], n0, jnp.pad(m0, ((0, 0), (0, LANE - HEADS))))
            s_init = states[3]
            gcv0 = _to_rows(gcv0, G)
            x0r = x0r.reshape(B, S5_STATE)
            x0i = x0i.reshape(B, S5_STATE)
            lcv0 = _to_rows(lcv0, G)
        in_place = G == 1

        proj = _in_proj(xr, lp['prenorm_g'], params['w_in_p'], li, tm_proj, PROJ_COLS // PROJ_COL_TILES)

        y_ml, c_all, n1, m1 = _mlstm(proj, lp['ml_ifb'], lp['ml_norm_g'], ml_init, c_all, li=li, **mat)
        y_gd, s_all, cvq, cvk, cvv = _gdn(proj, lp['gd_conv_w'], lp['gd_alog'], lp['gd_dtb'], lp['gd_norm_g'],
                                          gcv0, s_init, s_all, li=li, **dict(mat, L=min(L, GDN_CHUNK)))
        gcv1 = jnp.concatenate([cvq, cvk, cvv], axis=-1)
        y_s5, x1r, x1i = _s5(proj, lp['s5_abr'], lp['s5_abi'], lp['s5_wbr'], lp['s5_wbi'], lp['s5_wcr'],
                             lp['s5_wci'], lp['s5_d'], lp['s5_glu_w'], lp['s5_glu_b'], x0r, x0i,
                             G=G, nT=nT, TT=TT, t_valid=t_valid)
        y_lru, h1, lcv1 = _lru(proj, lp['lru_conv_w'], lp['lru_conv_b'], lp['lru_wa'], lp['lru_ba'],
                               lp['lru_wx'], lp['lru_bx'], lp['lru_lam'], h0, lcv0,
                               G=G, nT=nT, TT=TT, t_valid=t_valid)
        xr = _merge((y_ml, y_gd, y_s5, y_lru), proj, xr, p if in_place else _to_rows(p[li], G),
                    lp['w_branch'], lp['w_out'], lp['w_ple_gate'], lp['w_ple'], lp['postnorm_g'], tm_merge,
                    p_layer=li if in_place else None, out_btd=in_place and li == depth - 1)

        n1 = n1.reshape(G, HEADS, SEQ_BLOCK, DH).transpose(0, 2, 1, 3)
        new_states.append((n1.reshape(B, HEADS, DH), m1[:, :HEADS],
                           _from_rows(gcv1, G, CONV_W - 1), x1r.reshape(B, S5_G, S5_P),
                           x1i.reshape(B, S5_G, S5_P), h1, _from_rows(lcv1, G, CONV_W - 1)))
    y = xr if G == 1 else _from_rows(xr, G, T)
    n_s, m_s, gcv_s, xr_s, xi_s, h_s, lcv_s = (jnp.stack([ns[j] for ns in new_states]) for j in range(7))
    return y, (c_all, n_s, m_s, s_all, gcv_s, xr_s, xi_s, h_s, lcv_s)


def _prepare_params(prenorm_g, postnorm_g, w_in, ml_bi, ml_bf, ml_norm_g, gd_conv_w, gd_a_log, gd_dt_bias,
                    gd_norm_g, s5_a_re, s5_a_im, s5_log_dt, s5_b_re, s5_b_im, s5_c_re, s5_c_im, s5_d,
                    s5_glu_w, s5_glu_b, lru_conv_w, lru_conv_b, lru_wa, lru_ba, lru_wx, lru_bx, lru_lam,
                    w_branch, w_out, w_ple, w_ple_gate):
    depth = w_in.shape[0]
    row = lambda a: a.reshape(depth, 1, -1)
    per_layer = lambda f, *a: jnp.stack([f(*[x[i] for x in a]) for i in range(depth)])
    prm = dict(
        prenorm_g=row(prenorm_g), postnorm_g=row(postnorm_g), w_in_p=_permute_w_in(w_in),
        ml_ifb=per_layer(_pad_lanes, ml_bi, ml_bf), ml_norm_g=row(ml_norm_g),
        gd_conv_w=gd_conv_w, gd_norm_g=row(gd_norm_g),
        gd_alog=per_layer(lambda a: _pad_lanes(a), gd_a_log),
        gd_dtb=per_layer(lambda a: _pad_lanes(a), gd_dt_bias),
        s5_d=row(s5_d), s5_glu_w=s5_glu_w.astype(BF16), s5_glu_b=row(s5_glu_b),
        lru_conv_w=lru_conv_w, lru_conv_b=row(lru_conv_b),
        lru_wa=per_layer(_block_diag, lru_wa).astype(BF16), lru_ba=row(lru_ba),
        lru_wx=per_layer(_block_diag, lru_wx).astype(BF16), lru_bx=row(lru_bx), lru_lam=row(lru_lam),
        w_branch=w_branch.astype(BF16), w_out=w_out.astype(BF16), w_ple=w_ple.astype(BF16),
        w_ple_gate=w_ple_gate.astype(BF16),
    )
    abr, abi, wbr, wbi, wcr, wci = [], [], [], [], [], []
    nb = BR // LANE
    gpb = S5_G // nb
    for i in range(depth):
        a_r, a_i, bbr, bbi = _s5_prep(s5_a_re[i], s5_a_im[i], s5_log_dt[i], s5_b_re[i], s5_b_im[i])
        abr.append(a_r)
        abi.append(a_i)
        in_blocks = lambda bb: jnp.stack([_block_diag(bb.reshape(S5_N, S5_G, S5_P).transpose(1, 0, 2)
                                                      [k * gpb:(k + 1) * gpb]) for k in range(nb)])
        out_blocks = lambda c: jnp.stack([_block_diag(jnp.transpose(c, (0, 2, 1))[k * gpb:(k + 1) * gpb])
                                          for k in range(nb)])
        wbr.append(in_blocks(bbr).astype(BF16))
        wbi.append(in_blocks(bbi).astype(BF16))
        wcr.append(out_blocks(s5_c_re[i]).astype(BF16))
        wci.append(out_blocks(s5_c_im[i]).astype(BF16))
    prm.update(s5_abr=jnp.stack(abr), s5_abi=jnp.stack(abi), s5_wbr=jnp.stack(wbr), s5_wbi=jnp.stack(wbi),
               s5_wcr=jnp.stack(wcr), s5_wci=jnp.stack(wci))
    return prm


def kernel(x_prompt, x_sample, state_mlstm_c, state_mlstm_n, state_mlstm_m, state_gdn_s, state_gdn_conv, state_s5_re, state_s5_im, state_lru_h, state_lru_conv, p_prompt, p_sample, prenorm_g, postnorm_g, w_in, ml_bi, ml_bf, ml_norm_g, gd_conv_w, gd_a_log, gd_dt_bias, gd_norm_g, s5_a_re, s5_a_im, s5_log_dt, s5_b_re, s5_b_im, s5_c_re, s5_c_im, s5_d, s5_glu_w, s5_glu_b, lru_conv_w, lru_conv_b, lru_wa, lru_ba, lru_wx, lru_bx, lru_lam, w_branch, w_out, w_ple, w_ple_gate):
    prm = _prepare_params(prenorm_g, postnorm_g, w_in, ml_bi, ml_bf, ml_norm_g, gd_conv_w, gd_a_log,
                          gd_dt_bias, gd_norm_g, s5_a_re, s5_a_im, s5_log_dt, s5_b_re, s5_b_im, s5_c_re,
                          s5_c_im, s5_d, s5_glu_w, s5_glu_b, lru_conv_w, lru_conv_b, lru_wa, lru_ba, lru_wx,
                          lru_bx, lru_lam, w_branch, w_out, w_ple, w_ple_gate)

    t_p = x_prompt.shape[1]
    tt_p = math.gcd(t_p, PROMPT_CHUNK)
    y_prompt, pr = _group(x_prompt, p_prompt, None, prm, T=t_p, t_valid=t_p, TT=tt_p, L=tt_p,
                          tm_proj=TM_PROJ, tm_merge=TM_MERGE)

    t_s = x_sample.shape[1]
    t_pad = -(-t_s // SEQ_BLOCK) * SEQ_BLOCK
    pad_t = lambda a, ax: jnp.pad(a, [(0, t_pad - t_s) if i == ax else (0, 0) for i in range(a.ndim)])
    sample_states = (state_mlstm_c, state_mlstm_n, state_mlstm_m, state_gdn_s, state_gdn_conv,
                     state_s5_re, state_s5_im, state_lru_h, state_lru_conv)
    y_s, sa = _group(pad_t(x_sample, 1), pad_t(p_sample, 2), sample_states, prm, T=t_pad, t_valid=t_s,
                     TT=t_pad, L=t_pad, tm_proj=TM_PROJ, tm_merge=TM_MERGE, gps=SAMPLE_GROUPS_PER_STEP)
    y_sample = y_s[:, :t_s]
    return (y_prompt, y_sample) + pr + sa
```

```python
import functools
import math

import jax
import jax.numpy as jnp
from jax import lax
from jax.experimental import pallas as pl
from jax.experimental.pallas import tpu as pltpu

F32 = jnp.float32
BF16 = jnp.bfloat16

D_MODEL = 1024
BR = 512
HEADS = 4
DH = 128
CONV_W = 4
S5_G = 32
S5_N = 16
S5_P = 64
S5_STATE = S5_G * S5_P
LRU_BLOCKS = 8
LRU_BD = 64
LRU_C = 8.0
PLE_DIM = 256
EPS = 1e-6
NEG = -1e30

LANE = 128
SEQ_BLOCK = 8
CARRY_ROWS = (CONV_W - 1) * SEQ_BLOCK
PROMPT_CHUNK = 128
GDN_CHUNK = 64
SAMPLE_GROUPS_PER_STEP = 2
TM_PROJ = 512
TM_MERGE = 512

U_GD_QKV = 0
U_ML = 12
U_GD_Z = 32
U_S5_U = 36
U_S5_Z = 40
U_LRU_X = 44
U_LRU_Z = 48
N_UNITS = 52
PROJ_COLS = N_UNITS * LANE
U_ML_IF = 0
U_GD_AB = 1
GATE_COLS = 4 * D_MODEL
W_CHUNK = 4 * LANE

VMEM_LIMIT = 56 * 1024 * 1024


def _cparams(sem):
    return pltpu.CompilerParams(dimension_semantics=sem, vmem_limit_bytes=VMEM_LIMIT)


def _sigmoid(x):
    return 0.5 * jnp.tanh(0.5 * x) + 0.5


def _silu(x):
    return x * _sigmoid(x)


def _softplus(x):
    return jnp.maximum(x, 0.0) + jnp.log1p(jnp.exp(-jnp.abs(x)))


def _dot(a, b):
    return jnp.dot(a.astype(BF16), b.astype(BF16), preferred_element_type=F32)


def _dot_nt(a, b):
    return lax.dot_general(a.astype(BF16), b.astype(BF16), (((1,), (1,)), ((), ())),
                           preferred_element_type=F32)


def _dot_tn(a, b):
    return lax.dot_general(a.astype(BF16), b.astype(BF16), (((0,), (0,)), ((), ())),
                           preferred_element_type=F32)


def _cumsum_rows(x):
    n = x.shape[0]
    ri = lax.broadcasted_iota(jnp.int32, x.shape, 0)
    s = 1
    while s < n:
        x = x + jnp.where(ri >= s, pltpu.roll(x, s, axis=0), 0.0)
        s *= 2
    return x


def _row_from_col(col, eye):
    return jnp.sum(jnp.where(eye, col, 0.0), axis=0, keepdims=True)


def _head_rms(h, g):
    return h * lax.rsqrt(jnp.mean(h * h, axis=-1, keepdims=True) + EPS) * g


def _rms(x, g):
    return x * lax.rsqrt(jnp.mean(x * x, axis=-1, keepdims=True) + EPS) * g


def _in_proj_kernel(x_ref, g_ref, w_ref, wg_ref, o_ref, og_ref):
    h = _rms(x_ref[...], g_ref[...]).astype(BF16)
    o_ref[...] = jnp.dot(h, w_ref[...], preferred_element_type=F32)
    og_ref[...] = jnp.dot(h, wg_ref[...], preferred_element_type=F32)


def _in_proj(x, g, w_all, wg_all, li, tm):
    n = x.shape[0]
    resident = lambda c: pl.BlockSpec((None, D_MODEL, c), lambda i: (li, 0, 0), pipeline_mode=pl.Buffered(1))
    return pl.pallas_call(
        _in_proj_kernel,
        grid=(n // tm,),
        in_specs=[pl.BlockSpec((tm, D_MODEL), lambda i: (i, 0)),
                  pl.BlockSpec((1, D_MODEL), lambda i: (0, 0)),
                  resident(PROJ_COLS), resident(2 * LANE)],
        out_specs=[pl.BlockSpec((tm, PROJ_COLS), lambda i: (i, 0)),
                   pl.BlockSpec((tm, 2 * LANE), lambda i: (i, 0))],
        out_shape=[jax.ShapeDtypeStruct((n, PROJ_COLS), F32), jax.ShapeDtypeStruct((n, 2 * LANE), F32)],
        compiler_params=_cparams(("parallel",)),
        name="in_proj",
    )(x, g, w_all, wg_all)


def _token_cumsum(x, tok_in_chunk, L):
    s = 1
    while s < L:
        x = x + jnp.where(tok_in_chunk >= s, pltpu.roll(x, s * SEQ_BLOCK, axis=0), 0.0)
        s *= 2
    return x


def _tile_token_ids(ti, TT, L):
    R = TT * SEQ_BLOCK
    t_local = lax.broadcasted_iota(jnp.int32, (R, LANE), 0) // SEQ_BLOCK
    return ti * TT + t_local, t_local % L


def _token_scan(x, op, fill):
    s = 1
    while s < x.shape[0]:
        shifted = jnp.concatenate([jnp.full((s,) + x.shape[1:], fill, x.dtype), x[:-s]], axis=0)
        x = op(x, shifted)
        s *= 2
    return x


ML_R, ML_M, ML_INTER, ML_ENEG, ML_WS = (j * HEADS for j in range(5))


def _mlstm_gate_tile(if_ref, ifb_ref, m_ref, sc_ref, gt_ref, *, ti, T, t_valid):
    shape = (T, SEQ_BLOCK, LANE)
    lane = lax.broadcasted_iota(jnp.int32, shape, 2)
    valid = (ti * T + lax.broadcasted_iota(jnp.int32, shape, 0)) < t_valid
    rot = lambda x, k: pltpu.roll(x.reshape(T * SEQ_BLOCK, LANE), k, axis=1).reshape(shape)
    ifv = (if_ref[...] + ifb_ref[...]).reshape(shape)
    logf = jnp.minimum(ifv, 0.0) - jnp.log1p(jnp.exp(-jnp.abs(ifv)))
    bcum = rot(_token_scan(jnp.where(valid, logf, 0.0), jnp.add, 0.0), LANE - HEADS)
    ig = jnp.where(valid, ifv, NEG)
    r = ig - bcum
    m_prev = m_ref[...]
    big_m = jnp.maximum(m_prev[None], _token_scan(r, jnp.maximum, NEG))
    inter = jnp.exp(m_prev[None] - big_m)
    eneg = jnp.exp(-(bcum + big_m))
    b_last = bcum[T - 1]
    g = b_last[None] - bcum + ig
    m_new = jnp.maximum(b_last + m_prev, jnp.max(g, axis=0))
    ws = jnp.exp(g - m_new[None])
    head_lanes = lane[0] < HEADS
    sc_ref[...] = jnp.where(head_lanes, jnp.exp(b_last + m_prev - m_new), 0.0)
    m_ref[...] = jnp.where(head_lanes, m_new, 0.0)
    packed = jnp.where(lane < ML_M, r,
                       jnp.where(lane < ML_INTER, rot(big_m, ML_M),
                                 jnp.where(lane < ML_ENEG, rot(inter, ML_INTER),
                                           jnp.where(lane < ML_WS, rot(eneg, ML_ENEG), rot(ws, ML_WS)))))
    gt_ref[...] = packed.reshape(T * SEQ_BLOCK, LANE)


def _own_layer(state_ref, li, has_prev, first_step):
    if has_prev:
        return state_ref

    @pl.when(first_step)
    def _():
        for other in range(state_ref.shape[0]):
            if other != li:
                state_ref[other] = jnp.zeros(state_ref.shape[1:], F32)

    return state_ref.at[li]


def _mlstm_kernel(*refs, L, gps, t_valid, li, has_init, has_prev, single_tile):
    q_ref, k_ref, v_ref, o_ref, z_ref, if_ref, ifb_ref, ng_ref = refs[:8]
    n_in = 8 + (3 if has_init else 0) + (1 if has_prev else 0)
    y_ref, c_ref, n_ref, m_ref, gt_ref, sc_ref, rt_ref = refs[n_in:]
    ti = pl.program_id(1)
    h = pl.program_id(2)
    R = L * SEQ_BLOCK
    seqs = range(gps * SEQ_BLOCK)
    rows = [pl.ds((j // SEQ_BLOCK) * R + j % SEQ_BLOCK, L, stride=SEQ_BLOCK) for j in seqs]
    n_rows = [pl.ds((j // SEQ_BLOCK) * SEQ_BLOCK * HEADS + h * SEQ_BLOCK + j % SEQ_BLOCK, 1) for j in seqs]
    use_transpose = L == LANE
    c_ref = _own_layer(c_ref, li, has_prev, (ti == 0) & (h == 0))
    c_src = refs[8] if (has_init and single_tile) else c_ref

    @pl.when((ti == 0) & (h == 0))
    def _():
        if has_init:
            if not single_tile:
                c_ref[...] = refs[8][...]
            n_ref[...] = refs[9][...]
            m_ref[...] = refs[10][...]
        else:
            c_ref[...] = jnp.zeros(c_ref.shape, F32)
            n_ref[...] = jnp.zeros(n_ref.shape, F32)
            m_ref[...] = jnp.zeros(m_ref.shape, F32)

    @pl.when(h == 0)
    def _():
        for gi in range(gps):
            grp = pl.ds(gi * SEQ_BLOCK, SEQ_BLOCK)
            tile = pl.ds(gi * R, R)
            _mlstm_gate_tile(if_ref.at[tile], ifb_ref, m_ref.at[grp], sc_ref.at[grp], gt_ref.at[tile],
                             ti=ti, T=L, t_valid=t_valid)
        if use_transpose:
            for b in seqs:
                rt_ref[b * SEQ_BLOCK:(b + 1) * SEQ_BLOCK, :] = gt_ref[rows[b], :].T[:SEQ_BLOCK, :]

    ri = lax.broadcasted_iota(jnp.int32, (L, L), 0)
    ci = lax.broadcasted_iota(jnp.int32, (L, L), 1)
    causal = ci <= ri
    eye = ci == ri
    to_lane0 = jnp.where(h == 0, 0, LANE - h)
    sc_all = pltpu.roll(sc_ref[...], to_lane0, axis=1)
    col = lambda tile, off: tile[:, off:off + 1]

    st = []
    for b in seqs:
        gq = pltpu.roll(gt_ref[rows[b], :], to_lane0, axis=1)
        q = q_ref[rows[b], :] * (DH ** -0.5)
        v = v_ref[rows[b], :]
        kw = k_ref[rows[b], :] * col(gq, ML_WS)
        cmat = c_src[b, h]
        sc = sc_all[b:b + 1, 0:1]
        st.append(dict(gq=gq, q=q, v=v, kw=kw, sc=sc, qk=_dot_nt(q, k_ref[rows[b], :]), qc=_dot(q, cmat)))
        c_ref[b, h] = sc * cmat + _dot_tn(kw, v)
    for b in seqs:
        s_ = st[b]
        if use_transpose:
            r_row = rt_ref[pl.ds(b * SEQ_BLOCK + h, 1), :]
        else:
            r_row = _row_from_col(col(s_['gq'], ML_R), eye)
        s_['s'] = s_['qk'] * jnp.where(causal, jnp.exp(r_row - col(s_['gq'], ML_M)), 0.0)
    for b in seqs:
        s_ = st[b]
        srow = n_rows[b]
        nvec = n_ref[srow, :]
        inter = col(s_['gq'], ML_INTER)
        s_['num'] = _dot(s_['s'], s_['v']) + inter * s_['qc']
        den = (jnp.sum(s_['s'], axis=1, keepdims=True)
               + inter * jnp.sum(s_['q'] * nvec, axis=1, keepdims=True))
        s_['den'] = jnp.maximum(jnp.abs(den), col(s_['gq'], ML_ENEG))
        n_ref[srow, :] = s_['sc'] * nvec + jnp.sum(s_['kw'], axis=0, keepdims=True)
    for b in seqs:
        s_ = st[b]
        yn = _head_rms(s_['num'] / s_['den'], ng_ref[...])
        y_ref[rows[b], :] = yn * _sigmoid(o_ref[rows[b], :]) * _silu(z_ref[rows[b], :])


def _state_specs(li, n_layers, B, has_prev, gps):
    nseq = gps * SEQ_BLOCK
    one = pl.BlockSpec((None, nseq, HEADS, DH, DH), lambda g, t, h: (li, g, 0, 0, 0))
    every = pl.BlockSpec((n_layers, nseq, HEADS, DH, DH), lambda g, t, h: (0, g, 0, 0, 0))
    shape = jax.ShapeDtypeStruct((n_layers, B, HEADS, DH, DH), F32)
    return one, (one if has_prev else every), shape


def _mlstm(proj, projg, ifb, ng, init, prev_c, *, li, n_layers, B, G, nT, TT, L, t_valid, gps):
    assert TT == L and G % gps == 0 and (gps == 1 or nT == 1)
    R = TT * SEQ_BLOCK * gps
    n = proj.shape[0]
    blk = lambda u: pl.BlockSpec((R, DH), lambda g, t, h, u=u: (g * nT + t, u + h))
    st4, c_out_spec, c_shape = _state_specs(li, n_layers, B, prev_c is not None, gps)
    n_spec = pl.BlockSpec((gps * SEQ_BLOCK * HEADS, LANE), lambda g, t, h: (g, 0))
    m_spec = pl.BlockSpec((gps * SEQ_BLOCK, LANE), lambda g, t, h: (g, 0))
    in_specs = [blk(U_ML), blk(U_ML + 4), blk(U_ML + 8), blk(U_ML + 12), blk(U_ML + 16),
                pl.BlockSpec((R, LANE), lambda g, t, h: (g * nT + t, U_ML_IF)),
                pl.BlockSpec((1, LANE), lambda g, t, h: (0, 0)),
                pl.BlockSpec((1, DH), lambda g, t, h: (0, h))]
    args = [proj] * 5 + [projg, ifb, ng]
    if init is not None:
        in_specs += [st4, n_spec, m_spec]
        args += list(init)
    aliases = {}
    if prev_c is not None:
        aliases = {len(args): 1}
        in_specs.append(pl.BlockSpec(memory_space=pl.ANY))
        args.append(prev_c)
    kern = functools.partial(_mlstm_kernel, L=L, gps=gps, t_valid=t_valid, li=li, has_init=init is not None,
                             has_prev=prev_c is not None, single_tile=nT == 1)
    return pl.pallas_call(
        kern,
        grid=(G // gps, nT, HEADS),
        in_specs=in_specs,
        out_specs=[pl.BlockSpec((R, DH), lambda g, t, h: (g * nT + t, h)), c_out_spec, n_spec, m_spec],
        out_shape=[jax.ShapeDtypeStruct((n, BR), F32), c_shape,
                   jax.ShapeDtypeStruct((B * HEADS, LANE), F32), jax.ShapeDtypeStruct((B, LANE), F32)],
        scratch_shapes=[pltpu.VMEM((R, LANE), F32), pltpu.VMEM((gps * SEQ_BLOCK, LANE), F32),
                        pltpu.VMEM((gps * SEQ_BLOCK * SEQ_BLOCK, LANE), F32)],
        input_output_aliases=aliases,
        compiler_params=_cparams(("arbitrary", "arbitrary", "arbitrary")),
        name="mlstm",
    )(*args)


def _causal_conv_tile(x_ref, w_ref, b_ref, xp_ref, act_ref, cv0_ref, cv_ref, *, ti, R, width, tv_local,
                      act_fn, carry_ref=None):
    @pl.when(ti == 0)
    def _():
        xp_ref[0:CARRY_ROWS, :] = cv0_ref[...]

    if carry_ref is not None:
        @pl.when(ti > 0)
        def _():
            xp_ref[0:CARRY_ROWS, :] = carry_ref[...]

    xp_ref[CARRY_ROWS:CARRY_ROWS + R, :] = x_ref[...]
    RB = next(rb for rb in (128, 32, SEQ_BLOCK) if R % rb == 0 and rb * min(width, BR) <= 128 * LANE)

    def blk(i, carry):
        r0 = pl.multiple_of(i * RB, SEQ_BLOCK)
        cw = min(width, BR)
        for c0 in range(0, width, cw):
            cs = pl.ds(c0, cw)
            acc = xp_ref[pl.ds(r0, RB), cs] * w_ref[0:1, cs]
            for j in range(1, CONV_W):
                acc = acc + xp_ref[pl.ds(r0 + j * SEQ_BLOCK, RB), cs] * w_ref[j:j + 1, cs]
            if b_ref is not None:
                acc = acc + b_ref[:, cs]
            act_ref[pl.ds(r0, RB), cs] = act_fn(acc)
        return carry

    lax.fori_loop(0, R // RB, blk, 0)

    cv_ref[...] = xp_ref[tv_local * SEQ_BLOCK:tv_local * SEQ_BLOCK + CARRY_ROWS, :]
    if carry_ref is not None:
        carry_ref[...] = xp_ref[R:R + CARRY_ROWS, :]
    else:
        xp_ref[0:CARRY_ROWS, :] = xp_ref[R:R + CARRY_ROWS, :]


def _gdn_kernel(*refs, L, TT, gps, t_valid, nT, li, has_init, has_prev):
    (q_ref, k_ref, v_ref, z_ref, ab_ref, cwq_ref, cwk_ref, cwv_ref, alog_ref, dtb_ref, ng_ref,
     cvq0_ref, cvk0_ref, cvv0_ref) = refs[:14]
    n_in = 14 + (1 if has_init else 0) + (1 if has_prev else 0)
    (y_ref, s_ref, cvq_ref, cvk_ref, cvv_ref,
     xpq_ref, xpk_ref, xpv_ref, aq_ref, ak_ref, av_ref, carry_ref, gb_ref) = refs[n_in:]
    ti = pl.program_id(1)
    h = pl.program_id(2)
    R = TT * SEQ_BLOCK
    s_ref = _own_layer(s_ref, li, has_prev, (ti == 0) & (h == 0))
    single_chunk = nT == 1 and TT == L
    s_src = refs[14] if (has_init and single_chunk) else s_ref

    @pl.when((ti == 0) & (h == 0))
    def _():
        if has_init and single_chunk:
            pass
        elif has_init:
            s_ref[...] = refs[14][...]
        else:
            s_ref[...] = jnp.zeros(s_ref.shape, F32)

    head_lanes = pl.ds(pl.multiple_of(h * DH, DH), DH)
    for j, (x_ref, cw_ref, xp_ref, a_ref, cv0_ref, cv_ref) in enumerate((
            (q_ref, cwq_ref, xpq_ref, aq_ref, cvq0_ref, cvq_ref),
            (k_ref, cwk_ref, xpk_ref, ak_ref, cvk0_ref, cvk_ref),
            (v_ref, cwv_ref, xpv_ref, av_ref, cvv0_ref, cvv_ref))):
        for gi in range(gps):
            tile = pl.ds(gi * R, R)
            cvr = pl.ds(gi * CARRY_ROWS, CARRY_ROWS)
            _causal_conv_tile(x_ref.at[tile], cw_ref, None, xp_ref.at[pl.ds(gi * (R + CARRY_ROWS), R + CARRY_ROWS)],
                              a_ref.at[tile], cv0_ref.at[cvr], cv_ref.at[cvr, head_lanes], ti=ti, R=R, width=DH,
                              tv_local=t_valid - (nT - 1) * TT, act_fn=_silu, carry_ref=carry_ref.at[j, h, cvr])

    @pl.when(h == 0)
    def _():
        tok, tok_in_chunk = _tile_token_ids(ti, TT, L)
        valid = tok < t_valid
        lane = lax.broadcasted_iota(jnp.int32, (R, LANE), 1)
        for gi in range(gps):
            tile = pl.ds(gi * R, R)
            abv = ab_ref[tile, :]
            g_all = -jnp.exp(alog_ref[...]) * _softplus(abv + dtb_ref[...])
            gam = _token_cumsum(jnp.where(valid, g_all, 0.0), tok_in_chunk, L)
            gb_ref[tile, :] = jnp.where(lane < HEADS, gam, jnp.where(valid, _sigmoid(abv), 0.0))

    ri = lax.broadcasted_iota(jnp.int32, (L, L), 0)
    ci = lax.broadcasted_iota(jnp.int32, (L, L), 1)
    causal = ci <= ri
    strict = ci < ri
    eye = ci == ri
    lane = lax.broadcasted_iota(jnp.int32, (L, LANE), 1)
    sel_a = lane == h
    sel_b = lane == HEADS + h
    pick = lambda tile, sel: jnp.sum(jnp.where(sel, tile, 0.0), axis=1, keepdims=True)
    n_double = int(math.log2(L)) - 1

    seqs = range(gps * SEQ_BLOCK)
    for c in range(TT // L):
        rows, st = [], []
        for b in seqs:
            r = pl.ds((b // SEQ_BLOCK) * R + c * L * SEQ_BLOCK + b % SEQ_BLOCK, L, stride=SEQ_BLOCK)
            rows.append(r)
            gbv = gb_ref[r, :]
            gam = pick(gbv, sel_a)
            beta = pick(gbv, sel_b)
            dec = jnp.where(causal, jnp.exp(gam - _row_from_col(gam, eye)), 0.0)
            q = aq_ref[r, :]
            k = ak_ref[r, :]
            q = q * lax.rsqrt(jnp.sum(q * q, axis=-1, keepdims=True) + EPS) * (DH ** -0.5)
            k = k * lax.rsqrt(jnp.sum(k * k, axis=-1, keepdims=True) + EPS)
            eg = jnp.exp(gam)
            g_last = gam[L - 1:L, :]
            kbeta = k * beta
            st.append(dict(dec=dec, q=q, k=k, kbeta=kbeta, eg=eg, g_last=g_last,
                           kd=k * jnp.exp(g_last - gam),
                           rhs=jnp.concatenate([av_ref[r, :] * beta, kbeta * eg], axis=1)))
        for b in seqs:
            s_ = st[b]
            amat = jnp.where(strict, _dot_nt(s_['kbeta'], s_['k']) * s_['dec'], 0.0)
            s_['qk'] = _dot_nt(s_['q'], s_['k']) * s_['dec']
            s_['o'] = _dot(s_['q'] * s_['eg'], s_src[b, h])
            s_['x'] = -amat
            s_['p'] = amat
        for b in seqs:
            st[b]['p'] = _dot(st[b]['p'], st[b]['p'])
        for i in range(n_double):
            for b in seqs:
                s_ = st[b]
                s_['xp'] = _dot(s_['x'], s_['p'])
                if i + 1 < n_double:
                    s_['p2'] = _dot(s_['p'], s_['p'])
            for b in seqs:
                s_ = st[b]
                s_['x'] = s_['x'] + s_['p'] + s_['xp']
                if i + 1 < n_double:
                    s_['p'] = s_['p2']
        for b in seqs:
            s_ = st[b]
            s_['sol'] = s_['rhs'] + _dot(s_['x'], s_['rhs'])
        for b in seqs:
            s_ = st[b]
            s_['v_new'] = s_['sol'][:, :DH] - _dot(s_['sol'][:, DH:], s_src[b, h])
        for b in seqs:
            s_ = st[b]
            o = s_['o'] + _dot(s_['qk'], s_['v_new'])
            s_ref[b, h] = jnp.exp(s_['g_last']) * s_src[b, h] + _dot_tn(s_['kd'], s_['v_new'])
            y_ref[rows[b], :] = _head_rms(o, ng_ref[...]) * _silu(z_ref[rows[b], :])


def _gdn(proj, projg, cw, alog, dtb, ng, cv0, s_init, prev_s, *, li, n_layers, B, G, nT, TT, L, t_valid, gps):
    assert G % gps == 0 and (gps == 1 or nT == 1)
    R = TT * SEQ_BLOCK * gps
    n = proj.shape[0]
    upb = BR // LANE
    blk = lambda u: pl.BlockSpec((R, DH), lambda g, t, h, u=u: (g * nT + t, u + h))
    st4, s_out_spec, s_shape = _state_specs(li, n_layers, B, prev_s is not None, gps)
    cvs = lambda j: pl.BlockSpec((gps * CARRY_ROWS, DH), lambda g, t, h, j=j: (g, j * upb + h))
    cws = lambda j: pl.BlockSpec((CONV_W, DH), lambda g, t, h, j=j: (0, j * upb + h))
    one = pl.BlockSpec((1, LANE), lambda g, t, h: (0, 0))
    cvo = pl.BlockSpec((gps * CARRY_ROWS, BR), lambda g, t, h: (g, 0))
    in_specs = [blk(U_GD_QKV), blk(U_GD_QKV + 4), blk(U_GD_QKV + 8), blk(U_GD_Z),
                pl.BlockSpec((R, LANE), lambda g, t, h: (g * nT + t, U_GD_AB)),
                cws(0), cws(1), cws(2), one, one,
                pl.BlockSpec((1, DH), lambda g, t, h: (0, h)),
                cvs(0), cvs(1), cvs(2)]
    args = [proj] * 4 + [projg, cw, cw, cw, alog, dtb, ng, cv0, cv0, cv0]
    if s_init is not None:
        in_specs.append(st4)
        args.append(s_init)
    aliases = {}
    if prev_s is not None:
        aliases = {len(args): 1}
        in_specs.append(pl.BlockSpec(memory_space=pl.ANY))
        args.append(prev_s)
    kern = functools.partial(_gdn_kernel, L=L, TT=TT, gps=gps, t_valid=t_valid, nT=nT, li=li,
                             has_init=s_init is not None, has_prev=prev_s is not None)
    cv_shape = jax.ShapeDtypeStruct((cv0.shape[0], BR), F32)
    return pl.pallas_call(
        kern,
        grid=(G // gps, nT, HEADS),
        in_specs=in_specs,
        out_specs=[pl.BlockSpec((R, DH), lambda g, t, h: (g * nT + t, h)), s_out_spec, cvo, cvo, cvo],
        out_shape=[jax.ShapeDtypeStruct((n, BR), F32), s_shape, cv_shape, cv_shape, cv_shape],
        scratch_shapes=([pltpu.VMEM((R + gps * CARRY_ROWS, DH), F32)] * 3 + [pltpu.VMEM((R, DH), F32)] * 3
                        + [pltpu.VMEM((3, HEADS, gps * CARRY_ROWS, DH), F32), pltpu.VMEM((R, LANE), F32)]),
        input_output_aliases=aliases,
        compiler_params=_cparams(("arbitrary", "arbitrary", "arbitrary")),
        name="gdn",
    )(*args)


def _s5_prep_kernel(are_ref, aim_ref, ldt_ref, bre_ref, bim_ref, abr_ref, abi_ref, bbr_ref, bbi_ref):
    a_re = are_ref[...]
    a_im = aim_ref[...]
    dt = jnp.exp(ldt_ref[...])
    mag = jnp.exp(dt * a_re)
    ang = dt * a_im
    ab_r = mag * jnp.cos(ang)
    ab_i = mag * jnp.sin(ang)
    den = a_re * a_re + a_im * a_im
    nr = ab_r - 1.0
    ni = ab_i
    f_r = (nr * a_re + ni * a_im) / den
    f_i = (ni * a_re - nr * a_im) / den
    abr_ref[...] = ab_r
    abi_ref[...] = ab_i
    f_r = f_r[0:1, :]
    f_i = f_i[0:1, :]
    bbr_ref[...] = f_r * bre_ref[...] - f_i * bim_ref[...]
    bbi_ref[...] = f_r * bim_ref[...] + f_i * bre_ref[...]


def _s5_prep(a_re, a_im, log_dt, b_re, b_im):
    rep = lambda a: jnp.broadcast_to(a.reshape(1, S5_STATE), (SEQ_BLOCK, S5_STATE))
    ldt = rep(jnp.broadcast_to(log_dt[:, None], (S5_G, S5_P)))
    bt = lambda b: jnp.transpose(b, (2, 0, 1)).reshape(S5_N, S5_STATE)
    shp = lambda r: jax.ShapeDtypeStruct((r, S5_STATE), F32)
    abr, abi, bbr, bbi = pl.pallas_call(
        _s5_prep_kernel,
        out_shape=[shp(SEQ_BLOCK), shp(SEQ_BLOCK), shp(S5_N), shp(S5_N)],
        name="s5_prep",
    )(rep(a_re), rep(a_im), ldt, bt(b_re), bt(b_im))
    return abr, abi, bbr, bbi


def _gelu_tanh(x):
    return 0.5 * x * (1.0 + jnp.tanh(math.sqrt(2.0 / math.pi) * (x + 0.044715 * (x * x * x))))


def _s5_kernel(u_ref, z_ref, abr_ref, abi_ref, wbr_ref, wbi_ref, wcr_ref, wci_ref, d_ref, gw_ref, gb_ref,
               x0r_ref, x0i_ref, y_ref, xr_ref, xi_ref, hr_ref, hi_ref, ys_ref, *, TT, n_steps):
    ti = pl.program_id(1)
    NB = BR // LANE
    SB = S5_STATE // NB

    @pl.when(ti == 0)
    def _():
        xr_ref[...] = x0r_ref[...]
        xi_ref[...] = x0i_ref[...]

    for kb in range(NB):
        ub = u_ref[:, kb * LANE:(kb + 1) * LANE].astype(BF16)
        hr_ref[:, kb * SB:(kb + 1) * SB] = jnp.dot(ub, wbr_ref[kb], preferred_element_type=F32)
        hi_ref[:, kb * SB:(kb + 1) * SB] = jnp.dot(ub, wbi_ref[kb], preferred_element_type=F32)

    for kb in range(NB):
        sl = pl.ds(kb * SB, SB)
        ar = abr_ref[:, sl]
        ai = abi_ref[:, sl]

        def step(t, carry):
            xr, xi = carry
            rows = pl.ds(pl.multiple_of(t * SEQ_BLOCK, SEQ_BLOCK), SEQ_BLOCK)
            nxr = ar * xr - ai * xi + hr_ref[rows, sl]
            nxi = ar * xi + ai * xr + hi_ref[rows, sl]
            hr_ref[rows, sl] = nxr
            hi_ref[rows, sl] = nxi
            return nxr, nxi

        xr, xi = lax.fori_loop(0, n_steps, step, (xr_ref[:, sl], xi_ref[:, sl]))
        xr_ref[:, sl] = xr
        xi_ref[:, sl] = xi

    for kb in range(NB):
        sl = pl.ds(kb * SB, SB)
        cs = pl.ds(kb * LANE, LANE)
        yk = (jnp.dot(hr_ref[:, sl].astype(BF16), wcr_ref[kb], preferred_element_type=F32)
              - jnp.dot(hi_ref[:, sl].astype(BF16), wci_ref[kb], preferred_element_type=F32)
              + d_ref[:, cs] * u_ref[:, cs])
        ys_ref[:, cs] = _gelu_tanh(yk)
    ys = ys_ref[...]
    glu = ys * _sigmoid(jnp.dot(ys.astype(BF16), gw_ref[...], preferred_element_type=F32) + gb_ref[...])
    y_ref[...] = glu * _silu(z_ref[...])


def _s5(proj, abr, abi, wbr, wbi, wcr, wci, d, gw, gb, x0r, x0i, *, G, nT, TT, t_valid):
    assert nT == 1 or t_valid == nT * TT
    R = TT * SEQ_BLOCK
    n = proj.shape[0]
    full = lambda a: pl.BlockSpec(a.shape, lambda g, t, nd=a.ndim: (0,) * nd)
    st = pl.BlockSpec((SEQ_BLOCK, S5_STATE), lambda g, t: (g, 0))
    kern = functools.partial(_s5_kernel, TT=TT, n_steps=t_valid - (nT - 1) * TT)
    return pl.pallas_call(
        kern,
        grid=(G, nT),
        in_specs=[pl.BlockSpec((R, BR), lambda g, t: (g * nT + t, U_S5_U // 4)),
                  pl.BlockSpec((R, BR), lambda g, t: (g * nT + t, U_S5_Z // 4)),
                  full(abr), full(abi), full(wbr), full(wbi), full(wcr), full(wci), full(d), full(gw), full(gb),
                  st, st],
        out_specs=[pl.BlockSpec((R, BR), lambda g, t: (g * nT + t, 0)), st, st],
        out_shape=[jax.ShapeDtypeStruct((n, BR), F32),
                   jax.ShapeDtypeStruct(x0r.shape, F32),
                   jax.ShapeDtypeStruct(x0i.shape, F32)],
        scratch_shapes=[pltpu.VMEM((R, S5_STATE), F32), pltpu.VMEM((R, S5_STATE), F32),
                        pltpu.VMEM((R, BR), F32)],
        compiler_params=_cparams(("arbitrary", "arbitrary")),
        name="s5",
    )(proj, proj, abr, abi, wbr, wbi, wcr, wci, d, gw, gb, x0r, x0i)


def _lru_kernel(x_ref, z_ref, cw_ref, cb_ref, wa_ref, ba_ref, wx_ref, bx_ref, lam_ref, h0_ref, cv0_ref,
                y_ref, h_ref, cv_ref, xp_ref, xl_ref, a_ref, *, TT, n_steps, nT):
    ti = pl.program_id(1)
    R = TT * SEQ_BLOCK

    @pl.when(ti == 0)
    def _():
        h_ref[...] = h0_ref[...]

    _causal_conv_tile(x_ref, cw_ref, cb_ref, xp_ref, xl_ref, cv0_ref, cv_ref, ti=ti, R=R, width=BR,
                      tv_local=n_steps, act_fn=lambda a: a)

    xl = xl_ref[...]
    xb = xl.astype(BF16)
    r = _sigmoid(jnp.dot(xb, wa_ref[...], preferred_element_type=F32) + ba_ref[...])
    i = _sigmoid(jnp.dot(xb, wx_ref[...], preferred_element_type=F32) + bx_ref[...])
    log_a = -LRU_C * r * _softplus(-lam_ref[...])
    a = jnp.exp(log_a)
    a_ref[...] = a
    xl_ref[...] = jnp.sqrt(1.0 - a * a) * (i * xl)

    def step(t, h):
        rows = pl.ds(pl.multiple_of(t * SEQ_BLOCK, SEQ_BLOCK), SEQ_BLOCK)
        hn = a_ref[rows, :] * h + xl_ref[rows, :]
        xl_ref[rows, :] = hn
        return hn

    h_ref[...] = lax.fori_loop(0, n_steps, step, h_ref[...])
    y_ref[...] = xl_ref[...] * _silu(z_ref[...])


def _lru(proj, cw, cb, wa, ba, wx, bx, lam, h0, cv0, *, G, nT, TT, t_valid):
    assert nT == 1 or t_valid == nT * TT
    R = TT * SEQ_BLOCK
    n = proj.shape[0]
    full = lambda a: pl.BlockSpec(a.shape, lambda g, t, nd=a.ndim: (0,) * nd)
    st = pl.BlockSpec((SEQ_BLOCK, BR), lambda g, t: (g, 0))
    cvs = pl.BlockSpec((CARRY_ROWS, BR), lambda g, t: (g, 0))
    kern = functools.partial(_lru_kernel, TT=TT, n_steps=t_valid - (nT - 1) * TT, nT=nT)
    return pl.pallas_call(
        kern,
        grid=(G, nT),
        in_specs=[pl.BlockSpec((R, BR), lambda g, t: (g * nT + t, U_LRU_X // 4)),
                  pl.BlockSpec((R, BR), lambda g, t: (g * nT + t, U_LRU_Z // 4)),
                  full(cw), full(cb), full(wa), full(ba), full(wx), full(bx), full(lam), st, cvs],
        out_specs=[pl.BlockSpec((R, BR), lambda g, t: (g * nT + t, 0)), st, cvs],
        out_shape=[jax.ShapeDtypeStruct((n, BR), F32),
                   jax.ShapeDtypeStruct(h0.shape, F32),
                   jax.ShapeDtypeStruct(cv0.shape, F32)],
        scratch_shapes=[pltpu.VMEM((R + CARRY_ROWS, BR), F32), pltpu.VMEM((R, BR), F32),
                        pltpu.VMEM((R, BR), F32)],
        compiler_params=_cparams(("arbitrary", "arbitrary")),
        name="lru",
    )(proj, proj, cw, cb, wa, ba, wx, bx, lam, h0, cv0)


def _merge_kernel(yml_ref, ygd_ref, ys5_ref, ylru_ref, x_ref, p_ref, g_ref, wg_ref,
                  wbr_ref, wout_ref, wpg_ref, wple_ref, pg_ref, o_ref, *, p_btd, out_btd):
    tm = x_ref.shape[0]
    if p_btd:
        p = jnp.swapaxes(p_ref[...], 0, 1).reshape(tm, PLE_DIM)
    else:
        p = p_ref[...]
    x = x_ref[...]
    h = _rms(x, g_ref[...]).astype(BF16)
    merged = None
    for nb, y_ref in enumerate((yml_ref, ygd_ref, ys5_ref, ylru_ref)):
        pbr = jnp.dot(y_ref[...].astype(BF16), wbr_ref[nb], preferred_element_type=F32)
        gates = jnp.dot(h, wg_ref[:, nb * D_MODEL:(nb + 1) * D_MODEL], preferred_element_type=F32)
        term = _sigmoid(gates) * pbr
        merged = term if merged is None else merged + term
    out = jnp.dot(merged.astype(BF16), wout_ref[...], preferred_element_type=F32)
    r = x + _rms(out, pg_ref[...])
    gate = _sigmoid(jnp.dot(r.astype(BF16), wpg_ref[...], preferred_element_type=F32))
    res = r + gate * jnp.dot(p.astype(BF16), wple_ref[...], preferred_element_type=F32)
    if out_btd:
        o_ref[...] = jnp.swapaxes(res.reshape(tm // SEQ_BLOCK, SEQ_BLOCK, D_MODEL), 0, 1)
    else:
        o_ref[...] = res


def _merge(ys, x, p, g, wg_all, li, wbr, wout, wpg, wple, pg, tm, *, p_layer=None, out_btd=False):
    n = x.shape[0]
    tt = tm // SEQ_BLOCK
    row = lambda c: pl.BlockSpec((tm, c), lambda i: (i, 0))
    full = lambda a: pl.BlockSpec(a.shape, lambda i, nd=a.ndim: (0,) * nd, pipeline_mode=pl.Buffered(1))
    p_spec = row(PLE_DIM) if p_layer is None else pl.BlockSpec((None, SEQ_BLOCK, tt, PLE_DIM),
                                                               lambda i: (p_layer, 0, i, 0))
    if out_btd:
        out_spec = pl.BlockSpec((SEQ_BLOCK, tt, D_MODEL), lambda i: (0, i, 0))
        out_shape = jax.ShapeDtypeStruct((SEQ_BLOCK, n // SEQ_BLOCK, D_MODEL), F32)
    else:
        out_spec, out_shape = row(D_MODEL), jax.ShapeDtypeStruct((n, D_MODEL), F32)
    return pl.pallas_call(
        functools.partial(_merge_kernel, p_btd=p_layer is not None, out_btd=out_btd),
        grid=(n // tm,),
        in_specs=[row(BR), row(BR), row(BR), row(BR), row(D_MODEL), p_spec, full(g),
                  pl.BlockSpec((None, D_MODEL, GATE_COLS), lambda i: (li, 0, 0), pipeline_mode=pl.Buffered(1)),
                  full(wbr), full(wout), full(wpg), full(wple), full(pg)],
        out_specs=out_spec,
        out_shape=out_shape,
        compiler_params=_cparams(("parallel",)),
        name="merge",
    )(*ys, x, p, g, wg_all, wbr, wout, wpg, wple, pg)


W_WINDOW = W_CHUNK + LANE


def _w_relayout_kernel(tbl_ref, win_ref, tail_ref, o_ref):
    shift = tbl_ref[pl.program_id(1), 1]
    src = jnp.concatenate([win_ref[0], tail_ref[...]], axis=1)
    width = src.shape[1]
    o_ref[...] = pltpu.roll(src, jnp.where(shift == 0, 0, width - shift), axis=1)[:, :W_CHUNK].astype(BF16)


def _w_relayout(w_in, runs, n_chunks):
    depth, d_model, d_in = w_in.shape
    tail_unit = (d_in - 1) // LANE
    last_start = (tail_unit * LANE - W_WINDOW) // LANE
    tbl = [None] * n_chunks
    for c0, src, count in runs:
        for j in range(count):
            off = src + j * W_CHUNK
            start = min(off // LANE, last_start)
            assert off - start * LANE + W_CHUNK <= W_WINDOW + d_in - tail_unit * LANE
            tbl[c0 + j] = (start, off - start * LANE)
    assert all(t is not None for t in tbl)
    grid_spec = pltpu.PrefetchScalarGridSpec(
        num_scalar_prefetch=1,
        grid=(depth, n_chunks),
        in_specs=[pl.BlockSpec((pl.Element(1), pl.Element(d_model), pl.Element(W_WINDOW)),
                               lambda l, c, t: (l, 0, t[c, 0] * LANE)),
                  pl.BlockSpec((None, d_model, LANE), lambda l, c, t: (l, 0, tail_unit))],
        out_specs=pl.BlockSpec((None, d_model, W_CHUNK), lambda l, c, t: (l, 0, c)),
    )
    return pl.pallas_call(
        _w_relayout_kernel,
        grid_spec=grid_spec,
        out_shape=jax.ShapeDtypeStruct((depth, d_model, n_chunks * W_CHUNK), BF16),
        compiler_params=_cparams(("arbitrary", "arbitrary")),
        name="w_relayout",
    )(jnp.asarray(tbl, jnp.int32), w_in, w_in)


def _permute_w_in(w_in):
    sizes = (BR, BR, BR, HEADS, HEADS, BR, BR, 3 * BR, HEADS, HEADS, BR, BR, BR, BR, BR, GATE_COLS)
    offs = [0]
    for s in sizes:
        offs.append(offs[-1] + s)
    (ml_q, _, _, ml_i, _, ml_o, _, gd_qkv, gd_a, _, gd_z, _, _, _, _, gates) = offs[:-1]
    upc = W_CHUNK // LANE
    w_main = _w_relayout(w_in, [(U_GD_QKV // upc, gd_qkv, 3), (U_ML // upc, ml_q, 3), (U_ML // upc + 3, ml_o, 2),
                                (U_GD_Z // upc, gd_z, 5)], N_UNITS // upc)
    w_gates = _w_relayout(w_in, [(0, gates, GATE_COLS // W_CHUNK)], GATE_COLS // W_CHUNK)
    unit = lambda off: jnp.pad(w_in[:, :, off:off + 2 * HEADS], ((0, 0), (0, 0), (0, LANE - 2 * HEADS)))
    w_scalar = jnp.concatenate([unit(ml_i), unit(gd_a)], axis=-1).astype(BF16)
    return w_main, w_gates, w_scalar


def _pad_lanes(*vecs):
    v = jnp.concatenate(vecs)
    return jnp.pad(v, (0, LANE - v.shape[0])).reshape(1, LANE)


def _block_diag(blocks):
    n, a, b = blocks.shape
    eye = jnp.eye(n, dtype=blocks.dtype)
    return jnp.einsum('ij,iab->iajb', eye, blocks).reshape(n * a, n * b)


def _to_rows(x, G):
    B, T, C = x.shape
    return x.reshape(G, SEQ_BLOCK, T, C).transpose(0, 2, 1, 3).reshape(G * T * SEQ_BLOCK, C)


def _from_rows(y, G, T):
    C = y.shape[-1]
    return y.reshape(G, T, SEQ_BLOCK, C).transpose(0, 2, 1, 3).reshape(G * SEQ_BLOCK, T, C)


def _group(x, p, states, params, *, T, t_valid, TT, L, tm_proj, tm_merge, gps=1):
    B = x.shape[0]
    G = B // SEQ_BLOCK
    nT = T // TT
    depth = p.shape[0]
    xr = _to_rows(x, G)
    tm_proj = min(tm_proj, xr.shape[0])
    tm_merge = min(tm_merge, xr.shape[0])
    new_states = []
    c_all = s_all = None
    gps = math.gcd(gps, G)
    mat = dict(n_layers=depth, B=B, G=G, nT=nT, TT=TT, L=L, t_valid=t_valid, gps=gps)
    for li in range(depth):
        lp = {k: v[li] for k, v in params.items() if not k.startswith('w_in_')}
        if states is None:
            ml_init = s_init = None
            gcv0 = jnp.zeros((G * CARRY_ROWS, 3 * BR), F32)
            x0r = jnp.zeros((B, S5_STATE), F32)
            x0i = jnp.zeros((B, S5_STATE), F32)
            h0 = jnp.zeros((B, BR), F32)
            lcv0 = jnp.zeros((G * CARRY_ROWS, BR), F32)
        else:
            (n0, m0, gcv0, x0r, x0i, h0, lcv0) = [states[j][li] for j in (1, 2, 4, 5, 6, 7, 8)]
            n0 = n0.reshape(G, SEQ_BLOCK, HEADS, DH).transpose(0, 2, 1, 3).reshape(B * HEADS, DH)
            ml_init = (states[0], n0, jnp.pad(m0, ((0, 0), (0, LANE - HEADS))))
            s_init = states[3]
            gcv0 = _to_rows(gcv0, G)
            x0r = x0r.reshape(B, S5_STATE)
            x0i = x0i.reshape(B, S5_STATE)
            lcv0 = _to_rows(lcv0, G)
        in_place = G == 1

        proj, projg = _in_proj(xr, lp['prenorm_g'], params['w_in_main'], params['w_in_scalar'], li, tm_proj)

        y_ml, c_all, n1, m1 = _mlstm(proj, projg, lp['ml_ifb'], lp['ml_norm_g'], ml_init, c_all, li=li, **mat)
        y_gd, s_all, cvq, cvk, cvv = _gdn(proj, projg, lp['gd_conv_w'], lp['gd_alog'], lp['gd_dtb'], lp['gd_norm_g'],
                                          gcv0, s_init, s_all, li=li, **dict(mat, L=min(L, GDN_CHUNK)))
        gcv1 = jnp.concatenate([cvq, cvk, cvv], axis=-1)
        y_s5, x1r, x1i = _s5(proj, lp['s5_abr'], lp['s5_abi'], lp['s5_wbr'], lp['s5_wbi'], lp['s5_wcr'],
                             lp['s5_wci'], lp['s5_d'], lp['s5_glu_w'], lp['s5_glu_b'], x0r, x0i,
                             G=G, nT=nT, TT=TT, t_valid=t_valid)
        y_lru, h1, lcv1 = _lru(proj, lp['lru_conv_w'], lp['lru_conv_b'], lp['lru_wa'], lp['lru_ba'],
                               lp['lru_wx'], lp['lru_bx'], lp['lru_lam'], h0, lcv0,
                               G=G, nT=nT, TT=TT, t_valid=t_valid)
        xr = _merge((y_ml, y_gd, y_s5, y_lru), xr, p if in_place else _to_rows(p[li], G),
                    lp['prenorm_g'], params['w_in_gates'], li, lp['w_branch'], lp['w_out'], lp['w_ple_gate'], lp['w_ple'], lp['postnorm_g'], tm_merge,
                    p_layer=li if in_place else None, out_btd=in_place and li == depth - 1)

        n1 = n1.reshape(G, HEADS, SEQ_BLOCK, DH).transpose(0, 2, 1, 3)
        new_states.append((n1.reshape(B, HEADS, DH), m1[:, :HEADS],
                           _from_rows(gcv1, G, CONV_W - 1), x1r.reshape(B, S5_G, S5_P),
                           x1i.reshape(B, S5_G, S5_P), h1, _from_rows(lcv1, G, CONV_W - 1)))
    y = xr if G == 1 else _from_rows(xr, G, T)
    n_s, m_s, gcv_s, xr_s, xi_s, h_s, lcv_s = (jnp.stack([ns[j] for ns in new_states]) for j in range(7))
    return y, (c_all, n_s, m_s, s_all, gcv_s, xr_s, xi_s, h_s, lcv_s)


def _prepare_params(prenorm_g, postnorm_g, w_in, ml_bi, ml_bf, ml_norm_g, gd_conv_w, gd_a_log, gd_dt_bias,
                    gd_norm_g, s5_a_re, s5_a_im, s5_log_dt, s5_b_re, s5_b_im, s5_c_re, s5_c_im, s5_d,
                    s5_glu_w, s5_glu_b, lru_conv_w, lru_conv_b, lru_wa, lru_ba, lru_wx, lru_bx, lru_lam,
                    w_branch, w_out, w_ple, w_ple_gate):
    depth = w_in.shape[0]
    row = lambda a: a.reshape(depth, 1, -1)
    per_layer = lambda f, *a: jnp.stack([f(*[x[i] for x in a]) for i in range(depth)])
    prm = dict(
        prenorm_g=row(prenorm_g), postnorm_g=row(postnorm_g),
        ml_ifb=per_layer(_pad_lanes, ml_bi, ml_bf), ml_norm_g=row(ml_norm_g),
        gd_conv_w=gd_conv_w, gd_norm_g=row(gd_norm_g),
        gd_alog=per_layer(lambda a: _pad_lanes(a), gd_a_log),
        gd_dtb=per_layer(lambda a: _pad_lanes(a), gd_dt_bias),
        s5_d=row(s5_d), s5_glu_w=s5_glu_w.astype(BF16), s5_glu_b=row(s5_glu_b),
        lru_conv_w=lru_conv_w, lru_conv_b=row(lru_conv_b),
        lru_wa=per_layer(_block_diag, lru_wa).astype(BF16), lru_ba=row(lru_ba),
        lru_wx=per_layer(_block_diag, lru_wx).astype(BF16), lru_bx=row(lru_bx), lru_lam=row(lru_lam),
        w_branch=w_branch.astype(BF16), w_out=w_out.astype(BF16), w_ple=w_ple.astype(BF16),
        w_ple_gate=w_ple_gate.astype(BF16),
    )
    abr, abi, wbr, wbi, wcr, wci = [], [], [], [], [], []
    nb = BR // LANE
    gpb = S5_G // nb
    for i in range(depth):
        a_r, a_i, bbr, bbi = _s5_prep(s5_a_re[i], s5_a_im[i], s5_log_dt[i], s5_b_re[i], s5_b_im[i])
        abr.append(a_r)
        abi.append(a_i)
        in_blocks = lambda bb: jnp.stack([_block_diag(bb.reshape(S5_N, S5_G, S5_P).transpose(1, 0, 2)
                                                      [k * gpb:(k + 1) * gpb]) for k in range(nb)])
        out_blocks = lambda c: jnp.stack([_block_diag(jnp.transpose(c, (0, 2, 1))[k * gpb:(k + 1) * gpb])
                                          for k in range(nb)])
        wbr.append(in_blocks(bbr).astype(BF16))
        wbi.append(in_blocks(bbi).astype(BF16))
        wcr.append(out_blocks(s5_c_re[i]).astype(BF16))
        wci.append(out_blocks(s5_c_im[i]).astype(BF16))
    prm.update(s5_abr=jnp.stack(abr), s5_abi=jnp.stack(abi), s5_wbr=jnp.stack(wbr), s5_wbi=jnp.stack(wbi),
               s5_wcr=jnp.stack(wcr), s5_wci=jnp.stack(wci))
    prm['w_in_main'], prm['w_in_gates'], prm['w_in_scalar'] = _permute_w_in(w_in)
    return prm


def kernel(x_prompt, x_sample, state_mlstm_c, state_mlstm_n, state_mlstm_m, state_gdn_s, state_gdn_conv, state_s5_re, state_s5_im, state_lru_h, state_lru_conv, p_prompt, p_sample, prenorm_g, postnorm_g, w_in, ml_bi, ml_bf, ml_norm_g, gd_conv_w, gd_a_log, gd_dt_bias, gd_norm_g, s5_a_re, s5_a_im, s5_log_dt, s5_b_re, s5_b_im, s5_c_re, s5_c_im, s5_d, s5_glu_w, s5_glu_b, lru_conv_w, lru_conv_b, lru_wa, lru_ba, lru_wx, lru_bx, lru_lam, w_branch, w_out, w_ple, w_ple_gate):
    prm = _prepare_params(prenorm_g, postnorm_g, w_in, ml_bi, ml_bf, ml_norm_g, gd_conv_w, gd_a_log,
                          gd_dt_bias, gd_norm_g, s5_a_re, s5_a_im, s5_log_dt, s5_b_re, s5_b_im, s5_c_re,
                          s5_c_im, s5_d, s5_glu_w, s5_glu_b, lru_conv_w, lru_conv_b, lru_wa, lru_ba, lru_wx,
                          lru_bx, lru_lam, w_branch, w_out, w_ple, w_ple_gate)

    t_p = x_prompt.shape[1]
    tt_p = math.gcd(t_p, PROMPT_CHUNK)
    y_prompt, pr = _group(x_prompt, p_prompt, None, prm, T=t_p, t_valid=t_p, TT=tt_p, L=tt_p,
                          tm_proj=TM_PROJ, tm_merge=TM_MERGE)

    t_s = x_sample.shape[1]
    t_pad = -(-t_s // SEQ_BLOCK) * SEQ_BLOCK
    pad_t = lambda a, ax: jnp.pad(a, [(0, t_pad - t_s) if i == ax else (0, 0) for i in range(a.ndim)])
    sample_states = (state_mlstm_c, state_mlstm_n, state_mlstm_m, state_gdn_s, state_gdn_conv,
                     state_s5_re, state_s5_im, state_lru_h, state_lru_conv)
    y_s, sa = _group(pad_t(x_sample, 1), pad_t(p_sample, 2), sample_states, prm, T=t_pad, t_valid=t_s,
                     TT=t_pad, L=t_pad, tm_proj=TM_PROJ, tm_merge=TM_MERGE, gps=SAMPLE_GROUPS_PER_STEP)
    y_sample = y_s[:, :t_s]
    return (y_prompt, y_sample) + pr + sa
```

```python
import functools
import math

import jax
import jax.numpy as jnp
from jax import lax
from jax.experimental import pallas as pl
from jax.experimental.pallas import tpu as pltpu

F32 = jnp.float32
BF16 = jnp.bfloat16

D_MODEL = 1024
BR = 512
HEADS = 4
DH = 128
CONV_W = 4
S5_G = 32
S5_N = 16
S5_P = 64
S5_STATE = S5_G * S5_P
LRU_BLOCKS = 8
LRU_BD = 64
LRU_C = 8.0
PLE_DIM = 256
EPS = 1e-6
NEG = -1e30

LANE = 128
SEQ_BLOCK = 8
CARRY_ROWS = (CONV_W - 1) * SEQ_BLOCK
PROMPT_CHUNK = 128
GDN_CHUNK = 64
SAMPLE_GROUPS_PER_STEP = 2
TM_PROJ = 512
TM_MERGE = 512

U_GD_QKV = 0
U_ML = 12
U_GD_Z = 32
U_S5_U = 36
U_S5_Z = 40
U_LRU_X = 44
U_LRU_Z = 48
N_UNITS = 52
PROJ_COLS = N_UNITS * LANE
U_ML_IF = 0
U_GD_AB = 1
GATE_COLS = 4 * D_MODEL
W_CHUNK = 4 * LANE

VMEM_LIMIT = 56 * 1024 * 1024


def _cparams(sem):
    return pltpu.CompilerParams(dimension_semantics=sem, vmem_limit_bytes=VMEM_LIMIT)


def _sigmoid(x):
    return 0.5 * jnp.tanh(0.5 * x) + 0.5


def _silu(x):
    return x * _sigmoid(x)


def _softplus(x):
    return jnp.maximum(x, 0.0) + jnp.log1p(jnp.exp(-jnp.abs(x)))


def _dot(a, b):
    return jnp.dot(a.astype(BF16), b.astype(BF16), preferred_element_type=F32)


def _dot_nt(a, b):
    return lax.dot_general(a.astype(BF16), b.astype(BF16), (((1,), (1,)), ((), ())),
                           preferred_element_type=F32)


def _dot_tn(a, b):
    return lax.dot_general(a.astype(BF16), b.astype(BF16), (((0,), (0,)), ((), ())),
                           preferred_element_type=F32)


def _cumsum_rows(x):
    n = x.shape[0]
    ri = lax.broadcasted_iota(jnp.int32, x.shape, 0)
    s = 1
    while s < n:
        x = x + jnp.where(ri >= s, pltpu.roll(x, s, axis=0), 0.0)
        s *= 2
    return x


def _row_from_col(col, eye):
    return jnp.sum(jnp.where(eye, col, 0.0), axis=0, keepdims=True)


def _head_rms(h, g):
    return h * lax.rsqrt(jnp.mean(h * h, axis=-1, keepdims=True) + EPS) * g


def _rms(x, g):
    return x * lax.rsqrt(jnp.mean(x * x, axis=-1, keepdims=True) + EPS) * g


def _in_proj_kernel(x_ref, g_ref, w_ref, wg_ref, o_ref, og_ref):
    h = _rms(x_ref[...], g_ref[...]).astype(BF16)
    o_ref[...] = _dot_nt(h, w_ref[...])
    og_ref[...] = _dot_nt(h, wg_ref[...])


def _in_proj(x, g, w_all, wg_all, li, tm):
    n = x.shape[0]
    resident = lambda c: pl.BlockSpec((None, c, D_MODEL), lambda i: (li, 0, 0), pipeline_mode=pl.Buffered(1))
    return pl.pallas_call(
        _in_proj_kernel,
        grid=(n // tm,),
        in_specs=[pl.BlockSpec((tm, D_MODEL), lambda i: (i, 0)),
                  pl.BlockSpec((1, D_MODEL), lambda i: (0, 0)),
                  resident(PROJ_COLS), resident(2 * LANE)],
        out_specs=[pl.BlockSpec((tm, PROJ_COLS), lambda i: (i, 0)),
                   pl.BlockSpec((tm, 2 * LANE), lambda i: (i, 0))],
        out_shape=[jax.ShapeDtypeStruct((n, PROJ_COLS), F32), jax.ShapeDtypeStruct((n, 2 * LANE), F32)],
        compiler_params=_cparams(("parallel",)),
        name="in_proj",
    )(x, g, w_all, wg_all)


def _token_cumsum(x, tok_in_chunk, L):
    s = 1
    while s < L:
        x = x + jnp.where(tok_in_chunk >= s, pltpu.roll(x, s * SEQ_BLOCK, axis=0), 0.0)
        s *= 2
    return x


def _tile_token_ids(ti, TT, L):
    R = TT * SEQ_BLOCK
    t_local = lax.broadcasted_iota(jnp.int32, (R, LANE), 0) // SEQ_BLOCK
    return ti * TT + t_local, t_local % L


def _token_scan(x, op, fill):
    s = 1
    while s < x.shape[0]:
        shifted = jnp.concatenate([jnp.full((s,) + x.shape[1:], fill, x.dtype), x[:-s]], axis=0)
        x = op(x, shifted)
        s *= 2
    return x


ML_R, ML_M, ML_INTER, ML_ENEG, ML_WS = (j * HEADS for j in range(5))


def _mlstm_gate_tile(if_ref, ifb_ref, m_ref, sc_ref, gt_ref, *, ti, T, t_valid):
    shape = (T, SEQ_BLOCK, LANE)
    lane = lax.broadcasted_iota(jnp.int32, shape, 2)
    valid = (ti * T + lax.broadcasted_iota(jnp.int32, shape, 0)) < t_valid
    rot = lambda x, k: pltpu.roll(x.reshape(T * SEQ_BLOCK, LANE), k, axis=1).reshape(shape)
    ifv = (if_ref[...] + ifb_ref[...]).reshape(shape)
    logf = jnp.minimum(ifv, 0.0) - jnp.log1p(jnp.exp(-jnp.abs(ifv)))
    bcum = rot(_token_scan(jnp.where(valid, logf, 0.0), jnp.add, 0.0), LANE - HEADS)
    ig = jnp.where(valid, ifv, NEG)
    r = ig - bcum
    m_prev = m_ref[...]
    big_m = jnp.maximum(m_prev[None], _token_scan(r, jnp.maximum, NEG))
    inter = jnp.exp(m_prev[None] - big_m)
    eneg = jnp.exp(-(bcum + big_m))
    b_last = bcum[T - 1]
    g = b_last[None] - bcum + ig
    m_new = jnp.maximum(b_last + m_prev, jnp.max(g, axis=0))
    ws = jnp.exp(g - m_new[None])
    head_lanes = lane[0] < HEADS
    sc_ref[...] = jnp.where(head_lanes, jnp.exp(b_last + m_prev - m_new), 0.0)
    m_ref[...] = jnp.where(head_lanes, m_new, 0.0)
    packed = jnp.where(lane < ML_M, r,
                       jnp.where(lane < ML_INTER, rot(big_m, ML_M),
                                 jnp.where(lane < ML_ENEG, rot(inter, ML_INTER),
                                           jnp.where(lane < ML_WS, rot(eneg, ML_ENEG), rot(ws, ML_WS)))))
    gt_ref[...] = packed.reshape(T * SEQ_BLOCK, LANE)


def _own_layer(state_ref, li, has_prev, first_step):
    if has_prev:
        return state_ref

    @pl.when(first_step)
    def _():
        for other in range(state_ref.shape[0]):
            if other != li:
                state_ref[other] = jnp.zeros(state_ref.shape[1:], F32)

    return state_ref.at[li]


def _mlstm_kernel(*refs, L, gps, t_valid, li, has_init, has_prev, single_tile):
    q_ref, k_ref, v_ref, o_ref, z_ref, if_ref, ifb_ref, ng_ref = refs[:8]
    n_in = 8 + (3 if has_init else 0) + (1 if has_prev else 0)
    y_ref, c_ref, n_ref, m_ref, gt_ref, sc_ref, rt_ref = refs[n_in:]
    ti = pl.program_id(1)
    h = pl.program_id(2)
    R = L * SEQ_BLOCK
    seqs = range(gps * SEQ_BLOCK)
    rows = [pl.ds((j // SEQ_BLOCK) * R + j % SEQ_BLOCK, L, stride=SEQ_BLOCK) for j in seqs]
    n_rows = [pl.ds((j // SEQ_BLOCK) * SEQ_BLOCK * HEADS + h * SEQ_BLOCK + j % SEQ_BLOCK, 1) for j in seqs]
    use_transpose = L == LANE
    c_ref = _own_layer(c_ref, li, has_prev, (ti == 0) & (h == 0))
    c_src = refs[8] if (has_init and single_tile) else c_ref

    @pl.when((ti == 0) & (h == 0))
    def _():
        if has_init:
            if not single_tile:
                c_ref[...] = refs[8][...]
            n_ref[...] = refs[9][...]
            m_ref[...] = refs[10][...]
        else:
            c_ref[...] = jnp.zeros(c_ref.shape, F32)
            n_ref[...] = jnp.zeros(n_ref.shape, F32)
            m_ref[...] = jnp.zeros(m_ref.shape, F32)

    @pl.when(h == 0)
    def _():
        for gi in range(gps):
            grp = pl.ds(gi * SEQ_BLOCK, SEQ_BLOCK)
            tile = pl.ds(gi * R, R)
            _mlstm_gate_tile(if_ref.at[tile], ifb_ref, m_ref.at[grp], sc_ref.at[grp], gt_ref.at[tile],
                             ti=ti, T=L, t_valid=t_valid)
        if use_transpose:
            for b in seqs:
                rt_ref[b * SEQ_BLOCK:(b + 1) * SEQ_BLOCK, :] = gt_ref[rows[b], :].T[:SEQ_BLOCK, :]

    ri = lax.broadcasted_iota(jnp.int32, (L, L), 0)
    ci = lax.broadcasted_iota(jnp.int32, (L, L), 1)
    causal = ci <= ri
    eye = ci == ri
    to_lane0 = jnp.where(h == 0, 0, LANE - h)
    sc_all = pltpu.roll(sc_ref[...], to_lane0, axis=1)
    col = lambda tile, off: tile[:, off:off + 1]

    st = []
    for b in seqs:
        gq = pltpu.roll(gt_ref[rows[b], :], to_lane0, axis=1)
        q = q_ref[rows[b], :] * (DH ** -0.5)
        v = v_ref[rows[b], :]
        kw = k_ref[rows[b], :] * col(gq, ML_WS)
        cmat = c_src[b, h]
        sc = sc_all[b:b + 1, 0:1]
        st.append(dict(gq=gq, q=q, v=v, kw=kw, sc=sc, qk=_dot_nt(q, k_ref[rows[b], :]), qc=_dot(q, cmat)))
        c_ref[b, h] = sc * cmat + _dot_tn(kw, v)
    for b in seqs:
        s_ = st[b]
        if use_transpose:
            r_row = rt_ref[pl.ds(b * SEQ_BLOCK + h, 1), :]
        else:
            r_row = _row_from_col(col(s_['gq'], ML_R), eye)
        s_['s'] = s_['qk'] * jnp.where(causal, jnp.exp(r_row - col(s_['gq'], ML_M)), 0.0)
    for b in seqs:
        s_ = st[b]
        srow = n_rows[b]
        nvec = n_ref[srow, :]
        inter = col(s_['gq'], ML_INTER)
        s_['num'] = _dot(s_['s'], s_['v']) + inter * s_['qc']
        den = (jnp.sum(s_['s'], axis=1, keepdims=True)
               + inter * jnp.sum(s_['q'] * nvec, axis=1, keepdims=True))
        s_['den'] = jnp.maximum(jnp.abs(den), col(s_['gq'], ML_ENEG))
        n_ref[srow, :] = s_['sc'] * nvec + jnp.sum(s_['kw'], axis=0, keepdims=True)
    for b in seqs:
        s_ = st[b]
        yn = _head_rms(s_['num'] / s_['den'], ng_ref[...])
        y_ref[rows[b], :] = yn * _sigmoid(o_ref[rows[b], :]) * _silu(z_ref[rows[b], :])


def _state_specs(li, n_layers, B, has_prev, gps):
    nseq = gps * SEQ_BLOCK
    one = pl.BlockSpec((None, nseq, HEADS, DH, DH), lambda g, t, h: (li, g, 0, 0, 0))
    every = pl.BlockSpec((n_layers, nseq, HEADS, DH, DH), lambda g, t, h: (0, g, 0, 0, 0))
    shape = jax.ShapeDtypeStruct((n_layers, B, HEADS, DH, DH), F32)
    return one, (one if has_prev else every), shape


def _mlstm(proj, projg, ifb, ng, init, prev_c, *, li, n_layers, B, G, nT, TT, L, t_valid, gps):
    assert TT == L and G % gps == 0 and (gps == 1 or nT == 1)
    R = TT * SEQ_BLOCK * gps
    n = proj.shape[0]
    blk = lambda u: pl.BlockSpec((R, DH), lambda g, t, h, u=u: (g * nT + t, u + h))
    st4, c_out_spec, c_shape = _state_specs(li, n_layers, B, prev_c is not None, gps)
    n_spec = pl.BlockSpec((gps * SEQ_BLOCK * HEADS, LANE), lambda g, t, h: (g, 0))
    m_spec = pl.BlockSpec((gps * SEQ_BLOCK, LANE), lambda g, t, h: (g, 0))
    in_specs = [blk(U_ML), blk(U_ML + 4), blk(U_ML + 8), blk(U_ML + 12), blk(U_ML + 16),
                pl.BlockSpec((R, LANE), lambda g, t, h: (g * nT + t, U_ML_IF)),
                pl.BlockSpec((1, LANE), lambda g, t, h: (0, 0)),
                pl.BlockSpec((1, DH), lambda g, t, h: (0, h))]
    args = [proj] * 5 + [projg, ifb, ng]
    if init is not None:
        in_specs += [st4, n_spec, m_spec]
        args += list(init)
    aliases = {}
    if prev_c is not None:
        aliases = {len(args): 1}
        in_specs.append(pl.BlockSpec(memory_space=pl.ANY))
        args.append(prev_c)
    kern = functools.partial(_mlstm_kernel, L=L, gps=gps, t_valid=t_valid, li=li, has_init=init is not None,
                             has_prev=prev_c is not None, single_tile=nT == 1)
    return pl.pallas_call(
        kern,
        grid=(G // gps, nT, HEADS),
        in_specs=in_specs,
        out_specs=[pl.BlockSpec((R, DH), lambda g, t, h: (g * nT + t, h)), c_out_spec, n_spec, m_spec],
        out_shape=[jax.ShapeDtypeStruct((n, BR), F32), c_shape,
                   jax.ShapeDtypeStruct((B * HEADS, LANE), F32), jax.ShapeDtypeStruct((B, LANE), F32)],
        scratch_shapes=[pltpu.VMEM((R, LANE), F32), pltpu.VMEM((gps * SEQ_BLOCK, LANE), F32),
                        pltpu.VMEM((gps * SEQ_BLOCK * SEQ_BLOCK, LANE), F32)],
        input_output_aliases=aliases,
        compiler_params=_cparams(("arbitrary", "arbitrary", "arbitrary")),
        name="mlstm",
    )(*args)


def _causal_conv_tile(x_ref, w_ref, b_ref, xp_ref, act_ref, cv0_ref, cv_ref, *, ti, R, width, tv_local,
                      act_fn, carry_ref=None):
    @pl.when(ti == 0)
    def _():
        xp_ref[0:CARRY_ROWS, :] = cv0_ref[...]

    if carry_ref is not None:
        @pl.when(ti > 0)
        def _():
            xp_ref[0:CARRY_ROWS, :] = carry_ref[...]

    xp_ref[CARRY_ROWS:CARRY_ROWS + R, :] = x_ref[...]
    RB = next(rb for rb in (128, 32, SEQ_BLOCK) if R % rb == 0 and rb * min(width, BR) <= 128 * LANE)

    def blk(i, carry):
        r0 = pl.multiple_of(i * RB, SEQ_BLOCK)
        cw = min(width, BR)
        for c0 in range(0, width, cw):
            cs = pl.ds(c0, cw)
            acc = xp_ref[pl.ds(r0, RB), cs] * w_ref[0:1, cs]
            for j in range(1, CONV_W):
                acc = acc + xp_ref[pl.ds(r0 + j * SEQ_BLOCK, RB), cs] * w_ref[j:j + 1, cs]
            if b_ref is not None:
                acc = acc + b_ref[:, cs]
            act_ref[pl.ds(r0, RB), cs] = act_fn(acc)
        return carry

    lax.fori_loop(0, R // RB, blk, 0)

    cv_ref[...] = xp_ref[tv_local * SEQ_BLOCK:tv_local * SEQ_BLOCK + CARRY_ROWS, :]
    if carry_ref is not None:
        carry_ref[...] = xp_ref[R:R + CARRY_ROWS, :]
    else:
        xp_ref[0:CARRY_ROWS, :] = xp_ref[R:R + CARRY_ROWS, :]


def _gdn_kernel(*refs, L, TT, gps, t_valid, nT, li, has_init, has_prev):
    (q_ref, k_ref, v_ref, z_ref, ab_ref, cwq_ref, cwk_ref, cwv_ref, alog_ref, dtb_ref, ng_ref,
     cvq0_ref, cvk0_ref, cvv0_ref) = refs[:14]
    n_in = 14 + (1 if has_init else 0) + (1 if has_prev else 0)
    (y_ref, s_ref, cvq_ref, cvk_ref, cvv_ref,
     xpq_ref, xpk_ref, xpv_ref, aq_ref, ak_ref, av_ref, carry_ref, gb_ref) = refs[n_in:]
    ti = pl.program_id(1)
    h = pl.program_id(2)
    R = TT * SEQ_BLOCK
    s_ref = _own_layer(s_ref, li, has_prev, (ti == 0) & (h == 0))
    single_chunk = nT == 1 and TT == L
    s_src = refs[14] if (has_init and single_chunk) else s_ref

    @pl.when((ti == 0) & (h == 0))
    def _():
        if has_init and single_chunk:
            pass
        elif has_init:
            s_ref[...] = refs[14][...]
        else:
            s_ref[...] = jnp.zeros(s_ref.shape, F32)

    head_lanes = pl.ds(pl.multiple_of(h * DH, DH), DH)
    for j, (x_ref, cw_ref, xp_ref, a_ref, cv0_ref, cv_ref) in enumerate((
            (q_ref, cwq_ref, xpq_ref, aq_ref, cvq0_ref, cvq_ref),
            (k_ref, cwk_ref, xpk_ref, ak_ref, cvk0_ref, cvk_ref),
            (v_ref, cwv_ref, xpv_ref, av_ref, cvv0_ref, cvv_ref))):
        for gi in range(gps):
            tile = pl.ds(gi * R, R)
            cvr = pl.ds(gi * CARRY_ROWS, CARRY_ROWS)
            _causal_conv_tile(x_ref.at[tile], cw_ref, None, xp_ref.at[pl.ds(gi * (R + CARRY_ROWS), R + CARRY_ROWS)],
                              a_ref.at[tile], cv0_ref.at[cvr], cv_ref.at[cvr, head_lanes], ti=ti, R=R, width=DH,
                              tv_local=t_valid - (nT - 1) * TT, act_fn=_silu, carry_ref=carry_ref.at[j, h, cvr])

    @pl.when(h == 0)
    def _():
        tok, tok_in_chunk = _tile_token_ids(ti, TT, L)
        valid = tok < t_valid
        lane = lax.broadcasted_iota(jnp.int32, (R, LANE), 1)
        for gi in range(gps):
            tile = pl.ds(gi * R, R)
            abv = ab_ref[tile, :]
            g_all = -jnp.exp(alog_ref[...]) * _softplus(abv + dtb_ref[...])
            gam = _token_cumsum(jnp.where(valid, g_all, 0.0), tok_in_chunk, L)
            gb_ref[tile, :] = jnp.where(lane < HEADS, gam, jnp.where(valid, _sigmoid(abv), 0.0))

    ri = lax.broadcasted_iota(jnp.int32, (L, L), 0)
    ci = lax.broadcasted_iota(jnp.int32, (L, L), 1)
    causal = ci <= ri
    strict = ci < ri
    eye = ci == ri
    lane = lax.broadcasted_iota(jnp.int32, (L, LANE), 1)
    sel_a = lane == h
    sel_b = lane == HEADS + h
    pick = lambda tile, sel: jnp.sum(jnp.where(sel, tile, 0.0), axis=1, keepdims=True)
    n_double = int(math.log2(L)) - 1

    seqs = range(gps * SEQ_BLOCK)
    for c in range(TT // L):
        rows, st = [], []
        for b in seqs:
            r = pl.ds((b // SEQ_BLOCK) * R + c * L * SEQ_BLOCK + b % SEQ_BLOCK, L, stride=SEQ_BLOCK)
            rows.append(r)
            gbv = gb_ref[r, :]
            gam = pick(gbv, sel_a)
            beta = pick(gbv, sel_b)
            dec = jnp.where(causal, jnp.exp(gam - _row_from_col(gam, eye)), 0.0)
            q = aq_ref[r, :]
            k = ak_ref[r, :]
            q = q * lax.rsqrt(jnp.sum(q * q, axis=-1, keepdims=True) + EPS) * (DH ** -0.5)
            k = k * lax.rsqrt(jnp.sum(k * k, axis=-1, keepdims=True) + EPS)
            eg = jnp.exp(gam)
            g_last = gam[L - 1:L, :]
            kbeta = k * beta
            st.append(dict(dec=dec, q=q, k=k, kbeta=kbeta, eg=eg, g_last=g_last,
                           kd=k * jnp.exp(g_last - gam),
                           rhs=jnp.concatenate([av_ref[r, :] * beta, kbeta * eg], axis=1)))
        for b in seqs:
            s_ = st[b]
            amat = jnp.where(strict, _dot_nt(s_['kbeta'], s_['k']) * s_['dec'], 0.0)
            s_['qk'] = _dot_nt(s_['q'], s_['k']) * s_['dec']
            s_['o'] = _dot(s_['q'] * s_['eg'], s_src[b, h])
            s_['x'] = -amat
            s_['p'] = amat
        for b in seqs:
            st[b]['p'] = _dot(st[b]['p'], st[b]['p'])
        for i in range(n_double):
            for b in seqs:
                s_ = st[b]
                s_['xp'] = _dot(s_['x'], s_['p'])
                if i + 1 < n_double:
                    s_['p2'] = _dot(s_['p'], s_['p'])
            for b in seqs:
                s_ = st[b]
                s_['x'] = s_['x'] + s_['p'] + s_['xp']
                if i + 1 < n_double:
                    s_['p'] = s_['p2']
        for b in seqs:
            s_ = st[b]
            s_['sol'] = s_['rhs'] + _dot(s_['x'], s_['rhs'])
        for b in seqs:
            s_ = st[b]
            s_['v_new'] = s_['sol'][:, :DH] - _dot(s_['sol'][:, DH:], s_src[b, h])
        for b in seqs:
            s_ = st[b]
            o = s_['o'] + _dot(s_['qk'], s_['v_new'])
            s_ref[b, h] = jnp.exp(s_['g_last']) * s_src[b, h] + _dot_tn(s_['kd'], s_['v_new'])
            y_ref[rows[b], :] = _head_rms(o, ng_ref[...]) * _silu(z_ref[rows[b], :])


def _gdn(proj, projg, cw, alog, dtb, ng, cv0, s_init, prev_s, *, li, n_layers, B, G, nT, TT, L, t_valid, gps):
    assert G % gps == 0 and (gps == 1 or nT == 1)
    R = TT * SEQ_BLOCK * gps
    n = proj.shape[0]
    upb = BR // LANE
    blk = lambda u: pl.BlockSpec((R, DH), lambda g, t, h, u=u: (g * nT + t, u + h))
    st4, s_out_spec, s_shape = _state_specs(li, n_layers, B, prev_s is not None, gps)
    cvs = lambda j: pl.BlockSpec((gps * CARRY_ROWS, DH), lambda g, t, h, j=j: (g, j * upb + h))
    cws = lambda j: pl.BlockSpec((CONV_W, DH), lambda g, t, h, j=j: (0, j * upb + h))
    one = pl.BlockSpec((1, LANE), lambda g, t, h: (0, 0))
    cvo = pl.BlockSpec((gps * CARRY_ROWS, BR), lambda g, t, h: (g, 0))
    in_specs = [blk(U_GD_QKV), blk(U_GD_QKV + 4), blk(U_GD_QKV + 8), blk(U_GD_Z),
                pl.BlockSpec((R, LANE), lambda g, t, h: (g * nT + t, U_GD_AB)),
                cws(0), cws(1), cws(2), one, one,
                pl.BlockSpec((1, DH), lambda g, t, h: (0, h)),
                cvs(0), cvs(1), cvs(2)]
    args = [proj] * 4 + [projg, cw, cw, cw, alog, dtb, ng, cv0, cv0, cv0]
    if s_init is not None:
        in_specs.append(st4)
        args.append(s_init)
    aliases = {}
    if prev_s is not None:
        aliases = {len(args): 1}
        in_specs.append(pl.BlockSpec(memory_space=pl.ANY))
        args.append(prev_s)
    kern = functools.partial(_gdn_kernel, L=L, TT=TT, gps=gps, t_valid=t_valid, nT=nT, li=li,
                             has_init=s_init is not None, has_prev=prev_s is not None)
    cv_shape = jax.ShapeDtypeStruct((cv0.shape[0], BR), F32)
    return pl.pallas_call(
        kern,
        grid=(G // gps, nT, HEADS),
        in_specs=in_specs,
        out_specs=[pl.BlockSpec((R, DH), lambda g, t, h: (g * nT + t, h)), s_out_spec, cvo, cvo, cvo],
        out_shape=[jax.ShapeDtypeStruct((n, BR), F32), s_shape, cv_shape, cv_shape, cv_shape],
        scratch_shapes=([pltpu.VMEM((R + gps * CARRY_ROWS, DH), F32)] * 3 + [pltpu.VMEM((R, DH), F32)] * 3
                        + [pltpu.VMEM((3, HEADS, gps * CARRY_ROWS, DH), F32), pltpu.VMEM((R, LANE), F32)]),
        input_output_aliases=aliases,
        compiler_params=_cparams(("arbitrary", "arbitrary", "arbitrary")),
        name="gdn",
    )(*args)


def _s5_prep_kernel(are_ref, aim_ref, ldt_ref, bre_ref, bim_ref, abr_ref, abi_ref, bbr_ref, bbi_ref):
    a_re = are_ref[...]
    a_im = aim_ref[...]
    dt = jnp.exp(ldt_ref[...])
    mag = jnp.exp(dt * a_re)
    ang = dt * a_im
    ab_r = mag * jnp.cos(ang)
    ab_i = mag * jnp.sin(ang)
    den = a_re * a_re + a_im * a_im
    nr = ab_r - 1.0
    ni = ab_i
    f_r = (nr * a_re + ni * a_im) / den
    f_i = (ni * a_re - nr * a_im) / den
    abr_ref[...] = ab_r
    abi_ref[...] = ab_i
    f_r = f_r[0:1, :]
    f_i = f_i[0:1, :]
    bbr_ref[...] = f_r * bre_ref[...] - f_i * bim_ref[...]
    bbi_ref[...] = f_r * bim_ref[...] + f_i * bre_ref[...]


def _s5_prep(a_re, a_im, log_dt, b_re, b_im):
    rep = lambda a: jnp.broadcast_to(a.reshape(1, S5_STATE), (SEQ_BLOCK, S5_STATE))
    ldt = rep(jnp.broadcast_to(log_dt[:, None], (S5_G, S5_P)))
    bt = lambda b: jnp.transpose(b, (2, 0, 1)).reshape(S5_N, S5_STATE)
    shp = lambda r: jax.ShapeDtypeStruct((r, S5_STATE), F32)
    abr, abi, bbr, bbi = pl.pallas_call(
        _s5_prep_kernel,
        out_shape=[shp(SEQ_BLOCK), shp(SEQ_BLOCK), shp(S5_N), shp(S5_N)],
        name="s5_prep",
    )(rep(a_re), rep(a_im), ldt, bt(b_re), bt(b_im))
    return abr, abi, bbr, bbi


def _gelu_tanh(x):
    return 0.5 * x * (1.0 + jnp.tanh(math.sqrt(2.0 / math.pi) * (x + 0.044715 * (x * x * x))))


def _s5_kernel(u_ref, z_ref, abr_ref, abi_ref, wbr_ref, wbi_ref, wcr_ref, wci_ref, d_ref, gw_ref, gb_ref,
               x0r_ref, x0i_ref, y_ref, xr_ref, xi_ref, hr_ref, hi_ref, ys_ref, *, TT, n_steps):
    ti = pl.program_id(1)
    NB = BR // LANE
    SB = S5_STATE // NB

    @pl.when(ti == 0)
    def _():
        xr_ref[...] = x0r_ref[...]
        xi_ref[...] = x0i_ref[...]

    for kb in range(NB):
        ub = u_ref[:, kb * LANE:(kb + 1) * LANE].astype(BF16)
        hr_ref[:, kb * SB:(kb + 1) * SB] = jnp.dot(ub, wbr_ref[kb], preferred_element_type=F32)
        hi_ref[:, kb * SB:(kb + 1) * SB] = jnp.dot(ub, wbi_ref[kb], preferred_element_type=F32)

    for kb in range(NB):
        sl = pl.ds(kb * SB, SB)
        ar = abr_ref[:, sl]
        ai = abi_ref[:, sl]

        def step(t, carry):
            xr, xi = carry
            rows = pl.ds(pl.multiple_of(t * SEQ_BLOCK, SEQ_BLOCK), SEQ_BLOCK)
            nxr = ar * xr - ai * xi + hr_ref[rows, sl]
            nxi = ar * xi + ai * xr + hi_ref[rows, sl]
            hr_ref[rows, sl] = nxr
            hi_ref[rows, sl] = nxi
            return nxr, nxi

        xr, xi = lax.fori_loop(0, n_steps, step, (xr_ref[:, sl], xi_ref[:, sl]))
        xr_ref[:, sl] = xr
        xi_ref[:, sl] = xi

    for kb in range(NB):
        sl = pl.ds(kb * SB, SB)
        cs = pl.ds(kb * LANE, LANE)
        yk = (jnp.dot(hr_ref[:, sl].astype(BF16), wcr_ref[kb], preferred_element_type=F32)
              - jnp.dot(hi_ref[:, sl].astype(BF16), wci_ref[kb], preferred_element_type=F32)
              + d_ref[:, cs] * u_ref[:, cs])
        ys_ref[:, cs] = _gelu_tanh(yk)
    ys = ys_ref[...]
    glu = ys * _sigmoid(jnp.dot(ys.astype(BF16), gw_ref[...], preferred_element_type=F32) + gb_ref[...])
    y_ref[...] = glu * _silu(z_ref[...])


def _s5(proj, abr, abi, wbr, wbi, wcr, wci, d, gw, gb, x0r, x0i, *, G, nT, TT, t_valid):
    assert nT == 1 or t_valid == nT * TT
    R = TT * SEQ_BLOCK
    n = proj.shape[0]
    full = lambda a: pl.BlockSpec(a.shape, lambda g, t, nd=a.ndim: (0,) * nd)
    st = pl.BlockSpec((SEQ_BLOCK, S5_STATE), lambda g, t: (g, 0))
    kern = functools.partial(_s5_kernel, TT=TT, n_steps=t_valid - (nT - 1) * TT)
    return pl.pallas_call(
        kern,
        grid=(G, nT),
        in_specs=[pl.BlockSpec((R, BR), lambda g, t: (g * nT + t, U_S5_U // 4)),
                  pl.BlockSpec((R, BR), lambda g, t: (g * nT + t, U_S5_Z // 4)),
                  full(abr), full(abi), full(wbr), full(wbi), full(wcr), full(wci), full(d), full(gw), full(gb),
                  st, st],
        out_specs=[pl.BlockSpec((R, BR), lambda g, t: (g * nT + t, 0)), st, st],
        out_shape=[jax.ShapeDtypeStruct((n, BR), F32),
                   jax.ShapeDtypeStruct(x0r.shape, F32),
                   jax.ShapeDtypeStruct(x0i.shape, F32)],
        scratch_shapes=[pltpu.VMEM((R, S5_STATE), F32), pltpu.VMEM((R, S5_STATE), F32),
                        pltpu.VMEM((R, BR), F32)],
        compiler_params=_cparams(("arbitrary", "arbitrary")),
        name="s5",
    )(proj, proj, abr, abi, wbr, wbi, wcr, wci, d, gw, gb, x0r, x0i)


def _lru_kernel(x_ref, z_ref, cw_ref, cb_ref, wa_ref, ba_ref, wx_ref, bx_ref, lam_ref, h0_ref, cv0_ref,
                y_ref, h_ref, cv_ref, xp_ref, xl_ref, a_ref, *, TT, n_steps, nT):
    ti = pl.program_id(1)
    R = TT * SEQ_BLOCK

    @pl.when(ti == 0)
    def _():
        h_ref[...] = h0_ref[...]

    _causal_conv_tile(x_ref, cw_ref, cb_ref, xp_ref, xl_ref, cv0_ref, cv_ref, ti=ti, R=R, width=BR,
                      tv_local=n_steps, act_fn=lambda a: a)

    xl = xl_ref[...]
    xb = xl.astype(BF16)
    r = _sigmoid(jnp.dot(xb, wa_ref[...], preferred_element_type=F32) + ba_ref[...])
    i = _sigmoid(jnp.dot(xb, wx_ref[...], preferred_element_type=F32) + bx_ref[...])
    log_a = -LRU_C * r * _softplus(-lam_ref[...])
    a = jnp.exp(log_a)
    a_ref[...] = a
    xl_ref[...] = jnp.sqrt(1.0 - a * a) * (i * xl)

    def step(t, h):
        rows = pl.ds(pl.multiple_of(t * SEQ_BLOCK, SEQ_BLOCK), SEQ_BLOCK)
        hn = a_ref[rows, :] * h + xl_ref[rows, :]
        xl_ref[rows, :] = hn
        return hn

    h_ref[...] = lax.fori_loop(0, n_steps, step, h_ref[...])
    y_ref[...] = xl_ref[...] * _silu(z_ref[...])


def _lru(proj, cw, cb, wa, ba, wx, bx, lam, h0, cv0, *, G, nT, TT, t_valid):
    assert nT == 1 or t_valid == nT * TT
    R = TT * SEQ_BLOCK
    n = proj.shape[0]
    full = lambda a: pl.BlockSpec(a.shape, lambda g, t, nd=a.ndim: (0,) * nd)
    st = pl.BlockSpec((SEQ_BLOCK, BR), lambda g, t: (g, 0))
    cvs = pl.BlockSpec((CARRY_ROWS, BR), lambda g, t: (g, 0))
    kern = functools.partial(_lru_kernel, TT=TT, n_steps=t_valid - (nT - 1) * TT, nT=nT)
    return pl.pallas_call(
        kern,
        grid=(G, nT),
        in_specs=[pl.BlockSpec((R, BR), lambda g, t: (g * nT + t, U_LRU_X // 4)),
                  pl.BlockSpec((R, BR), lambda g, t: (g * nT + t, U_LRU_Z // 4)),
                  full(cw), full(cb), full(wa), full(ba), full(wx), full(bx), full(lam), st, cvs],
        out_specs=[pl.BlockSpec((R, BR), lambda g, t: (g * nT + t, 0)), st, cvs],
        out_shape=[jax.ShapeDtypeStruct((n, BR), F32),
                   jax.ShapeDtypeStruct(h0.shape, F32),
                   jax.ShapeDtypeStruct(cv0.shape, F32)],
        scratch_shapes=[pltpu.VMEM((R + CARRY_ROWS, BR), F32), pltpu.VMEM((R, BR), F32),
                        pltpu.VMEM((R, BR), F32)],
        compiler_params=_cparams(("arbitrary", "arbitrary")),
        name="lru",
    )(proj, proj, cw, cb, wa, ba, wx, bx, lam, h0, cv0)


def _merge_kernel(yml_ref, ygd_ref, ys5_ref, ylru_ref, x_ref, p_ref, g_ref, wg_ref,
                  wbr_ref, wout_ref, wpg_ref, wple_ref, pg_ref, o_ref, *, p_btd, out_btd):
    tm = x_ref.shape[0]
    if p_btd:
        p = jnp.swapaxes(p_ref[...], 0, 1).reshape(tm, PLE_DIM)
    else:
        p = p_ref[...]
    x = x_ref[...]
    h = _rms(x, g_ref[...]).astype(BF16)
    merged = None
    for nb, y_ref in enumerate((yml_ref, ygd_ref, ys5_ref, ylru_ref)):
        pbr = jnp.dot(y_ref[...].astype(BF16), wbr_ref[nb], preferred_element_type=F32)
        gates = _dot_nt(h, wg_ref[nb * D_MODEL:(nb + 1) * D_MODEL, :])
        term = _sigmoid(gates) * pbr
        merged = term if merged is None else merged + term
    out = jnp.dot(merged.astype(BF16), wout_ref[...], preferred_element_type=F32)
    r = x + _rms(out, pg_ref[...])
    gate = _sigmoid(jnp.dot(r.astype(BF16), wpg_ref[...], preferred_element_type=F32))
    res = r + gate * jnp.dot(p.astype(BF16), wple_ref[...], preferred_element_type=F32)
    if out_btd:
        o_ref[...] = jnp.swapaxes(res.reshape(tm // SEQ_BLOCK, SEQ_BLOCK, D_MODEL), 0, 1)
    else:
        o_ref[...] = res


def _merge(ys, x, p, g, wg_all, li, wbr, wout, wpg, wple, pg, tm, *, p_layer=None, out_btd=False):
    n = x.shape[0]
    tt = tm // SEQ_BLOCK
    row = lambda c: pl.BlockSpec((tm, c), lambda i: (i, 0))
    full = lambda a: pl.BlockSpec(a.shape, lambda i, nd=a.ndim: (0,) * nd, pipeline_mode=pl.Buffered(1))
    p_spec = row(PLE_DIM) if p_layer is None else pl.BlockSpec((None, SEQ_BLOCK, tt, PLE_DIM),
                                                               lambda i: (p_layer, 0, i, 0))
    if out_btd:
        out_spec = pl.BlockSpec((SEQ_BLOCK, tt, D_MODEL), lambda i: (0, i, 0))
        out_shape = jax.ShapeDtypeStruct((SEQ_BLOCK, n // SEQ_BLOCK, D_MODEL), F32)
    else:
        out_spec, out_shape = row(D_MODEL), jax.ShapeDtypeStruct((n, D_MODEL), F32)
    return pl.pallas_call(
        functools.partial(_merge_kernel, p_btd=p_layer is not None, out_btd=out_btd),
        grid=(n // tm,),
        in_specs=[row(BR), row(BR), row(BR), row(BR), row(D_MODEL), p_spec, full(g),
                  pl.BlockSpec((None, GATE_COLS, D_MODEL), lambda i: (li, 0, 0), pipeline_mode=pl.Buffered(1)),
                  full(wbr), full(wout), full(wpg), full(wple), full(pg)],
        out_specs=out_spec,
        out_shape=out_shape,
        compiler_params=_cparams(("parallel",)),
        name="merge",
    )(*ys, x, p, g, wg_all, wbr, wout, wpg, wple, pg)


def _w_relayout_kernel(tbl_ref, win_ref, o_ref):
    o_ref[...] = win_ref[0].astype(BF16)


def _w_relayout(wt, runs, n_chunks):
    depth, d_in, d_model = wt.shape
    tbl = [None] * n_chunks
    for c0, src, count in runs:
        for j in range(count):
            off = src + j * W_CHUNK
            assert off % SEQ_BLOCK == 0 and off + W_CHUNK <= d_in
            tbl[c0 + j] = (off // SEQ_BLOCK, 0)
    assert all(t is not None for t in tbl)
    grid_spec = pltpu.PrefetchScalarGridSpec(
        num_scalar_prefetch=1,
        grid=(depth, n_chunks),
        in_specs=[pl.BlockSpec((pl.Element(1), pl.Element(W_CHUNK), pl.Element(d_model)),
                               lambda l, c, t: (l, t[c, 0] * SEQ_BLOCK, 0))],
        out_specs=pl.BlockSpec((None, W_CHUNK, d_model), lambda l, c, t: (l, c, 0)),
    )
    return pl.pallas_call(
        _w_relayout_kernel,
        grid_spec=grid_spec,
        out_shape=jax.ShapeDtypeStruct((depth, n_chunks * W_CHUNK, d_model), BF16),
        compiler_params=_cparams(("arbitrary", "arbitrary")),
        name="w_relayout",
    )(jnp.asarray(tbl, jnp.int32), wt)


def _permute_w_in(w_in):
    sizes = (BR, BR, BR, HEADS, HEADS, BR, BR, 3 * BR, HEADS, HEADS, BR, BR, BR, BR, BR, GATE_COLS)
    offs = [0]
    for s in sizes:
        offs.append(offs[-1] + s)
    (ml_q, _, _, ml_i, _, ml_o, _, gd_qkv, gd_a, _, gd_z, _, _, _, _, gates) = offs[:-1]
    wt = jnp.swapaxes(w_in, 1, 2)
    upc = W_CHUNK // LANE
    w_main = _w_relayout(wt, [(U_GD_QKV // upc, gd_qkv, 3), (U_ML // upc, ml_q, 3), (U_ML // upc + 3, ml_o, 2),
                              (U_GD_Z // upc, gd_z, 5)], N_UNITS // upc)
    w_gates = _w_relayout(wt, [(0, gates, GATE_COLS // W_CHUNK)], GATE_COLS // W_CHUNK)
    unit = lambda off: jnp.pad(wt[:, off:off + 2 * HEADS, :], ((0, 0), (0, LANE - 2 * HEADS), (0, 0)))
    w_scalar = jnp.concatenate([unit(ml_i), unit(gd_a)], axis=1)
    return w_main, w_gates, w_scalar


def _pad_lanes(*vecs):
    v = jnp.concatenate(vecs)
    return jnp.pad(v, (0, LANE - v.shape[0])).reshape(1, LANE)


def _block_diag(blocks):
    n, a, b = blocks.shape
    eye = jnp.eye(n, dtype=blocks.dtype)
    return jnp.einsum('ij,iab->iajb', eye, blocks).reshape(n * a, n * b)


def _to_rows(x, G):
    B, T, C = x.shape
    return x.reshape(G, SEQ_BLOCK, T, C).transpose(0, 2, 1, 3).reshape(G * T * SEQ_BLOCK, C)


def _from_rows(y, G, T):
    C = y.shape[-1]
    return y.reshape(G, T, SEQ_BLOCK, C).transpose(0, 2, 1, 3).reshape(G * SEQ_BLOCK, T, C)


def _group(x, p, states, params, *, T, t_valid, TT, L, tm_proj, tm_merge, gps=1):
    B = x.shape[0]
    G = B // SEQ_BLOCK
    nT = T // TT
    depth = p.shape[0]
    xr = _to_rows(x, G)
    tm_proj = min(tm_proj, xr.shape[0])
    tm_merge = min(tm_merge, xr.shape[0])
    new_states = []
    c_all = s_all = None
    gps = math.gcd(gps, G)
    mat = dict(n_layers=depth, B=B, G=G, nT=nT, TT=TT, L=L, t_valid=t_valid, gps=gps)
    for li in range(depth):
        lp = {k: v[li] for k, v in params.items() if not k.startswith('w_in_')}
        if states is None:
            ml_init = s_init = None
            gcv0 = jnp.zeros((G * CARRY_ROWS, 3 * BR), F32)
            x0r = jnp.zeros((B, S5_STATE), F32)
            x0i = jnp.zeros((B, S5_STATE), F32)
            h0 = jnp.zeros((B, BR), F32)
            lcv0 = jnp.zeros((G * CARRY_ROWS, BR), F32)
        else:
            (n0, m0, gcv0, x0r, x0i, h0, lcv0) = [states[j][li] for j in (1, 2, 4, 5, 6, 7, 8)]
            n0 = n0.reshape(G, SEQ_BLOCK, HEADS, DH).transpose(0, 2, 1, 3).reshape(B * HEADS, DH)
            ml_init = (states[0], n0, jnp.pad(m0, ((0, 0), (0, LANE - HEADS))))
            s_init = states[3]
            gcv0 = _to_rows(gcv0, G)
            x0r = x0r.reshape(B, S5_STATE)
            x0i = x0i.reshape(B, S5_STATE)
            lcv0 = _to_rows(lcv0, G)
        in_place = G == 1

        proj, projg = _in_proj(xr, lp['prenorm_g'], params['w_in_main'], params['w_in_scalar'], li, tm_proj)

        y_ml, c_all, n1, m1 = _mlstm(proj, projg, lp['ml_ifb'], lp['ml_norm_g'], ml_init, c_all, li=li, **mat)
        y_gd, s_all, cvq, cvk, cvv = _gdn(proj, projg, lp['gd_conv_w'], lp['gd_alog'], lp['gd_dtb'], lp['gd_norm_g'],
                                          gcv0, s_init, s_all, li=li, **dict(mat, L=min(L, GDN_CHUNK)))
        gcv1 = jnp.concatenate([cvq, cvk, cvv], axis=-1)
        y_s5, x1r, x1i = _s5(proj, lp['s5_abr'], lp['s5_abi'], lp['s5_wbr'], lp['s5_wbi'], lp['s5_wcr'],
                             lp['s5_wci'], lp['s5_d'], lp['s5_glu_w'], lp['s5_glu_b'], x0r, x0i,
                             G=G, nT=nT, TT=TT, t_valid=t_valid)
        y_lru, h1, lcv1 = _lru(proj, lp['lru_conv_w'], lp['lru_conv_b'], lp['lru_wa'], lp['lru_ba'],
                               lp['lru_wx'], lp['lru_bx'], lp['lru_lam'], h0, lcv0,
                               G=G, nT=nT, TT=TT, t_valid=t_valid)
        xr = _merge((y_ml, y_gd, y_s5, y_lru), xr, p if in_place else _to_rows(p[li], G),
                    lp['prenorm_g'], params['w_in_gates'], li, lp['w_branch'], lp['w_out'], lp['w_ple_gate'], lp['w_ple'], lp['postnorm_g'], tm_merge,
                    p_layer=li if in_place else None, out_btd=in_place and li == depth - 1)

        n1 = n1.reshape(G, HEADS, SEQ_BLOCK, DH).transpose(0, 2, 1, 3)
        new_states.append((n1.reshape(B, HEADS, DH), m1[:, :HEADS],
                           _from_rows(gcv1, G, CONV_W - 1), x1r.reshape(B, S5_G, S5_P),
                           x1i.reshape(B, S5_G, S5_P), h1, _from_rows(lcv1, G, CONV_W - 1)))
    y = xr if G == 1 else _from_rows(xr, G, T)
    n_s, m_s, gcv_s, xr_s, xi_s, h_s, lcv_s = (jnp.stack([ns[j] for ns in new_states]) for j in range(7))
    return y, (c_all, n_s, m_s, s_all, gcv_s, xr_s, xi_s, h_s, lcv_s)


def _prepare_params(prenorm_g, postnorm_g, w_in, ml_bi, ml_bf, ml_norm_g, gd_conv_w, gd_a_log, gd_dt_bias,
                    gd_norm_g, s5_a_re, s5_a_im, s5_log_dt, s5_b_re, s5_b_im, s5_c_re, s5_c_im, s5_d,
                    s5_glu_w, s5_glu_b, lru_conv_w, lru_conv_b, lru_wa, lru_ba, lru_wx, lru_bx, lru_lam,
                    w_branch, w_out, w_ple, w_ple_gate):
    depth = w_in.shape[0]
    row = lambda a: a.reshape(depth, 1, -1)
    per_layer = lambda f, *a: jnp.stack([f(*[x[i] for x in a]) for i in range(depth)])
    prm = dict(
        prenorm_g=row(prenorm_g), postnorm_g=row(postnorm_g),
        ml_ifb=per_layer(_pad_lanes, ml_bi, ml_bf), ml_norm_g=row(ml_norm_g),
        gd_conv_w=gd_conv_w, gd_norm_g=row(gd_norm_g),
        gd_alog=per_layer(lambda a: _pad_lanes(a), gd_a_log),
        gd_dtb=per_layer(lambda a: _pad_lanes(a), gd_dt_bias),
        s5_d=row(s5_d), s5_glu_w=s5_glu_w.astype(BF16), s5_glu_b=row(s5_glu_b),
        lru_conv_w=lru_conv_w, lru_conv_b=row(lru_conv_b),
        lru_wa=per_layer(_block_diag, lru_wa).astype(BF16), lru_ba=row(lru_ba),
        lru_wx=per_layer(_block_diag, lru_wx).astype(BF16), lru_bx=row(lru_bx), lru_lam=row(lru_lam),
        w_branch=w_branch.astype(BF16), w_out=w_out.astype(BF16), w_ple=w_ple.astype(BF16),
        w_ple_gate=w_ple_gate.astype(BF16),
    )
    abr, abi, wbr, wbi, wcr, wci = [], [], [], [], [], []
    nb = BR // LANE
    gpb = S5_G // nb
    for i in range(depth):
        a_r, a_i, bbr, bbi = _s5_prep(s5_a_re[i], s5_a_im[i], s5_log_dt[i], s5_b_re[i], s5_b_im[i])
        abr.append(a_r)
        abi.append(a_i)
        in_blocks = lambda bb: jnp.stack([_block_diag(bb.reshape(S5_N, S5_G, S5_P).transpose(1, 0, 2)
                                                      [k * gpb:(k + 1) * gpb]) for k in range(nb)])
        out_blocks = lambda c: jnp.stack([_block_diag(jnp.transpose(c, (0, 2, 1))[k * gpb:(k + 1) * gpb])
                                          for k in range(nb)])
        wbr.append(in_blocks(bbr).astype(BF16))
        wbi.append(in_blocks(bbi).astype(BF16))
        wcr.append(out_blocks(s5_c_re[i]).astype(BF16))
        wci.append(out_blocks(s5_c_im[i]).astype(BF16))
    prm.update(s5_abr=jnp.stack(abr), s5_abi=jnp.stack(abi), s5_wbr=jnp.stack(wbr), s5_wbi=jnp.stack(wbi),
               s5_wcr=jnp.stack(wcr), s5_wci=jnp.stack(wci))
    prm['w_in_main'], prm['w_in_gates'], prm['w_in_scalar'] = _permute_w_in(w_in)
    return prm


def kernel(x_prompt, x_sample, state_mlstm_c, state_mlstm_n, state_mlstm_m, state_gdn_s, state_gdn_conv, state_s5_re, state_s5_im, state_lru_h, state_lru_conv, p_prompt, p_sample, prenorm_g, postnorm_g, w_in, ml_bi, ml_bf, ml_norm_g, gd_conv_w, gd_a_log, gd_dt_bias, gd_norm_g, s5_a_re, s5_a_im, s5_log_dt, s5_b_re, s5_b_im, s5_c_re, s5_c_im, s5_d, s5_glu_w, s5_glu_b, lru_conv_w, lru_conv_b, lru_wa, lru_ba, lru_wx, lru_bx, lru_lam, w_branch, w_out, w_ple, w_ple_gate):
    prm = _prepare_params(prenorm_g, postnorm_g, w_in, ml_bi, ml_bf, ml_norm_g, gd_conv_w, gd_a_log,
                          gd_dt_bias, gd_norm_g, s5_a_re, s5_a_im, s5_log_dt, s5_b_re, s5_b_im, s5_c_re,
                          s5_c_im, s5_d, s5_glu_w, s5_glu_b, lru_conv_w, lru_conv_b, lru_wa, lru_ba, lru_wx,
                          lru_bx, lru_lam, w_branch, w_out, w_ple, w_ple_gate)

    t_p = x_prompt.shape[1]
    tt_p = math.gcd(t_p, PROMPT_CHUNK)
    y_prompt, pr = _group(x_prompt, p_prompt, None, prm, T=t_p, t_valid=t_p, TT=tt_p, L=tt_p,
                          tm_proj=TM_PROJ, tm_merge=TM_MERGE)

    t_s = x_sample.shape[1]
    t_pad = -(-t_s // SEQ_BLOCK) * SEQ_BLOCK
    pad_t = lambda a, ax: jnp.pad(a, [(0, t_pad - t_s) if i == ax else (0, 0) for i in range(a.ndim)])
    sample_states = (state_mlstm_c, state_mlstm_n, state_mlstm_m, state_gdn_s, state_gdn_conv,
                     state_s5_re, state_s5_im, state_lru_h, state_lru_conv)
    y_s, sa = _group(pad_t(x_sample, 1), pad_t(p_sample, 2), sample_states, prm, T=t_pad, t_valid=t_s,
                     TT=t_pad, L=t_pad, tm_proj=TM_PROJ, tm_merge=TM_MERGE, gps=SAMPLE_GROUPS_PER_STEP)
    y_sample = y_s[:, :t_s]
    return (y_prompt, y_sample) + pr + sa
```

```python
import functools
import math

import jax
import jax.numpy as jnp
from jax import lax
from jax.experimental import pallas as pl
from jax.experimental.pallas import tpu as pltpu

F32 = jnp.float32
BF16 = jnp.bfloat16

D_MODEL = 1024
BR = 512
HEADS = 4
DH = 128
CONV_W = 4
S5_G = 32
S5_N = 16
S5_P = 64
S5_STATE = S5_G * S5_P
LRU_BLOCKS = 8
LRU_BD = 64
LRU_C = 8.0
PLE_DIM = 256
EPS = 1e-6
NEG = -1e30

LANE = 128
SEQ_BLOCK = 8
CARRY_ROWS = (CONV_W - 1) * SEQ_BLOCK
PROMPT_CHUNK = 128
GDN_CHUNK = 64
SAMPLE_GROUPS_PER_STEP = 2
TM_PROJ = 512
TM_MERGE = 512

U_GD_QKV = 0
U_ML = 12
U_GD_Z = 32
U_S5_U = 36
U_S5_Z = 40
U_LRU_X = 44
U_LRU_Z = 48
N_UNITS = 52
PROJ_COLS = N_UNITS * LANE
U_ML_IF = 0
U_GD_AB = 1
GATE_COLS = 4 * D_MODEL
W_CHUNK = 4 * LANE

VMEM_LIMIT = 56 * 1024 * 1024


def _cparams(sem):
    return pltpu.CompilerParams(dimension_semantics=sem, vmem_limit_bytes=VMEM_LIMIT)


def _sigmoid(x):
    return 0.5 * jnp.tanh(0.5 * x) + 0.5


def _silu(x):
    return x * _sigmoid(x)


def _softplus(x):
    return jnp.maximum(x, 0.0) + jnp.log1p(jnp.exp(-jnp.abs(x)))


def _dot(a, b):
    return jnp.dot(a.astype(BF16), b.astype(BF16), preferred_element_type=F32)


def _dot_nt(a, b):
    return lax.dot_general(a.astype(BF16), b.astype(BF16), (((1,), (1,)), ((), ())),
                           preferred_element_type=F32)


def _dot_tn(a, b):
    return lax.dot_general(a.astype(BF16), b.astype(BF16), (((0,), (0,)), ((), ())),
                           preferred_element_type=F32)


def _cumsum_rows(x):
    n = x.shape[0]
    ri = lax.broadcasted_iota(jnp.int32, x.shape, 0)
    s = 1
    while s < n:
        x = x + jnp.where(ri >= s, pltpu.roll(x, s, axis=0), 0.0)
        s *= 2
    return x


def _row_from_col(col, eye):
    return jnp.sum(jnp.where(eye, col, 0.0), axis=0, keepdims=True)


def _head_rms(h, g):
    return h * lax.rsqrt(jnp.mean(h * h, axis=-1, keepdims=True) + EPS) * g


def _rms(x, g):
    return x * lax.rsqrt(jnp.mean(x * x, axis=-1, keepdims=True) + EPS) * g


def _in_proj_kernel(x_ref, g_ref, w_ref, wg_ref, o_ref, og_ref, *, x_btd):
    x = x_ref[...]
    if x_btd:
        x = jnp.swapaxes(x, 0, 1).reshape(o_ref.shape[0], D_MODEL)
    h = _rms(x, g_ref[...]).astype(BF16)
    o_ref[...] = _dot_nt(h, w_ref[...])
    og_ref[...] = _dot_nt(h, wg_ref[...])


def _in_proj(x, g, w_all, wg_all, li, tm):
    x_btd = x.ndim == 3
    n = x.shape[0] * x.shape[1] if x_btd else x.shape[0]
    x_spec = (pl.BlockSpec((SEQ_BLOCK, tm // SEQ_BLOCK, D_MODEL), lambda i: (0, i, 0)) if x_btd
              else pl.BlockSpec((tm, D_MODEL), lambda i: (i, 0)))
    resident = lambda c: pl.BlockSpec((None, c, D_MODEL), lambda i: (li, 0, 0), pipeline_mode=pl.Buffered(1))
    return pl.pallas_call(
        functools.partial(_in_proj_kernel, x_btd=x_btd),
        grid=(n // tm,),
        in_specs=[x_spec,
                  pl.BlockSpec((1, D_MODEL), lambda i: (0, 0)),
                  resident(PROJ_COLS), resident(2 * LANE)],
        out_specs=[pl.BlockSpec((tm, PROJ_COLS), lambda i: (i, 0)),
                   pl.BlockSpec((tm, 2 * LANE), lambda i: (i, 0))],
        out_shape=[jax.ShapeDtypeStruct((n, PROJ_COLS), F32), jax.ShapeDtypeStruct((n, 2 * LANE), F32)],
        compiler_params=_cparams(("parallel",)),
        name="in_proj",
    )(x, g, w_all, wg_all)


def _token_cumsum(x, tok_in_chunk, L):
    s = 1
    while s < L:
        x = x + jnp.where(tok_in_chunk >= s, pltpu.roll(x, s * SEQ_BLOCK, axis=0), 0.0)
        s *= 2
    return x


def _tile_token_ids(ti, TT, L):
    R = TT * SEQ_BLOCK
    t_local = lax.broadcasted_iota(jnp.int32, (R, LANE), 0) // SEQ_BLOCK
    return ti * TT + t_local, t_local % L


def _token_scan(x, op, fill):
    s = 1
    while s < x.shape[0]:
        shifted = jnp.concatenate([jnp.full((s,) + x.shape[1:], fill, x.dtype), x[:-s]], axis=0)
        x = op(x, shifted)
        s *= 2
    return x


ML_R, ML_M, ML_INTER, ML_ENEG, ML_WS = (j * HEADS for j in range(5))


def _mlstm_gate_tile(if_ref, ifb_ref, m_ref, sc_ref, gt_ref, *, ti, T, t_valid):
    shape = (T, SEQ_BLOCK, LANE)
    lane = lax.broadcasted_iota(jnp.int32, shape, 2)
    valid = (ti * T + lax.broadcasted_iota(jnp.int32, shape, 0)) < t_valid
    rot = lambda x, k: pltpu.roll(x.reshape(T * SEQ_BLOCK, LANE), k, axis=1).reshape(shape)
    ifv = (if_ref[...] + ifb_ref[...]).reshape(shape)
    logf = jnp.minimum(ifv, 0.0) - jnp.log1p(jnp.exp(-jnp.abs(ifv)))
    bcum = rot(_token_scan(jnp.where(valid, logf, 0.0), jnp.add, 0.0), LANE - HEADS)
    ig = jnp.where(valid, ifv, NEG)
    r = ig - bcum
    m_prev = m_ref[...]
    big_m = jnp.maximum(m_prev[None], _token_scan(r, jnp.maximum, NEG))
    inter = jnp.exp(m_prev[None] - big_m)
    eneg = jnp.exp(-(bcum + big_m))
    b_last = bcum[T - 1]
    g = b_last[None] - bcum + ig
    m_new = jnp.maximum(b_last + m_prev, jnp.max(g, axis=0))
    ws = jnp.exp(g - m_new[None])
    head_lanes = lane[0] < HEADS
    sc_ref[...] = jnp.where(head_lanes, jnp.exp(b_last + m_prev - m_new), 0.0)
    m_ref[...] = jnp.where(head_lanes, m_new, 0.0)
    packed = jnp.where(lane < ML_M, r,
                       jnp.where(lane < ML_INTER, rot(big_m, ML_M),
                                 jnp.where(lane < ML_ENEG, rot(inter, ML_INTER),
                                           jnp.where(lane < ML_WS, rot(eneg, ML_ENEG), rot(ws, ML_WS)))))
    gt_ref[...] = packed.reshape(T * SEQ_BLOCK, LANE)


def _own_layer(state_ref, li, has_prev, first_step):
    if has_prev:
        return state_ref

    @pl.when(first_step)
    def _():
        for other in range(state_ref.shape[0]):
            if other != li:
                state_ref[other] = jnp.zeros(state_ref.shape[1:], F32)

    return state_ref.at[li]


def _mlstm_kernel(*refs, L, gps, t_valid, li, has_init, has_prev, single_tile):
    q_ref, k_ref, v_ref, o_ref, z_ref, if_ref, ifb_ref, ng_ref = refs[:8]
    n_in = 8 + (3 if has_init else 0) + (1 if has_prev else 0)
    y_ref, c_ref, n_ref, m_ref, gt_ref, sc_ref, rt_ref = refs[n_in:]
    ti = pl.program_id(1)
    h = pl.program_id(2)
    R = L * SEQ_BLOCK
    seqs = range(gps * SEQ_BLOCK)
    rows = [pl.ds((j // SEQ_BLOCK) * R + j % SEQ_BLOCK, L, stride=SEQ_BLOCK) for j in seqs]
    n_rows = [pl.ds((j // SEQ_BLOCK) * SEQ_BLOCK * HEADS + h * SEQ_BLOCK + j % SEQ_BLOCK, 1) for j in seqs]
    use_transpose = L == LANE
    c_ref = _own_layer(c_ref, li, has_prev, (ti == 0) & (h == 0))
    c_src = refs[8] if (has_init and single_tile) else c_ref

    @pl.when((ti == 0) & (h == 0))
    def _():
        if has_init:
            if not single_tile:
                c_ref[...] = refs[8][...]
            n_ref[...] = refs[9][...]
            m_ref[...] = refs[10][...]
        else:
            c_ref[...] = jnp.zeros(c_ref.shape, F32)
            n_ref[...] = jnp.zeros(n_ref.shape, F32)
            m_ref[...] = jnp.zeros(m_ref.shape, F32)

    @pl.when(h == 0)
    def _():
        for gi in range(gps):
            grp = pl.ds(gi * SEQ_BLOCK, SEQ_BLOCK)
            tile = pl.ds(gi * R, R)
            _mlstm_gate_tile(if_ref.at[tile], ifb_ref, m_ref.at[grp], sc_ref.at[grp], gt_ref.at[tile],
                             ti=ti, T=L, t_valid=t_valid)
        if use_transpose:
            for b in seqs:
                rt_ref[b * SEQ_BLOCK:(b + 1) * SEQ_BLOCK, :] = gt_ref[rows[b], :].T[:SEQ_BLOCK, :]

    ri = lax.broadcasted_iota(jnp.int32, (L, L), 0)
    ci = lax.broadcasted_iota(jnp.int32, (L, L), 1)
    causal = ci <= ri
    eye = ci == ri
    to_lane0 = jnp.where(h == 0, 0, LANE - h)
    sc_all = pltpu.roll(sc_ref[...], to_lane0, axis=1)
    col = lambda tile, off: tile[:, off:off + 1]

    st = []
    for b in seqs:
        gq = pltpu.roll(gt_ref[rows[b], :], to_lane0, axis=1)
        q = q_ref[rows[b], :] * (DH ** -0.5)
        v = v_ref[rows[b], :]
        kw = k_ref[rows[b], :] * col(gq, ML_WS)
        cmat = c_src[b, h]
        sc = sc_all[b:b + 1, 0:1]
        st.append(dict(gq=gq, q=q, v=v, kw=kw, sc=sc, qk=_dot_nt(q, k_ref[rows[b], :]), qc=_dot(q, cmat)))
        c_ref[b, h] = sc * cmat + _dot_tn(kw, v)
    for b in seqs:
        s_ = st[b]
        if use_transpose:
            r_row = rt_ref[pl.ds(b * SEQ_BLOCK + h, 1), :]
        else:
            r_row = _row_from_col(col(s_['gq'], ML_R), eye)
        s_['s'] = s_['qk'] * jnp.where(causal, jnp.exp(r_row - col(s_['gq'], ML_M)), 0.0)
    for b in seqs:
        s_ = st[b]
        srow = n_rows[b]
        nvec = n_ref[srow, :]
        inter = col(s_['gq'], ML_INTER)
        s_['num'] = _dot(s_['s'], s_['v']) + inter * s_['qc']
        den = (jnp.sum(s_['s'], axis=1, keepdims=True)
               + inter * jnp.sum(s_['q'] * nvec, axis=1, keepdims=True))
        s_['den'] = jnp.maximum(jnp.abs(den), col(s_['gq'], ML_ENEG))
        n_ref[srow, :] = s_['sc'] * nvec + jnp.sum(s_['kw'], axis=0, keepdims=True)
    for b in seqs:
        s_ = st[b]
        yn = _head_rms(s_['num'] / s_['den'], ng_ref[...])
        y_ref[rows[b], :] = yn * _sigmoid(o_ref[rows[b], :]) * _silu(z_ref[rows[b], :])


def _state_specs(li, n_layers, B, has_prev, gps):
    nseq = gps * SEQ_BLOCK
    one = pl.BlockSpec((None, nseq, HEADS, DH, DH), lambda g, t, h: (li, g, 0, 0, 0))
    every = pl.BlockSpec((n_layers, nseq, HEADS, DH, DH), lambda g, t, h: (0, g, 0, 0, 0))
    shape = jax.ShapeDtypeStruct((n_layers, B, HEADS, DH, DH), F32)
    return one, (one if has_prev else every), shape


def _mlstm(proj, projg, ifb, ng, init, prev_c, *, li, n_layers, B, G, nT, TT, L, t_valid, gps):
    assert TT == L and G % gps == 0 and (gps == 1 or nT == 1)
    R = TT * SEQ_BLOCK * gps
    n = proj.shape[0]
    blk = lambda u: pl.BlockSpec((R, DH), lambda g, t, h, u=u: (g * nT + t, u + h))
    st4, c_out_spec, c_shape = _state_specs(li, n_layers, B, prev_c is not None, gps)
    n_spec = pl.BlockSpec((gps * SEQ_BLOCK * HEADS, LANE), lambda g, t, h: (g, 0))
    m_spec = pl.BlockSpec((gps * SEQ_BLOCK, LANE), lambda g, t, h: (g, 0))
    in_specs = [blk(U_ML), blk(U_ML + 4), blk(U_ML + 8), blk(U_ML + 12), blk(U_ML + 16),
                pl.BlockSpec((R, LANE), lambda g, t, h: (g * nT + t, U_ML_IF)),
                pl.BlockSpec((1, LANE), lambda g, t, h: (0, 0)),
                pl.BlockSpec((1, DH), lambda g, t, h: (0, h))]
    args = [proj] * 5 + [projg, ifb, ng]
    if init is not None:
        in_specs += [st4, n_spec, m_spec]
        args += list(init)
    aliases = {}
    if prev_c is not None:
        aliases = {len(args): 1}
        in_specs.append(pl.BlockSpec(memory_space=pl.ANY))
        args.append(prev_c)
    kern = functools.partial(_mlstm_kernel, L=L, gps=gps, t_valid=t_valid, li=li, has_init=init is not None,
                             has_prev=prev_c is not None, single_tile=nT == 1)
    return pl.pallas_call(
        kern,
        grid=(G // gps, nT, HEADS),
        in_specs=in_specs,
        out_specs=[pl.BlockSpec((R, DH), lambda g, t, h: (g * nT + t, h)), c_out_spec, n_spec, m_spec],
        out_shape=[jax.ShapeDtypeStruct((n, BR), F32), c_shape,
                   jax.ShapeDtypeStruct((B * HEADS, LANE), F32), jax.ShapeDtypeStruct((B, LANE), F32)],
        scratch_shapes=[pltpu.VMEM((R, LANE), F32), pltpu.VMEM((gps * SEQ_BLOCK, LANE), F32),
                        pltpu.VMEM((gps * SEQ_BLOCK * SEQ_BLOCK, LANE), F32)],
        input_output_aliases=aliases,
        compiler_params=_cparams(("arbitrary", "arbitrary", "arbitrary")),
        name="mlstm",
    )(*args)


def _causal_conv_tile(x_ref, w_ref, b_ref, xp_ref, act_ref, cv0_ref, cv_ref, *, ti, R, width, tv_local,
                      act_fn, carry_ref=None):
    @pl.when(ti == 0)
    def _():
        xp_ref[0:CARRY_ROWS, :] = cv0_ref[...]

    if carry_ref is not None:
        @pl.when(ti > 0)
        def _():
            xp_ref[0:CARRY_ROWS, :] = carry_ref[...]

    xp_ref[CARRY_ROWS:CARRY_ROWS + R, :] = x_ref[...]
    RB = next(rb for rb in (128, 32, SEQ_BLOCK) if R % rb == 0 and rb * min(width, BR) <= 128 * LANE)

    def blk(i, carry):
        r0 = pl.multiple_of(i * RB, SEQ_BLOCK)
        cw = min(width, BR)
        for c0 in range(0, width, cw):
            cs = pl.ds(c0, cw)
            acc = xp_ref[pl.ds(r0, RB), cs] * w_ref[0:1, cs]
            for j in range(1, CONV_W):
                acc = acc + xp_ref[pl.ds(r0 + j * SEQ_BLOCK, RB), cs] * w_ref[j:j + 1, cs]
            if b_ref is not None:
                acc = acc + b_ref[:, cs]
            act_ref[pl.ds(r0, RB), cs] = act_fn(acc)
        return carry

    lax.fori_loop(0, R // RB, blk, 0)

    cv_ref[...] = xp_ref[tv_local * SEQ_BLOCK:tv_local * SEQ_BLOCK + CARRY_ROWS, :]
    if carry_ref is not None:
        carry_ref[...] = xp_ref[R:R + CARRY_ROWS, :]
    else:
        xp_ref[0:CARRY_ROWS, :] = xp_ref[R:R + CARRY_ROWS, :]


def _gdn_kernel(*refs, L, TT, gps, t_valid, nT, li, has_init, has_prev):
    (q_ref, k_ref, v_ref, z_ref, ab_ref, cwq_ref, cwk_ref, cwv_ref, alog_ref, dtb_ref, ng_ref,
     cvq0_ref, cvk0_ref, cvv0_ref) = refs[:14]
    n_in = 14 + (1 if has_init else 0) + (1 if has_prev else 0)
    (y_ref, s_ref, cvq_ref, cvk_ref, cvv_ref,
     xpq_ref, xpk_ref, xpv_ref, aq_ref, ak_ref, av_ref, carry_ref, gb_ref) = refs[n_in:]
    ti = pl.program_id(1)
    h = pl.program_id(2)
    R = TT * SEQ_BLOCK
    s_ref = _own_layer(s_ref, li, has_prev, (ti == 0) & (h == 0))
    single_chunk = nT == 1 and TT == L
    s_src = refs[14] if (has_init and single_chunk) else s_ref

    @pl.when((ti == 0) & (h == 0))
    def _():
        if has_init and single_chunk:
            pass
        elif has_init:
            s_ref[...] = refs[14][...]
        else:
            s_ref[...] = jnp.zeros(s_ref.shape, F32)

    head_lanes = pl.ds(pl.multiple_of(h * DH, DH), DH)
    for j, (x_ref, cw_ref, xp_ref, a_ref, cv0_ref, cv_ref) in enumerate((
            (q_ref, cwq_ref, xpq_ref, aq_ref, cvq0_ref, cvq_ref),
            (k_ref, cwk_ref, xpk_ref, ak_ref, cvk0_ref, cvk_ref),
            (v_ref, cwv_ref, xpv_ref, av_ref, cvv0_ref, cvv_ref))):
        for gi in range(gps):
            tile = pl.ds(gi * R, R)
            cvr = pl.ds(gi * CARRY_ROWS, CARRY_ROWS)
            _causal_conv_tile(x_ref.at[tile], cw_ref, None, xp_ref.at[pl.ds(gi * (R + CARRY_ROWS), R + CARRY_ROWS)],
                              a_ref.at[tile], cv0_ref.at[cvr], cv_ref.at[cvr, head_lanes], ti=ti, R=R, width=DH,
                              tv_local=t_valid - (nT - 1) * TT, act_fn=_silu, carry_ref=carry_ref.at[j, h, cvr])

    @pl.when(h == 0)
    def _():
        tok, tok_in_chunk = _tile_token_ids(ti, TT, L)
        valid = tok < t_valid
        lane = lax.broadcasted_iota(jnp.int32, (R, LANE), 1)
        for gi in range(gps):
            tile = pl.ds(gi * R, R)
            abv = ab_ref[tile, :]
            g_all = -jnp.exp(alog_ref[...]) * _softplus(abv + dtb_ref[...])
            gam = _token_cumsum(jnp.where(valid, g_all, 0.0), tok_in_chunk, L)
            gb_ref[tile, :] = jnp.where(lane < HEADS, gam, jnp.where(valid, _sigmoid(abv), 0.0))

    ri = lax.broadcasted_iota(jnp.int32, (L, L), 0)
    ci = lax.broadcasted_iota(jnp.int32, (L, L), 1)
    causal = ci <= ri
    strict = ci < ri
    eye = ci == ri
    lane = lax.broadcasted_iota(jnp.int32, (L, LANE), 1)
    sel_a = lane == h
    sel_b = lane == HEADS + h
    pick = lambda tile, sel: jnp.sum(jnp.where(sel, tile, 0.0), axis=1, keepdims=True)
    n_double = int(math.log2(L)) - 1

    seqs = range(gps * SEQ_BLOCK)
    for c in range(TT // L):
        rows, st = [], []
        for b in seqs:
            r = pl.ds((b // SEQ_BLOCK) * R + c * L * SEQ_BLOCK + b % SEQ_BLOCK, L, stride=SEQ_BLOCK)
            rows.append(r)
            gbv = gb_ref[r, :]
            gam = pick(gbv, sel_a)
            beta = pick(gbv, sel_b)
            dec = jnp.where(causal, jnp.exp(gam - _row_from_col(gam, eye)), 0.0)
            q = aq_ref[r, :]
            k = ak_ref[r, :]
            q = q * lax.rsqrt(jnp.sum(q * q, axis=-1, keepdims=True) + EPS) * (DH ** -0.5)
            k = k * lax.rsqrt(jnp.sum(k * k, axis=-1, keepdims=True) + EPS)
            eg = jnp.exp(gam)
            g_last = gam[L - 1:L, :]
            kbeta = k * beta
            st.append(dict(dec=dec, q=q, k=k, kbeta=kbeta, eg=eg, g_last=g_last,
                           kd=k * jnp.exp(g_last - gam),
                           rhs=jnp.concatenate([av_ref[r, :] * beta, kbeta * eg], axis=1)))
        for b in seqs:
            s_ = st[b]
            amat = jnp.where(strict, _dot_nt(s_['kbeta'], s_['k']) * s_['dec'], 0.0)
            s_['qk'] = _dot_nt(s_['q'], s_['k']) * s_['dec']
            s_['o'] = _dot(s_['q'] * s_['eg'], s_src[b, h])
            s_['x'] = -amat
            s_['p'] = amat
        for b in seqs:
            st[b]['p'] = _dot(st[b]['p'], st[b]['p'])
        for i in range(n_double):
            for b in seqs:
                s_ = st[b]
                s_['xp'] = _dot(s_['x'], s_['p'])
                if i + 1 < n_double:
                    s_['p2'] = _dot(s_['p'], s_['p'])
            for b in seqs:
                s_ = st[b]
                s_['x'] = s_['x'] + s_['p'] + s_['xp']
                if i + 1 < n_double:
                    s_['p'] = s_['p2']
        for b in seqs:
            s_ = st[b]
            s_['sol'] = s_['rhs'] + _dot(s_['x'], s_['rhs'])
        for b in seqs:
            s_ = st[b]
            s_['v_new'] = s_['sol'][:, :DH] - _dot(s_['sol'][:, DH:], s_src[b, h])
        for b in seqs:
            s_ = st[b]
            o = s_['o'] + _dot(s_['qk'], s_['v_new'])
            s_ref[b, h] = jnp.exp(s_['g_last']) * s_src[b, h] + _dot_tn(s_['kd'], s_['v_new'])
            y_ref[rows[b], :] = _head_rms(o, ng_ref[...]) * _silu(z_ref[rows[b], :])


def _gdn(proj, projg, cw, alog, dtb, ng, cv0, s_init, prev_s, *, li, n_layers, B, G, nT, TT, L, t_valid, gps):
    assert G % gps == 0 and (gps == 1 or nT == 1)
    R = TT * SEQ_BLOCK * gps
    n = proj.shape[0]
    upb = BR // LANE
    blk = lambda u: pl.BlockSpec((R, DH), lambda g, t, h, u=u: (g * nT + t, u + h))
    st4, s_out_spec, s_shape = _state_specs(li, n_layers, B, prev_s is not None, gps)
    cvs = lambda j: pl.BlockSpec((gps * CARRY_ROWS, DH), lambda g, t, h, j=j: (g, j * upb + h))
    cws = lambda j: pl.BlockSpec((CONV_W, DH), lambda g, t, h, j=j: (0, j * upb + h))
    one = pl.BlockSpec((1, LANE), lambda g, t, h: (0, 0))
    cvo = pl.BlockSpec((gps * CARRY_ROWS, BR), lambda g, t, h: (g, 0))
    in_specs = [blk(U_GD_QKV), blk(U_GD_QKV + 4), blk(U_GD_QKV + 8), blk(U_GD_Z),
                pl.BlockSpec((R, LANE), lambda g, t, h: (g * nT + t, U_GD_AB)),
                cws(0), cws(1), cws(2), one, one,
                pl.BlockSpec((1, DH), lambda g, t, h: (0, h)),
                cvs(0), cvs(1), cvs(2)]
    args = [proj] * 4 + [projg, cw, cw, cw, alog, dtb, ng, cv0, cv0, cv0]
    if s_init is not None:
        in_specs.append(st4)
        args.append(s_init)
    aliases = {}
    if prev_s is not None:
        aliases = {len(args): 1}
        in_specs.append(pl.BlockSpec(memory_space=pl.ANY))
        args.append(prev_s)
    kern = functools.partial(_gdn_kernel, L=L, TT=TT, gps=gps, t_valid=t_valid, nT=nT, li=li,
                             has_init=s_init is not None, has_prev=prev_s is not None)
    cv_shape = jax.ShapeDtypeStruct((cv0.shape[0], BR), F32)
    return pl.pallas_call(
        kern,
        grid=(G // gps, nT, HEADS),
        in_specs=in_specs,
        out_specs=[pl.BlockSpec((R, DH), lambda g, t, h: (g * nT + t, h)), s_out_spec, cvo, cvo, cvo],
        out_shape=[jax.ShapeDtypeStruct((n, BR), F32), s_shape, cv_shape, cv_shape, cv_shape],
        scratch_shapes=([pltpu.VMEM((R + gps * CARRY_ROWS, DH), F32)] * 3 + [pltpu.VMEM((R, DH), F32)] * 3
                        + [pltpu.VMEM((3, HEADS, gps * CARRY_ROWS, DH), F32), pltpu.VMEM((R, LANE), F32)]),
        input_output_aliases=aliases,
        compiler_params=_cparams(("arbitrary", "arbitrary", "arbitrary")),
        name="gdn",
    )(*args)


def _s5_prep_kernel(are_ref, aim_ref, ldt_ref, bre_ref, bim_ref, abr_ref, abi_ref, bbr_ref, bbi_ref):
    a_re = are_ref[...]
    a_im = aim_ref[...]
    dt = jnp.exp(ldt_ref[...])
    mag = jnp.exp(dt * a_re)
    ang = dt * a_im
    ab_r = mag * jnp.cos(ang)
    ab_i = mag * jnp.sin(ang)
    den = a_re * a_re + a_im * a_im
    nr = ab_r - 1.0
    ni = ab_i
    f_r = (nr * a_re + ni * a_im) / den
    f_i = (ni * a_re - nr * a_im) / den
    abr_ref[...] = ab_r
    abi_ref[...] = ab_i
    f_r = f_r[0:1, :]
    f_i = f_i[0:1, :]
    bbr_ref[...] = f_r * bre_ref[...] - f_i * bim_ref[...]
    bbi_ref[...] = f_r * bim_ref[...] + f_i * bre_ref[...]


def _s5_prep(a_re, a_im, log_dt, b_re, b_im):
    rep = lambda a: jnp.broadcast_to(a.reshape(1, S5_STATE), (SEQ_BLOCK, S5_STATE))
    ldt = rep(jnp.broadcast_to(log_dt[:, None], (S5_G, S5_P)))
    bt = lambda b: jnp.transpose(b, (2, 0, 1)).reshape(S5_N, S5_STATE)
    shp = lambda r: jax.ShapeDtypeStruct((r, S5_STATE), F32)
    abr, abi, bbr, bbi = pl.pallas_call(
        _s5_prep_kernel,
        out_shape=[shp(SEQ_BLOCK), shp(SEQ_BLOCK), shp(S5_N), shp(S5_N)],
        name="s5_prep",
    )(rep(a_re), rep(a_im), ldt, bt(b_re), bt(b_im))
    return abr, abi, bbr, bbi


def _gelu_tanh(x):
    return 0.5 * x * (1.0 + jnp.tanh(math.sqrt(2.0 / math.pi) * (x + 0.044715 * (x * x * x))))


def _s5_kernel(u_ref, z_ref, abr_ref, abi_ref, wbr_ref, wbi_ref, wcr_ref, wci_ref, d_ref, gw_ref, gb_ref,
               x0r_ref, x0i_ref, y_ref, xr_ref, xi_ref, hr_ref, hi_ref, ys_ref, *, TT, n_steps):
    ti = pl.program_id(1)
    NB = BR // LANE
    SB = S5_STATE // NB

    @pl.when(ti == 0)
    def _():
        xr_ref[...] = x0r_ref[...]
        xi_ref[...] = x0i_ref[...]

    for kb in range(NB):
        ub = u_ref[:, kb * LANE:(kb + 1) * LANE].astype(BF16)
        hr_ref[:, kb * SB:(kb + 1) * SB] = jnp.dot(ub, wbr_ref[kb], preferred_element_type=F32)
        hi_ref[:, kb * SB:(kb + 1) * SB] = jnp.dot(ub, wbi_ref[kb], preferred_element_type=F32)

    for kb in range(NB):
        sl = pl.ds(kb * SB, SB)
        ar = abr_ref[:, sl]
        ai = abi_ref[:, sl]

        def step(t, carry):
            xr, xi = carry
            rows = pl.ds(pl.multiple_of(t * SEQ_BLOCK, SEQ_BLOCK), SEQ_BLOCK)
            nxr = ar * xr - ai * xi + hr_ref[rows, sl]
            nxi = ar * xi + ai * xr + hi_ref[rows, sl]
            hr_ref[rows, sl] = nxr
            hi_ref[rows, sl] = nxi
            return nxr, nxi

        xr, xi = lax.fori_loop(0, n_steps, step, (xr_ref[:, sl], xi_ref[:, sl]))
        xr_ref[:, sl] = xr
        xi_ref[:, sl] = xi

    for kb in range(NB):
        sl = pl.ds(kb * SB, SB)
        cs = pl.ds(kb * LANE, LANE)
        yk = (jnp.dot(hr_ref[:, sl].astype(BF16), wcr_ref[kb], preferred_element_type=F32)
              - jnp.dot(hi_ref[:, sl].astype(BF16), wci_ref[kb], preferred_element_type=F32)
              + d_ref[:, cs] * u_ref[:, cs])
        ys_ref[:, cs] = _gelu_tanh(yk)
    ys = ys_ref[...]
    glu = ys * _sigmoid(jnp.dot(ys.astype(BF16), gw_ref[...], preferred_element_type=F32) + gb_ref[...])
    y_ref[...] = glu * _silu(z_ref[...])


def _s5(proj, abr, abi, wbr, wbi, wcr, wci, d, gw, gb, x0r, x0i, *, G, nT, TT, t_valid):
    assert nT == 1 or t_valid == nT * TT
    R = TT * SEQ_BLOCK
    n = proj.shape[0]
    full = lambda a: pl.BlockSpec(a.shape, lambda g, t, nd=a.ndim: (0,) * nd)
    st = pl.BlockSpec((SEQ_BLOCK, S5_STATE), lambda g, t: (g, 0))
    kern = functools.partial(_s5_kernel, TT=TT, n_steps=t_valid - (nT - 1) * TT)
    return pl.pallas_call(
        kern,
        grid=(G, nT),
        in_specs=[pl.BlockSpec((R, BR), lambda g, t: (g * nT + t, U_S5_U // 4)),
                  pl.BlockSpec((R, BR), lambda g, t: (g * nT + t, U_S5_Z // 4)),
                  full(abr), full(abi), full(wbr), full(wbi), full(wcr), full(wci), full(d), full(gw), full(gb),
                  st, st],
        out_specs=[pl.BlockSpec((R, BR), lambda g, t: (g * nT + t, 0)), st, st],
        out_shape=[jax.ShapeDtypeStruct((n, BR), F32),
                   jax.ShapeDtypeStruct(x0r.shape, F32),
                   jax.ShapeDtypeStruct(x0i.shape, F32)],
        scratch_shapes=[pltpu.VMEM((R, S5_STATE), F32), pltpu.VMEM((R, S5_STATE), F32),
                        pltpu.VMEM((R, BR), F32)],
        compiler_params=_cparams(("arbitrary", "arbitrary")),
        name="s5",
    )(proj, proj, abr, abi, wbr, wbi, wcr, wci, d, gw, gb, x0r, x0i)


def _lru_kernel(x_ref, z_ref, cw_ref, cb_ref, wa_ref, ba_ref, wx_ref, bx_ref, lam_ref, h0_ref, cv0_ref,
                y_ref, h_ref, cv_ref, xp_ref, xl_ref, a_ref, *, TT, n_steps, nT):
    ti = pl.program_id(1)
    R = TT * SEQ_BLOCK

    @pl.when(ti == 0)
    def _():
        h_ref[...] = h0_ref[...]

    _causal_conv_tile(x_ref, cw_ref, cb_ref, xp_ref, xl_ref, cv0_ref, cv_ref, ti=ti, R=R, width=BR,
                      tv_local=n_steps, act_fn=lambda a: a)

    xl = xl_ref[...]
    xb = xl.astype(BF16)
    r = _sigmoid(jnp.dot(xb, wa_ref[...], preferred_element_type=F32) + ba_ref[...])
    i = _sigmoid(jnp.dot(xb, wx_ref[...], preferred_element_type=F32) + bx_ref[...])
    log_a = -LRU_C * r * _softplus(-lam_ref[...])
    a = jnp.exp(log_a)
    a_ref[...] = a
    xl_ref[...] = jnp.sqrt(1.0 - a * a) * (i * xl)

    def step(t, h):
        rows = pl.ds(pl.multiple_of(t * SEQ_BLOCK, SEQ_BLOCK), SEQ_BLOCK)
        hn = a_ref[rows, :] * h + xl_ref[rows, :]
        xl_ref[rows, :] = hn
        return hn

    h_ref[...] = lax.fori_loop(0, n_steps, step, h_ref[...])
    y_ref[...] = xl_ref[...] * _silu(z_ref[...])


def _lru(proj, cw, cb, wa, ba, wx, bx, lam, h0, cv0, *, G, nT, TT, t_valid):
    assert nT == 1 or t_valid == nT * TT
    R = TT * SEQ_BLOCK
    n = proj.shape[0]
    full = lambda a: pl.BlockSpec(a.shape, lambda g, t, nd=a.ndim: (0,) * nd)
    st = pl.BlockSpec((SEQ_BLOCK, BR), lambda g, t: (g, 0))
    cvs = pl.BlockSpec((CARRY_ROWS, BR), lambda g, t: (g, 0))
    kern = functools.partial(_lru_kernel, TT=TT, n_steps=t_valid - (nT - 1) * TT, nT=nT)
    return pl.pallas_call(
        kern,
        grid=(G, nT),
        in_specs=[pl.BlockSpec((R, BR), lambda g, t: (g * nT + t, U_LRU_X // 4)),
                  pl.BlockSpec((R, BR), lambda g, t: (g * nT + t, U_LRU_Z // 4)),
                  full(cw), full(cb), full(wa), full(ba), full(wx), full(bx), full(lam), st, cvs],
        out_specs=[pl.BlockSpec((R, BR), lambda g, t: (g * nT + t, 0)), st, cvs],
        out_shape=[jax.ShapeDtypeStruct((n, BR), F32),
                   jax.ShapeDtypeStruct(h0.shape, F32),
                   jax.ShapeDtypeStruct(cv0.shape, F32)],
        scratch_shapes=[pltpu.VMEM((R + CARRY_ROWS, BR), F32), pltpu.VMEM((R, BR), F32),
                        pltpu.VMEM((R, BR), F32)],
        compiler_params=_cparams(("arbitrary", "arbitrary")),
        name="lru",
    )(proj, proj, cw, cb, wa, ba, wx, bx, lam, h0, cv0)


def _merge_kernel(yml_ref, ygd_ref, ys5_ref, ylru_ref, x_ref, p_ref, g_ref, wg_ref,
                  wbr_ref, wout_ref, wpg_ref, wple_ref, pg_ref, o_ref, *, x_btd, p_btd, out_btd):
    tm = yml_ref.shape[0]
    rows = lambda ref, btd, c: jnp.swapaxes(ref[...], 0, 1).reshape(tm, c) if btd else ref[...]
    p = rows(p_ref, p_btd, PLE_DIM)
    x = rows(x_ref, x_btd, D_MODEL)
    h = _rms(x, g_ref[...]).astype(BF16)
    merged = None
    for nb, y_ref in enumerate((yml_ref, ygd_ref, ys5_ref, ylru_ref)):
        pbr = jnp.dot(y_ref[...].astype(BF16), wbr_ref[nb], preferred_element_type=F32)
        gates = _dot_nt(h, wg_ref[nb * D_MODEL:(nb + 1) * D_MODEL, :])
        term = _sigmoid(gates) * pbr
        merged = term if merged is None else merged + term
    out = jnp.dot(merged.astype(BF16), wout_ref[...], preferred_element_type=F32)
    r = x + _rms(out, pg_ref[...])
    gate = _sigmoid(jnp.dot(r.astype(BF16), wpg_ref[...], preferred_element_type=F32))
    res = r + gate * jnp.dot(p.astype(BF16), wple_ref[...], preferred_element_type=F32)
    if out_btd:
        o_ref[...] = jnp.swapaxes(res.reshape(tm // SEQ_BLOCK, SEQ_BLOCK, D_MODEL), 0, 1)
    else:
        o_ref[...] = res


def _merge(ys, x, p, g, wg_all, li, wbr, wout, wpg, wple, pg, tm, *, p_layer=None, out_btd=False):
    x_btd = x.ndim == 3
    n = ys[0].shape[0]
    tt = tm // SEQ_BLOCK
    row = lambda c: pl.BlockSpec((tm, c), lambda i: (i, 0))
    x_spec = pl.BlockSpec((SEQ_BLOCK, tt, D_MODEL), lambda i: (0, i, 0)) if x_btd else row(D_MODEL)
    full = lambda a: pl.BlockSpec(a.shape, lambda i, nd=a.ndim: (0,) * nd, pipeline_mode=pl.Buffered(1))
    p_spec = row(PLE_DIM) if p_layer is None else pl.BlockSpec((None, SEQ_BLOCK, tt, PLE_DIM),
                                                               lambda i: (p_layer, 0, i, 0))
    if out_btd:
        out_spec = pl.BlockSpec((SEQ_BLOCK, tt, D_MODEL), lambda i: (0, i, 0))
        out_shape = jax.ShapeDtypeStruct((SEQ_BLOCK, n // SEQ_BLOCK, D_MODEL), F32)
    else:
        out_spec, out_shape = row(D_MODEL), jax.ShapeDtypeStruct((n, D_MODEL), F32)
    return pl.pallas_call(
        functools.partial(_merge_kernel, x_btd=x_btd, p_btd=p_layer is not None, out_btd=out_btd),
        grid=(n // tm,),
        in_specs=[row(BR), row(BR), row(BR), row(BR), x_spec, p_spec, full(g),
                  pl.BlockSpec((None, GATE_COLS, D_MODEL), lambda i: (li, 0, 0), pipeline_mode=pl.Buffered(1)),
                  full(wbr), full(wout), full(wpg), full(wple), full(pg)],
        out_specs=out_spec,
        out_shape=out_shape,
        compiler_params=_cparams(("parallel",)),
        name="merge",
    )(*ys, x, p, g, wg_all, wbr, wout, wpg, wple, pg)


def _w_relayout_kernel(tbl_ref, win_ref, o_ref):
    o_ref[...] = win_ref[0].astype(BF16)


def _w_relayout(wt, runs, n_chunks):
    depth, d_in, d_model = wt.shape
    tbl = [None] * n_chunks
    for c0, src, count in runs:
        for j in range(count):
            off = src + j * W_CHUNK
            assert off % SEQ_BLOCK == 0 and off + W_CHUNK <= d_in
            tbl[c0 + j] = (off // SEQ_BLOCK, 0)
    assert all(t is not None for t in tbl)
    grid_spec = pltpu.PrefetchScalarGridSpec(
        num_scalar_prefetch=1,
        grid=(depth, n_chunks),
        in_specs=[pl.BlockSpec((pl.Element(1), pl.Element(W_CHUNK), pl.Element(d_model)),
                               lambda l, c, t: (l, t[c, 0] * SEQ_BLOCK, 0))],
        out_specs=pl.BlockSpec((None, W_CHUNK, d_model), lambda l, c, t: (l, c, 0)),
    )
    return pl.pallas_call(
        _w_relayout_kernel,
        grid_spec=grid_spec,
        out_shape=jax.ShapeDtypeStruct((depth, n_chunks * W_CHUNK, d_model), BF16),
        compiler_params=_cparams(("arbitrary", "arbitrary")),
        name="w_relayout",
    )(jnp.asarray(tbl, jnp.int32), wt)


def _permute_w_in(w_in):
    sizes = (BR, BR, BR, HEADS, HEADS, BR, BR, 3 * BR, HEADS, HEADS, BR, BR, BR, BR, BR, GATE_COLS)
    offs = [0]
    for s in sizes:
        offs.append(offs[-1] + s)
    (ml_q, _, _, ml_i, _, ml_o, _, gd_qkv, gd_a, _, gd_z, _, _, _, _, gates) = offs[:-1]
    wt = jnp.swapaxes(w_in, 1, 2)
    upc = W_CHUNK // LANE
    w_main = _w_relayout(wt, [(U_GD_QKV // upc, gd_qkv, 3), (U_ML // upc, ml_q, 3), (U_ML // upc + 3, ml_o, 2),
                              (U_GD_Z // upc, gd_z, 5)], N_UNITS // upc)
    w_gates = _w_relayout(wt, [(0, gates, GATE_COLS // W_CHUNK)], GATE_COLS // W_CHUNK)
    unit = lambda off: jnp.pad(wt[:, off:off + 2 * HEADS, :], ((0, 0), (0, LANE - 2 * HEADS), (0, 0)))
    w_scalar = jnp.concatenate([unit(ml_i), unit(gd_a)], axis=1)
    return w_main, w_gates, w_scalar


def _pad_lanes(*vecs):
    v = jnp.concatenate(vecs)
    return jnp.pad(v, (0, LANE - v.shape[0])).reshape(1, LANE)


def _block_diag(blocks):
    n, a, b = blocks.shape
    eye = jnp.eye(n, dtype=blocks.dtype)
    return jnp.einsum('ij,iab->iajb', eye, blocks).reshape(n * a, n * b)


def _to_rows(x, G):
    B, T, C = x.shape
    return x.reshape(G, SEQ_BLOCK, T, C).transpose(0, 2, 1, 3).reshape(G * T * SEQ_BLOCK, C)


def _from_rows(y, G, T):
    C = y.shape[-1]
    return y.reshape(G, T, SEQ_BLOCK, C).transpose(0, 2, 1, 3).reshape(G * SEQ_BLOCK, T, C)


def _group(x, p, states, params, *, T, t_valid, TT, L, tm_proj, tm_merge, gps=1):
    B = x.shape[0]
    G = B // SEQ_BLOCK
    nT = T // TT
    depth = p.shape[0]
    xr = x if G == 1 else _to_rows(x, G)
    tm_proj = min(tm_proj, B * T)
    tm_merge = min(tm_merge, B * T)
    new_states = []
    c_all = s_all = None
    gps = math.gcd(gps, G)
    mat = dict(n_layers=depth, B=B, G=G, nT=nT, TT=TT, L=L, t_valid=t_valid, gps=gps)
    for li in range(depth):
        lp = {k: v[li] for k, v in params.items() if not k.startswith('w_in_')}
        if states is None:
            ml_init = s_init = None
            gcv0 = jnp.zeros((G * CARRY_ROWS, 3 * BR), F32)
            x0r = jnp.zeros((B, S5_STATE), F32)
            x0i = jnp.zeros((B, S5_STATE), F32)
            h0 = jnp.zeros((B, BR), F32)
            lcv0 = jnp.zeros((G * CARRY_ROWS, BR), F32)
        else:
            (n0, m0, gcv0, x0r, x0i, h0, lcv0) = [states[j][li] for j in (1, 2, 4, 5, 6, 7, 8)]
            n0 = n0.reshape(G, SEQ_BLOCK, HEADS, DH).transpose(0, 2, 1, 3).reshape(B * HEADS, DH)
            ml_init = (states[0], n0, jnp.pad(m0, ((0, 0), (0, LANE - HEADS))))
            s_init = states[3]
            gcv0 = _to_rows(gcv0, G)
            x0r = x0r.reshape(B, S5_STATE)
            x0i = x0i.reshape(B, S5_STATE)
            lcv0 = _to_rows(lcv0, G)
        in_place = G == 1

        proj, projg = _in_proj(xr, lp['prenorm_g'], params['w_in_main'], params['w_in_scalar'], li, tm_proj)

        y_ml, c_all, n1, m1 = _mlstm(proj, projg, lp['ml_ifb'], lp['ml_norm_g'], ml_init, c_all, li=li, **mat)
        y_gd, s_all, cvq, cvk, cvv = _gdn(proj, projg, lp['gd_conv_w'], lp['gd_alog'], lp['gd_dtb'], lp['gd_norm_g'],
                                          gcv0, s_init, s_all, li=li, **dict(mat, L=min(L, GDN_CHUNK)))
        gcv1 = jnp.concatenate([cvq, cvk, cvv], axis=-1)
        y_s5, x1r, x1i = _s5(proj, lp['s5_abr'], lp['s5_abi'], lp['s5_wbr'], lp['s5_wbi'], lp['s5_wcr'],
                             lp['s5_wci'], lp['s5_d'], lp['s5_glu_w'], lp['s5_glu_b'], x0r, x0i,
                             G=G, nT=nT, TT=TT, t_valid=t_valid)
        y_lru, h1, lcv1 = _lru(proj, lp['lru_conv_w'], lp['lru_conv_b'], lp['lru_wa'], lp['lru_ba'],
                               lp['lru_wx'], lp['lru_bx'], lp['lru_lam'], h0, lcv0,
                               G=G, nT=nT, TT=TT, t_valid=t_valid)
        xr = _merge((y_ml, y_gd, y_s5, y_lru), xr, p if in_place else _to_rows(p[li], G),
                    lp['prenorm_g'], params['w_in_gates'], li, lp['w_branch'], lp['w_out'], lp['w_ple_gate'], lp['w_ple'], lp['postnorm_g'], tm_merge,
                    p_layer=li if in_place else None, out_btd=in_place and li == depth - 1)

        n1 = n1.reshape(G, HEADS, SEQ_BLOCK, DH).transpose(0, 2, 1, 3)
        new_states.append((n1.reshape(B, HEADS, DH), m1[:, :HEADS],
                           _from_rows(gcv1, G, CONV_W - 1), x1r.reshape(B, S5_G, S5_P),
                           x1i.reshape(B, S5_G, S5_P), h1, _from_rows(lcv1, G, CONV_W - 1)))
    y = xr if G == 1 else _from_rows(xr, G, T)
    n_s, m_s, gcv_s, xr_s, xi_s, h_s, lcv_s = (jnp.stack([ns[j] for ns in new_states]) for j in range(7))
    return y, (c_all, n_s, m_s, s_all, gcv_s, xr_s, xi_s, h_s, lcv_s)


def _prepare_params(prenorm_g, postnorm_g, w_in, ml_bi, ml_bf, ml_norm_g, gd_conv_w, gd_a_log, gd_dt_bias,
                    gd_norm_g, s5_a_re, s5_a_im, s5_log_dt, s5_b_re, s5_b_im, s5_c_re, s5_c_im, s5_d,
                    s5_glu_w, s5_glu_b, lru_conv_w, lru_conv_b, lru_wa, lru_ba, lru_wx, lru_bx, lru_lam,
                    w_branch, w_out, w_ple, w_ple_gate):
    depth = w_in.shape[0]
    row = lambda a: a.reshape(depth, 1, -1)
    per_layer = lambda f, *a: jnp.stack([f(*[x[i] for x in a]) for i in range(depth)])
    prm = dict(
        prenorm_g=row(prenorm_g), postnorm_g=row(postnorm_g),
        ml_ifb=per_layer(_pad_lanes, ml_bi, ml_bf), ml_norm_g=row(ml_norm_g),
        gd_conv_w=gd_conv_w, gd_norm_g=row(gd_norm_g),
        gd_alog=per_layer(lambda a: _pad_lanes(a), gd_a_log),
        gd_dtb=per_layer(lambda a: _pad_lanes(a), gd_dt_bias),
        s5_d=row(s5_d), s5_glu_w=s5_glu_w.astype(BF16), s5_glu_b=row(s5_glu_b),
        lru_conv_w=lru_conv_w, lru_conv_b=row(lru_conv_b),
        lru_wa=per_layer(_block_diag, lru_wa).astype(BF16), lru_ba=row(lru_ba),
        lru_wx=per_layer(_block_diag, lru_wx).astype(BF16), lru_bx=row(lru_bx), lru_lam=row(lru_lam),
        w_branch=w_branch.astype(BF16), w_out=w_out.astype(BF16), w_ple=w_ple.astype(BF16),
        w_ple_gate=w_ple_gate.astype(BF16),
    )
    abr, abi, wbr, wbi, wcr, wci = [], [], [], [], [], []
    nb = BR // LANE
    gpb = S5_G // nb
    for i in range(depth):
        a_r, a_i, bbr, bbi = _s5_prep(s5_a_re[i], s5_a_im[i], s5_log_dt[i], s5_b_re[i], s5_b_im[i])
        abr.append(a_r)
        abi.append(a_i)
        in_blocks = lambda bb: jnp.stack([_block_diag(bb.reshape(S5_N, S5_G, S5_P).transpose(1, 0, 2)
                                                      [k * gpb:(k + 1) * gpb]) for k in range(nb)])
        out_blocks = lambda c: jnp.stack([_block_diag(jnp.transpose(c, (0, 2, 1))[k * gpb:(k + 1) * gpb])
                                          for k in range(nb)])
        wbr.append(in_blocks(bbr).astype(BF16))
        wbi.append(in_blocks(bbi).astype(BF16))
        wcr.append(out_blocks(s5_c_re[i]).astype(BF16))
        wci.append(out_blocks(s5_c_im[i]).astype(BF16))
    prm.update(s5_abr=jnp.stack(abr), s5_abi=jnp.stack(abi), s5_wbr=jnp.stack(wbr), s5_wbi=jnp.stack(wbi),
               s5_wcr=jnp.stack(wcr), s5_wci=jnp.stack(wci))
    prm['w_in_main'], prm['w_in_gates'], prm['w_in_scalar'] = _permute_w_in(w_in)
    return prm


def kernel(x_prompt, x_sample, state_mlstm_c, state_mlstm_n, state_mlstm_m, state_gdn_s, state_gdn_conv, state_s5_re, state_s5_im, state_lru_h, state_lru_conv, p_prompt, p_sample, prenorm_g, postnorm_g, w_in, ml_bi, ml_bf, ml_norm_g, gd_conv_w, gd_a_log, gd_dt_bias, gd_norm_g, s5_a_re, s5_a_im, s5_log_dt, s5_b_re, s5_b_im, s5_c_re, s5_c_im, s5_d, s5_glu_w, s5_glu_b, lru_conv_w, lru_conv_b, lru_wa, lru_ba, lru_wx, lru_bx, lru_lam, w_branch, w_out, w_ple, w_ple_gate):
    prm = _prepare_params(prenorm_g, postnorm_g, w_in, ml_bi, ml_bf, ml_norm_g, gd_conv_w, gd_a_log,
                          gd_dt_bias, gd_norm_g, s5_a_re, s5_a_im, s5_log_dt, s5_b_re, s5_b_im, s5_c_re,
                          s5_c_im, s5_d, s5_glu_w, s5_glu_b, lru_conv_w, lru_conv_b, lru_wa, lru_ba, lru_wx,
                          lru_bx, lru_lam, w_branch, w_out, w_ple, w_ple_gate)

    t_p = x_prompt.shape[1]
    tt_p = math.gcd(t_p, PROMPT_CHUNK)
    y_prompt, pr = _group(x_prompt, p_prompt, None, prm, T=t_p, t_valid=t_p, TT=tt_p, L=tt_p,
                          tm_proj=TM_PROJ, tm_merge=TM_MERGE)

    t_s = x_sample.shape[1]
    t_pad = -(-t_s // SEQ_BLOCK) * SEQ_BLOCK
    pad_t = lambda a, ax: jnp.pad(a, [(0, t_pad - t_s) if i == ax else (0, 0) for i in range(a.ndim)])
    sample_states = (state_mlstm_c, state_mlstm_n, state_mlstm_m, state_gdn_s, state_gdn_conv,
                     state_s5_re, state_s5_im, state_lru_h, state_lru_conv)
    y_s, sa = _group(pad_t(x_sample, 1), pad_t(p_sample, 2), sample_states, prm, T=t_pad, t_valid=t_s,
                     TT=t_pad, L=t_pad, tm_proj=TM_PROJ, tm_merge=TM_MERGE, gps=SAMPLE_GROUPS_PER_STEP)
    y_sample = y_s[:, :t_s]
    return (y_prompt, y_sample) + pr + sa
```

```python
import functools
import math

import jax
import jax.numpy as jnp
from jax import lax
from jax.experimental import pallas as pl
from jax.experimental.pallas import tpu as pltpu

F32 = jnp.float32
BF16 = jnp.bfloat16

D_MODEL = 1024
BR = 512
HEADS = 4
DH = 128
CONV_W = 4
S5_G = 32
S5_N = 16
S5_P = 64
S5_STATE = S5_G * S5_P
LRU_BLOCKS = 8
LRU_BD = 64
LRU_C = 8.0
PLE_DIM = 256
EPS = 1e-6
NEG = -1e30

LANE = 128
SEQ_BLOCK = 8
CARRY_ROWS = (CONV_W - 1) * SEQ_BLOCK
PROMPT_CHUNK = 128
GDN_CHUNK = 64
SAMPLE_GROUPS_PER_STEP = 2
TM_PROJ = 512
TM_MERGE = 512

U_GD_QKV = 0
U_ML = 12
U_GD_Z = 32
U_S5_U = 36
U_S5_Z = 40
U_LRU_X = 44
U_LRU_Z = 48
N_UNITS = 52
PROJ_COLS = N_UNITS * LANE
U_ML_IF = 0
U_GD_AB = 1
GATE_COLS = 4 * D_MODEL
W_CHUNK = 4 * LANE

VMEM_LIMIT = 56 * 1024 * 1024


def _cparams(sem):
    return pltpu.CompilerParams(dimension_semantics=sem, vmem_limit_bytes=VMEM_LIMIT)


def _sigmoid(x):
    return 0.5 * jnp.tanh(0.5 * x) + 0.5


def _silu(x):
    return x * _sigmoid(x)


def _softplus(x):
    return jnp.maximum(x, 0.0) + jnp.log1p(jnp.exp(-jnp.abs(x)))


def _dot(a, b):
    return jnp.dot(a.astype(BF16), b.astype(BF16), preferred_element_type=F32)


def _dot_nt(a, b):
    return lax.dot_general(a.astype(BF16), b.astype(BF16), (((1,), (1,)), ((), ())),
                           preferred_element_type=F32)


def _dot_tn(a, b):
    return lax.dot_general(a.astype(BF16), b.astype(BF16), (((0,), (0,)), ((), ())),
                           preferred_element_type=F32)


def _cumsum_rows(x):
    n = x.shape[0]
    ri = lax.broadcasted_iota(jnp.int32, x.shape, 0)
    s = 1
    while s < n:
        x = x + jnp.where(ri >= s, pltpu.roll(x, s, axis=0), 0.0)
        s *= 2
    return x


def _row_from_col(col, eye):
    return jnp.sum(jnp.where(eye, col, 0.0), axis=0, keepdims=True)


def _head_rms(h, g):
    return h * lax.rsqrt(jnp.mean(h * h, axis=-1, keepdims=True) + EPS) * g


def _rms(x, g):
    return x * lax.rsqrt(jnp.mean(x * x, axis=-1, keepdims=True) + EPS) * g


def _in_proj_kernel(x_ref, g_ref, w_ref, wg_ref, o_ref, og_ref, *, x_btd):
    x = x_ref[...]
    if x_btd:
        x = jnp.swapaxes(x, 0, 1).reshape(o_ref.shape[0], D_MODEL)
    h = _rms(x, g_ref[...]).astype(BF16)
    o_ref[...] = _dot_nt(h, w_ref[...])
    og_ref[...] = _dot_nt(h, wg_ref[...])


def _in_proj(x, g, w_all, wg_all, li, tm):
    x_btd = x.ndim == 3
    n = x.shape[0] * x.shape[1] if x_btd else x.shape[0]
    x_spec = (pl.BlockSpec((SEQ_BLOCK, tm // SEQ_BLOCK, D_MODEL), lambda i: (0, i, 0)) if x_btd
              else pl.BlockSpec((tm, D_MODEL), lambda i: (i, 0)))
    resident = lambda c: pl.BlockSpec((None, c, D_MODEL), lambda i: (li, 0, 0), pipeline_mode=pl.Buffered(1))
    return pl.pallas_call(
        functools.partial(_in_proj_kernel, x_btd=x_btd),
        grid=(n // tm,),
        in_specs=[x_spec,
                  pl.BlockSpec((1, D_MODEL), lambda i: (0, 0)),
                  resident(PROJ_COLS), resident(2 * LANE)],
        out_specs=[pl.BlockSpec((tm, PROJ_COLS), lambda i: (i, 0)),
                   pl.BlockSpec((tm, 2 * LANE), lambda i: (i, 0))],
        out_shape=[jax.ShapeDtypeStruct((n, PROJ_COLS), F32), jax.ShapeDtypeStruct((n, 2 * LANE), F32)],
        compiler_params=_cparams(("parallel",)),
        name="in_proj",
    )(x, g, w_all, wg_all)


def _token_cumsum(x, tok_in_chunk, L):
    s = 1
    while s < L:
        x = x + jnp.where(tok_in_chunk >= s, pltpu.roll(x, s * SEQ_BLOCK, axis=0), 0.0)
        s *= 2
    return x


def _tile_token_ids(ti, TT, L):
    R = TT * SEQ_BLOCK
    t_local = lax.broadcasted_iota(jnp.int32, (R, LANE), 0) // SEQ_BLOCK
    return ti * TT + t_local, t_local % L


def _token_scan(x, op, fill):
    s = 1
    while s < x.shape[0]:
        shifted = jnp.concatenate([jnp.full((s,) + x.shape[1:], fill, x.dtype), x[:-s]], axis=0)
        x = op(x, shifted)
        s *= 2
    return x


ML_R, ML_M, ML_INTER, ML_ENEG, ML_WS = (j * HEADS for j in range(5))


def _mlstm_gate_tile(if_ref, ifb_ref, m_ref, sc_ref, gt_ref, *, ti, T, t_valid):
    shape = (T, SEQ_BLOCK, LANE)
    lane = lax.broadcasted_iota(jnp.int32, shape, 2)
    valid = (ti * T + lax.broadcasted_iota(jnp.int32, shape, 0)) < t_valid
    rot = lambda x, k: pltpu.roll(x.reshape(T * SEQ_BLOCK, LANE), k, axis=1).reshape(shape)
    ifv = (if_ref[...] + ifb_ref[...]).reshape(shape)
    logf = jnp.minimum(ifv, 0.0) - jnp.log1p(jnp.exp(-jnp.abs(ifv)))
    bcum = rot(_token_scan(jnp.where(valid, logf, 0.0), jnp.add, 0.0), LANE - HEADS)
    ig = jnp.where(valid, ifv, NEG)
    r = ig - bcum
    m_prev = m_ref[...]
    big_m = jnp.maximum(m_prev[None], _token_scan(r, jnp.maximum, NEG))
    inter = jnp.exp(m_prev[None] - big_m)
    eneg = jnp.exp(-(bcum + big_m))
    b_last = bcum[T - 1]
    g = b_last[None] - bcum + ig
    m_new = jnp.maximum(b_last + m_prev, jnp.max(g, axis=0))
    ws = jnp.exp(g - m_new[None])
    head_lanes = lane[0] < HEADS
    sc_ref[...] = jnp.where(head_lanes, jnp.exp(b_last + m_prev - m_new), 0.0)
    m_ref[...] = jnp.where(head_lanes, m_new, 0.0)
    packed = jnp.where(lane < ML_M, r,
                       jnp.where(lane < ML_INTER, rot(big_m, ML_M),
                                 jnp.where(lane < ML_ENEG, rot(inter, ML_INTER),
                                           jnp.where(lane < ML_WS, rot(eneg, ML_ENEG), rot(ws, ML_WS)))))
    gt_ref[...] = packed.reshape(T * SEQ_BLOCK, LANE)


def _own_layer(state_ref, li, has_prev, first_step):
    if has_prev:
        return state_ref

    @pl.when(first_step)
    def _():
        for other in range(state_ref.shape[0]):
            if other != li:
                state_ref[other] = jnp.zeros(state_ref.shape[1:], F32)

    return state_ref.at[li]


def _mlstm_kernel(*refs, L, gps, t_valid, li, has_init, has_prev, single_tile):
    q_ref, k_ref, v_ref, o_ref, z_ref, if_ref, ifb_ref, ng_ref = refs[:8]
    n_in = 8 + (3 if has_init else 0) + (1 if has_prev else 0)
    y_ref, c_ref, n_ref, m_ref, gt_ref, sc_ref, rt_ref = refs[n_in:]
    ti = pl.program_id(1)
    h = pl.program_id(2)
    R = L * SEQ_BLOCK
    seqs = range(gps * SEQ_BLOCK)
    rows = [pl.ds((j // SEQ_BLOCK) * R + j % SEQ_BLOCK, L, stride=SEQ_BLOCK) for j in seqs]
    n_rows = [pl.ds((j // SEQ_BLOCK) * SEQ_BLOCK * HEADS + h * SEQ_BLOCK + j % SEQ_BLOCK, 1) for j in seqs]
    use_transpose = L == LANE
    c_ref = _own_layer(c_ref, li, has_prev, (ti == 0) & (h == 0))
    c_src = refs[8] if (has_init and single_tile) else c_ref

    @pl.when((ti == 0) & (h == 0))
    def _():
        if has_init:
            if not single_tile:
                c_ref[...] = refs[8][...]
            n_ref[...] = refs[9][...]
            m_ref[...] = refs[10][...]
        else:
            c_ref[...] = jnp.zeros(c_ref.shape, F32)
            n_ref[...] = jnp.zeros(n_ref.shape, F32)
            m_ref[...] = jnp.zeros(m_ref.shape, F32)

    @pl.when(h == 0)
    def _():
        for gi in range(gps):
            grp = pl.ds(gi * SEQ_BLOCK, SEQ_BLOCK)
            tile = pl.ds(gi * R, R)
            _mlstm_gate_tile(if_ref.at[tile], ifb_ref, m_ref.at[grp], sc_ref.at[grp], gt_ref.at[tile],
                             ti=ti, T=L, t_valid=t_valid)
        if use_transpose:
            for b in seqs:
                rt_ref[b * SEQ_BLOCK:(b + 1) * SEQ_BLOCK, :] = gt_ref[rows[b], :].T[:SEQ_BLOCK, :]

    ri = lax.broadcasted_iota(jnp.int32, (L, L), 0)
    ci = lax.broadcasted_iota(jnp.int32, (L, L), 1)
    causal = ci <= ri
    eye = ci == ri
    to_lane0 = jnp.where(h == 0, 0, LANE - h)
    sc_all = pltpu.roll(sc_ref[...], to_lane0, axis=1)
    col = lambda tile, off: tile[:, off:off + 1]

    st = []
    for b in seqs:
        gq = pltpu.roll(gt_ref[rows[b], :], to_lane0, axis=1)
        q = q_ref[rows[b], :] * (DH ** -0.5)
        v = v_ref[rows[b], :]
        kw = k_ref[rows[b], :] * col(gq, ML_WS)
        cmat = c_src[b, h]
        sc = sc_all[b:b + 1, 0:1]
        st.append(dict(gq=gq, q=q, v=v, kw=kw, sc=sc, qk=_dot_nt(q, k_ref[rows[b], :]), qc=_dot(q, cmat)))
        c_ref[b, h] = sc * cmat + _dot_tn(kw, v)
    for b in seqs:
        s_ = st[b]
        if use_transpose:
            r_row = rt_ref[pl.ds(b * SEQ_BLOCK + h, 1), :]
        else:
            r_row = _row_from_col(col(s_['gq'], ML_R), eye)
        s_['s'] = s_['qk'] * jnp.where(causal, jnp.exp(r_row - col(s_['gq'], ML_M)), 0.0)
    for b in seqs:
        s_ = st[b]
        srow = n_rows[b]
        nvec = n_ref[srow, :]
        inter = col(s_['gq'], ML_INTER)
        s_['num'] = _dot(s_['s'], s_['v']) + inter * s_['qc']
        den = (jnp.sum(s_['s'], axis=1, keepdims=True)
               + inter * jnp.sum(s_['q'] * nvec, axis=1, keepdims=True))
        s_['den'] = jnp.maximum(jnp.abs(den), col(s_['gq'], ML_ENEG))
        n_ref[srow, :] = s_['sc'] * nvec + jnp.sum(s_['kw'], axis=0, keepdims=True)
    for b in seqs:
        s_ = st[b]
        yn = _head_rms(s_['num'] / s_['den'], ng_ref[...])
        y_ref[rows[b], :] = yn * _sigmoid(o_ref[rows[b], :]) * _silu(z_ref[rows[b], :])


def _state_specs(li, n_layers, B, has_prev, gps):
    nseq = gps * SEQ_BLOCK
    one = pl.BlockSpec((None, nseq, HEADS, DH, DH), lambda g, t, h: (li, g, 0, 0, 0))
    every = pl.BlockSpec((n_layers, nseq, HEADS, DH, DH), lambda g, t, h: (0, g, 0, 0, 0))
    shape = jax.ShapeDtypeStruct((n_layers, B, HEADS, DH, DH), F32)
    return one, (one if has_prev else every), shape


def _mlstm(proj, projg, ifb, ng, init, prev_c, *, li, n_layers, B, G, nT, TT, L, t_valid, gps):
    assert TT == L and G % gps == 0 and (gps == 1 or nT == 1)
    R = TT * SEQ_BLOCK * gps
    n = proj.shape[0]
    blk = lambda u: pl.BlockSpec((R, DH), lambda g, t, h, u=u: (g * nT + t, u + h))
    st4, c_out_spec, c_shape = _state_specs(li, n_layers, B, prev_c is not None, gps)
    n_spec = pl.BlockSpec((gps * SEQ_BLOCK * HEADS, LANE), lambda g, t, h: (g, 0))
    m_spec = pl.BlockSpec((gps * SEQ_BLOCK, LANE), lambda g, t, h: (g, 0))
    in_specs = [blk(U_ML), blk(U_ML + 4), blk(U_ML + 8), blk(U_ML + 12), blk(U_ML + 16),
                pl.BlockSpec((R, LANE), lambda g, t, h: (g * nT + t, U_ML_IF)),
                pl.BlockSpec((1, LANE), lambda g, t, h: (0, 0)),
                pl.BlockSpec((1, DH), lambda g, t, h: (0, h))]
    args = [proj] * 5 + [projg, ifb, ng]
    if init is not None:
        in_specs += [st4, n_spec, m_spec]
        args += list(init)
    aliases = {}
    if prev_c is not None:
        aliases = {len(args): 1}
        in_specs.append(pl.BlockSpec(memory_space=pl.ANY))
        args.append(prev_c)
    kern = functools.partial(_mlstm_kernel, L=L, gps=gps, t_valid=t_valid, li=li, has_init=init is not None,
                             has_prev=prev_c is not None, single_tile=nT == 1)
    return pl.pallas_call(
        kern,
        grid=(G // gps, nT, HEADS),
        in_specs=in_specs,
        out_specs=[pl.BlockSpec((R, DH), lambda g, t, h: (g * nT + t, h)), c_out_spec, n_spec, m_spec],
        out_shape=[jax.ShapeDtypeStruct((n, BR), F32), c_shape,
                   jax.ShapeDtypeStruct((B * HEADS, LANE), F32), jax.ShapeDtypeStruct((B, LANE), F32)],
        scratch_shapes=[pltpu.VMEM((R, LANE), F32), pltpu.VMEM((gps * SEQ_BLOCK, LANE), F32),
                        pltpu.VMEM((gps * SEQ_BLOCK * SEQ_BLOCK, LANE), F32)],
        input_output_aliases=aliases,
        compiler_params=_cparams(("arbitrary", "arbitrary", "arbitrary")),
        name="mlstm",
    )(*args)


def _causal_conv_tile(x_ref, w_ref, b_ref, xp_ref, act_ref, cv0_ref, cv_ref, *, ti, R, width, tv_local,
                      act_fn, carry_ref=None):
    @pl.when(ti == 0)
    def _():
        xp_ref[0:CARRY_ROWS, :] = cv0_ref[...]

    if carry_ref is not None:
        @pl.when(ti > 0)
        def _():
            xp_ref[0:CARRY_ROWS, :] = carry_ref[...]

    xp_ref[CARRY_ROWS:CARRY_ROWS + R, :] = x_ref[...]
    RB = next(rb for rb in (128, 32, SEQ_BLOCK) if R % rb == 0 and rb * min(width, BR) <= 128 * LANE)

    def blk(i, carry):
        r0 = pl.multiple_of(i * RB, SEQ_BLOCK)
        cw = min(width, BR)
        for c0 in range(0, width, cw):
            cs = pl.ds(c0, cw)
            acc = xp_ref[pl.ds(r0, RB), cs] * w_ref[0:1, cs]
            for j in range(1, CONV_W):
                acc = acc + xp_ref[pl.ds(r0 + j * SEQ_BLOCK, RB), cs] * w_ref[j:j + 1, cs]
            if b_ref is not None:
                acc = acc + b_ref[:, cs]
            act_ref[pl.ds(r0, RB), cs] = act_fn(acc)
        return carry

    lax.fori_loop(0, R // RB, blk, 0)

    cv_ref[...] = xp_ref[tv_local * SEQ_BLOCK:tv_local * SEQ_BLOCK + CARRY_ROWS, :]
    if carry_ref is not None:
        carry_ref[...] = xp_ref[R:R + CARRY_ROWS, :]
    else:
        xp_ref[0:CARRY_ROWS, :] = xp_ref[R:R + CARRY_ROWS, :]


def _gdn_kernel(*refs, L, TT, gps, t_valid, nT, li, has_init, has_prev):
    (q_ref, k_ref, v_ref, z_ref, ab_ref, cwq_ref, cwk_ref, cwv_ref, alog_ref, dtb_ref, ng_ref,
     cvq0_ref, cvk0_ref, cvv0_ref) = refs[:14]
    n_in = 14 + (1 if has_init else 0) + (1 if has_prev else 0)
    (y_ref, s_ref, cvq_ref, cvk_ref, cvv_ref,
     xpq_ref, xpk_ref, xpv_ref, aq_ref, ak_ref, av_ref, carry_ref, gb_ref) = refs[n_in:]
    ti = pl.program_id(1)
    h = pl.program_id(2)
    R = TT * SEQ_BLOCK
    s_ref = _own_layer(s_ref, li, has_prev, (ti == 0) & (h == 0))
    single_chunk = nT == 1 and TT == L
    s_src = refs[14] if (has_init and single_chunk) else s_ref

    @pl.when((ti == 0) & (h == 0))
    def _():
        if has_init and single_chunk:
            pass
        elif has_init:
            s_ref[...] = refs[14][...]
        else:
            s_ref[...] = jnp.zeros(s_ref.shape, F32)

    head_lanes = pl.ds(pl.multiple_of(h * DH, DH), DH)
    for j, (x_ref, cw_ref, xp_ref, a_ref, cv0_ref, cv_ref) in enumerate((
            (q_ref, cwq_ref, xpq_ref, aq_ref, cvq0_ref, cvq_ref),
            (k_ref, cwk_ref, xpk_ref, ak_ref, cvk0_ref, cvk_ref),
            (v_ref, cwv_ref, xpv_ref, av_ref, cvv0_ref, cvv_ref))):
        for gi in range(gps):
            tile = pl.ds(gi * R, R)
            cvr = pl.ds(gi * CARRY_ROWS, CARRY_ROWS)
            _causal_conv_tile(x_ref.at[tile], cw_ref, None, xp_ref.at[pl.ds(gi * (R + CARRY_ROWS), R + CARRY_ROWS)],
                              a_ref.at[tile], cv0_ref.at[cvr], cv_ref.at[cvr, head_lanes], ti=ti, R=R, width=DH,
                              tv_local=t_valid - (nT - 1) * TT, act_fn=_silu, carry_ref=carry_ref.at[j, h, cvr])

    @pl.when(h == 0)
    def _():
        tok, tok_in_chunk = _tile_token_ids(ti, TT, L)
        valid = tok < t_valid
        lane = lax.broadcasted_iota(jnp.int32, (R, LANE), 1)
        for gi in range(gps):
            tile = pl.ds(gi * R, R)
            abv = ab_ref[tile, :]
            g_all = -jnp.exp(alog_ref[...]) * _softplus(abv + dtb_ref[...])
            gam = _token_cumsum(jnp.where(valid, g_all, 0.0), tok_in_chunk, L)
            gb_ref[tile, :] = jnp.where(lane < HEADS, gam, jnp.where(valid, _sigmoid(abv), 0.0))

    ri = lax.broadcasted_iota(jnp.int32, (L, L), 0)
    ci = lax.broadcasted_iota(jnp.int32, (L, L), 1)
    causal = ci <= ri
    strict = ci < ri
    eye = ci == ri
    lane = lax.broadcasted_iota(jnp.int32, (L, LANE), 1)
    sel_a = lane == h
    sel_b = lane == HEADS + h
    pick = lambda tile, sel: jnp.sum(jnp.where(sel, tile, 0.0), axis=1, keepdims=True)
    n_double = int(math.log2(L)) - 1

    seqs = range(gps * SEQ_BLOCK)
    for c in range(TT // L):
        rows, st = [], []
        for b in seqs:
            r = pl.ds((b // SEQ_BLOCK) * R + c * L * SEQ_BLOCK + b % SEQ_BLOCK, L, stride=SEQ_BLOCK)
            rows.append(r)
            gbv = gb_ref[r, :]
            gam = pick(gbv, sel_a)
            beta = pick(gbv, sel_b)
            dec = jnp.where(causal, jnp.exp(gam - _row_from_col(gam, eye)), 0.0)
            q = aq_ref[r, :]
            k = ak_ref[r, :]
            q = q * lax.rsqrt(jnp.sum(q * q, axis=-1, keepdims=True) + EPS) * (DH ** -0.5)
            k = k * lax.rsqrt(jnp.sum(k * k, axis=-1, keepdims=True) + EPS)
            eg = jnp.exp(gam)
            g_last = gam[L - 1:L, :]
            kbeta = k * beta
            st.append(dict(dec=dec, q=q, k=k, kbeta=kbeta, eg=eg, g_last=g_last,
                           kd=k * jnp.exp(g_last - gam),
                           rhs=jnp.concatenate([av_ref[r, :] * beta, kbeta * eg], axis=1)))
        for b in seqs:
            s_ = st[b]
            amat = jnp.where(strict, _dot_nt(s_['kbeta'], s_['k']) * s_['dec'], 0.0)
            s_['qk'] = _dot_nt(s_['q'], s_['k']) * s_['dec']
            s_['o'] = _dot(s_['q'] * s_['eg'], s_src[b, h])
            s_['x'] = -amat
            s_['p'] = amat
        for b in seqs:
            st[b]['p'] = _dot(st[b]['p'], st[b]['p'])
        for i in range(n_double):
            for b in seqs:
                s_ = st[b]
                s_['xp'] = _dot(s_['x'], s_['p'])
                if i + 1 < n_double:
                    s_['p2'] = _dot(s_['p'], s_['p'])
            for b in seqs:
                s_ = st[b]
                s_['x'] = s_['x'] + s_['p'] + s_['xp']
                if i + 1 < n_double:
                    s_['p'] = s_['p2']
        for b in seqs:
            s_ = st[b]
            s_['sol'] = s_['rhs'] + _dot(s_['x'], s_['rhs'])
        for b in seqs:
            s_ = st[b]
            s_['v_new'] = s_['sol'][:, :DH] - _dot(s_['sol'][:, DH:], s_src[b, h])
        for b in seqs:
            s_ = st[b]
            o = s_['o'] + _dot(s_['qk'], s_['v_new'])
            s_ref[b, h] = jnp.exp(s_['g_last']) * s_src[b, h] + _dot_tn(s_['kd'], s_['v_new'])
            y_ref[rows[b], :] = _head_rms(o, ng_ref[...]) * _silu(z_ref[rows[b], :])


def _gdn(proj, projg, cw, alog, dtb, ng, cv0, s_init, prev_s, *, li, n_layers, B, G, nT, TT, L, t_valid, gps):
    assert G % gps == 0 and (gps == 1 or nT == 1)
    R = TT * SEQ_BLOCK * gps
    n = proj.shape[0]
    upb = BR // LANE
    blk = lambda u: pl.BlockSpec((R, DH), lambda g, t, h, u=u: (g * nT + t, u + h))
    st4, s_out_spec, s_shape = _state_specs(li, n_layers, B, prev_s is not None, gps)
    cvs = lambda j: pl.BlockSpec((gps * CARRY_ROWS, DH), lambda g, t, h, j=j: (g, j * upb + h))
    cws = lambda j: pl.BlockSpec((CONV_W, DH), lambda g, t, h, j=j: (0, j * upb + h))
    one = pl.BlockSpec((1, LANE), lambda g, t, h: (0, 0))
    cvo = pl.BlockSpec((gps * CARRY_ROWS, BR), lambda g, t, h: (g, 0))
    in_specs = [blk(U_GD_QKV), blk(U_GD_QKV + 4), blk(U_GD_QKV + 8), blk(U_GD_Z),
                pl.BlockSpec((R, LANE), lambda g, t, h: (g * nT + t, U_GD_AB)),
                cws(0), cws(1), cws(2), one, one,
                pl.BlockSpec((1, DH), lambda g, t, h: (0, h)),
                cvs(0), cvs(1), cvs(2)]
    args = [proj] * 4 + [projg, cw, cw, cw, alog, dtb, ng, cv0, cv0, cv0]
    if s_init is not None:
        in_specs.append(st4)
        args.append(s_init)
    aliases = {}
    if prev_s is not None:
        aliases = {len(args): 1}
        in_specs.append(pl.BlockSpec(memory_space=pl.ANY))
        args.append(prev_s)
    kern = functools.partial(_gdn_kernel, L=L, TT=TT, gps=gps, t_valid=t_valid, nT=nT, li=li,
                             has_init=s_init is not None, has_prev=prev_s is not None)
    cv_shape = jax.ShapeDtypeStruct((cv0.shape[0], BR), F32)
    return pl.pallas_call(
        kern,
        grid=(G // gps, nT, HEADS),
        in_specs=in_specs,
        out_specs=[pl.BlockSpec((R, DH), lambda g, t, h: (g * nT + t, h)), s_out_spec, cvo, cvo, cvo],
        out_shape=[jax.ShapeDtypeStruct((n, BR), F32), s_shape, cv_shape, cv_shape, cv_shape],
        scratch_shapes=([pltpu.VMEM((R + gps * CARRY_ROWS, DH), F32)] * 3 + [pltpu.VMEM((R, DH), F32)] * 3
                        + [pltpu.VMEM((3, HEADS, gps * CARRY_ROWS, DH), F32), pltpu.VMEM((R, LANE), F32)]),
        input_output_aliases=aliases,
        compiler_params=_cparams(("arbitrary", "arbitrary", "arbitrary")),
        name="gdn",
    )(*args)


def _s5_prep_kernel(are_ref, aim_ref, ldt_ref, bre_ref, bim_ref, abr_ref, abi_ref, bbr_ref, bbi_ref):
    a_re = are_ref[...]
    a_im = aim_ref[...]
    dt = jnp.exp(ldt_ref[...])
    mag = jnp.exp(dt * a_re)
    ang = dt * a_im
    ab_r = mag * jnp.cos(ang)
    ab_i = mag * jnp.sin(ang)
    den = a_re * a_re + a_im * a_im
    nr = ab_r - 1.0
    ni = ab_i
    f_r = (nr * a_re + ni * a_im) / den
    f_i = (ni * a_re - nr * a_im) / den
    abr_ref[...] = ab_r
    abi_ref[...] = ab_i
    f_r = f_r[0:1, :]
    f_i = f_i[0:1, :]
    bbr_ref[...] = f_r * bre_ref[...] - f_i * bim_ref[...]
    bbi_ref[...] = f_r * bim_ref[...] + f_i * bre_ref[...]


def _s5_prep(a_re, a_im, log_dt, b_re, b_im):
    rep = lambda a: jnp.broadcast_to(a.reshape(1, S5_STATE), (SEQ_BLOCK, S5_STATE))
    ldt = rep(jnp.broadcast_to(log_dt[:, None], (S5_G, S5_P)))
    bt = lambda b: jnp.transpose(b, (2, 0, 1)).reshape(S5_N, S5_STATE)
    shp = lambda r: jax.ShapeDtypeStruct((r, S5_STATE), F32)
    abr, abi, bbr, bbi = pl.pallas_call(
        _s5_prep_kernel,
        out_shape=[shp(SEQ_BLOCK), shp(SEQ_BLOCK), shp(S5_N), shp(S5_N)],
        name="s5_prep",
    )(rep(a_re), rep(a_im), ldt, bt(b_re), bt(b_im))
    return abr, abi, bbr, bbi


def _gelu_tanh(x):
    return 0.5 * x * (1.0 + jnp.tanh(math.sqrt(2.0 / math.pi) * (x + 0.044715 * (x * x * x))))


def _s5_kernel(u_ref, z_ref, abr_ref, abi_ref, wbr_ref, wbi_ref, wcr_ref, wci_ref, d_ref, gw_ref, gb_ref,
               x0r_ref, x0i_ref, y_ref, xr_ref, xi_ref, hr_ref, hi_ref, ys_ref, *, TT, n_steps):
    ti = pl.program_id(1)
    NB = BR // LANE
    SB = S5_STATE // NB

    @pl.when(ti == 0)
    def _():
        xr_ref[...] = x0r_ref[...]
        xi_ref[...] = x0i_ref[...]

    for kb in range(NB):
        ub = u_ref[:, kb * LANE:(kb + 1) * LANE].astype(BF16)
        hr_ref[:, kb * SB:(kb + 1) * SB] = jnp.dot(ub, wbr_ref[kb], preferred_element_type=F32)
        hi_ref[:, kb * SB:(kb + 1) * SB] = jnp.dot(ub, wbi_ref[kb], preferred_element_type=F32)

    for kb in range(NB):
        sl = pl.ds(kb * SB, SB)
        ar = abr_ref[:, sl]
        ai = abi_ref[:, sl]

        def step(t, carry):
            xr, xi = carry
            rows = pl.ds(pl.multiple_of(t * SEQ_BLOCK, SEQ_BLOCK), SEQ_BLOCK)
            nxr = ar * xr - ai * xi + hr_ref[rows, sl]
            nxi = ar * xi + ai * xr + hi_ref[rows, sl]
            hr_ref[rows, sl] = nxr
            hi_ref[rows, sl] = nxi
            return nxr, nxi

        xr, xi = lax.fori_loop(0, n_steps, step, (xr_ref[:, sl], xi_ref[:, sl]))
        xr_ref[:, sl] = xr
        xi_ref[:, sl] = xi

    for kb in range(NB):
        sl = pl.ds(kb * SB, SB)
        cs = pl.ds(kb * LANE, LANE)
        yk = (jnp.dot(hr_ref[:, sl].astype(BF16), wcr_ref[kb], preferred_element_type=F32)
              - jnp.dot(hi_ref[:, sl].astype(BF16), wci_ref[kb], preferred_element_type=F32)
              + d_ref[:, cs] * u_ref[:, cs])
        ys_ref[:, cs] = _gelu_tanh(yk)
    ys = ys_ref[...]
    glu = ys * _sigmoid(jnp.dot(ys.astype(BF16), gw_ref[...], preferred_element_type=F32) + gb_ref[...])
    y_ref[...] = glu * _silu(z_ref[...])


def _s5(proj, abr, abi, wbr, wbi, wcr, wci, d, gw, gb, x0r, x0i, *, G, nT, TT, t_valid):
    assert nT == 1 or t_valid == nT * TT
    R = TT * SEQ_BLOCK
    n = proj.shape[0]
    full = lambda a: pl.BlockSpec(a.shape, lambda g, t, nd=a.ndim: (0,) * nd)
    st = pl.BlockSpec((SEQ_BLOCK, S5_STATE), lambda g, t: (g, 0))
    kern = functools.partial(_s5_kernel, TT=TT, n_steps=t_valid - (nT - 1) * TT)
    return pl.pallas_call(
        kern,
        grid=(G, nT),
        in_specs=[pl.BlockSpec((R, BR), lambda g, t: (g * nT + t, U_S5_U // 4)),
                  pl.BlockSpec((R, BR), lambda g, t: (g * nT + t, U_S5_Z // 4)),
                  full(abr), full(abi), full(wbr), full(wbi), full(wcr), full(wci), full(d), full(gw), full(gb),
                  st, st],
        out_specs=[pl.BlockSpec((R, BR), lambda g, t: (g * nT + t, 0)), st, st],
        out_shape=[jax.ShapeDtypeStruct((n, BR), F32),
                   jax.ShapeDtypeStruct(x0r.shape, F32),
                   jax.ShapeDtypeStruct(x0i.shape, F32)],
        scratch_shapes=[pltpu.VMEM((R, S5_STATE), F32), pltpu.VMEM((R, S5_STATE), F32),
                        pltpu.VMEM((R, BR), F32)],
        compiler_params=_cparams(("arbitrary", "arbitrary")),
        name="s5",
    )(proj, proj, abr, abi, wbr, wbi, wcr, wci, d, gw, gb, x0r, x0i)


def _lru_kernel(x_ref, z_ref, cw_ref, cb_ref, wa_ref, ba_ref, wx_ref, bx_ref, lam_ref, h0_ref, cv0_ref,
                y_ref, h_ref, cv_ref, xp_ref, xl_ref, a_ref, *, TT, n_steps, nT):
    ti = pl.program_id(1)
    R = TT * SEQ_BLOCK

    @pl.when(ti == 0)
    def _():
        h_ref[...] = h0_ref[...]

    _causal_conv_tile(x_ref, cw_ref, cb_ref, xp_ref, xl_ref, cv0_ref, cv_ref, ti=ti, R=R, width=BR,
                      tv_local=n_steps, act_fn=lambda a: a)

    xl = xl_ref[...]
    xb = xl.astype(BF16)
    r = _sigmoid(jnp.dot(xb, wa_ref[...], preferred_element_type=F32) + ba_ref[...])
    i = _sigmoid(jnp.dot(xb, wx_ref[...], preferred_element_type=F32) + bx_ref[...])
    log_a = -LRU_C * r * _softplus(-lam_ref[...])
    a = jnp.exp(log_a)
    a_ref[...] = a
    xl_ref[...] = jnp.sqrt(1.0 - a * a) * (i * xl)

    def step(t, h):
        rows = pl.ds(pl.multiple_of(t * SEQ_BLOCK, SEQ_BLOCK), SEQ_BLOCK)
        hn = a_ref[rows, :] * h + xl_ref[rows, :]
        xl_ref[rows, :] = hn
        return hn

    h_ref[...] = lax.fori_loop(0, n_steps, step, h_ref[...])
    y_ref[...] = xl_ref[...] * _silu(z_ref[...])


def _lru(proj, cw, cb, wa, ba, wx, bx, lam, h0, cv0, *, G, nT, TT, t_valid):
    assert nT == 1 or t_valid == nT * TT
    R = TT * SEQ_BLOCK
    n = proj.shape[0]
    full = lambda a: pl.BlockSpec(a.shape, lambda g, t, nd=a.ndim: (0,) * nd)
    st = pl.BlockSpec((SEQ_BLOCK, BR), lambda g, t: (g, 0))
    cvs = pl.BlockSpec((CARRY_ROWS, BR), lambda g, t: (g, 0))
    kern = functools.partial(_lru_kernel, TT=TT, n_steps=t_valid - (nT - 1) * TT, nT=nT)
    return pl.pallas_call(
        kern,
        grid=(G, nT),
        in_specs=[pl.BlockSpec((R, BR), lambda g, t: (g * nT + t, U_LRU_X // 4)),
                  pl.BlockSpec((R, BR), lambda g, t: (g * nT + t, U_LRU_Z // 4)),
                  full(cw), full(cb), full(wa), full(ba), full(wx), full(bx), full(lam), st, cvs],
        out_specs=[pl.BlockSpec((R, BR), lambda g, t: (g * nT + t, 0)), st, cvs],
        out_shape=[jax.ShapeDtypeStruct((n, BR), F32),
                   jax.ShapeDtypeStruct(h0.shape, F32),
                   jax.ShapeDtypeStruct(cv0.shape, F32)],
        scratch_shapes=[pltpu.VMEM((R + CARRY_ROWS, BR), F32), pltpu.VMEM((R, BR), F32),
                        pltpu.VMEM((R, BR), F32)],
        compiler_params=_cparams(("arbitrary", "arbitrary")),
        name="lru",
    )(proj, proj, cw, cb, wa, ba, wx, bx, lam, h0, cv0)


def _merge_kernel(yml_ref, ygd_ref, ys5_ref, ylru_ref, x_ref, p_ref, g_ref, wg_ref,
                  wbr_ref, wout_ref, wpg_ref, wple_ref, pg_ref, o_ref, *, x_btd, p_btd, out_btd):
    tm = yml_ref.shape[0]
    rows = lambda ref, btd, c: jnp.swapaxes(ref[...], 0, 1).reshape(tm, c) if btd else ref[...]
    p = rows(p_ref, p_btd, PLE_DIM)
    x = rows(x_ref, x_btd, D_MODEL)
    h = _rms(x, g_ref[...]).astype(BF16)
    merged = None
    for nb, y_ref in enumerate((yml_ref, ygd_ref, ys5_ref, ylru_ref)):
        pbr = jnp.dot(y_ref[...].astype(BF16), wbr_ref[nb], preferred_element_type=F32)
        gates = _dot_nt(h, wg_ref[nb * D_MODEL:(nb + 1) * D_MODEL, :])
        term = _sigmoid(gates) * pbr
        merged = term if merged is None else merged + term
    out = jnp.dot(merged.astype(BF16), wout_ref[...], preferred_element_type=F32)
    r = x + _rms(out, pg_ref[...])
    gate = _sigmoid(jnp.dot(r.astype(BF16), wpg_ref[...], preferred_element_type=F32))
    res = r + gate * jnp.dot(p.astype(BF16), wple_ref[...], preferred_element_type=F32)
    if out_btd:
        o_ref[...] = jnp.swapaxes(res.reshape(tm // SEQ_BLOCK, SEQ_BLOCK, D_MODEL), 0, 1)
    else:
        o_ref[...] = res


def _merge(ys, x, p, g, wg_all, li, wbr, wout, wpg, wple, pg, tm, *, p_layer=None, out_btd=False):
    x_btd = x.ndim == 3
    n = ys[0].shape[0]
    tt = tm // SEQ_BLOCK
    row = lambda c: pl.BlockSpec((tm, c), lambda i: (i, 0))
    x_spec = pl.BlockSpec((SEQ_BLOCK, tt, D_MODEL), lambda i: (0, i, 0)) if x_btd else row(D_MODEL)
    full = lambda a: pl.BlockSpec(a.shape, lambda i, nd=a.ndim: (0,) * nd, pipeline_mode=pl.Buffered(1))
    p_spec = row(PLE_DIM) if p_layer is None else pl.BlockSpec((None, SEQ_BLOCK, tt, PLE_DIM),
                                                               lambda i: (p_layer, 0, i, 0))
    if out_btd:
        out_spec = pl.BlockSpec((SEQ_BLOCK, tt, D_MODEL), lambda i: (0, i, 0))
        out_shape = jax.ShapeDtypeStruct((SEQ_BLOCK, n // SEQ_BLOCK, D_MODEL), F32)
    else:
        out_spec, out_shape = row(D_MODEL), jax.ShapeDtypeStruct((n, D_MODEL), F32)
    return pl.pallas_call(
        functools.partial(_merge_kernel, x_btd=x_btd, p_btd=p_layer is not None, out_btd=out_btd),
        grid=(n // tm,),
        in_specs=[row(BR), row(BR), row(BR), row(BR), x_spec, p_spec, full(g),
                  pl.BlockSpec((None, GATE_COLS, D_MODEL), lambda i: (li, 0, 0), pipeline_mode=pl.Buffered(1)),
                  full(wbr), full(wout), full(wpg), full(wple), full(pg)],
        out_specs=out_spec,
        out_shape=out_shape,
        compiler_params=_cparams(("parallel",)),
        name="merge",
    )(*ys, x, p, g, wg_all, wbr, wout, wpg, wple, pg)


def _w_relayout_kernel(tbl_ref, win_ref, o_ref):
    o_ref[...] = win_ref[0].astype(BF16)


def _w_relayout(wt, runs, n_chunks):
    depth, d_in, d_model = wt.shape
    tbl = [None] * n_chunks
    for c0, src, count in runs:
        for j in range(count):
            off = src + j * W_CHUNK
            assert off % SEQ_BLOCK == 0 and off + W_CHUNK <= d_in
            tbl[c0 + j] = (off // SEQ_BLOCK, 0)
    assert all(t is not None for t in tbl)
    grid_spec = pltpu.PrefetchScalarGridSpec(
        num_scalar_prefetch=1,
        grid=(depth, n_chunks),
        in_specs=[pl.BlockSpec((pl.Element(1), pl.Element(W_CHUNK), pl.Element(d_model)),
                               lambda l, c, t: (l, t[c, 0] * SEQ_BLOCK, 0))],
        out_specs=pl.BlockSpec((None, W_CHUNK, d_model), lambda l, c, t: (l, c, 0)),
    )
    return pl.pallas_call(
        _w_relayout_kernel,
        grid_spec=grid_spec,
        out_shape=jax.ShapeDtypeStruct((depth, n_chunks * W_CHUNK, d_model), BF16),
        compiler_params=_cparams(("arbitrary", "arbitrary")),
        name="w_relayout",
    )(jnp.asarray(tbl, jnp.int32), wt)


def _permute_w_in(w_in):
    sizes = (BR, BR, BR, HEADS, HEADS, BR, BR, 3 * BR, HEADS, HEADS, BR, BR, BR, BR, BR, GATE_COLS)
    offs = [0]
    for s in sizes:
        offs.append(offs[-1] + s)
    (ml_q, _, _, ml_i, _, ml_o, _, gd_qkv, gd_a, _, gd_z, _, _, _, _, gates) = offs[:-1]
    wt = jnp.swapaxes(w_in, 1, 2)
    upc = W_CHUNK // LANE
    w_main = _w_relayout(wt, [(U_GD_QKV // upc, gd_qkv, 3), (U_ML // upc, ml_q, 3), (U_ML // upc + 3, ml_o, 2),
                              (U_GD_Z // upc, gd_z, 5)], N_UNITS // upc)
    w_gates = _w_relayout(wt, [(0, gates, GATE_COLS // W_CHUNK)], GATE_COLS // W_CHUNK)
    unit = lambda off: jnp.pad(wt[:, off:off + 2 * HEADS, :], ((0, 0), (0, LANE - 2 * HEADS), (0, 0)))
    w_scalar = jnp.concatenate([unit(ml_i), unit(gd_a)], axis=1)
    return w_main, w_gates, w_scalar


def _pad_lanes(*vecs):
    v = jnp.concatenate(vecs)
    return jnp.pad(v, (0, LANE - v.shape[0])).reshape(1, LANE)


def _block_diag(blocks):
    n, a, b = blocks.shape
    eye = jnp.eye(n, dtype=blocks.dtype)
    return jnp.einsum('ij,iab->iajb', eye, blocks).reshape(n * a, n * b)


def _to_rows(x, G):
    B, T, C = x.shape
    return x.reshape(G, SEQ_BLOCK, T, C).transpose(0, 2, 1, 3).reshape(G * T * SEQ_BLOCK, C)


def _from_rows(y, G, T):
    C = y.shape[-1]
    return y.reshape(G, T, SEQ_BLOCK, C).transpose(0, 2, 1, 3).reshape(G * SEQ_BLOCK, T, C)


def _group(x, p, states, params, *, T, t_valid, TT, L, tm_proj, tm_merge, gps=1):
    B = x.shape[0]
    G = B // SEQ_BLOCK
    nT = T // TT
    depth = p.shape[0]
    xr = x if G == 1 else _to_rows(x, G)
    tm_proj = min(tm_proj, B * T)
    tm_merge = min(tm_merge, B * T)
    new_states = []
    c_all = s_all = None
    gps = math.gcd(gps, G)
    mat = dict(n_layers=depth, B=B, G=G, nT=nT, TT=TT, L=L, t_valid=t_valid, gps=gps)
    for li in range(depth):
        if li > 0:
            mat['gps'] = math.gcd(2 * gps, G) if nT == 1 else gps
        lp = {k: v[li] for k, v in params.items() if not k.startswith('w_in_')}
        if states is None:
            ml_init = s_init = None
            gcv0 = jnp.zeros((G * CARRY_ROWS, 3 * BR), F32)
            x0r = jnp.zeros((B, S5_STATE), F32)
            x0i = jnp.zeros((B, S5_STATE), F32)
            h0 = jnp.zeros((B, BR), F32)
            lcv0 = jnp.zeros((G * CARRY_ROWS, BR), F32)
        else:
            (n0, m0, gcv0, x0r, x0i, h0, lcv0) = [states[j][li] for j in (1, 2, 4, 5, 6, 7, 8)]
            n0 = n0.reshape(G, SEQ_BLOCK, HEADS, DH).transpose(0, 2, 1, 3).reshape(B * HEADS, DH)
            ml_init = (states[0], n0, jnp.pad(m0, ((0, 0), (0, LANE - HEADS))))
            s_init = states[3]
            gcv0 = _to_rows(gcv0, G)
            x0r = x0r.reshape(B, S5_STATE)
            x0i = x0i.reshape(B, S5_STATE)
            lcv0 = _to_rows(lcv0, G)
        in_place = G == 1

        proj, projg = _in_proj(xr, lp['prenorm_g'], params['w_in_main'], params['w_in_scalar'], li, tm_proj)

        y_ml, c_all, n1, m1 = _mlstm(proj, projg, lp['ml_ifb'], lp['ml_norm_g'], ml_init, c_all, li=li, **mat)
        y_gd, s_all, cvq, cvk, cvv = _gdn(proj, projg, lp['gd_conv_w'], lp['gd_alog'], lp['gd_dtb'], lp['gd_norm_g'],
                                          gcv0, s_init, s_all, li=li, **dict(mat, L=min(L, GDN_CHUNK)))
        gcv1 = jnp.concatenate([cvq, cvk, cvv], axis=-1)
        y_s5, x1r, x1i = _s5(proj, lp['s5_abr'], lp['s5_abi'], lp['s5_wbr'], lp['s5_wbi'], lp['s5_wcr'],
                             lp['s5_wci'], lp['s5_d'], lp['s5_glu_w'], lp['s5_glu_b'], x0r, x0i,
                             G=G, nT=nT, TT=TT, t_valid=t_valid)
        y_lru, h1, lcv1 = _lru(proj, lp['lru_conv_w'], lp['lru_conv_b'], lp['lru_wa'], lp['lru_ba'],
                               lp['lru_wx'], lp['lru_bx'], lp['lru_lam'], h0, lcv0,
                               G=G, nT=nT, TT=TT, t_valid=t_valid)
        xr = _merge((y_ml, y_gd, y_s5, y_lru), xr, p if in_place else _to_rows(p[li], G),
                    lp['prenorm_g'], params['w_in_gates'], li, lp['w_branch'], lp['w_out'], lp['w_ple_gate'], lp['w_ple'], lp['postnorm_g'], tm_merge,
                    p_layer=li if in_place else None, out_btd=in_place and li == depth - 1)

        n1 = n1.reshape(G, HEADS, SEQ_BLOCK, DH).transpose(0, 2, 1, 3)
        new_states.append((n1.reshape(B, HEADS, DH), m1[:, :HEADS],
                           _from_rows(gcv1, G, CONV_W - 1), x1r.reshape(B, S5_G, S5_P),
                           x1i.reshape(B, S5_G, S5_P), h1, _from_rows(lcv1, G, CONV_W - 1)))
    y = xr if G == 1 else _from_rows(xr, G, T)
    n_s, m_s, gcv_s, xr_s, xi_s, h_s, lcv_s = (jnp.stack([ns[j] for ns in new_states]) for j in range(7))
    return y, (c_all, n_s, m_s, s_all, gcv_s, xr_s, xi_s, h_s, lcv_s)


def _prepare_params(prenorm_g, postnorm_g, w_in, ml_bi, ml_bf, ml_norm_g, gd_conv_w, gd_a_log, gd_dt_bias,
                    gd_norm_g, s5_a_re, s5_a_im, s5_log_dt, s5_b_re, s5_b_im, s5_c_re, s5_c_im, s5_d,
                    s5_glu_w, s5_glu_b, lru_conv_w, lru_conv_b, lru_wa, lru_ba, lru_wx, lru_bx, lru_lam,
                    w_branch, w_out, w_ple, w_ple_gate):
    depth = w_in.shape[0]
    row = lambda a: a.reshape(depth, 1, -1)
    per_layer = lambda f, *a: jnp.stack([f(*[x[i] for x in a]) for i in range(depth)])
    prm = dict(
        prenorm_g=row(prenorm_g), postnorm_g=row(postnorm_g),
        ml_ifb=per_layer(_pad_lanes, ml_bi, ml_bf), ml_norm_g=row(ml_norm_g),
        gd_conv_w=gd_conv_w, gd_norm_g=row(gd_norm_g),
        gd_alog=per_layer(lambda a: _pad_lanes(a), gd_a_log),
        gd_dtb=per_layer(lambda a: _pad_lanes(a), gd_dt_bias),
        s5_d=row(s5_d), s5_glu_w=s5_glu_w.astype(BF16), s5_glu_b=row(s5_glu_b),
        lru_conv_w=lru_conv_w, lru_conv_b=row(lru_conv_b),
        lru_wa=per_layer(_block_diag, lru_wa).astype(BF16), lru_ba=row(lru_ba),
        lru_wx=per_layer(_block_diag, lru_wx).astype(BF16), lru_bx=row(lru_bx), lru_lam=row(lru_lam),
        w_branch=w_branch.astype(BF16), w_out=w_out.astype(BF16), w_ple=w_ple.astype(BF16),
        w_ple_gate=w_ple_gate.astype(BF16),
    )
    abr, abi, wbr, wbi, wcr, wci = [], [], [], [], [], []
    nb = BR // LANE
    gpb = S5_G // nb
    for i in range(depth):
        a_r, a_i, bbr, bbi = _s5_prep(s5_a_re[i], s5_a_im[i], s5_log_dt[i], s5_b_re[i], s5_b_im[i])
        abr.append(a_r)
        abi.append(a_i)
        in_blocks = lambda bb: jnp.stack([_block_diag(bb.reshape(S5_N, S5_G, S5_P).transpose(1, 0, 2)
                                                      [k * gpb:(k + 1) * gpb]) for k in range(nb)])
        out_blocks = lambda c: jnp.stack([_block_diag(jnp.transpose(c, (0, 2, 1))[k * gpb:(k + 1) * gpb])
                                          for k in range(nb)])
        wbr.append(in_blocks(bbr).astype(BF16))
        wbi.append(in_blocks(bbi).astype(BF16))
        wcr.append(out_blocks(s5_c_re[i]).astype(BF16))
        wci.append(out_blocks(s5_c_im[i]).astype(BF16))
    prm.update(s5_abr=jnp.stack(abr), s5_abi=jnp.stack(abi), s5_wbr=jnp.stack(wbr), s5_wbi=jnp.stack(wbi),
               s5_wcr=jnp.stack(wcr), s5_wci=jnp.stack(wci))
    prm['w_in_main'], prm['w_in_gates'], prm['w_in_scalar'] = _permute_w_in(w_in)
    return prm


def kernel(x_prompt, x_sample, state_mlstm_c, state_mlstm_n, state_mlstm_m, state_gdn_s, state_gdn_conv, state_s5_re, state_s5_im, state_lru_h, state_lru_conv, p_prompt, p_sample, prenorm_g, postnorm_g, w_in, ml_bi, ml_bf, ml_norm_g, gd_conv_w, gd_a_log, gd_dt_bias, gd_norm_g, s5_a_re, s5_a_im, s5_log_dt, s5_b_re, s5_b_im, s5_c_re, s5_c_im, s5_d, s5_glu_w, s5_glu_b, lru_conv_w, lru_conv_b, lru_wa, lru_ba, lru_wx, lru_bx, lru_lam, w_branch, w_out, w_ple, w_ple_gate):
    prm = _prepare_params(prenorm_g, postnorm_g, w_in, ml_bi, ml_bf, ml_norm_g, gd_conv_w, gd_a_log,
                          gd_dt_bias, gd_norm_g, s5_a_re, s5_a_im, s5_log_dt, s5_b_re, s5_b_im, s5_c_re,
                          s5_c_im, s5_d, s5_glu_w, s5_glu_b, lru_conv_w, lru_conv_b, lru_wa, lru_ba, lru_wx,
                          lru_bx, lru_lam, w_branch, w_out, w_ple, w_ple_gate)

    t_p = x_prompt.shape[1]
    tt_p = math.gcd(t_p, PROMPT_CHUNK)
    y_prompt, pr = _group(x_prompt, p_prompt, None, prm, T=t_p, t_valid=t_p, TT=tt_p, L=tt_p,
                          tm_proj=TM_PROJ, tm_merge=TM_MERGE)

    t_s = x_sample.shape[1]
    t_pad = -(-t_s // SEQ_BLOCK) * SEQ_BLOCK
    pad_t = lambda a, ax: jnp.pad(a, [(0, t_pad - t_s) if i == ax else (0, 0) for i in range(a.ndim)])
    sample_states = (state_mlstm_c, state_mlstm_n, state_mlstm_m, state_gdn_s, state_gdn_conv,
                     state_s5_re, state_s5_im, state_lru_h, state_lru_conv)
    y_s, sa = _group(pad_t(x_sample, 1), pad_t(p_sample, 2), sample_states, prm, T=t_pad, t_valid=t_s,
                     TT=t_pad, L=t_pad, tm_proj=TM_PROJ, tm_merge=TM_MERGE, gps=SAMPLE_GROUPS_PER_STEP)
    y_sample = y_s[:, :t_s]
    return (y_prompt, y_sample) + pr + sa
```

```python
import functools
import math

import jax
import jax.numpy as jnp
from jax import lax
from jax.experimental import pallas as pl
from jax.experimental.pallas import tpu as pltpu

F32 = jnp.float32
BF16 = jnp.bfloat16

D_MODEL = 1024
BR = 512
HEADS = 4
DH = 128
CONV_W = 4
S5_G = 32
S5_N = 16
S5_P = 64
S5_STATE = S5_G * S5_P
LRU_BLOCKS = 8
LRU_BD = 64
LRU_C = 8.0
PLE_DIM = 256
EPS = 1e-6
NEG = -1e30

LANE = 128
SEQ_BLOCK = 8
CARRY_ROWS = (CONV_W - 1) * SEQ_BLOCK
PROMPT_CHUNK = 128
GDN_CHUNK = 64
SAMPLE_GROUPS_PER_STEP = 2
TM_PROJ = 512
TM_MERGE = 512

U_GD_QKV = 0
U_ML = 12
U_GD_Z = 32
U_S5_U = 36
U_S5_Z = 40
U_LRU_X = 44
U_LRU_Z = 48
N_UNITS = 52
PROJ_COLS = N_UNITS * LANE
U_ML_IF = 0
U_GD_AB = 1
GATE_COLS = 4 * D_MODEL
W_CHUNK = 4 * LANE

VMEM_LIMIT = 56 * 1024 * 1024


def _cparams(sem):
    return pltpu.CompilerParams(dimension_semantics=sem, vmem_limit_bytes=VMEM_LIMIT)


def _sigmoid(x):
    return 0.5 * jnp.tanh(0.5 * x) + 0.5


def _silu(x):
    return x * _sigmoid(x)


def _softplus(x):
    return jnp.maximum(x, 0.0) + jnp.log1p(jnp.exp(-jnp.abs(x)))


def _dot(a, b):
    return jnp.dot(a.astype(BF16), b.astype(BF16), preferred_element_type=F32)


def _dot_nt(a, b):
    return lax.dot_general(a.astype(BF16), b.astype(BF16), (((1,), (1,)), ((), ())),
                           preferred_element_type=F32)


def _dot_tn(a, b):
    return lax.dot_general(a.astype(BF16), b.astype(BF16), (((0,), (0,)), ((), ())),
                           preferred_element_type=F32)


def _cumsum_rows(x):
    n = x.shape[0]
    ri = lax.broadcasted_iota(jnp.int32, x.shape, 0)
    s = 1
    while s < n:
        x = x + jnp.where(ri >= s, pltpu.roll(x, s, axis=0), 0.0)
        s *= 2
    return x


def _row_from_col(col, eye):
    return jnp.sum(jnp.where(eye, col, 0.0), axis=0, keepdims=True)


def _head_rms(h, g):
    return h * lax.rsqrt(jnp.mean(h * h, axis=-1, keepdims=True) + EPS) * g


def _rms(x, g):
    return x * lax.rsqrt(jnp.mean(x * x, axis=-1, keepdims=True) + EPS) * g


def _in_proj_kernel(x_ref, g_ref, w_ref, wg_ref, o_ref, og_ref, *, x_btd):
    x = x_ref[...]
    if x_btd:
        x = jnp.swapaxes(x, 0, 1).reshape(o_ref.shape[0], D_MODEL)
    h = _rms(x, g_ref[...]).astype(BF16)
    o_ref[...] = _dot_nt(h, w_ref[...])
    og_ref[...] = _dot_nt(h, wg_ref[...])


def _in_proj(x, g, w_all, wg_all, li, tm):
    x_btd = x.ndim == 3
    n = x.shape[0] * x.shape[1] if x_btd else x.shape[0]
    x_spec = (pl.BlockSpec((SEQ_BLOCK, tm // SEQ_BLOCK, D_MODEL), lambda i: (0, i, 0)) if x_btd
              else pl.BlockSpec((tm, D_MODEL), lambda i: (i, 0)))
    resident = lambda c: pl.BlockSpec((None, c, D_MODEL), lambda i: (li, 0, 0), pipeline_mode=pl.Buffered(1))
    return pl.pallas_call(
        functools.partial(_in_proj_kernel, x_btd=x_btd),
        grid=(n // tm,),
        in_specs=[x_spec,
                  pl.BlockSpec((1, D_MODEL), lambda i: (0, 0)),
                  resident(PROJ_COLS), resident(2 * LANE)],
        out_specs=[pl.BlockSpec((tm, PROJ_COLS), lambda i: (i, 0)),
                   pl.BlockSpec((tm, 2 * LANE), lambda i: (i, 0))],
        out_shape=[jax.ShapeDtypeStruct((n, PROJ_COLS), F32), jax.ShapeDtypeStruct((n, 2 * LANE), F32)],
        compiler_params=_cparams(("parallel",)),
        name="in_proj",
    )(x, g, w_all, wg_all)


def _token_cumsum(x, tok_in_chunk, L):
    s = 1
    while s < L:
        x = x + jnp.where(tok_in_chunk >= s, pltpu.roll(x, s * SEQ_BLOCK, axis=0), 0.0)
        s *= 2
    return x


def _tile_token_ids(ti, TT, L):
    R = TT * SEQ_BLOCK
    t_local = lax.broadcasted_iota(jnp.int32, (R, LANE), 0) // SEQ_BLOCK
    return ti * TT + t_local, t_local % L


def _token_scan(x, op, fill):
    s = 1
    while s < x.shape[0]:
        shifted = jnp.concatenate([jnp.full((s,) + x.shape[1:], fill, x.dtype), x[:-s]], axis=0)
        x = op(x, shifted)
        s *= 2
    return x


ML_R, ML_M, ML_INTER, ML_ENEG, ML_WS = (j * HEADS for j in range(5))


def _mlstm_gate_tile(if_ref, ifb_ref, m_ref, sc_ref, gt_ref, *, ti, T, t_valid):
    shape = (T, SEQ_BLOCK, LANE)
    lane = lax.broadcasted_iota(jnp.int32, shape, 2)
    valid = (ti * T + lax.broadcasted_iota(jnp.int32, shape, 0)) < t_valid
    rot = lambda x, k: pltpu.roll(x.reshape(T * SEQ_BLOCK, LANE), k, axis=1).reshape(shape)
    ifv = (if_ref[...] + ifb_ref[...]).reshape(shape)
    logf = jnp.minimum(ifv, 0.0) - jnp.log1p(jnp.exp(-jnp.abs(ifv)))
    bcum = rot(_token_scan(jnp.where(valid, logf, 0.0), jnp.add, 0.0), LANE - HEADS)
    ig = jnp.where(valid, ifv, NEG)
    r = ig - bcum
    m_prev = m_ref[...]
    big_m = jnp.maximum(m_prev[None], _token_scan(r, jnp.maximum, NEG))
    inter = jnp.exp(m_prev[None] - big_m)
    eneg = jnp.exp(-(bcum + big_m))
    b_last = bcum[T - 1]
    g = b_last[None] - bcum + ig
    m_new = jnp.maximum(b_last + m_prev, jnp.max(g, axis=0))
    ws = jnp.exp(g - m_new[None])
    head_lanes = lane[0] < HEADS
    sc_ref[...] = jnp.where(head_lanes, jnp.exp(b_last + m_prev - m_new), 0.0)
    m_ref[...] = jnp.where(head_lanes, m_new, 0.0)
    packed = jnp.where(lane < ML_M, r,
                       jnp.where(lane < ML_INTER, rot(big_m, ML_M),
                                 jnp.where(lane < ML_ENEG, rot(inter, ML_INTER),
                                           jnp.where(lane < ML_WS, rot(eneg, ML_ENEG), rot(ws, ML_WS)))))
    gt_ref[...] = packed.reshape(T * SEQ_BLOCK, LANE)


def _own_layer(state_ref, li, has_prev, first_step):
    if has_prev:
        return state_ref

    @pl.when(first_step)
    def _():
        for other in range(state_ref.shape[0]):
            if other != li:
                state_ref[other] = jnp.zeros(state_ref.shape[1:], F32)

    return state_ref.at[li]


def _mlstm_kernel(*refs, L, gps, t_valid, li, has_init, has_prev, single_tile):
    q_ref, k_ref, v_ref, o_ref, z_ref, if_ref, ifb_ref, ng_ref = refs[:8]
    n_in = 8 + (3 if has_init else 0) + (1 if has_prev else 0)
    y_ref, c_ref, n_ref, m_ref, gt_ref, sc_ref, rt_ref = refs[n_in:]
    ti = pl.program_id(1)
    h = pl.program_id(2)
    R = L * SEQ_BLOCK
    seqs = range(gps * SEQ_BLOCK)
    rows = [pl.ds((j // SEQ_BLOCK) * R + j % SEQ_BLOCK, L, stride=SEQ_BLOCK) for j in seqs]
    n_rows = [pl.ds((j // SEQ_BLOCK) * SEQ_BLOCK * HEADS + h * SEQ_BLOCK + j % SEQ_BLOCK, 1) for j in seqs]
    use_transpose = L == LANE
    c_ref = _own_layer(c_ref, li, has_prev, (ti == 0) & (h == 0))
    c_src = refs[8] if (has_init and single_tile) else c_ref

    @pl.when((ti == 0) & (h == 0))
    def _():
        if has_init:
            if not single_tile:
                c_ref[...] = refs[8][...]
            n_ref[...] = refs[9][...]
            m_ref[...] = refs[10][...]
        else:
            c_ref[...] = jnp.zeros(c_ref.shape, F32)
            n_ref[...] = jnp.zeros(n_ref.shape, F32)
            m_ref[...] = jnp.zeros(m_ref.shape, F32)

    @pl.when(h == 0)
    def _():
        for gi in range(gps):
            grp = pl.ds(gi * SEQ_BLOCK, SEQ_BLOCK)
            tile = pl.ds(gi * R, R)
            _mlstm_gate_tile(if_ref.at[tile], ifb_ref, m_ref.at[grp], sc_ref.at[grp], gt_ref.at[tile],
                             ti=ti, T=L, t_valid=t_valid)
        if use_transpose:
            for b in seqs:
                rt_ref[b * SEQ_BLOCK:(b + 1) * SEQ_BLOCK, :] = gt_ref[rows[b], :].T[:SEQ_BLOCK, :]

    ri = lax.broadcasted_iota(jnp.int32, (L, L), 0)
    ci = lax.broadcasted_iota(jnp.int32, (L, L), 1)
    causal = ci <= ri
    eye = ci == ri
    to_lane0 = jnp.where(h == 0, 0, LANE - h)
    sc_all = pltpu.roll(sc_ref[...], to_lane0, axis=1)
    col = lambda tile, off: tile[:, off:off + 1]

    st = []
    for b in seqs:
        gq = pltpu.roll(gt_ref[rows[b], :], to_lane0, axis=1)
        q = q_ref[rows[b], :] * (DH ** -0.5)
        v = v_ref[rows[b], :]
        kw = k_ref[rows[b], :] * col(gq, ML_WS)
        cmat = c_src[b, h]
        sc = sc_all[b:b + 1, 0:1]
        st.append(dict(gq=gq, q=q, v=v, kw=kw, sc=sc, qk=_dot_nt(q, k_ref[rows[b], :]), qc=_dot(q, cmat)))
        c_ref[b, h] = sc * cmat + _dot_tn(kw, v)
    for b in seqs:
        s_ = st[b]
        if use_transpose:
            r_row = rt_ref[pl.ds(b * SEQ_BLOCK + h, 1), :]
        else:
            r_row = _row_from_col(col(s_['gq'], ML_R), eye)
        s_['s'] = s_['qk'] * jnp.where(causal, jnp.exp(r_row - col(s_['gq'], ML_M)), 0.0)
    for b in seqs:
        s_ = st[b]
        srow = n_rows[b]
        nvec = n_ref[srow, :]
        inter = col(s_['gq'], ML_INTER)
        s_['num'] = _dot(s_['s'], s_['v']) + inter * s_['qc']
        den = (jnp.sum(s_['s'], axis=1, keepdims=True)
               + inter * jnp.sum(s_['q'] * nvec, axis=1, keepdims=True))
        s_['den'] = jnp.maximum(jnp.abs(den), col(s_['gq'], ML_ENEG))
        n_ref[srow, :] = s_['sc'] * nvec + jnp.sum(s_['kw'], axis=0, keepdims=True)
    for b in seqs:
        s_ = st[b]
        yn = _head_rms(s_['num'] / s_['den'], ng_ref[...])
        y_ref[rows[b], :] = yn * _sigmoid(o_ref[rows[b], :]) * _silu(z_ref[rows[b], :])


def _state_specs(li, n_layers, B, has_prev, gps):
    nseq = gps * SEQ_BLOCK
    one = pl.BlockSpec((None, nseq, HEADS, DH, DH), lambda g, t, h: (li, g, 0, 0, 0))
    every = pl.BlockSpec((n_layers, nseq, HEADS, DH, DH), lambda g, t, h: (0, g, 0, 0, 0))
    shape = jax.ShapeDtypeStruct((n_layers, B, HEADS, DH, DH), F32)
    return one, (one if has_prev else every), shape


def _mlstm(proj, projg, ifb, ng, init, prev_c, *, li, n_layers, B, G, nT, TT, L, t_valid, gps):
    assert TT == L and G % gps == 0 and (gps == 1 or nT == 1)
    R = TT * SEQ_BLOCK * gps
    n = proj.shape[0]
    blk = lambda u: pl.BlockSpec((R, DH), lambda g, t, h, u=u: (g * nT + t, u + h))
    st4, c_out_spec, c_shape = _state_specs(li, n_layers, B, prev_c is not None, gps)
    n_spec = pl.BlockSpec((gps * SEQ_BLOCK * HEADS, LANE), lambda g, t, h: (g, 0))
    m_spec = pl.BlockSpec((gps * SEQ_BLOCK, LANE), lambda g, t, h: (g, 0))
    in_specs = [blk(U_ML), blk(U_ML + 4), blk(U_ML + 8), blk(U_ML + 12), blk(U_ML + 16),
                pl.BlockSpec((R, LANE), lambda g, t, h: (g * nT + t, U_ML_IF)),
                pl.BlockSpec((1, LANE), lambda g, t, h: (0, 0)),
                pl.BlockSpec((1, DH), lambda g, t, h: (0, h))]
    args = [proj] * 5 + [projg, ifb, ng]
    if init is not None:
        in_specs += [st4, n_spec, m_spec]
        args += list(init)
    aliases = {}
    if prev_c is not None:
        aliases = {len(args): 1}
        in_specs.append(pl.BlockSpec(memory_space=pl.ANY))
        args.append(prev_c)
    kern = functools.partial(_mlstm_kernel, L=L, gps=gps, t_valid=t_valid, li=li, has_init=init is not None,
                             has_prev=prev_c is not None, single_tile=nT == 1)
    return pl.pallas_call(
        kern,
        grid=(G // gps, nT, HEADS),
        in_specs=in_specs,
        out_specs=[pl.BlockSpec((R, DH), lambda g, t, h: (g * nT + t, h)), c_out_spec, n_spec, m_spec],
        out_shape=[jax.ShapeDtypeStruct((n, BR), F32), c_shape,
                   jax.ShapeDtypeStruct((B * HEADS, LANE), F32), jax.ShapeDtypeStruct((B, LANE), F32)],
        scratch_shapes=[pltpu.VMEM((R, LANE), F32), pltpu.VMEM((gps * SEQ_BLOCK, LANE), F32),
                        pltpu.VMEM((gps * SEQ_BLOCK * SEQ_BLOCK, LANE), F32)],
        input_output_aliases=aliases,
        compiler_params=_cparams(("arbitrary", "arbitrary", "arbitrary")),
        name="mlstm",
    )(*args)


def _causal_conv_tile(x_ref, w_ref, b_ref, xp_ref, act_ref, cv0_ref, cv_ref, *, ti, R, width, tv_local,
                      act_fn, carry_ref=None):
    @pl.when(ti == 0)
    def _():
        xp_ref[0:CARRY_ROWS, :] = cv0_ref[...]

    if carry_ref is not None:
        @pl.when(ti > 0)
        def _():
            xp_ref[0:CARRY_ROWS, :] = carry_ref[...]

    xp_ref[CARRY_ROWS:CARRY_ROWS + R, :] = x_ref[...]
    RB = next(rb for rb in (128, 32, SEQ_BLOCK) if R % rb == 0 and rb * min(width, BR) <= 128 * LANE)

    def blk(i, carry):
        r0 = pl.multiple_of(i * RB, SEQ_BLOCK)
        cw = min(width, BR)
        for c0 in range(0, width, cw):
            cs = pl.ds(c0, cw)
            acc = xp_ref[pl.ds(r0, RB), cs] * w_ref[0:1, cs]
            for j in range(1, CONV_W):
                acc = acc + xp_ref[pl.ds(r0 + j * SEQ_BLOCK, RB), cs] * w_ref[j:j + 1, cs]
            if b_ref is not None:
                acc = acc + b_ref[:, cs]
            act_ref[pl.ds(r0, RB), cs] = act_fn(acc)
        return carry

    trips = R // RB
    lax.fori_loop(0, trips, blk, 0, unroll=trips if trips <= 8 else 2)

    cv_ref[...] = xp_ref[tv_local * SEQ_BLOCK:tv_local * SEQ_BLOCK + CARRY_ROWS, :]
    if carry_ref is not None:
        carry_ref[...] = xp_ref[R:R + CARRY_ROWS, :]
    else:
        xp_ref[0:CARRY_ROWS, :] = xp_ref[R:R + CARRY_ROWS, :]


def _gdn_kernel(*refs, L, TT, gps, t_valid, nT, li, has_init, has_prev):
    (q_ref, k_ref, v_ref, z_ref, ab_ref, cwq_ref, cwk_ref, cwv_ref, alog_ref, dtb_ref, ng_ref,
     cvq0_ref, cvk0_ref, cvv0_ref) = refs[:14]
    n_in = 14 + (1 if has_init else 0) + (1 if has_prev else 0)
    (y_ref, s_ref, cvq_ref, cvk_ref, cvv_ref,
     xpq_ref, xpk_ref, xpv_ref, aq_ref, ak_ref, av_ref, carry_ref, gb_ref) = refs[n_in:]
    ti = pl.program_id(1)
    h = pl.program_id(2)
    R = TT * SEQ_BLOCK
    s_ref = _own_layer(s_ref, li, has_prev, (ti == 0) & (h == 0))
    single_chunk = nT == 1 and TT == L
    s_src = refs[14] if (has_init and single_chunk) else s_ref

    @pl.when((ti == 0) & (h == 0))
    def _():
        if has_init and single_chunk:
            pass
        elif has_init:
            s_ref[...] = refs[14][...]
        else:
            s_ref[...] = jnp.zeros(s_ref.shape, F32)

    head_lanes = pl.ds(pl.multiple_of(h * DH, DH), DH)
    for j, (x_ref, cw_ref, xp_ref, a_ref, cv0_ref, cv_ref) in enumerate((
            (q_ref, cwq_ref, xpq_ref, aq_ref, cvq0_ref, cvq_ref),
            (k_ref, cwk_ref, xpk_ref, ak_ref, cvk0_ref, cvk_ref),
            (v_ref, cwv_ref, xpv_ref, av_ref, cvv0_ref, cvv_ref))):
        for gi in range(gps):
            tile = pl.ds(gi * R, R)
            cvr = pl.ds(gi * CARRY_ROWS, CARRY_ROWS)
            _causal_conv_tile(x_ref.at[tile], cw_ref, None, xp_ref.at[pl.ds(gi * (R + CARRY_ROWS), R + CARRY_ROWS)],
                              a_ref.at[tile], cv0_ref.at[cvr], cv_ref.at[cvr, head_lanes], ti=ti, R=R, width=DH,
                              tv_local=t_valid - (nT - 1) * TT, act_fn=_silu, carry_ref=carry_ref.at[j, h, cvr])

    @pl.when(h == 0)
    def _():
        tok, tok_in_chunk = _tile_token_ids(ti, TT, L)
        valid = tok < t_valid
        lane = lax.broadcasted_iota(jnp.int32, (R, LANE), 1)
        for gi in range(gps):
            tile = pl.ds(gi * R, R)
            abv = ab_ref[tile, :]
            g_all = -jnp.exp(alog_ref[...]) * _softplus(abv + dtb_ref[...])
            gam = _token_cumsum(jnp.where(valid, g_all, 0.0), tok_in_chunk, L)
            gb_ref[tile, :] = jnp.where(lane < HEADS, gam, jnp.where(valid, _sigmoid(abv), 0.0))

    ri = lax.broadcasted_iota(jnp.int32, (L, L), 0)
    ci = lax.broadcasted_iota(jnp.int32, (L, L), 1)
    causal = ci <= ri
    strict = ci < ri
    eye = ci == ri
    lane = lax.broadcasted_iota(jnp.int32, (L, LANE), 1)
    sel_a = lane == h
    sel_b = lane == HEADS + h
    pick = lambda tile, sel: jnp.sum(jnp.where(sel, tile, 0.0), axis=1, keepdims=True)
    n_double = int(math.log2(L)) - 1

    seqs = range(gps * SEQ_BLOCK)
    for c in range(TT // L):
        rows, st = [], []
        for b in seqs:
            r = pl.ds((b // SEQ_BLOCK) * R + c * L * SEQ_BLOCK + b % SEQ_BLOCK, L, stride=SEQ_BLOCK)
            rows.append(r)
            gbv = gb_ref[r, :]
            gam = pick(gbv, sel_a)
            beta = pick(gbv, sel_b)
            dec = jnp.where(causal, jnp.exp(gam - _row_from_col(gam, eye)), 0.0)
            q = aq_ref[r, :]
            k = ak_ref[r, :]
            q = q * lax.rsqrt(jnp.sum(q * q, axis=-1, keepdims=True) + EPS) * (DH ** -0.5)
            k = k * lax.rsqrt(jnp.sum(k * k, axis=-1, keepdims=True) + EPS)
            eg = jnp.exp(gam)
            g_last = gam[L - 1:L, :]
            kbeta = k * beta
            st.append(dict(dec=dec, q=q, k=k, kbeta=kbeta, eg=eg, g_last=g_last,
                           kd=k * jnp.exp(g_last - gam),
                           rhs=jnp.concatenate([av_ref[r, :] * beta, kbeta * eg], axis=1)))
        for b in seqs:
            s_ = st[b]
            amat = jnp.where(strict, _dot_nt(s_['kbeta'], s_['k']) * s_['dec'], 0.0)
            s_['qk'] = _dot_nt(s_['q'], s_['k']) * s_['dec']
            s_['o'] = _dot(s_['q'] * s_['eg'], s_src[b, h])
            s_['x'] = -amat
            s_['p'] = amat
        for b in seqs:
            st[b]['p'] = _dot(st[b]['p'], st[b]['p'])
        for i in range(n_double):
            for b in seqs:
                s_ = st[b]
                s_['xp'] = _dot(s_['x'], s_['p'])
                if i + 1 < n_double:
                    s_['p2'] = _dot(s_['p'], s_['p'])
            for b in seqs:
                s_ = st[b]
                s_['x'] = s_['x'] + s_['p'] + s_['xp']
                if i + 1 < n_double:
                    s_['p'] = s_['p2']
        for b in seqs:
            s_ = st[b]
            s_['sol'] = s_['rhs'] + _dot(s_['x'], s_['rhs'])
        for b in seqs:
            s_ = st[b]
            s_['v_new'] = s_['sol'][:, :DH] - _dot(s_['sol'][:, DH:], s_src[b, h])
        for b in seqs:
            s_ = st[b]
            o = s_['o'] + _dot(s_['qk'], s_['v_new'])
            s_ref[b, h] = jnp.exp(s_['g_last']) * s_src[b, h] + _dot_tn(s_['kd'], s_['v_new'])
            y_ref[rows[b], :] = _head_rms(o, ng_ref[...]) * _silu(z_ref[rows[b], :])


def _gdn(proj, projg, cw, alog, dtb, ng, cv0, s_init, prev_s, *, li, n_layers, B, G, nT, TT, L, t_valid, gps):
    assert G % gps == 0 and (gps == 1 or nT == 1)
    R = TT * SEQ_BLOCK * gps
    n = proj.shape[0]
    upb = BR // LANE
    blk = lambda u: pl.BlockSpec((R, DH), lambda g, t, h, u=u: (g * nT + t, u + h))
    st4, s_out_spec, s_shape = _state_specs(li, n_layers, B, prev_s is not None, gps)
    cvs = lambda j: pl.BlockSpec((gps * CARRY_ROWS, DH), lambda g, t, h, j=j: (g, j * upb + h))
    cws = lambda j: pl.BlockSpec((CONV_W, DH), lambda g, t, h, j=j: (0, j * upb + h))
    one = pl.BlockSpec((1, LANE), lambda g, t, h: (0, 0))
    cvo = pl.BlockSpec((gps * CARRY_ROWS, BR), lambda g, t, h: (g, 0))
    in_specs = [blk(U_GD_QKV), blk(U_GD_QKV + 4), blk(U_GD_QKV + 8), blk(U_GD_Z),
                pl.BlockSpec((R, LANE), lambda g, t, h: (g * nT + t, U_GD_AB)),
                cws(0), cws(1), cws(2), one, one,
                pl.BlockSpec((1, DH), lambda g, t, h: (0, h)),
                cvs(0), cvs(1), cvs(2)]
    args = [proj] * 4 + [projg, cw, cw, cw, alog, dtb, ng, cv0, cv0, cv0]
    if s_init is not None:
        in_specs.append(st4)
        args.append(s_init)
    aliases = {}
    if prev_s is not None:
        aliases = {len(args): 1}
        in_specs.append(pl.BlockSpec(memory_space=pl.ANY))
        args.append(prev_s)
    kern = functools.partial(_gdn_kernel, L=L, TT=TT, gps=gps, t_valid=t_valid, nT=nT, li=li,
                             has_init=s_init is not None, has_prev=prev_s is not None)
    cv_shape = jax.ShapeDtypeStruct((cv0.shape[0], BR), F32)
    return pl.pallas_call(
        kern,
        grid=(G // gps, nT, HEADS),
        in_specs=in_specs,
        out_specs=[pl.BlockSpec((R, DH), lambda g, t, h: (g * nT + t, h)), s_out_spec, cvo, cvo, cvo],
        out_shape=[jax.ShapeDtypeStruct((n, BR), F32), s_shape, cv_shape, cv_shape, cv_shape],
        scratch_shapes=([pltpu.VMEM((R + gps * CARRY_ROWS, DH), F32)] * 3 + [pltpu.VMEM((R, DH), F32)] * 3
                        + [pltpu.VMEM((3, HEADS, gps * CARRY_ROWS, DH), F32), pltpu.VMEM((R, LANE), F32)]),
        input_output_aliases=aliases,
        compiler_params=_cparams(("arbitrary", "arbitrary", "arbitrary")),
        name="gdn",
    )(*args)


def _s5_prep_kernel(are_ref, aim_ref, ldt_ref, bre_ref, bim_ref, abr_ref, abi_ref, bbr_ref, bbi_ref):
    a_re = are_ref[...]
    a_im = aim_ref[...]
    dt = jnp.exp(ldt_ref[...])
    mag = jnp.exp(dt * a_re)
    ang = dt * a_im
    ab_r = mag * jnp.cos(ang)
    ab_i = mag * jnp.sin(ang)
    den = a_re * a_re + a_im * a_im
    nr = ab_r - 1.0
    ni = ab_i
    f_r = (nr * a_re + ni * a_im) / den
    f_i = (ni * a_re - nr * a_im) / den
    abr_ref[...] = ab_r
    abi_ref[...] = ab_i
    f_r = f_r[0:1, :]
    f_i = f_i[0:1, :]
    bbr_ref[...] = f_r * bre_ref[...] - f_i * bim_ref[...]
    bbi_ref[...] = f_r * bim_ref[...] + f_i * bre_ref[...]


def _s5_prep(a_re, a_im, log_dt, b_re, b_im):
    rep = lambda a: jnp.broadcast_to(a.reshape(1, S5_STATE), (SEQ_BLOCK, S5_STATE))
    ldt = rep(jnp.broadcast_to(log_dt[:, None], (S5_G, S5_P)))
    bt = lambda b: jnp.transpose(b, (2, 0, 1)).reshape(S5_N, S5_STATE)
    shp = lambda r: jax.ShapeDtypeStruct((r, S5_STATE), F32)
    abr, abi, bbr, bbi = pl.pallas_call(
        _s5_prep_kernel,
        out_shape=[shp(SEQ_BLOCK), shp(SEQ_BLOCK), shp(S5_N), shp(S5_N)],
        name="s5_prep",
    )(rep(a_re), rep(a_im), ldt, bt(b_re), bt(b_im))
    return abr, abi, bbr, bbi


def _gelu_tanh(x):
    return 0.5 * x * (1.0 + jnp.tanh(math.sqrt(2.0 / math.pi) * (x + 0.044715 * (x * x * x))))


def _s5_kernel(u_ref, z_ref, abr_ref, abi_ref, wbr_ref, wbi_ref, wcr_ref, wci_ref, d_ref, gw_ref, gb_ref,
               x0r_ref, x0i_ref, y_ref, xr_ref, xi_ref, hr_ref, hi_ref, ys_ref, *, TT, n_steps):
    ti = pl.program_id(1)
    NB = BR // LANE
    SB = S5_STATE // NB

    @pl.when(ti == 0)
    def _():
        xr_ref[...] = x0r_ref[...]
        xi_ref[...] = x0i_ref[...]

    for kb in range(NB):
        ub = u_ref[:, kb * LANE:(kb + 1) * LANE].astype(BF16)
        hr_ref[:, kb * SB:(kb + 1) * SB] = jnp.dot(ub, wbr_ref[kb], preferred_element_type=F32)
        hi_ref[:, kb * SB:(kb + 1) * SB] = jnp.dot(ub, wbi_ref[kb], preferred_element_type=F32)

    for kb in range(NB):
        sl = pl.ds(kb * SB, SB)
        ar = abr_ref[:, sl]
        ai = abi_ref[:, sl]

        def step(t, carry):
            xr, xi = carry
            rows = pl.ds(pl.multiple_of(t * SEQ_BLOCK, SEQ_BLOCK), SEQ_BLOCK)
            nxr = ar * xr - ai * xi + hr_ref[rows, sl]
            nxi = ar * xi + ai * xr + hi_ref[rows, sl]
            hr_ref[rows, sl] = nxr
            hi_ref[rows, sl] = nxi
            return nxr, nxi

        xr, xi = lax.fori_loop(0, n_steps, step, (xr_ref[:, sl], xi_ref[:, sl]))
        xr_ref[:, sl] = xr
        xi_ref[:, sl] = xi

    for kb in range(NB):
        sl = pl.ds(kb * SB, SB)
        cs = pl.ds(kb * LANE, LANE)
        yk = (jnp.dot(hr_ref[:, sl].astype(BF16), wcr_ref[kb], preferred_element_type=F32)
              - jnp.dot(hi_ref[:, sl].astype(BF16), wci_ref[kb], preferred_element_type=F32)
              + d_ref[:, cs] * u_ref[:, cs])
        ys_ref[:, cs] = _gelu_tanh(yk)
    ys = ys_ref[...]
    glu = ys * _sigmoid(jnp.dot(ys.astype(BF16), gw_ref[...], preferred_element_type=F32) + gb_ref[...])
    y_ref[...] = glu * _silu(z_ref[...])


def _s5(proj, abr, abi, wbr, wbi, wcr, wci, d, gw, gb, x0r, x0i, *, G, nT, TT, t_valid):
    assert nT == 1 or t_valid == nT * TT
    R = TT * SEQ_BLOCK
    n = proj.shape[0]
    full = lambda a: pl.BlockSpec(a.shape, lambda g, t, nd=a.ndim: (0,) * nd)
    st = pl.BlockSpec((SEQ_BLOCK, S5_STATE), lambda g, t: (g, 0))
    kern = functools.partial(_s5_kernel, TT=TT, n_steps=t_valid - (nT - 1) * TT)
    return pl.pallas_call(
        kern,
        grid=(G, nT),
        in_specs=[pl.BlockSpec((R, BR), lambda g, t: (g * nT + t, U_S5_U // 4)),
                  pl.BlockSpec((R, BR), lambda g, t: (g * nT + t, U_S5_Z // 4)),
                  full(abr), full(abi), full(wbr), full(wbi), full(wcr), full(wci), full(d), full(gw), full(gb),
                  st, st],
        out_specs=[pl.BlockSpec((R, BR), lambda g, t: (g * nT + t, 0)), st, st],
        out_shape=[jax.ShapeDtypeStruct((n, BR), F32),
                   jax.ShapeDtypeStruct(x0r.shape, F32),
                   jax.ShapeDtypeStruct(x0i.shape, F32)],
        scratch_shapes=[pltpu.VMEM((R, S5_STATE), F32), pltpu.VMEM((R, S5_STATE), F32),
                        pltpu.VMEM((R, BR), F32)],
        compiler_params=_cparams(("arbitrary", "arbitrary")),
        name="s5",
    )(proj, proj, abr, abi, wbr, wbi, wcr, wci, d, gw, gb, x0r, x0i)


def _lru_kernel(x_ref, z_ref, cw_ref, cb_ref, wa_ref, ba_ref, wx_ref, bx_ref, lam_ref, h0_ref, cv0_ref,
                y_ref, h_ref, cv_ref, xp_ref, xl_ref, a_ref, *, TT, n_steps, nT):
    ti = pl.program_id(1)
    R = TT * SEQ_BLOCK

    @pl.when(ti == 0)
    def _():
        h_ref[...] = h0_ref[...]

    _causal_conv_tile(x_ref, cw_ref, cb_ref, xp_ref, xl_ref, cv0_ref, cv_ref, ti=ti, R=R, width=BR,
                      tv_local=n_steps, act_fn=lambda a: a)

    xl = xl_ref[...]
    xb = xl.astype(BF16)
    r = _sigmoid(jnp.dot(xb, wa_ref[...], preferred_element_type=F32) + ba_ref[...])
    i = _sigmoid(jnp.dot(xb, wx_ref[...], preferred_element_type=F32) + bx_ref[...])
    log_a = -LRU_C * r * _softplus(-lam_ref[...])
    a = jnp.exp(log_a)
    a_ref[...] = a
    xl_ref[...] = jnp.sqrt(1.0 - a * a) * (i * xl)

    def step(t, h):
        rows = pl.ds(pl.multiple_of(t * SEQ_BLOCK, SEQ_BLOCK), SEQ_BLOCK)
        hn = a_ref[rows, :] * h + xl_ref[rows, :]
        xl_ref[rows, :] = hn
        return hn

    h_ref[...] = lax.fori_loop(0, n_steps, step, h_ref[...], unroll=4)
    y_ref[...] = xl_ref[...] * _silu(z_ref[...])


def _lru(proj, cw, cb, wa, ba, wx, bx, lam, h0, cv0, *, G, nT, TT, t_valid):
    assert nT == 1 or t_valid == nT * TT
    R = TT * SEQ_BLOCK
    n = proj.shape[0]
    full = lambda a: pl.BlockSpec(a.shape, lambda g, t, nd=a.ndim: (0,) * nd)
    st = pl.BlockSpec((SEQ_BLOCK, BR), lambda g, t: (g, 0))
    cvs = pl.BlockSpec((CARRY_ROWS, BR), lambda g, t: (g, 0))
    kern = functools.partial(_lru_kernel, TT=TT, n_steps=t_valid - (nT - 1) * TT, nT=nT)
    return pl.pallas_call(
        kern,
        grid=(G, nT),
        in_specs=[pl.BlockSpec((R, BR), lambda g, t: (g * nT + t, U_LRU_X // 4)),
                  pl.BlockSpec((R, BR), lambda g, t: (g * nT + t, U_LRU_Z // 4)),
                  full(cw), full(cb), full(wa), full(ba), full(wx), full(bx), full(lam), st, cvs],
        out_specs=[pl.BlockSpec((R, BR), lambda g, t: (g * nT + t, 0)), st, cvs],
        out_shape=[jax.ShapeDtypeStruct((n, BR), F32),
                   jax.ShapeDtypeStruct(h0.shape, F32),
                   jax.ShapeDtypeStruct(cv0.shape, F32)],
        scratch_shapes=[pltpu.VMEM((R + CARRY_ROWS, BR), F32), pltpu.VMEM((R, BR), F32),
                        pltpu.VMEM((R, BR), F32)],
        compiler_params=_cparams(("arbitrary", "arbitrary")),
        name="lru",
    )(proj, proj, cw, cb, wa, ba, wx, bx, lam, h0, cv0)


def _merge_kernel(yml_ref, ygd_ref, ys5_ref, ylru_ref, x_ref, p_ref, g_ref, wg_ref,
                  wbr_ref, wout_ref, wpg_ref, wple_ref, pg_ref, o_ref, *, x_btd, p_btd, out_btd):
    tm = yml_ref.shape[0]
    rows = lambda ref, btd, c: jnp.swapaxes(ref[...], 0, 1).reshape(tm, c) if btd else ref[...]
    p = rows(p_ref, p_btd, PLE_DIM)
    x = rows(x_ref, x_btd, D_MODEL)
    h = _rms(x, g_ref[...]).astype(BF16)
    merged = None
    for nb, y_ref in enumerate((yml_ref, ygd_ref, ys5_ref, ylru_ref)):
        pbr = jnp.dot(y_ref[...].astype(BF16), wbr_ref[nb], preferred_element_type=F32)
        gates = _dot_nt(h, wg_ref[nb * D_MODEL:(nb + 1) * D_MODEL, :])
        term = _sigmoid(gates) * pbr
        merged = term if merged is None else merged + term
    out = jnp.dot(merged.astype(BF16), wout_ref[...], preferred_element_type=F32)
    r = x + _rms(out, pg_ref[...])
    gate = _sigmoid(jnp.dot(r.astype(BF16), wpg_ref[...], preferred_element_type=F32))
    res = r + gate * jnp.dot(p.astype(BF16), wple_ref[...], preferred_element_type=F32)
    if out_btd:
        o_ref[...] = jnp.swapaxes(res.reshape(tm // SEQ_BLOCK, SEQ_BLOCK, D_MODEL), 0, 1)
    else:
        o_ref[...] = res


def _merge(ys, x, p, g, wg_all, li, wbr, wout, wpg, wple, pg, tm, *, p_layer=None, out_btd=False):
    x_btd = x.ndim == 3
    n = ys[0].shape[0]
    tt = tm // SEQ_BLOCK
    row = lambda c: pl.BlockSpec((tm, c), lambda i: (i, 0))
    x_spec = pl.BlockSpec((SEQ_BLOCK, tt, D_MODEL), lambda i: (0, i, 0)) if x_btd else row(D_MODEL)
    full = lambda a: pl.BlockSpec(a.shape, lambda i, nd=a.ndim: (0,) * nd, pipeline_mode=pl.Buffered(1))
    p_spec = row(PLE_DIM) if p_layer is None else pl.BlockSpec((None, SEQ_BLOCK, tt, PLE_DIM),
                                                               lambda i: (p_layer, 0, i, 0))
    if out_btd:
        out_spec = pl.BlockSpec((SEQ_BLOCK, tt, D_MODEL), lambda i: (0, i, 0))
        out_shape = jax.ShapeDtypeStruct((SEQ_BLOCK, n // SEQ_BLOCK, D_MODEL), F32)
    else:
        out_spec, out_shape = row(D_MODEL), jax.ShapeDtypeStruct((n, D_MODEL), F32)
    return pl.pallas_call(
        functools.partial(_merge_kernel, x_btd=x_btd, p_btd=p_layer is not None, out_btd=out_btd),
        grid=(n // tm,),
        in_specs=[row(BR), row(BR), row(BR), row(BR), x_spec, p_spec, full(g),
                  pl.BlockSpec((None, GATE_COLS, D_MODEL), lambda i: (li, 0, 0), pipeline_mode=pl.Buffered(1)),
                  full(wbr), full(wout), full(wpg), full(wple), full(pg)],
        out_specs=out_spec,
        out_shape=out_shape,
        compiler_params=_cparams(("parallel",)),
        name="merge",
    )(*ys, x, p, g, wg_all, wbr, wout, wpg, wple, pg)


def _w_relayout_kernel(tbl_ref, win_ref, o_ref):
    o_ref[...] = win_ref[0].astype(BF16)


def _w_relayout(wt, runs, n_chunks):
    depth, d_in, d_model = wt.shape
    tbl = [None] * n_chunks
    for c0, src, count in runs:
        for j in range(count):
            off = src + j * W_CHUNK
            assert off % SEQ_BLOCK == 0 and off + W_CHUNK <= d_in
            tbl[c0 + j] = (off // SEQ_BLOCK, 0)
    assert all(t is not None for t in tbl)
    grid_spec = pltpu.PrefetchScalarGridSpec(
        num_scalar_prefetch=1,
        grid=(depth, n_chunks),
        in_specs=[pl.BlockSpec((pl.Element(1), pl.Element(W_CHUNK), pl.Element(d_model)),
                               lambda l, c, t: (l, t[c, 0] * SEQ_BLOCK, 0))],
        out_specs=pl.BlockSpec((None, W_CHUNK, d_model), lambda l, c, t: (l, c, 0)),
    )
    return pl.pallas_call(
        _w_relayout_kernel,
        grid_spec=grid_spec,
        out_shape=jax.ShapeDtypeStruct((depth, n_chunks * W_CHUNK, d_model), BF16),
        compiler_params=_cparams(("arbitrary", "arbitrary")),
        name="w_relayout",
    )(jnp.asarray(tbl, jnp.int32), wt)


def _permute_w_in(w_in):
    sizes = (BR, BR, BR, HEADS, HEADS, BR, BR, 3 * BR, HEADS, HEADS, BR, BR, BR, BR, BR, GATE_COLS)
    offs = [0]
    for s in sizes:
        offs.append(offs[-1] + s)
    (ml_q, _, _, ml_i, _, ml_o, _, gd_qkv, gd_a, _, gd_z, _, _, _, _, gates) = offs[:-1]
    wt = jnp.swapaxes(w_in, 1, 2)
    upc = W_CHUNK // LANE
    w_main = _w_relayout(wt, [(U_GD_QKV // upc, gd_qkv, 3), (U_ML // upc, ml_q, 3), (U_ML // upc + 3, ml_o, 2),
                              (U_GD_Z // upc, gd_z, 5)], N_UNITS // upc)
    w_gates = _w_relayout(wt, [(0, gates, GATE_COLS // W_CHUNK)], GATE_COLS // W_CHUNK)
    unit = lambda off: jnp.pad(wt[:, off:off + 2 * HEADS, :], ((0, 0), (0, LANE - 2 * HEADS), (0, 0)))
    w_scalar = jnp.concatenate([unit(ml_i), unit(gd_a)], axis=1)
    return w_main, w_gates, w_scalar


def _pad_lanes(*vecs):
    v = jnp.concatenate(vecs)
    return jnp.pad(v, (0, LANE - v.shape[0])).reshape(1, LANE)


def _block_diag(blocks):
    n, a, b = blocks.shape
    eye = jnp.eye(n, dtype=blocks.dtype)
    return jnp.einsum('ij,iab->iajb', eye, blocks).reshape(n * a, n * b)


def _to_rows(x, G):
    B, T, C = x.shape
    return x.reshape(G, SEQ_BLOCK, T, C).transpose(0, 2, 1, 3).reshape(G * T * SEQ_BLOCK, C)


def _from_rows(y, G, T):
    C = y.shape[-1]
    return y.reshape(G, T, SEQ_BLOCK, C).transpose(0, 2, 1, 3).reshape(G * SEQ_BLOCK, T, C)


def _group(x, p, states, params, *, T, t_valid, TT, L, tm_proj, tm_merge, gps=1):
    B = x.shape[0]
    G = B // SEQ_BLOCK
    nT = T // TT
    depth = p.shape[0]
    xr = x if G == 1 else _to_rows(x, G)
    tm_proj = min(tm_proj, B * T)
    tm_merge = min(tm_merge, B * T)
    new_states = []
    c_all = s_all = None
    gps = math.gcd(gps, G)
    mat = dict(n_layers=depth, B=B, G=G, nT=nT, TT=TT, L=L, t_valid=t_valid, gps=gps)
    for li in range(depth):
        if li > 0:
            mat['gps'] = math.gcd(2 * gps, G) if nT == 1 else gps
        lp = {k: v[li] for k, v in params.items() if not k.startswith('w_in_')}
        if states is None:
            ml_init = s_init = None
            gcv0 = jnp.zeros((G * CARRY_ROWS, 3 * BR), F32)
            x0r = jnp.zeros((B, S5_STATE), F32)
            x0i = jnp.zeros((B, S5_STATE), F32)
            h0 = jnp.zeros((B, BR), F32)
            lcv0 = jnp.zeros((G * CARRY_ROWS, BR), F32)
        else:
            (n0, m0, gcv0, x0r, x0i, h0, lcv0) = [states[j][li] for j in (1, 2, 4, 5, 6, 7, 8)]
            n0 = n0.reshape(G, SEQ_BLOCK, HEADS, DH).transpose(0, 2, 1, 3).reshape(B * HEADS, DH)
            ml_init = (states[0], n0, jnp.pad(m0, ((0, 0), (0, LANE - HEADS))))
            s_init = states[3]
            gcv0 = _to_rows(gcv0, G)
            x0r = x0r.reshape(B, S5_STATE)
            x0i = x0i.reshape(B, S5_STATE)
            lcv0 = _to_rows(lcv0, G)
        in_place = G == 1

        proj, projg = _in_proj(xr, lp['prenorm_g'], params['w_in_main'], params['w_in_scalar'], li, tm_proj)

        y_ml, c_all, n1, m1 = _mlstm(proj, projg, lp['ml_ifb'], lp['ml_norm_g'], ml_init, c_all, li=li, **mat)
        y_gd, s_all, cvq, cvk, cvv = _gdn(proj, projg, lp['gd_conv_w'], lp['gd_alog'], lp['gd_dtb'], lp['gd_norm_g'],
                                          gcv0, s_init, s_all, li=li, **dict(mat, L=min(L, GDN_CHUNK)))
        gcv1 = jnp.concatenate([cvq, cvk, cvv], axis=-1)
        y_s5, x1r, x1i = _s5(proj, lp['s5_abr'], lp['s5_abi'], lp['s5_wbr'], lp['s5_wbi'], lp['s5_wcr'],
                             lp['s5_wci'], lp['s5_d'], lp['s5_glu_w'], lp['s5_glu_b'], x0r, x0i,
                             G=G, nT=nT, TT=TT, t_valid=t_valid)
        y_lru, h1, lcv1 = _lru(proj, lp['lru_conv_w'], lp['lru_conv_b'], lp['lru_wa'], lp['lru_ba'],
                               lp['lru_wx'], lp['lru_bx'], lp['lru_lam'], h0, lcv0,
                               G=G, nT=nT, TT=TT, t_valid=t_valid)
        xr = _merge((y_ml, y_gd, y_s5, y_lru), xr, p if in_place else _to_rows(p[li], G),
                    lp['prenorm_g'], params['w_in_gates'], li, lp['w_branch'], lp['w_out'], lp['w_ple_gate'], lp['w_ple'], lp['postnorm_g'], tm_merge,
                    p_layer=li if in_place else None, out_btd=in_place and li == depth - 1)

        n1 = n1.reshape(G, HEADS, SEQ_BLOCK, DH).transpose(0, 2, 1, 3)
        new_states.append((n1.reshape(B, HEADS, DH), m1[:, :HEADS],
                           _from_rows(gcv1, G, CONV_W - 1), x1r.reshape(B, S5_G, S5_P),
                           x1i.reshape(B, S5_G, S5_P), h1, _from_rows(lcv1, G, CONV_W - 1)))
    y = xr if G == 1 else _from_rows(xr, G, T)
    n_s, m_s, gcv_s, xr_s, xi_s, h_s, lcv_s = (jnp.stack([ns[j] for ns in new_states]) for j in range(7))
    return y, (c_all, n_s, m_s, s_all, gcv_s, xr_s, xi_s, h_s, lcv_s)


def _prepare_params(prenorm_g, postnorm_g, w_in, ml_bi, ml_bf, ml_norm_g, gd_conv_w, gd_a_log, gd_dt_bias,
                    gd_norm_g, s5_a_re, s5_a_im, s5_log_dt, s5_b_re, s5_b_im, s5_c_re, s5_c_im, s5_d,
                    s5_glu_w, s5_glu_b, lru_conv_w, lru_conv_b, lru_wa, lru_ba, lru_wx, lru_bx, lru_lam,
                    w_branch, w_out, w_ple, w_ple_gate):
    depth = w_in.shape[0]
    row = lambda a: a.reshape(depth, 1, -1)
    per_layer = lambda f, *a: jnp.stack([f(*[x[i] for x in a]) for i in range(depth)])
    prm = dict(
        prenorm_g=row(prenorm_g), postnorm_g=row(postnorm_g),
        ml_ifb=per_layer(_pad_lanes, ml_bi, ml_bf), ml_norm_g=row(ml_norm_g),
        gd_conv_w=gd_conv_w, gd_norm_g=row(gd_norm_g),
        gd_alog=per_layer(lambda a: _pad_lanes(a), gd_a_log),
        gd_dtb=per_layer(lambda a: _pad_lanes(a), gd_dt_bias),
        s5_d=row(s5_d), s5_glu_w=s5_glu_w.astype(BF16), s5_glu_b=row(s5_glu_b),
        lru_conv_w=lru_conv_w, lru_conv_b=row(lru_conv_b),
        lru_wa=per_layer(_block_diag, lru_wa).astype(BF16), lru_ba=row(lru_ba),
        lru_wx=per_layer(_block_diag, lru_wx).astype(BF16), lru_bx=row(lru_bx), lru_lam=row(lru_lam),
        w_branch=w_branch.astype(BF16), w_out=w_out.astype(BF16), w_ple=w_ple.astype(BF16),
        w_ple_gate=w_ple_gate.astype(BF16),
    )
    abr, abi, wbr, wbi, wcr, wci = [], [], [], [], [], []
    nb = BR // LANE
    gpb = S5_G // nb
    for i in range(depth):
        a_r, a_i, bbr, bbi = _s5_prep(s5_a_re[i], s5_a_im[i], s5_log_dt[i], s5_b_re[i], s5_b_im[i])
        abr.append(a_r)
        abi.append(a_i)
        in_blocks = lambda bb: jnp.stack([_block_diag(bb.reshape(S5_N, S5_G, S5_P).transpose(1, 0, 2)
                                                      [k * gpb:(k + 1) * gpb]) for k in range(nb)])
        out_blocks = lambda c: jnp.stack([_block_diag(jnp.transpose(c, (0, 2, 1))[k * gpb:(k + 1) * gpb])
                                          for k in range(nb)])
        wbr.append(in_blocks(bbr).astype(BF16))
        wbi.append(in_blocks(bbi).astype(BF16))
        wcr.append(out_blocks(s5_c_re[i]).astype(BF16))
        wci.append(out_blocks(s5_c_im[i]).astype(BF16))
    prm.update(s5_abr=jnp.stack(abr), s5_abi=jnp.stack(abi), s5_wbr=jnp.stack(wbr), s5_wbi=jnp.stack(wbi),
               s5_wcr=jnp.stack(wcr), s5_wci=jnp.stack(wci))
    prm['w_in_main'], prm['w_in_gates'], prm['w_in_scalar'] = _permute_w_in(w_in)
    return prm


def kernel(x_prompt, x_sample, state_mlstm_c, state_mlstm_n, state_mlstm_m, state_gdn_s, state_gdn_conv, state_s5_re, state_s5_im, state_lru_h, state_lru_conv, p_prompt, p_sample, prenorm_g, postnorm_g, w_in, ml_bi, ml_bf, ml_norm_g, gd_conv_w, gd_a_log, gd_dt_bias, gd_norm_g, s5_a_re, s5_a_im, s5_log_dt, s5_b_re, s5_b_im, s5_c_re, s5_c_im, s5_d, s5_glu_w, s5_glu_b, lru_conv_w, lru_conv_b, lru_wa, lru_ba, lru_wx, lru_bx, lru_lam, w_branch, w_out, w_ple, w_ple_gate):
    prm = _prepare_params(prenorm_g, postnorm_g, w_in, ml_bi, ml_bf, ml_norm_g, gd_conv_w, gd_a_log,
                          gd_dt_bias, gd_norm_g, s5_a_re, s5_a_im, s5_log_dt, s5_b_re, s5_b_im, s5_c_re,
                          s5_c_im, s5_d, s5_glu_w, s5_glu_b, lru_conv_w, lru_conv_b, lru_wa, lru_ba, lru_wx,
                          lru_bx, lru_lam, w_branch, w_out, w_ple, w_ple_gate)

    t_p = x_prompt.shape[1]
    tt_p = math.gcd(t_p, PROMPT_CHUNK)
    y_prompt, pr = _group(x_prompt, p_prompt, None, prm, T=t_p, t_valid=t_p, TT=tt_p, L=tt_p,
                          tm_proj=TM_PROJ, tm_merge=TM_MERGE)

    t_s = x_sample.shape[1]
    t_pad = -(-t_s // SEQ_BLOCK) * SEQ_BLOCK
    pad_t = lambda a, ax: jnp.pad(a, [(0, t_pad - t_s) if i == ax else (0, 0) for i in range(a.ndim)])
    sample_states = (state_mlstm_c, state_mlstm_n, state_mlstm_m, state_gdn_s, state_gdn_conv,
                     state_s5_re, state_s5_im, state_lru_h, state_lru_conv)
    y_s, sa = _group(pad_t(x_sample, 1), pad_t(p_sample, 2), sample_states, prm, T=t_pad, t_valid=t_s,
                     TT=t_pad, L=t_pad, tm_proj=TM_PROJ, tm_merge=TM_MERGE, gps=SAMPLE_GROUPS_PER_STEP)
    y_sample = y_s[:, :t_s]
    return (y_prompt, y_sample) + pr + sa
```

```python
import functools
import math

import jax
import jax.numpy as jnp
from jax import lax
from jax.experimental import pallas as pl
from jax.experimental.pallas import tpu as pltpu

F32 = jnp.float32
BF16 = jnp.bfloat16

D_MODEL = 1024
BR = 512
HEADS = 4
DH = 128
CONV_W = 4
S5_G = 32
S5_N = 16
S5_P = 64
S5_STATE = S5_G * S5_P
LRU_BLOCKS = 8
LRU_BD = 64
LRU_C = 8.0
PLE_DIM = 256
EPS = 1e-6
NEG = -1e30

LANE = 128
SEQ_BLOCK = 8
CARRY_ROWS = (CONV_W - 1) * SEQ_BLOCK
PROMPT_CHUNK = 128
GDN_CHUNK = 64
SAMPLE_GROUPS_PER_STEP = 2
TM_PROJ = 512
TM_MERGE = 512

U_GD_QKV = 0
U_ML = 12
U_GD_Z = 32
U_S5_U = 36
U_S5_Z = 40
U_LRU_X = 44
U_LRU_Z = 48
N_UNITS = 52
PROJ_COLS = N_UNITS * LANE
U_ML_IF = 0
U_GD_AB = 1
GATE_COLS = 4 * D_MODEL
W_CHUNK = 4 * LANE

VMEM_LIMIT = 56 * 1024 * 1024


def _cparams(sem):
    return pltpu.CompilerParams(dimension_semantics=sem, vmem_limit_bytes=VMEM_LIMIT)


def _sigmoid(x):
    return 0.5 * jnp.tanh(0.5 * x) + 0.5


def _silu(x):
    return x * _sigmoid(x)


def _softplus(x):
    return jnp.maximum(x, 0.0) + jnp.log1p(jnp.exp(-jnp.abs(x)))


def _dot(a, b):
    return jnp.dot(a.astype(BF16), b.astype(BF16), preferred_element_type=F32)


def _dot_nt(a, b):
    return lax.dot_general(a.astype(BF16), b.astype(BF16), (((1,), (1,)), ((), ())),
                           preferred_element_type=F32)


def _dot_tn(a, b):
    return lax.dot_general(a.astype(BF16), b.astype(BF16), (((0,), (0,)), ((), ())),
                           preferred_element_type=F32)


def _cumsum_rows(x):
    n = x.shape[0]
    ri = lax.broadcasted_iota(jnp.int32, x.shape, 0)
    s = 1
    while s < n:
        x = x + jnp.where(ri >= s, pltpu.roll(x, s, axis=0), 0.0)
        s *= 2
    return x


def _row_from_col(col, eye):
    return jnp.sum(jnp.where(eye, col, 0.0), axis=0, keepdims=True)


def _head_rms(h, g):
    return h * lax.rsqrt(jnp.mean(h * h, axis=-1, keepdims=True) + EPS) * g


def _rms(x, g):
    return x * lax.rsqrt(jnp.mean(x * x, axis=-1, keepdims=True) + EPS) * g


def _in_proj_kernel(x_ref, g_ref, w_ref, wg_ref, o_ref, og_ref, *, x_btd):
    x = x_ref[...]
    if x_btd:
        x = jnp.swapaxes(x, 0, 1).reshape(o_ref.shape[0], D_MODEL)
    h = _rms(x, g_ref[...]).astype(BF16)
    o_ref[...] = _dot_nt(h, w_ref[...])
    og_ref[...] = _dot_nt(h, wg_ref[...])


def _in_proj(x, g, w_all, wg_all, li, tm):
    x_btd = x.ndim == 3
    n = x.shape[0] * x.shape[1] if x_btd else x.shape[0]
    x_spec = (pl.BlockSpec((SEQ_BLOCK, tm // SEQ_BLOCK, D_MODEL), lambda i: (0, i, 0)) if x_btd
              else pl.BlockSpec((tm, D_MODEL), lambda i: (i, 0)))
    resident = lambda c: pl.BlockSpec((None, c, D_MODEL), lambda i: (li, 0, 0), pipeline_mode=pl.Buffered(1))
    return pl.pallas_call(
        functools.partial(_in_proj_kernel, x_btd=x_btd),
        grid=(n // tm,),
        in_specs=[x_spec,
                  pl.BlockSpec((1, D_MODEL), lambda i: (0, 0)),
                  resident(PROJ_COLS), resident(2 * LANE)],
        out_specs=[pl.BlockSpec((tm, PROJ_COLS), lambda i: (i, 0)),
                   pl.BlockSpec((tm, 2 * LANE), lambda i: (i, 0))],
        out_shape=[jax.ShapeDtypeStruct((n, PROJ_COLS), F32), jax.ShapeDtypeStruct((n, 2 * LANE), F32)],
        compiler_params=_cparams(("parallel",)),
        name="in_proj",
    )(x, g, w_all, wg_all)


def _token_cumsum(x, tok_in_chunk, L):
    s = 1
    while s < L:
        x = x + jnp.where(tok_in_chunk >= s, pltpu.roll(x, s * SEQ_BLOCK, axis=0), 0.0)
        s *= 2
    return x


def _tile_token_ids(ti, TT, L):
    R = TT * SEQ_BLOCK
    t_local = lax.broadcasted_iota(jnp.int32, (R, LANE), 0) // SEQ_BLOCK
    return ti * TT + t_local, t_local % L


def _token_scan(x, op, fill):
    s = 1
    while s < x.shape[0]:
        shifted = jnp.concatenate([jnp.full((s,) + x.shape[1:], fill, x.dtype), x[:-s]], axis=0)
        x = op(x, shifted)
        s *= 2
    return x


ML_R, ML_M, ML_INTER, ML_ENEG, ML_WS = (j * HEADS for j in range(5))


def _mlstm_gate_tile(if_ref, ifb_ref, m_ref, sc_ref, gt_ref, *, ti, T, t_valid):
    shape = (T, SEQ_BLOCK, LANE)
    lane = lax.broadcasted_iota(jnp.int32, shape, 2)
    valid = (ti * T + lax.broadcasted_iota(jnp.int32, shape, 0)) < t_valid
    rot = lambda x, k: pltpu.roll(x.reshape(T * SEQ_BLOCK, LANE), k, axis=1).reshape(shape)
    ifv = (if_ref[...] + ifb_ref[...]).reshape(shape)
    logf = jnp.minimum(ifv, 0.0) - jnp.log1p(jnp.exp(-jnp.abs(ifv)))
    bcum = rot(_token_scan(jnp.where(valid, logf, 0.0), jnp.add, 0.0), LANE - HEADS)
    ig = jnp.where(valid, ifv, NEG)
    r = ig - bcum
    m_prev = m_ref[...]
    big_m = jnp.maximum(m_prev[None], _token_scan(r, jnp.maximum, NEG))
    inter = jnp.exp(m_prev[None] - big_m)
    eneg = jnp.exp(-(bcum + big_m))
    b_last = bcum[T - 1]
    g = b_last[None] - bcum + ig
    m_new = jnp.maximum(b_last + m_prev, jnp.max(g, axis=0))
    ws = jnp.exp(g - m_new[None])
    head_lanes = lane[0] < HEADS
    sc_ref[...] = jnp.where(head_lanes, jnp.exp(b_last + m_prev - m_new), 0.0)
    m_ref[...] = jnp.where(head_lanes, m_new, 0.0)
    packed = jnp.where(lane < ML_M, r,
                       jnp.where(lane < ML_INTER, rot(big_m, ML_M),
                                 jnp.where(lane < ML_ENEG, rot(inter, ML_INTER),
                                           jnp.where(lane < ML_WS, rot(eneg, ML_ENEG), rot(ws, ML_WS)))))
    gt_ref[...] = packed.reshape(T * SEQ_BLOCK, LANE)


def _own_layer(state_ref, li, has_prev, first_step):
    if has_prev:
        return state_ref

    @pl.when(first_step)
    def _():
        for other in range(state_ref.shape[0]):
            if other != li:
                state_ref[other] = jnp.zeros(state_ref.shape[1:], F32)

    return state_ref.at[li]


def _mlstm_kernel(*refs, L, gps, t_valid, li, has_init, has_prev, single_tile):
    q_ref, k_ref, v_ref, o_ref, z_ref, if_ref, ifb_ref, ng_ref = refs[:8]
    n_in = 8 + (3 if has_init else 0) + (1 if has_prev else 0)
    y_ref, c_ref, n_ref, m_ref, gt_ref, sc_ref, rt_ref = refs[n_in:]
    ti = pl.program_id(1)
    h = pl.program_id(2)
    R = L * SEQ_BLOCK
    seqs = range(gps * SEQ_BLOCK)
    rows = [pl.ds((j // SEQ_BLOCK) * R + j % SEQ_BLOCK, L, stride=SEQ_BLOCK) for j in seqs]
    n_rows = [pl.ds((j // SEQ_BLOCK) * SEQ_BLOCK * HEADS + h * SEQ_BLOCK + j % SEQ_BLOCK, 1) for j in seqs]
    use_transpose = L == LANE
    c_ref = _own_layer(c_ref, li, has_prev, (ti == 0) & (h == 0))
    c_src = refs[8] if (has_init and single_tile) else c_ref

    @pl.when((ti == 0) & (h == 0))
    def _():
        if has_init:
            if not single_tile:
                c_ref[...] = refs[8][...]
            n_ref[...] = refs[9][...]
            m_ref[...] = refs[10][...]
        else:
            c_ref[...] = jnp.zeros(c_ref.shape, F32)
            n_ref[...] = jnp.zeros(n_ref.shape, F32)
            m_ref[...] = jnp.zeros(m_ref.shape, F32)

    @pl.when(h == 0)
    def _():
        for gi in range(gps):
            grp = pl.ds(gi * SEQ_BLOCK, SEQ_BLOCK)
            tile = pl.ds(gi * R, R)
            _mlstm_gate_tile(if_ref.at[tile], ifb_ref, m_ref.at[grp], sc_ref.at[grp], gt_ref.at[tile],
                             ti=ti, T=L, t_valid=t_valid)
        if use_transpose:
            for b in seqs:
                rt_ref[b * SEQ_BLOCK:(b + 1) * SEQ_BLOCK, :] = gt_ref[rows[b], :].T[:SEQ_BLOCK, :]

    ri = lax.broadcasted_iota(jnp.int32, (L, L), 0)
    ci = lax.broadcasted_iota(jnp.int32, (L, L), 1)
    causal = ci <= ri
    eye = ci == ri
    to_lane0 = jnp.where(h == 0, 0, LANE - h)
    sc_all = pltpu.roll(sc_ref[...], to_lane0, axis=1)
    col = lambda tile, off: tile[:, off:off + 1]

    st = []
    for b in seqs:
        gq = pltpu.roll(gt_ref[rows[b], :], to_lane0, axis=1)
        q = q_ref[rows[b], :] * (DH ** -0.5)
        v = v_ref[rows[b], :]
        kw = k_ref[rows[b], :] * col(gq, ML_WS)
        cmat = c_src[b, h]
        sc = sc_all[b:b + 1, 0:1]
        st.append(dict(gq=gq, q=q, v=v, kw=kw, sc=sc, qk=_dot_nt(q, k_ref[rows[b], :]), qc=_dot(q, cmat)))
        c_ref[b, h] = sc * cmat + _dot_tn(kw, v)
    for b in seqs:
        s_ = st[b]
        if use_transpose:
            r_row = rt_ref[pl.ds(b * SEQ_BLOCK + h, 1), :]
        else:
            r_row = _row_from_col(col(s_['gq'], ML_R), eye)
        s_['s'] = s_['qk'] * jnp.where(causal, jnp.exp(r_row - col(s_['gq'], ML_M)), 0.0)
    for b in seqs:
        s_ = st[b]
        srow = n_rows[b]
        nvec = n_ref[srow, :]
        inter = col(s_['gq'], ML_INTER)
        s_['num'] = _dot(s_['s'], s_['v']) + inter * s_['qc']
        den = (jnp.sum(s_['s'], axis=1, keepdims=True)
               + inter * jnp.sum(s_['q'] * nvec, axis=1, keepdims=True))
        s_['den'] = jnp.maximum(jnp.abs(den), col(s_['gq'], ML_ENEG))
        n_ref[srow, :] = s_['sc'] * nvec + jnp.sum(s_['kw'], axis=0, keepdims=True)
    for b in seqs:
        s_ = st[b]
        yn = _head_rms(s_['num'] / s_['den'], ng_ref[...])
        y_ref[rows[b], :] = yn * _sigmoid(o_ref[rows[b], :]) * _silu(z_ref[rows[b], :])


def _state_specs(li, n_layers, B, has_prev, gps):
    nseq = gps * SEQ_BLOCK
    one = pl.BlockSpec((None, nseq, HEADS, DH, DH), lambda g, t, h: (li, g, 0, 0, 0))
    every = pl.BlockSpec((n_layers, nseq, HEADS, DH, DH), lambda g, t, h: (0, g, 0, 0, 0))
    shape = jax.ShapeDtypeStruct((n_layers, B, HEADS, DH, DH), F32)
    return one, (one if has_prev else every), shape


def _mlstm(proj, projg, ifb, ng, init, prev_c, *, li, n_layers, B, G, nT, TT, L, t_valid, gps):
    assert TT == L and G % gps == 0 and (gps == 1 or nT == 1)
    R = TT * SEQ_BLOCK * gps
    n = proj.shape[0]
    blk = lambda u: pl.BlockSpec((R, DH), lambda g, t, h, u=u: (g * nT + t, u + h))
    st4, c_out_spec, c_shape = _state_specs(li, n_layers, B, prev_c is not None, gps)
    n_spec = pl.BlockSpec((gps * SEQ_BLOCK * HEADS, LANE), lambda g, t, h: (g, 0))
    m_spec = pl.BlockSpec((gps * SEQ_BLOCK, LANE), lambda g, t, h: (g, 0))
    in_specs = [blk(U_ML), blk(U_ML + 4), blk(U_ML + 8), blk(U_ML + 12), blk(U_ML + 16),
                pl.BlockSpec((R, LANE), lambda g, t, h: (g * nT + t, U_ML_IF)),
                pl.BlockSpec((1, LANE), lambda g, t, h: (0, 0)),
                pl.BlockSpec((1, DH), lambda g, t, h: (0, h))]
    args = [proj] * 5 + [projg, ifb, ng]
    if init is not None:
        in_specs += [st4, n_spec, m_spec]
        args += list(init)
    aliases = {}
    if prev_c is not None:
        aliases = {len(args): 1}
        in_specs.append(pl.BlockSpec(memory_space=pl.ANY))
        args.append(prev_c)
    kern = functools.partial(_mlstm_kernel, L=L, gps=gps, t_valid=t_valid, li=li, has_init=init is not None,
                             has_prev=prev_c is not None, single_tile=nT == 1)
    return pl.pallas_call(
        kern,
        grid=(G // gps, nT, HEADS),
        in_specs=in_specs,
        out_specs=[pl.BlockSpec((R, DH), lambda g, t, h: (g * nT + t, h)), c_out_spec, n_spec, m_spec],
        out_shape=[jax.ShapeDtypeStruct((n, BR), F32), c_shape,
                   jax.ShapeDtypeStruct((B * HEADS, LANE), F32), jax.ShapeDtypeStruct((B, LANE), F32)],
        scratch_shapes=[pltpu.VMEM((R, LANE), F32), pltpu.VMEM((gps * SEQ_BLOCK, LANE), F32),
                        pltpu.VMEM((gps * SEQ_BLOCK * SEQ_BLOCK, LANE), F32)],
        input_output_aliases=aliases,
        compiler_params=_cparams(("arbitrary", "arbitrary", "arbitrary")),
        name="mlstm",
    )(*args)


def _causal_conv_tile(x_ref, w_ref, b_ref, xp_ref, act_ref, cv0_ref, cv_ref, *, ti, R, width, tv_local,
                      act_fn, carry_ref=None):
    @pl.when(ti == 0)
    def _():
        xp_ref[0:CARRY_ROWS, :] = cv0_ref[...]

    if carry_ref is not None:
        @pl.when(ti > 0)
        def _():
            xp_ref[0:CARRY_ROWS, :] = carry_ref[...]

    xp_ref[CARRY_ROWS:CARRY_ROWS + R, :] = x_ref[...]
    RB = next(rb for rb in (128, 32, SEQ_BLOCK) if R % rb == 0 and rb * min(width, BR) <= 128 * LANE)

    def blk(i, carry):
        r0 = pl.multiple_of(i * RB, SEQ_BLOCK)
        cw = min(width, BR)
        for c0 in range(0, width, cw):
            cs = pl.ds(c0, cw)
            acc = xp_ref[pl.ds(r0, RB), cs] * w_ref[0:1, cs]
            for j in range(1, CONV_W):
                acc = acc + xp_ref[pl.ds(r0 + j * SEQ_BLOCK, RB), cs] * w_ref[j:j + 1, cs]
            if b_ref is not None:
                acc = acc + b_ref[:, cs]
            act_ref[pl.ds(r0, RB), cs] = act_fn(acc)
        return carry

    trips = R // RB
    lax.fori_loop(0, trips, blk, 0, unroll=trips if trips <= 8 else 2)

    cv_ref[...] = xp_ref[tv_local * SEQ_BLOCK:tv_local * SEQ_BLOCK + CARRY_ROWS, :]
    if carry_ref is not None:
        carry_ref[...] = xp_ref[R:R + CARRY_ROWS, :]
    else:
        xp_ref[0:CARRY_ROWS, :] = xp_ref[R:R + CARRY_ROWS, :]


def _gdn_kernel(*refs, L, TT, gps, t_valid, nT, li, has_init, has_prev):
    (q_ref, k_ref, v_ref, z_ref, ab_ref, cwq_ref, cwk_ref, cwv_ref, alog_ref, dtb_ref, ng_ref,
     cvq0_ref, cvk0_ref, cvv0_ref) = refs[:14]
    n_in = 14 + (1 if has_init else 0) + (1 if has_prev else 0)
    (y_ref, s_ref, cvq_ref, cvk_ref, cvv_ref,
     xpq_ref, xpk_ref, xpv_ref, aq_ref, ak_ref, av_ref, carry_ref, gb_ref) = refs[n_in:]
    ti = pl.program_id(1)
    h = pl.program_id(2)
    R = TT * SEQ_BLOCK
    s_ref = _own_layer(s_ref, li, has_prev, (ti == 0) & (h == 0))
    single_chunk = nT == 1 and TT == L
    s_src = refs[14] if (has_init and single_chunk) else s_ref

    @pl.when((ti == 0) & (h == 0))
    def _():
        if has_init and single_chunk:
            pass
        elif has_init:
            s_ref[...] = refs[14][...]
        else:
            s_ref[...] = jnp.zeros(s_ref.shape, F32)

    head_lanes = pl.ds(pl.multiple_of(h * DH, DH), DH)
    for j, (x_ref, cw_ref, xp_ref, a_ref, cv0_ref, cv_ref) in enumerate((
            (q_ref, cwq_ref, xpq_ref, aq_ref, cvq0_ref, cvq_ref),
            (k_ref, cwk_ref, xpk_ref, ak_ref, cvk0_ref, cvk_ref),
            (v_ref, cwv_ref, xpv_ref, av_ref, cvv0_ref, cvv_ref))):
        for gi in range(gps):
            tile = pl.ds(gi * R, R)
            cvr = pl.ds(gi * CARRY_ROWS, CARRY_ROWS)
            _causal_conv_tile(x_ref.at[tile], cw_ref, None, xp_ref.at[pl.ds(gi * (R + CARRY_ROWS), R + CARRY_ROWS)],
                              a_ref.at[tile], cv0_ref.at[cvr], cv_ref.at[cvr, head_lanes], ti=ti, R=R, width=DH,
                              tv_local=t_valid - (nT - 1) * TT, act_fn=_silu, carry_ref=carry_ref.at[j, h, cvr])

    @pl.when(h == 0)
    def _():
        tok, tok_in_chunk = _tile_token_ids(ti, TT, L)
        valid = tok < t_valid
        lane = lax.broadcasted_iota(jnp.int32, (R, LANE), 1)
        for gi in range(gps):
            tile = pl.ds(gi * R, R)
            abv = ab_ref[tile, :]
            g_all = -jnp.exp(alog_ref[...]) * _softplus(abv + dtb_ref[...])
            gam = _token_cumsum(jnp.where(valid, g_all, 0.0), tok_in_chunk, L)
            gb_ref[tile, :] = jnp.where(lane < HEADS, gam, jnp.where(valid, _sigmoid(abv), 0.0))

    ri = lax.broadcasted_iota(jnp.int32, (L, L), 0)
    ci = lax.broadcasted_iota(jnp.int32, (L, L), 1)
    causal = ci <= ri
    strict = ci < ri
    eye = ci == ri
    lane = lax.broadcasted_iota(jnp.int32, (L, LANE), 1)
    sel_a = lane == h
    sel_b = lane == HEADS + h
    pick = lambda tile, sel: jnp.sum(jnp.where(sel, tile, 0.0), axis=1, keepdims=True)
    n_double = int(math.log2(L)) - 1

    seqs = range(gps * SEQ_BLOCK)
    n_seq = len(seqs)
    rows, st = [], []
    for c in range(TT // L):
        for b in seqs:
            r = pl.ds((b // SEQ_BLOCK) * R + c * L * SEQ_BLOCK + b % SEQ_BLOCK, L, stride=SEQ_BLOCK)
            rows.append(r)
            gbv = gb_ref[r, :]
            gam = pick(gbv, sel_a)
            beta = pick(gbv, sel_b)
            dec = jnp.where(causal, jnp.exp(gam - _row_from_col(gam, eye)), 0.0)
            q = aq_ref[r, :]
            k = ak_ref[r, :]
            q = q * lax.rsqrt(jnp.sum(q * q, axis=-1, keepdims=True) + EPS) * (DH ** -0.5)
            k = k * lax.rsqrt(jnp.sum(k * k, axis=-1, keepdims=True) + EPS)
            eg = jnp.exp(gam)
            g_last = gam[L - 1:L, :]
            kbeta = k * beta
            st.append(dict(dec=dec, q=q, k=k, kbeta=kbeta, eg=eg, g_last=g_last,
                           kd=k * jnp.exp(g_last - gam),
                           rhs=jnp.concatenate([av_ref[r, :] * beta, kbeta * eg], axis=1)))
    for s_ in st:
        amat = jnp.where(strict, _dot_nt(s_['kbeta'], s_['k']) * s_['dec'], 0.0)
        s_['qk'] = _dot_nt(s_['q'], s_['k']) * s_['dec']
        s_['x'] = -amat
        s_['p'] = amat
    for s_ in st:
        s_['p'] = _dot(s_['p'], s_['p'])
    for i in range(n_double):
        for s_ in st:
            s_['xp'] = _dot(s_['x'], s_['p'])
            if i + 1 < n_double:
                s_['p2'] = _dot(s_['p'], s_['p'])
        for s_ in st:
            s_['x'] = s_['x'] + s_['p'] + s_['xp']
            if i + 1 < n_double:
                s_['p'] = s_['p2']
    for s_ in st:
        s_['sol'] = s_['rhs'] + _dot(s_['x'], s_['rhs'])
    for c in range(TT // L):
        chunk = [(b, st[c * n_seq + b], rows[c * n_seq + b]) for b in seqs]
        for b, s_, _ in chunk:
            s_['o'] = _dot(s_['q'] * s_['eg'], s_src[b, h])
            s_['v_new'] = s_['sol'][:, :DH] - _dot(s_['sol'][:, DH:], s_src[b, h])
        for b, s_, r in chunk:
            o = s_['o'] + _dot(s_['qk'], s_['v_new'])
            s_ref[b, h] = jnp.exp(s_['g_last']) * s_src[b, h] + _dot_tn(s_['kd'], s_['v_new'])
            y_ref[r, :] = _head_rms(o, ng_ref[...]) * _silu(z_ref[r, :])


def _gdn(proj, projg, cw, alog, dtb, ng, cv0, s_init, prev_s, *, li, n_layers, B, G, nT, TT, L, t_valid, gps):
    assert G % gps == 0 and (gps == 1 or nT == 1)
    R = TT * SEQ_BLOCK * gps
    n = proj.shape[0]
    upb = BR // LANE
    blk = lambda u: pl.BlockSpec((R, DH), lambda g, t, h, u=u: (g * nT + t, u + h))
    st4, s_out_spec, s_shape = _state_specs(li, n_layers, B, prev_s is not None, gps)
    cvs = lambda j: pl.BlockSpec((gps * CARRY_ROWS, DH), lambda g, t, h, j=j: (g, j * upb + h))
    cws = lambda j: pl.BlockSpec((CONV_W, DH), lambda g, t, h, j=j: (0, j * upb + h))
    one = pl.BlockSpec((1, LANE), lambda g, t, h: (0, 0))
    cvo = pl.BlockSpec((gps * CARRY_ROWS, BR), lambda g, t, h: (g, 0))
    in_specs = [blk(U_GD_QKV), blk(U_GD_QKV + 4), blk(U_GD_QKV + 8), blk(U_GD_Z),
                pl.BlockSpec((R, LANE), lambda g, t, h: (g * nT + t, U_GD_AB)),
                cws(0), cws(1), cws(2), one, one,
                pl.BlockSpec((1, DH), lambda g, t, h: (0, h)),
                cvs(0), cvs(1), cvs(2)]
    args = [proj] * 4 + [projg, cw, cw, cw, alog, dtb, ng, cv0, cv0, cv0]
    if s_init is not None:
        in_specs.append(st4)
        args.append(s_init)
    aliases = {}
    if prev_s is not None:
        aliases = {len(args): 1}
        in_specs.append(pl.BlockSpec(memory_space=pl.ANY))
        args.append(prev_s)
    kern = functools.partial(_gdn_kernel, L=L, TT=TT, gps=gps, t_valid=t_valid, nT=nT, li=li,
                             has_init=s_init is not None, has_prev=prev_s is not None)
    cv_shape = jax.ShapeDtypeStruct((cv0.shape[0], BR), F32)
    return pl.pallas_call(
        kern,
        grid=(G // gps, nT, HEADS),
        in_specs=in_specs,
        out_specs=[pl.BlockSpec((R, DH), lambda g, t, h: (g * nT + t, h)), s_out_spec, cvo, cvo, cvo],
        out_shape=[jax.ShapeDtypeStruct((n, BR), F32), s_shape, cv_shape, cv_shape, cv_shape],
        scratch_shapes=([pltpu.VMEM((R + gps * CARRY_ROWS, DH), F32)] * 3 + [pltpu.VMEM((R, DH), F32)] * 3
                        + [pltpu.VMEM((3, HEADS, gps * CARRY_ROWS, DH), F32), pltpu.VMEM((R, LANE), F32)]),
        input_output_aliases=aliases,
        compiler_params=_cparams(("arbitrary", "arbitrary", "arbitrary")),
        name="gdn",
    )(*args)


def _s5_prep_kernel(are_ref, aim_ref, ldt_ref, bre_ref, bim_ref, abr_ref, abi_ref, bbr_ref, bbi_ref):
    a_re = are_ref[...]
    a_im = aim_ref[...]
    dt = jnp.exp(ldt_ref[...])
    mag = jnp.exp(dt * a_re)
    ang = dt * a_im
    ab_r = mag * jnp.cos(ang)
    ab_i = mag * jnp.sin(ang)
    den = a_re * a_re + a_im * a_im
    nr = ab_r - 1.0
    ni = ab_i
    f_r = (nr * a_re + ni * a_im) / den
    f_i = (ni * a_re - nr * a_im) / den
    abr_ref[...] = ab_r
    abi_ref[...] = ab_i
    f_r = f_r[0:1, :]
    f_i = f_i[0:1, :]
    bbr_ref[...] = f_r * bre_ref[...] - f_i * bim_ref[...]
    bbi_ref[...] = f_r * bim_ref[...] + f_i * bre_ref[...]


def _s5_prep(a_re, a_im, log_dt, b_re, b_im):
    rep = lambda a: jnp.broadcast_to(a.reshape(1, S5_STATE), (SEQ_BLOCK, S5_STATE))
    ldt = rep(jnp.broadcast_to(log_dt[:, None], (S5_G, S5_P)))
    bt = lambda b: jnp.transpose(b, (2, 0, 1)).reshape(S5_N, S5_STATE)
    shp = lambda r: jax.ShapeDtypeStruct((r, S5_STATE), F32)
    abr, abi, bbr, bbi = pl.pallas_call(
        _s5_prep_kernel,
        out_shape=[shp(SEQ_BLOCK), shp(SEQ_BLOCK), shp(S5_N), shp(S5_N)],
        name="s5_prep",
    )(rep(a_re), rep(a_im), ldt, bt(b_re), bt(b_im))
    return abr, abi, bbr, bbi


def _gelu_tanh(x):
    return 0.5 * x * (1.0 + jnp.tanh(math.sqrt(2.0 / math.pi) * (x + 0.044715 * (x * x * x))))


def _s5_kernel(u_ref, z_ref, abr_ref, abi_ref, wbr_ref, wbi_ref, wcr_ref, wci_ref, d_ref, gw_ref, gb_ref,
               x0r_ref, x0i_ref, y_ref, xr_ref, xi_ref, hr_ref, hi_ref, ys_ref, *, TT, n_steps):
    ti = pl.program_id(1)
    NB = BR // LANE
    SB = S5_STATE // NB

    @pl.when(ti == 0)
    def _():
        xr_ref[...] = x0r_ref[...]
        xi_ref[...] = x0i_ref[...]

    for kb in range(NB):
        ub = u_ref[:, kb * LANE:(kb + 1) * LANE].astype(BF16)
        hr_ref[:, kb * SB:(kb + 1) * SB] = jnp.dot(ub, wbr_ref[kb], preferred_element_type=F32)
        hi_ref[:, kb * SB:(kb + 1) * SB] = jnp.dot(ub, wbi_ref[kb], preferred_element_type=F32)

    for kb in range(NB):
        sl = pl.ds(kb * SB, SB)
        ar = abr_ref[:, sl]
        ai = abi_ref[:, sl]

        def step(t, carry):
            xr, xi = carry
            rows = pl.ds(pl.multiple_of(t * SEQ_BLOCK, SEQ_BLOCK), SEQ_BLOCK)
            nxr = ar * xr - ai * xi + hr_ref[rows, sl]
            nxi = ar * xi + ai * xr + hi_ref[rows, sl]
            hr_ref[rows, sl] = nxr
            hi_ref[rows, sl] = nxi
            return nxr, nxi

        xr, xi = lax.fori_loop(0, n_steps, step, (xr_ref[:, sl], xi_ref[:, sl]))
        xr_ref[:, sl] = xr
        xi_ref[:, sl] = xi

    for kb in range(NB):
        sl = pl.ds(kb * SB, SB)
        cs = pl.ds(kb * LANE, LANE)
        yk = (jnp.dot(hr_ref[:, sl].astype(BF16), wcr_ref[kb], preferred_element_type=F32)
              - jnp.dot(hi_ref[:, sl].astype(BF16), wci_ref[kb], preferred_element_type=F32)
              + d_ref[:, cs] * u_ref[:, cs])
        ys_ref[:, cs] = _gelu_tanh(yk)
    ys = ys_ref[...]
    glu = ys * _sigmoid(jnp.dot(ys.astype(BF16), gw_ref[...], preferred_element_type=F32) + gb_ref[...])
    y_ref[...] = glu * _silu(z_ref[...])


def _s5(proj, abr, abi, wbr, wbi, wcr, wci, d, gw, gb, x0r, x0i, *, G, nT, TT, t_valid):
    assert nT == 1 or t_valid == nT * TT
    R = TT * SEQ_BLOCK
    n = proj.shape[0]
    full = lambda a: pl.BlockSpec(a.shape, lambda g, t, nd=a.ndim: (0,) * nd)
    st = pl.BlockSpec((SEQ_BLOCK, S5_STATE), lambda g, t: (g, 0))
    kern = functools.partial(_s5_kernel, TT=TT, n_steps=t_valid - (nT - 1) * TT)
    return pl.pallas_call(
        kern,
        grid=(G, nT),
        in_specs=[pl.BlockSpec((R, BR), lambda g, t: (g * nT + t, U_S5_U // 4)),
                  pl.BlockSpec((R, BR), lambda g, t: (g * nT + t, U_S5_Z // 4)),
                  full(abr), full(abi), full(wbr), full(wbi), full(wcr), full(wci), full(d), full(gw), full(gb),
                  st, st],
        out_specs=[pl.BlockSpec((R, BR), lambda g, t: (g * nT + t, 0)), st, st],
        out_shape=[jax.ShapeDtypeStruct((n, BR), F32),
                   jax.ShapeDtypeStruct(x0r.shape, F32),
                   jax.ShapeDtypeStruct(x0i.shape, F32)],
        scratch_shapes=[pltpu.VMEM((R, S5_STATE), F32), pltpu.VMEM((R, S5_STATE), F32),
                        pltpu.VMEM((R, BR), F32)],
        compiler_params=_cparams(("arbitrary", "arbitrary")),
        name="s5",
    )(proj, proj, abr, abi, wbr, wbi, wcr, wci, d, gw, gb, x0r, x0i)


def _lru_kernel(x_ref, z_ref, cw_ref, cb_ref, wa_ref, ba_ref, wx_ref, bx_ref, lam_ref, h0_ref, cv0_ref,
                y_ref, h_ref, cv_ref, xp_ref, xl_ref, a_ref, *, TT, n_steps, nT):
    ti = pl.program_id(1)
    R = TT * SEQ_BLOCK

    @pl.when(ti == 0)
    def _():
        h_ref[...] = h0_ref[...]

    _causal_conv_tile(x_ref, cw_ref, cb_ref, xp_ref, xl_ref, cv0_ref, cv_ref, ti=ti, R=R, width=BR,
                      tv_local=n_steps, act_fn=lambda a: a)

    xl = xl_ref[...]
    xb = xl.astype(BF16)
    r = _sigmoid(jnp.dot(xb, wa_ref[...], preferred_element_type=F32) + ba_ref[...])
    i = _sigmoid(jnp.dot(xb, wx_ref[...], preferred_element_type=F32) + bx_ref[...])
    log_a = -LRU_C * r * _softplus(-lam_ref[...])
    a = jnp.exp(log_a)
    a_ref[...] = a
    xl_ref[...] = jnp.sqrt(1.0 - a * a) * (i * xl)

    def step(t, h):
        rows = pl.ds(pl.multiple_of(t * SEQ_BLOCK, SEQ_BLOCK), SEQ_BLOCK)
        hn = a_ref[rows, :] * h + xl_ref[rows, :]
        xl_ref[rows, :] = hn
        return hn

    h_ref[...] = lax.fori_loop(0, n_steps, step, h_ref[...], unroll=4)
    y_ref[...] = xl_ref[...] * _silu(z_ref[...])


def _lru(proj, cw, cb, wa, ba, wx, bx, lam, h0, cv0, *, G, nT, TT, t_valid):
    assert nT == 1 or t_valid == nT * TT
    R = TT * SEQ_BLOCK
    n = proj.shape[0]
    full = lambda a: pl.BlockSpec(a.shape, lambda g, t, nd=a.ndim: (0,) * nd)
    st = pl.BlockSpec((SEQ_BLOCK, BR), lambda g, t: (g, 0))
    cvs = pl.BlockSpec((CARRY_ROWS, BR), lambda g, t: (g, 0))
    kern = functools.partial(_lru_kernel, TT=TT, n_steps=t_valid - (nT - 1) * TT, nT=nT)
    return pl.pallas_call(
        kern,
        grid=(G, nT),
        in_specs=[pl.BlockSpec((R, BR), lambda g, t: (g * nT + t, U_LRU_X // 4)),
                  pl.BlockSpec((R, BR), lambda g, t: (g * nT + t, U_LRU_Z // 4)),
                  full(cw), full(cb), full(wa), full(ba), full(wx), full(bx), full(lam), st, cvs],
        out_specs=[pl.BlockSpec((R, BR), lambda g, t: (g * nT + t, 0)), st, cvs],
        out_shape=[jax.ShapeDtypeStruct((n, BR), F32),
                   jax.ShapeDtypeStruct(h0.shape, F32),
                   jax.ShapeDtypeStruct(cv0.shape, F32)],
        scratch_shapes=[pltpu.VMEM((R + CARRY_ROWS, BR), F32), pltpu.VMEM((R, BR), F32),
                        pltpu.VMEM((R, BR), F32)],
        compiler_params=_cparams(("arbitrary", "arbitrary")),
        name="lru",
    )(proj, proj, cw, cb, wa, ba, wx, bx, lam, h0, cv0)


def _merge_kernel(yml_ref, ygd_ref, ys5_ref, ylru_ref, x_ref, p_ref, g_ref, wg_ref,
                  wbr_ref, wout_ref, wpg_ref, wple_ref, pg_ref, o_ref, *, x_btd, p_btd, out_btd):
    tm = yml_ref.shape[0]
    rows = lambda ref, btd, c: jnp.swapaxes(ref[...], 0, 1).reshape(tm, c) if btd else ref[...]
    p = rows(p_ref, p_btd, PLE_DIM)
    x = rows(x_ref, x_btd, D_MODEL)
    h = _rms(x, g_ref[...]).astype(BF16)
    merged = None
    for nb, y_ref in enumerate((yml_ref, ygd_ref, ys5_ref, ylru_ref)):
        pbr = jnp.dot(y_ref[...].astype(BF16), wbr_ref[nb], preferred_element_type=F32)
        gates = _dot_nt(h, wg_ref[nb * D_MODEL:(nb + 1) * D_MODEL, :])
        term = _sigmoid(gates) * pbr
        merged = term if merged is None else merged + term
    out = jnp.dot(merged.astype(BF16), wout_ref[...], preferred_element_type=F32)
    r = x + _rms(out, pg_ref[...])
    gate = _sigmoid(jnp.dot(r.astype(BF16), wpg_ref[...], preferred_element_type=F32))
    res = r + gate * jnp.dot(p.astype(BF16), wple_ref[...], preferred_element_type=F32)
    if out_btd:
        o_ref[...] = jnp.swapaxes(res.reshape(tm // SEQ_BLOCK, SEQ_BLOCK, D_MODEL), 0, 1)
    else:
        o_ref[...] = res


def _merge(ys, x, p, g, wg_all, li, wbr, wout, wpg, wple, pg, tm, *, p_layer=None, out_btd=False):
    x_btd = x.ndim == 3
    n = ys[0].shape[0]
    tt = tm // SEQ_BLOCK
    row = lambda c: pl.BlockSpec((tm, c), lambda i: (i, 0))
    x_spec = pl.BlockSpec((SEQ_BLOCK, tt, D_MODEL), lambda i: (0, i, 0)) if x_btd else row(D_MODEL)
    full = lambda a: pl.BlockSpec(a.shape, lambda i, nd=a.ndim: (0,) * nd, pipeline_mode=pl.Buffered(1))
    p_spec = row(PLE_DIM) if p_layer is None else pl.BlockSpec((None, SEQ_BLOCK, tt, PLE_DIM),
                                                               lambda i: (p_layer, 0, i, 0))
    if out_btd:
        out_spec = pl.BlockSpec((SEQ_BLOCK, tt, D_MODEL), lambda i: (0, i, 0))
        out_shape = jax.ShapeDtypeStruct((SEQ_BLOCK, n // SEQ_BLOCK, D_MODEL), F32)
    else:
        out_spec, out_shape = row(D_MODEL), jax.ShapeDtypeStruct((n, D_MODEL), F32)
    return pl.pallas_call(
        functools.partial(_merge_kernel, x_btd=x_btd, p_btd=p_layer is not None, out_btd=out_btd),
        grid=(n // tm,),
        in_specs=[row(BR), row(BR), row(BR), row(BR), x_spec, p_spec, full(g),
                  pl.BlockSpec((None, GATE_COLS, D_MODEL), lambda i: (li, 0, 0), pipeline_mode=pl.Buffered(1)),
                  full(wbr), full(wout), full(wpg), full(wple), full(pg)],
        out_specs=out_spec,
        out_shape=out_shape,
        compiler_params=_cparams(("parallel",)),
        name="merge",
    )(*ys, x, p, g, wg_all, wbr, wout, wpg, wple, pg)


def _w_relayout_kernel(tbl_ref, win_ref, o_ref):
    o_ref[...] = win_ref[0].astype(BF16)


def _w_relayout(wt, runs, n_chunks):
    depth, d_in, d_model = wt.shape
    tbl = [None] * n_chunks
    for c0, src, count in runs:
        for j in range(count):
            off = src + j * W_CHUNK
            assert off % SEQ_BLOCK == 0 and off + W_CHUNK <= d_in
            tbl[c0 + j] = (off // SEQ_BLOCK, 0)
    assert all(t is not None for t in tbl)
    grid_spec = pltpu.PrefetchScalarGridSpec(
        num_scalar_prefetch=1,
        grid=(depth, n_chunks),
        in_specs=[pl.BlockSpec((pl.Element(1), pl.Element(W_CHUNK), pl.Element(d_model)),
                               lambda l, c, t: (l, t[c, 0] * SEQ_BLOCK, 0))],
        out_specs=pl.BlockSpec((None, W_CHUNK, d_model), lambda l, c, t: (l, c, 0)),
    )
    return pl.pallas_call(
        _w_relayout_kernel,
        grid_spec=grid_spec,
        out_shape=jax.ShapeDtypeStruct((depth, n_chunks * W_CHUNK, d_model), BF16),
        compiler_params=_cparams(("arbitrary", "arbitrary")),
        name="w_relayout",
    )(jnp.asarray(tbl, jnp.int32), wt)


def _permute_w_in(w_in):
    sizes = (BR, BR, BR, HEADS, HEADS, BR, BR, 3 * BR, HEADS, HEADS, BR, BR, BR, BR, BR, GATE_COLS)
    offs = [0]
    for s in sizes:
        offs.append(offs[-1] + s)
    (ml_q, _, _, ml_i, _, ml_o, _, gd_qkv, gd_a, _, gd_z, _, _, _, _, gates) = offs[:-1]
    wt = jnp.swapaxes(w_in, 1, 2)
    upc = W_CHUNK // LANE
    w_main = _w_relayout(wt, [(U_GD_QKV // upc, gd_qkv, 3), (U_ML // upc, ml_q, 3), (U_ML // upc + 3, ml_o, 2),
                              (U_GD_Z // upc, gd_z, 5)], N_UNITS // upc)
    w_gates = _w_relayout(wt, [(0, gates, GATE_COLS // W_CHUNK)], GATE_COLS // W_CHUNK)
    unit = lambda off: jnp.pad(wt[:, off:off + 2 * HEADS, :], ((0, 0), (0, LANE - 2 * HEADS), (0, 0)))
    w_scalar = jnp.concatenate([unit(ml_i), unit(gd_a)], axis=1)
    return w_main, w_gates, w_scalar


def _pad_lanes(*vecs):
    v = jnp.concatenate(vecs)
    return jnp.pad(v, (0, LANE - v.shape[0])).reshape(1, LANE)


def _block_diag(blocks):
    n, a, b = blocks.shape
    eye = jnp.eye(n, dtype=blocks.dtype)
    return jnp.einsum('ij,iab->iajb', eye, blocks).reshape(n * a, n * b)


def _to_rows(x, G):
    B, T, C = x.shape
    return x.reshape(G, SEQ_BLOCK, T, C).transpose(0, 2, 1, 3).reshape(G * T * SEQ_BLOCK, C)


def _from_rows(y, G, T):
    C = y.shape[-1]
    return y.reshape(G, T, SEQ_BLOCK, C).transpose(0, 2, 1, 3).reshape(G * SEQ_BLOCK, T, C)


def _group(x, p, states, params, *, T, t_valid, TT, L, tm_proj, tm_merge, gps=1):
    B = x.shape[0]
    G = B // SEQ_BLOCK
    nT = T // TT
    depth = p.shape[0]
    xr = x if G == 1 else _to_rows(x, G)
    tm_proj = min(tm_proj, B * T)
    tm_merge = min(tm_merge, B * T)
    new_states = []
    c_all = s_all = None
    gps = math.gcd(gps, G)
    mat = dict(n_layers=depth, B=B, G=G, nT=nT, TT=TT, L=L, t_valid=t_valid, gps=gps)
    for li in range(depth):
        if li > 0:
            mat['gps'] = math.gcd(2 * gps, G) if nT == 1 else gps
        lp = {k: v[li] for k, v in params.items() if not k.startswith('w_in_')}
        if states is None:
            ml_init = s_init = None
            gcv0 = jnp.zeros((G * CARRY_ROWS, 3 * BR), F32)
            x0r = jnp.zeros((B, S5_STATE), F32)
            x0i = jnp.zeros((B, S5_STATE), F32)
            h0 = jnp.zeros((B, BR), F32)
            lcv0 = jnp.zeros((G * CARRY_ROWS, BR), F32)
        else:
            (n0, m0, gcv0, x0r, x0i, h0, lcv0) = [states[j][li] for j in (1, 2, 4, 5, 6, 7, 8)]
            n0 = n0.reshape(G, SEQ_BLOCK, HEADS, DH).transpose(0, 2, 1, 3).reshape(B * HEADS, DH)
            ml_init = (states[0], n0, jnp.pad(m0, ((0, 0), (0, LANE - HEADS))))
            s_init = states[3]
            gcv0 = _to_rows(gcv0, G)
            x0r = x0r.reshape(B, S5_STATE)
            x0i = x0i.reshape(B, S5_STATE)
            lcv0 = _to_rows(lcv0, G)
        in_place = G == 1

        proj, projg = _in_proj(xr, lp['prenorm_g'], params['w_in_main'], params['w_in_scalar'], li, tm_proj)

        y_ml, c_all, n1, m1 = _mlstm(proj, projg, lp['ml_ifb'], lp['ml_norm_g'], ml_init, c_all, li=li, **mat)
        y_gd, s_all, cvq, cvk, cvv = _gdn(proj, projg, lp['gd_conv_w'], lp['gd_alog'], lp['gd_dtb'], lp['gd_norm_g'],
                                          gcv0, s_init, s_all, li=li, **dict(mat, L=min(L, GDN_CHUNK)))
        gcv1 = jnp.concatenate([cvq, cvk, cvv], axis=-1)
        y_s5, x1r, x1i = _s5(proj, lp['s5_abr'], lp['s5_abi'], lp['s5_wbr'], lp['s5_wbi'], lp['s5_wcr'],
                             lp['s5_wci'], lp['s5_d'], lp['s5_glu_w'], lp['s5_glu_b'], x0r, x0i,
                             G=G, nT=nT, TT=TT, t_valid=t_valid)
        y_lru, h1, lcv1 = _lru(proj, lp['lru_conv_w'], lp['lru_conv_b'], lp['lru_wa'], lp['lru_ba'],
                               lp['lru_wx'], lp['lru_bx'], lp['lru_lam'], h0, lcv0,
                               G=G, nT=nT, TT=TT, t_valid=t_valid)
        xr = _merge((y_ml, y_gd, y_s5, y_lru), xr, p if in_place else _to_rows(p[li], G),
                    lp['prenorm_g'], params['w_in_gates'], li, lp['w_branch'], lp['w_out'], lp['w_ple_gate'], lp['w_ple'], lp['postnorm_g'], tm_merge,
                    p_layer=li if in_place else None, out_btd=in_place and li == depth - 1)

        n1 = n1.reshape(G, HEADS, SEQ_BLOCK, DH).transpose(0, 2, 1, 3)
        new_states.append((n1.reshape(B, HEADS, DH), m1[:, :HEADS],
                           _from_rows(gcv1, G, CONV_W - 1), x1r.reshape(B, S5_G, S5_P),
                           x1i.reshape(B, S5_G, S5_P), h1, _from_rows(lcv1, G, CONV_W - 1)))
    y = xr if G == 1 else _from_rows(xr, G, T)
    n_s, m_s, gcv_s, xr_s, xi_s, h_s, lcv_s = (jnp.stack([ns[j] for ns in new_states]) for j in range(7))
    return y, (c_all, n_s, m_s, s_all, gcv_s, xr_s, xi_s, h_s, lcv_s)


def _prepare_params(prenorm_g, postnorm_g, w_in, ml_bi, ml_bf, ml_norm_g, gd_conv_w, gd_a_log, gd_dt_bias,
                    gd_norm_g, s5_a_re, s5_a_im, s5_log_dt, s5_b_re, s5_b_im, s5_c_re, s5_c_im, s5_d,
                    s5_glu_w, s5_glu_b, lru_conv_w, lru_conv_b, lru_wa, lru_ba, lru_wx, lru_bx, lru_lam,
                    w_branch, w_out, w_ple, w_ple_gate):
    depth = w_in.shape[0]
    row = lambda a: a.reshape(depth, 1, -1)
    per_layer = lambda f, *a: jnp.stack([f(*[x[i] for x in a]) for i in range(depth)])
    prm = dict(
        prenorm_g=row(prenorm_g), postnorm_g=row(postnorm_g),
        ml_ifb=per_layer(_pad_lanes, ml_bi, ml_bf), ml_norm_g=row(ml_norm_g),
        gd_conv_w=gd_conv_w, gd_norm_g=row(gd_norm_g),
        gd_alog=per_layer(lambda a: _pad_lanes(a), gd_a_log),
        gd_dtb=per_layer(lambda a: _pad_lanes(a), gd_dt_bias),
        s5_d=row(s5_d), s5_glu_w=s5_glu_w.astype(BF16), s5_glu_b=row(s5_glu_b),
        lru_conv_w=lru_conv_w, lru_conv_b=row(lru_conv_b),
        lru_wa=per_layer(_block_diag, lru_wa).astype(BF16), lru_ba=row(lru_ba),
        lru_wx=per_layer(_block_diag, lru_wx).astype(BF16), lru_bx=row(lru_bx), lru_lam=row(lru_lam),
        w_branch=w_branch.astype(BF16), w_out=w_out.astype(BF16), w_ple=w_ple.astype(BF16),
        w_ple_gate=w_ple_gate.astype(BF16),
    )
    abr, abi, wbr, wbi, wcr, wci = [], [], [], [], [], []
    nb = BR // LANE
    gpb = S5_G // nb
    for i in range(depth):
        a_r, a_i, bbr, bbi = _s5_prep(s5_a_re[i], s5_a_im[i], s5_log_dt[i], s5_b_re[i], s5_b_im[i])
        abr.append(a_r)
        abi.append(a_i)
        in_blocks = lambda bb: jnp.stack([_block_diag(bb.reshape(S5_N, S5_G, S5_P).transpose(1, 0, 2)
                                                      [k * gpb:(k + 1) * gpb]) for k in range(nb)])
        out_blocks = lambda c: jnp.stack([_block_diag(jnp.transpose(c, (0, 2, 1))[k * gpb:(k + 1) * gpb])
                                          for k in range(nb)])
        wbr.append(in_blocks(bbr).astype(BF16))
        wbi.append(in_blocks(bbi).astype(BF16))
        wcr.append(out_blocks(s5_c_re[i]).astype(BF16))
        wci.append(out_blocks(s5_c_im[i]).astype(BF16))
    prm.update(s5_abr=jnp.stack(abr), s5_abi=jnp.stack(abi), s5_wbr=jnp.stack(wbr), s5_wbi=jnp.stack(wbi),
               s5_wcr=jnp.stack(wcr), s5_wci=jnp.stack(wci))
    prm['w_in_main'], prm['w_in_gates'], prm['w_in_scalar'] = _permute_w_in(w_in)
    return prm


def kernel(x_prompt, x_sample, state_mlstm_c, state_mlstm_n, state_mlstm_m, state_gdn_s, state_gdn_conv, state_s5_re, state_s5_im, state_lru_h, state_lru_conv, p_prompt, p_sample, prenorm_g, postnorm_g, w_in, ml_bi, ml_bf, ml_norm_g, gd_conv_w, gd_a_log, gd_dt_bias, gd_norm_g, s5_a_re, s5_a_im, s5_log_dt, s5_b_re, s5_b_im, s5_c_re, s5_c_im, s5_d, s5_glu_w, s5_glu_b, lru_conv_w, lru_conv_b, lru_wa, lru_ba, lru_wx, lru_bx, lru_lam, w_branch, w_out, w_ple, w_ple_gate):
    prm = _prepare_params(prenorm_g, postnorm_g, w_in, ml_bi, ml_bf, ml_norm_g, gd_conv_w, gd_a_log,
                          gd_dt_bias, gd_norm_g, s5_a_re, s5_a_im, s5_log_dt, s5_b_re, s5_b_im, s5_c_re,
                          s5_c_im, s5_d, s5_glu_w, s5_glu_b, lru_conv_w, lru_conv_b, lru_wa, lru_ba, lru_wx,
                          lru_bx, lru_lam, w_branch, w_out, w_ple, w_ple_gate)

    t_p = x_prompt.shape[1]
    tt_p = math.gcd(t_p, PROMPT_CHUNK)
    y_prompt, pr = _group(x_prompt, p_prompt, None, prm, T=t_p, t_valid=t_p, TT=tt_p, L=tt_p,
                          tm_proj=TM_PROJ, tm_merge=TM_MERGE)

    t_s = x_sample.shape[1]
    t_pad = -(-t_s // SEQ_BLOCK) * SEQ_BLOCK
    pad_t = lambda a, ax: jnp.pad(a, [(0, t_pad - t_s) if i == ax else (0, 0) for i in range(a.ndim)])
    sample_states = (state_mlstm_c, state_mlstm_n, state_mlstm_m, state_gdn_s, state_gdn_conv,
                     state_s5_re, state_s5_im, state_lru_h, state_lru_conv)
    y_s, sa = _group(pad_t(x_sample, 1), pad_t(p_sample, 2), sample_states, prm, T=t_pad, t_valid=t_s,
                     TT=t_pad, L=t_pad, tm_proj=TM_PROJ, tm_merge=TM_MERGE, gps=SAMPLE_GROUPS_PER_STEP)
    y_sample = y_s[:, :t_s]
    return (y_prompt, y_sample) + pr + sa
```

```python
import functools
import math

import jax
import jax.numpy as jnp
from jax import lax
from jax.experimental import pallas as pl
from jax.experimental.pallas import tpu as pltpu

F32 = jnp.float32
BF16 = jnp.bfloat16

D_MODEL = 1024
BR = 512
HEADS = 4
DH = 128
CONV_W = 4
S5_G = 32
S5_N = 16
S5_P = 64
S5_STATE = S5_G * S5_P
LRU_BLOCKS = 8
LRU_BD = 64
LRU_C = 8.0
PLE_DIM = 256
EPS = 1e-6
NEG = -1e30

LANE = 128
SEQ_BLOCK = 8
CARRY_ROWS = (CONV_W - 1) * SEQ_BLOCK
PROMPT_CHUNK = 128
GDN_CHUNK = 64
GDN_TILE_FACTOR = 2
SAMPLE_GROUPS_PER_STEP = 2
TM_PROJ = 512
TM_MERGE = 512

U_GD_QKV = 0
U_ML = 12
U_GD_Z = 32
U_S5_U = 36
U_S5_Z = 40
U_LRU_X = 44
U_LRU_Z = 48
N_UNITS = 52
PROJ_COLS = N_UNITS * LANE
U_ML_IF = 0
U_GD_AB = 1
GATE_COLS = 4 * D_MODEL
W_CHUNK = 4 * LANE

VMEM_LIMIT = 56 * 1024 * 1024


def _cparams(sem):
    return pltpu.CompilerParams(dimension_semantics=sem, vmem_limit_bytes=VMEM_LIMIT)


def _sigmoid(x):
    return 0.5 * jnp.tanh(0.5 * x) + 0.5


def _silu(x):
    return x * _sigmoid(x)


def _softplus(x):
    return jnp.maximum(x, 0.0) + jnp.log1p(jnp.exp(-jnp.abs(x)))


def _dot(a, b):
    return jnp.dot(a.astype(BF16), b.astype(BF16), preferred_element_type=F32)


def _dot_nt(a, b):
    return lax.dot_general(a.astype(BF16), b.astype(BF16), (((1,), (1,)), ((), ())),
                           preferred_element_type=F32)


def _dot_tn(a, b):
    return lax.dot_general(a.astype(BF16), b.astype(BF16), (((0,), (0,)), ((), ())),
                           preferred_element_type=F32)


def _cumsum_rows(x):
    n = x.shape[0]
    ri = lax.broadcasted_iota(jnp.int32, x.shape, 0)
    s = 1
    while s < n:
        x = x + jnp.where(ri >= s, pltpu.roll(x, s, axis=0), 0.0)
        s *= 2
    return x


def _row_from_col(col, eye):
    return jnp.sum(jnp.where(eye, col, 0.0), axis=0, keepdims=True)


def _head_rms(h, g):
    return h * lax.rsqrt(jnp.mean(h * h, axis=-1, keepdims=True) + EPS) * g


def _rms(x, g):
    return x * lax.rsqrt(jnp.mean(x * x, axis=-1, keepdims=True) + EPS) * g


def _in_proj_kernel(x_ref, g_ref, w_ref, wg_ref, o_ref, og_ref, *, x_btd):
    x = x_ref[...]
    if x_btd:
        x = jnp.swapaxes(x, 0, 1).reshape(o_ref.shape[0], D_MODEL)
    h = _rms(x, g_ref[...]).astype(BF16)
    o_ref[...] = _dot_nt(h, w_ref[...])
    og_ref[...] = _dot_nt(h, wg_ref[...])


def _in_proj(x, g, w_all, wg_all, li, tm):
    x_btd = x.ndim == 3
    n = x.shape[0] * x.shape[1] if x_btd else x.shape[0]
    x_spec = (pl.BlockSpec((SEQ_BLOCK, tm // SEQ_BLOCK, D_MODEL), lambda i: (0, i, 0)) if x_btd
              else pl.BlockSpec((tm, D_MODEL), lambda i: (i, 0)))
    resident = lambda c: pl.BlockSpec((None, c, D_MODEL), lambda i: (li, 0, 0), pipeline_mode=pl.Buffered(1))
    return pl.pallas_call(
        functools.partial(_in_proj_kernel, x_btd=x_btd),
        grid=(n // tm,),
        in_specs=[x_spec,
                  pl.BlockSpec((1, D_MODEL), lambda i: (0, 0)),
                  resident(PROJ_COLS), resident(2 * LANE)],
        out_specs=[pl.BlockSpec((tm, PROJ_COLS), lambda i: (i, 0)),
                   pl.BlockSpec((tm, 2 * LANE), lambda i: (i, 0))],
        out_shape=[jax.ShapeDtypeStruct((n, PROJ_COLS), F32), jax.ShapeDtypeStruct((n, 2 * LANE), F32)],
        compiler_params=_cparams(("parallel",)),
        name="in_proj",
    )(x, g, w_all, wg_all)


def _token_cumsum(x, tok_in_chunk, L):
    s = 1
    while s < L:
        x = x + jnp.where(tok_in_chunk >= s, pltpu.roll(x, s * SEQ_BLOCK, axis=0), 0.0)
        s *= 2
    return x


def _tile_token_ids(ti, TT, L):
    R = TT * SEQ_BLOCK
    t_local = lax.broadcasted_iota(jnp.int32, (R, LANE), 0) // SEQ_BLOCK
    return ti * TT + t_local, t_local % L


def _token_scan(x, op, fill):
    s = 1
    while s < x.shape[0]:
        shifted = jnp.concatenate([jnp.full((s,) + x.shape[1:], fill, x.dtype), x[:-s]], axis=0)
        x = op(x, shifted)
        s *= 2
    return x


ML_R, ML_M, ML_INTER, ML_ENEG, ML_WS = (j * HEADS for j in range(5))


def _mlstm_gate_tile(if_ref, ifb_ref, m_ref, sc_ref, gt_ref, *, ti, T, t_valid):
    shape = (T, SEQ_BLOCK, LANE)
    lane = lax.broadcasted_iota(jnp.int32, shape, 2)
    valid = (ti * T + lax.broadcasted_iota(jnp.int32, shape, 0)) < t_valid
    rot = lambda x, k: pltpu.roll(x.reshape(T * SEQ_BLOCK, LANE), k, axis=1).reshape(shape)
    ifv = (if_ref[...] + ifb_ref[...]).reshape(shape)
    logf = jnp.minimum(ifv, 0.0) - jnp.log1p(jnp.exp(-jnp.abs(ifv)))
    bcum = rot(_token_scan(jnp.where(valid, logf, 0.0), jnp.add, 0.0), LANE - HEADS)
    ig = jnp.where(valid, ifv, NEG)
    r = ig - bcum
    m_prev = m_ref[...]
    big_m = jnp.maximum(m_prev[None], _token_scan(r, jnp.maximum, NEG))
    inter = jnp.exp(m_prev[None] - big_m)
    eneg = jnp.exp(-(bcum + big_m))
    b_last = bcum[T - 1]
    g = b_last[None] - bcum + ig
    m_new = jnp.maximum(b_last + m_prev, jnp.max(g, axis=0))
    ws = jnp.exp(g - m_new[None])
    head_lanes = lane[0] < HEADS
    sc_ref[...] = jnp.where(head_lanes, jnp.exp(b_last + m_prev - m_new), 0.0)
    m_ref[...] = jnp.where(head_lanes, m_new, 0.0)
    packed = jnp.where(lane < ML_M, r,
                       jnp.where(lane < ML_INTER, rot(big_m, ML_M),
                                 jnp.where(lane < ML_ENEG, rot(inter, ML_INTER),
                                           jnp.where(lane < ML_WS, rot(eneg, ML_ENEG), rot(ws, ML_WS)))))
    gt_ref[...] = packed.reshape(T * SEQ_BLOCK, LANE)


def _own_layer(state_ref, li, has_prev, first_step):
    if has_prev:
        return state_ref

    @pl.when(first_step)
    def _():
        for other in range(state_ref.shape[0]):
            if other != li:
                state_ref[other] = jnp.zeros(state_ref.shape[1:], F32)

    return state_ref.at[li]


def _mlstm_kernel(*refs, L, gps, t_valid, li, has_init, has_prev, single_tile):
    q_ref, k_ref, v_ref, o_ref, z_ref, if_ref, ifb_ref, ng_ref = refs[:8]
    n_in = 8 + (3 if has_init else 0) + (1 if has_prev else 0)
    y_ref, c_ref, n_ref, m_ref, gt_ref, sc_ref, rt_ref = refs[n_in:]
    ti = pl.program_id(1)
    h = pl.program_id(2)
    R = L * SEQ_BLOCK
    seqs = range(gps * SEQ_BLOCK)
    rows = [pl.ds((j // SEQ_BLOCK) * R + j % SEQ_BLOCK, L, stride=SEQ_BLOCK) for j in seqs]
    n_rows = [pl.ds((j // SEQ_BLOCK) * SEQ_BLOCK * HEADS + h * SEQ_BLOCK + j % SEQ_BLOCK, 1) for j in seqs]
    use_transpose = L == LANE
    c_ref = _own_layer(c_ref, li, has_prev, (ti == 0) & (h == 0))
    c_src = refs[8] if (has_init and single_tile) else c_ref

    @pl.when((ti == 0) & (h == 0))
    def _():
        if has_init:
            if not single_tile:
                c_ref[...] = refs[8][...]
            n_ref[...] = refs[9][...]
            m_ref[...] = refs[10][...]
        else:
            c_ref[...] = jnp.zeros(c_ref.shape, F32)
            n_ref[...] = jnp.zeros(n_ref.shape, F32)
            m_ref[...] = jnp.zeros(m_ref.shape, F32)

    @pl.when(h == 0)
    def _():
        for gi in range(gps):
            grp = pl.ds(gi * SEQ_BLOCK, SEQ_BLOCK)
            tile = pl.ds(gi * R, R)
            _mlstm_gate_tile(if_ref.at[tile], ifb_ref, m_ref.at[grp], sc_ref.at[grp], gt_ref.at[tile],
                             ti=ti, T=L, t_valid=t_valid)
        if use_transpose:
            for b in seqs:
                rt_ref[b * SEQ_BLOCK:(b + 1) * SEQ_BLOCK, :] = gt_ref[rows[b], :].T[:SEQ_BLOCK, :]

    ri = lax.broadcasted_iota(jnp.int32, (L, L), 0)
    ci = lax.broadcasted_iota(jnp.int32, (L, L), 1)
    causal = ci <= ri
    eye = ci == ri
    to_lane0 = jnp.where(h == 0, 0, LANE - h)
    sc_all = pltpu.roll(sc_ref[...], to_lane0, axis=1)
    col = lambda tile, off: tile[:, off:off + 1]

    st = []
    for b in seqs:
        gq = pltpu.roll(gt_ref[rows[b], :], to_lane0, axis=1)
        q = q_ref[rows[b], :] * (DH ** -0.5)
        v = v_ref[rows[b], :]
        kw = k_ref[rows[b], :] * col(gq, ML_WS)
        cmat = c_src[b, h]
        sc = sc_all[b:b + 1, 0:1]
        st.append(dict(gq=gq, q=q, v=v, kw=kw, sc=sc, qk=_dot_nt(q, k_ref[rows[b], :]), qc=_dot(q, cmat)))
        c_ref[b, h] = sc * cmat + _dot_tn(kw, v)
    for b in seqs:
        s_ = st[b]
        if use_transpose:
            r_row = rt_ref[pl.ds(b * SEQ_BLOCK + h, 1), :]
        else:
            r_row = _row_from_col(col(s_['gq'], ML_R), eye)
        s_['s'] = s_['qk'] * jnp.where(causal, jnp.exp(r_row - col(s_['gq'], ML_M)), 0.0)
    for b in seqs:
        s_ = st[b]
        srow = n_rows[b]
        nvec = n_ref[srow, :]
        inter = col(s_['gq'], ML_INTER)
        s_['num'] = _dot(s_['s'], s_['v']) + inter * s_['qc']
        den = (jnp.sum(s_['s'], axis=1, keepdims=True)
               + inter * jnp.sum(s_['q'] * nvec, axis=1, keepdims=True))
        s_['den'] = jnp.maximum(jnp.abs(den), col(s_['gq'], ML_ENEG))
        n_ref[srow, :] = s_['sc'] * nvec + jnp.sum(s_['kw'], axis=0, keepdims=True)
    for b in seqs:
        s_ = st[b]
        yn = _head_rms(s_['num'] / s_['den'], ng_ref[...])
        y_ref[rows[b], :] = yn * _sigmoid(o_ref[rows[b], :]) * _silu(z_ref[rows[b], :])


def _state_specs(li, n_layers, B, has_prev, gps):
    nseq = gps * SEQ_BLOCK
    one = pl.BlockSpec((None, nseq, HEADS, DH, DH), lambda g, t, h: (li, g, 0, 0, 0))
    every = pl.BlockSpec((n_layers, nseq, HEADS, DH, DH), lambda g, t, h: (0, g, 0, 0, 0))
    shape = jax.ShapeDtypeStruct((n_layers, B, HEADS, DH, DH), F32)
    return one, (one if has_prev else every), shape


def _mlstm(proj, projg, ifb, ng, init, prev_c, *, li, n_layers, B, G, nT, TT, L, t_valid, gps):
    assert TT == L and G % gps == 0 and (gps == 1 or nT == 1)
    R = TT * SEQ_BLOCK * gps
    n = proj.shape[0]
    blk = lambda u: pl.BlockSpec((R, DH), lambda g, t, h, u=u: (g * nT + t, u + h))
    st4, c_out_spec, c_shape = _state_specs(li, n_layers, B, prev_c is not None, gps)
    n_spec = pl.BlockSpec((gps * SEQ_BLOCK * HEADS, LANE), lambda g, t, h: (g, 0))
    m_spec = pl.BlockSpec((gps * SEQ_BLOCK, LANE), lambda g, t, h: (g, 0))
    in_specs = [blk(U_ML), blk(U_ML + 4), blk(U_ML + 8), blk(U_ML + 12), blk(U_ML + 16),
                pl.BlockSpec((R, LANE), lambda g, t, h: (g * nT + t, U_ML_IF)),
                pl.BlockSpec((1, LANE), lambda g, t, h: (0, 0)),
                pl.BlockSpec((1, DH), lambda g, t, h: (0, h))]
    args = [proj] * 5 + [projg, ifb, ng]
    if init is not None:
        in_specs += [st4, n_spec, m_spec]
        args += list(init)
    aliases = {}
    if prev_c is not None:
        aliases = {len(args): 1}
        in_specs.append(pl.BlockSpec(memory_space=pl.ANY))
        args.append(prev_c)
    kern = functools.partial(_mlstm_kernel, L=L, gps=gps, t_valid=t_valid, li=li, has_init=init is not None,
                             has_prev=prev_c is not None, single_tile=nT == 1)
    return pl.pallas_call(
        kern,
        grid=(G // gps, nT, HEADS),
        in_specs=in_specs,
        out_specs=[pl.BlockSpec((R, DH), lambda g, t, h: (g * nT + t, h)), c_out_spec, n_spec, m_spec],
        out_shape=[jax.ShapeDtypeStruct((n, BR), F32), c_shape,
                   jax.ShapeDtypeStruct((B * HEADS, LANE), F32), jax.ShapeDtypeStruct((B, LANE), F32)],
        scratch_shapes=[pltpu.VMEM((R, LANE), F32), pltpu.VMEM((gps * SEQ_BLOCK, LANE), F32),
                        pltpu.VMEM((gps * SEQ_BLOCK * SEQ_BLOCK, LANE), F32)],
        input_output_aliases=aliases,
        compiler_params=_cparams(("arbitrary", "arbitrary", "arbitrary")),
        name="mlstm",
    )(*args)


def _causal_conv_tile(x_ref, w_ref, b_ref, xp_ref, act_ref, cv0_ref, cv_ref, *, ti, R, width, tv_local,
                      act_fn, carry_ref=None):
    @pl.when(ti == 0)
    def _():
        xp_ref[0:CARRY_ROWS, :] = cv0_ref[...]

    if carry_ref is not None:
        @pl.when(ti > 0)
        def _():
            xp_ref[0:CARRY_ROWS, :] = carry_ref[...]

    xp_ref[CARRY_ROWS:CARRY_ROWS + R, :] = x_ref[...]
    RB = next(rb for rb in (128, 32, SEQ_BLOCK) if R % rb == 0 and rb * min(width, BR) <= 128 * LANE)

    def blk(i, carry):
        r0 = pl.multiple_of(i * RB, SEQ_BLOCK)
        cw = min(width, BR)
        for c0 in range(0, width, cw):
            cs = pl.ds(c0, cw)
            acc = xp_ref[pl.ds(r0, RB), cs] * w_ref[0:1, cs]
            for j in range(1, CONV_W):
                acc = acc + xp_ref[pl.ds(r0 + j * SEQ_BLOCK, RB), cs] * w_ref[j:j + 1, cs]
            if b_ref is not None:
                acc = acc + b_ref[:, cs]
            act_ref[pl.ds(r0, RB), cs] = act_fn(acc)
        return carry

    trips = R // RB
    lax.fori_loop(0, trips, blk, 0, unroll=trips if trips <= 8 else 2)

    cv_ref[...] = xp_ref[tv_local * SEQ_BLOCK:tv_local * SEQ_BLOCK + CARRY_ROWS, :]
    if carry_ref is not None:
        carry_ref[...] = xp_ref[R:R + CARRY_ROWS, :]
    else:
        xp_ref[0:CARRY_ROWS, :] = xp_ref[R:R + CARRY_ROWS, :]


def _gdn_kernel(*refs, L, TT, gps, t_valid, nT, li, has_init, has_prev):
    (q_ref, k_ref, v_ref, z_ref, ab_ref, cwq_ref, cwk_ref, cwv_ref, alog_ref, dtb_ref, ng_ref,
     cvq0_ref, cvk0_ref, cvv0_ref) = refs[:14]
    n_in = 14 + (1 if has_init else 0) + (1 if has_prev else 0)
    (y_ref, s_ref, cvq_ref, cvk_ref, cvv_ref,
     xpq_ref, xpk_ref, xpv_ref, aq_ref, ak_ref, av_ref, carry_ref, gb_ref) = refs[n_in:]
    ti = pl.program_id(1)
    h = pl.program_id(2)
    R = TT * SEQ_BLOCK
    s_ref = _own_layer(s_ref, li, has_prev, (ti == 0) & (h == 0))
    single_chunk = nT == 1 and TT == L
    s_src = refs[14] if (has_init and single_chunk) else s_ref

    @pl.when((ti == 0) & (h == 0))
    def _():
        if has_init and single_chunk:
            pass
        elif has_init:
            s_ref[...] = refs[14][...]
        else:
            s_ref[...] = jnp.zeros(s_ref.shape, F32)

    head_lanes = pl.ds(pl.multiple_of(h * DH, DH), DH)
    for j, (x_ref, cw_ref, xp_ref, a_ref, cv0_ref, cv_ref) in enumerate((
            (q_ref, cwq_ref, xpq_ref, aq_ref, cvq0_ref, cvq_ref),
            (k_ref, cwk_ref, xpk_ref, ak_ref, cvk0_ref, cvk_ref),
            (v_ref, cwv_ref, xpv_ref, av_ref, cvv0_ref, cvv_ref))):
        for gi in range(gps):
            tile = pl.ds(gi * R, R)
            cvr = pl.ds(gi * CARRY_ROWS, CARRY_ROWS)
            _causal_conv_tile(x_ref.at[tile], cw_ref, None, xp_ref.at[pl.ds(gi * (R + CARRY_ROWS), R + CARRY_ROWS)],
                              a_ref.at[tile], cv0_ref.at[cvr], cv_ref.at[cvr, head_lanes], ti=ti, R=R, width=DH,
                              tv_local=t_valid - (nT - 1) * TT, act_fn=_silu, carry_ref=carry_ref.at[j, h, cvr])

    @pl.when(h == 0)
    def _():
        tok, tok_in_chunk = _tile_token_ids(ti, TT, L)
        valid = tok < t_valid
        lane = lax.broadcasted_iota(jnp.int32, (R, LANE), 1)
        for gi in range(gps):
            tile = pl.ds(gi * R, R)
            abv = ab_ref[tile, :]
            g_all = -jnp.exp(alog_ref[...]) * _softplus(abv + dtb_ref[...])
            gam = _token_cumsum(jnp.where(valid, g_all, 0.0), tok_in_chunk, L)
            gb_ref[tile, :] = jnp.where(lane < HEADS, gam, jnp.where(valid, _sigmoid(abv), 0.0))

    ri = lax.broadcasted_iota(jnp.int32, (L, L), 0)
    ci = lax.broadcasted_iota(jnp.int32, (L, L), 1)
    causal = ci <= ri
    strict = ci < ri
    eye = ci == ri
    lane = lax.broadcasted_iota(jnp.int32, (L, LANE), 1)
    sel_a = lane == h
    sel_b = lane == HEADS + h
    pick = lambda tile, sel: jnp.sum(jnp.where(sel, tile, 0.0), axis=1, keepdims=True)
    n_double = int(math.log2(L)) - 1

    seqs = range(gps * SEQ_BLOCK)
    n_seq = len(seqs)
    rows, st = [], []
    for c in range(TT // L):
        for b in seqs:
            r = pl.ds((b // SEQ_BLOCK) * R + c * L * SEQ_BLOCK + b % SEQ_BLOCK, L, stride=SEQ_BLOCK)
            rows.append(r)
            gbv = gb_ref[r, :]
            gam = pick(gbv, sel_a)
            beta = pick(gbv, sel_b)
            dec = jnp.where(causal, jnp.exp(gam - _row_from_col(gam, eye)), 0.0)
            q = aq_ref[r, :]
            k = ak_ref[r, :]
            q = q * lax.rsqrt(jnp.sum(q * q, axis=-1, keepdims=True) + EPS) * (DH ** -0.5)
            k = k * lax.rsqrt(jnp.sum(k * k, axis=-1, keepdims=True) + EPS)
            eg = jnp.exp(gam)
            g_last = gam[L - 1:L, :]
            kbeta = k * beta
            st.append(dict(dec=dec, q=q, k=k, kbeta=kbeta, eg=eg, g_last=g_last,
                           kd=k * jnp.exp(g_last - gam),
                           rhs=jnp.concatenate([av_ref[r, :] * beta, kbeta * eg], axis=1)))
    for s_ in st:
        amat = jnp.where(strict, _dot_nt(s_['kbeta'], s_['k']) * s_['dec'], 0.0)
        s_['qk'] = _dot_nt(s_['q'], s_['k']) * s_['dec']
        s_['x'] = -amat
        s_['p'] = amat
    for s_ in st:
        s_['p'] = _dot(s_['p'], s_['p'])
    for i in range(n_double):
        for s_ in st:
            s_['xp'] = _dot(s_['x'], s_['p'])
            if i + 1 < n_double:
                s_['p2'] = _dot(s_['p'], s_['p'])
        for s_ in st:
            s_['x'] = s_['x'] + s_['p'] + s_['xp']
            if i + 1 < n_double:
                s_['p'] = s_['p2']
    for s_ in st:
        s_['sol'] = s_['rhs'] + _dot(s_['x'], s_['rhs'])
    for c in range(TT // L):
        chunk = [(b, st[c * n_seq + b], rows[c * n_seq + b]) for b in seqs]
        for b, s_, _ in chunk:
            s_['o'] = _dot(s_['q'] * s_['eg'], s_src[b, h])
            s_['v_new'] = s_['sol'][:, :DH] - _dot(s_['sol'][:, DH:], s_src[b, h])
        for b, s_, r in chunk:
            o = s_['o'] + _dot(s_['qk'], s_['v_new'])
            s_ref[b, h] = jnp.exp(s_['g_last']) * s_src[b, h] + _dot_tn(s_['kd'], s_['v_new'])
            y_ref[r, :] = _head_rms(o, ng_ref[...]) * _silu(z_ref[r, :])


def _gdn(proj, projg, cw, alog, dtb, ng, cv0, s_init, prev_s, *, li, n_layers, B, G, nT, TT, L, t_valid, gps):
    assert G % gps == 0 and (gps == 1 or nT == 1)
    R = TT * SEQ_BLOCK * gps
    n = proj.shape[0]
    upb = BR // LANE
    blk = lambda u: pl.BlockSpec((R, DH), lambda g, t, h, u=u: (g * nT + t, u + h))
    st4, s_out_spec, s_shape = _state_specs(li, n_layers, B, prev_s is not None, gps)
    cvs = lambda j: pl.BlockSpec((gps * CARRY_ROWS, DH), lambda g, t, h, j=j: (g, j * upb + h))
    cws = lambda j: pl.BlockSpec((CONV_W, DH), lambda g, t, h, j=j: (0, j * upb + h))
    one = pl.BlockSpec((1, LANE), lambda g, t, h: (0, 0))
    cvo = pl.BlockSpec((gps * CARRY_ROWS, BR), lambda g, t, h: (g, 0))
    in_specs = [blk(U_GD_QKV), blk(U_GD_QKV + 4), blk(U_GD_QKV + 8), blk(U_GD_Z),
                pl.BlockSpec((R, LANE), lambda g, t, h: (g * nT + t, U_GD_AB)),
                cws(0), cws(1), cws(2), one, one,
                pl.BlockSpec((1, DH), lambda g, t, h: (0, h)),
                cvs(0), cvs(1), cvs(2)]
    args = [proj] * 4 + [projg, cw, cw, cw, alog, dtb, ng, cv0, cv0, cv0]
    if s_init is not None:
        in_specs.append(st4)
        args.append(s_init)
    aliases = {}
    if prev_s is not None:
        aliases = {len(args): 1}
        in_specs.append(pl.BlockSpec(memory_space=pl.ANY))
        args.append(prev_s)
    kern = functools.partial(_gdn_kernel, L=L, TT=TT, gps=gps, t_valid=t_valid, nT=nT, li=li,
                             has_init=s_init is not None, has_prev=prev_s is not None)
    cv_shape = jax.ShapeDtypeStruct((cv0.shape[0], BR), F32)
    return pl.pallas_call(
        kern,
        grid=(G // gps, nT, HEADS),
        in_specs=in_specs,
        out_specs=[pl.BlockSpec((R, DH), lambda g, t, h: (g * nT + t, h)), s_out_spec, cvo, cvo, cvo],
        out_shape=[jax.ShapeDtypeStruct((n, BR), F32), s_shape, cv_shape, cv_shape, cv_shape],
        scratch_shapes=([pltpu.VMEM((R + gps * CARRY_ROWS, DH), F32)] * 3 + [pltpu.VMEM((R, DH), F32)] * 3
                        + [pltpu.VMEM((3, HEADS, gps * CARRY_ROWS, DH), F32), pltpu.VMEM((R, LANE), F32)]),
        input_output_aliases=aliases,
        compiler_params=_cparams(("arbitrary", "arbitrary", "arbitrary")),
        name="gdn",
    )(*args)


def _s5_prep_kernel(are_ref, aim_ref, ldt_ref, bre_ref, bim_ref, abr_ref, abi_ref, bbr_ref, bbi_ref):
    a_re = are_ref[...]
    a_im = aim_ref[...]
    dt = jnp.exp(ldt_ref[...])
    mag = jnp.exp(dt * a_re)
    ang = dt * a_im
    ab_r = mag * jnp.cos(ang)
    ab_i = mag * jnp.sin(ang)
    den = a_re * a_re + a_im * a_im
    nr = ab_r - 1.0
    ni = ab_i
    f_r = (nr * a_re + ni * a_im) / den
    f_i = (ni * a_re - nr * a_im) / den
    abr_ref[...] = ab_r
    abi_ref[...] = ab_i
    f_r = f_r[0:1, :]
    f_i = f_i[0:1, :]
    bbr_ref[...] = f_r * bre_ref[...] - f_i * bim_ref[...]
    bbi_ref[...] = f_r * bim_ref[...] + f_i * bre_ref[...]


def _s5_prep(a_re, a_im, log_dt, b_re, b_im):
    rep = lambda a: jnp.broadcast_to(a.reshape(1, S5_STATE), (SEQ_BLOCK, S5_STATE))
    ldt = rep(jnp.broadcast_to(log_dt[:, None], (S5_G, S5_P)))
    bt = lambda b: jnp.transpose(b, (2, 0, 1)).reshape(S5_N, S5_STATE)
    shp = lambda r: jax.ShapeDtypeStruct((r, S5_STATE), F32)
    abr, abi, bbr, bbi = pl.pallas_call(
        _s5_prep_kernel,
        out_shape=[shp(SEQ_BLOCK), shp(SEQ_BLOCK), shp(S5_N), shp(S5_N)],
        name="s5_prep",
    )(rep(a_re), rep(a_im), ldt, bt(b_re), bt(b_im))
    return abr, abi, bbr, bbi


def _gelu_tanh(x):
    return 0.5 * x * (1.0 + jnp.tanh(math.sqrt(2.0 / math.pi) * (x + 0.044715 * (x * x * x))))


def _s5_kernel(u_ref, z_ref, abr_ref, abi_ref, wbr_ref, wbi_ref, wcr_ref, wci_ref, d_ref, gw_ref, gb_ref,
               x0r_ref, x0i_ref, y_ref, xr_ref, xi_ref, hr_ref, hi_ref, ys_ref, *, TT, n_steps):
    ti = pl.program_id(1)
    NB = BR // LANE
    SB = S5_STATE // NB

    @pl.when(ti == 0)
    def _():
        xr_ref[...] = x0r_ref[...]
        xi_ref[...] = x0i_ref[...]

    for kb in range(NB):
        ub = u_ref[:, kb * LANE:(kb + 1) * LANE].astype(BF16)
        hr_ref[:, kb * SB:(kb + 1) * SB] = jnp.dot(ub, wbr_ref[kb], preferred_element_type=F32)
        hi_ref[:, kb * SB:(kb + 1) * SB] = jnp.dot(ub, wbi_ref[kb], preferred_element_type=F32)

    for kb in range(NB):
        sl = pl.ds(kb * SB, SB)
        ar = abr_ref[:, sl]
        ai = abi_ref[:, sl]

        def step(t, carry):
            xr, xi = carry
            rows = pl.ds(pl.multiple_of(t * SEQ_BLOCK, SEQ_BLOCK), SEQ_BLOCK)
            nxr = ar * xr - ai * xi + hr_ref[rows, sl]
            nxi = ar * xi + ai * xr + hi_ref[rows, sl]
            hr_ref[rows, sl] = nxr
            hi_ref[rows, sl] = nxi
            return nxr, nxi

        xr, xi = lax.fori_loop(0, n_steps, step, (xr_ref[:, sl], xi_ref[:, sl]))
        xr_ref[:, sl] = xr
        xi_ref[:, sl] = xi

    for kb in range(NB):
        sl = pl.ds(kb * SB, SB)
        cs = pl.ds(kb * LANE, LANE)
        yk = (jnp.dot(hr_ref[:, sl].astype(BF16), wcr_ref[kb], preferred_element_type=F32)
              - jnp.dot(hi_ref[:, sl].astype(BF16), wci_ref[kb], preferred_element_type=F32)
              + d_ref[:, cs] * u_ref[:, cs])
        ys_ref[:, cs] = _gelu_tanh(yk)
    ys = ys_ref[...]
    glu = ys * _sigmoid(jnp.dot(ys.astype(BF16), gw_ref[...], preferred_element_type=F32) + gb_ref[...])
    y_ref[...] = glu * _silu(z_ref[...])


def _s5(proj, abr, abi, wbr, wbi, wcr, wci, d, gw, gb, x0r, x0i, *, G, nT, TT, t_valid):
    assert nT == 1 or t_valid == nT * TT
    R = TT * SEQ_BLOCK
    n = proj.shape[0]
    full = lambda a: pl.BlockSpec(a.shape, lambda g, t, nd=a.ndim: (0,) * nd)
    st = pl.BlockSpec((SEQ_BLOCK, S5_STATE), lambda g, t: (g, 0))
    kern = functools.partial(_s5_kernel, TT=TT, n_steps=t_valid - (nT - 1) * TT)
    return pl.pallas_call(
        kern,
        grid=(G, nT),
        in_specs=[pl.BlockSpec((R, BR), lambda g, t: (g * nT + t, U_S5_U // 4)),
                  pl.BlockSpec((R, BR), lambda g, t: (g * nT + t, U_S5_Z // 4)),
                  full(abr), full(abi), full(wbr), full(wbi), full(wcr), full(wci), full(d), full(gw), full(gb),
                  st, st],
        out_specs=[pl.BlockSpec((R, BR), lambda g, t: (g * nT + t, 0)), st, st],
        out_shape=[jax.ShapeDtypeStruct((n, BR), F32),
                   jax.ShapeDtypeStruct(x0r.shape, F32),
                   jax.ShapeDtypeStruct(x0i.shape, F32)],
        scratch_shapes=[pltpu.VMEM((R, S5_STATE), F32), pltpu.VMEM((R, S5_STATE), F32),
                        pltpu.VMEM((R, BR), F32)],
        compiler_params=_cparams(("arbitrary", "arbitrary")),
        name="s5",
    )(proj, proj, abr, abi, wbr, wbi, wcr, wci, d, gw, gb, x0r, x0i)


def _lru_kernel(x_ref, z_ref, cw_ref, cb_ref, wa_ref, ba_ref, wx_ref, bx_ref, lam_ref, h0_ref, cv0_ref,
                y_ref, h_ref, cv_ref, xp_ref, xl_ref, a_ref, *, TT, n_steps, nT):
    ti = pl.program_id(1)
    R = TT * SEQ_BLOCK

    @pl.when(ti == 0)
    def _():
        h_ref[...] = h0_ref[...]

    _causal_conv_tile(x_ref, cw_ref, cb_ref, xp_ref, xl_ref, cv0_ref, cv_ref, ti=ti, R=R, width=BR,
                      tv_local=n_steps, act_fn=lambda a: a)

    xl = xl_ref[...]
    xb = xl.astype(BF16)
    r = _sigmoid(jnp.dot(xb, wa_ref[...], preferred_element_type=F32) + ba_ref[...])
    i = _sigmoid(jnp.dot(xb, wx_ref[...], preferred_element_type=F32) + bx_ref[...])
    log_a = -LRU_C * r * _softplus(-lam_ref[...])
    a = jnp.exp(log_a)
    a_ref[...] = a
    xl_ref[...] = jnp.sqrt(1.0 - a * a) * (i * xl)

    def step(t, h):
        rows = pl.ds(pl.multiple_of(t * SEQ_BLOCK, SEQ_BLOCK), SEQ_BLOCK)
        hn = a_ref[rows, :] * h + xl_ref[rows, :]
        xl_ref[rows, :] = hn
        return hn

    h_ref[...] = lax.fori_loop(0, n_steps, step, h_ref[...], unroll=4)
    y_ref[...] = xl_ref[...] * _silu(z_ref[...])


def _lru(proj, cw, cb, wa, ba, wx, bx, lam, h0, cv0, *, G, nT, TT, t_valid):
    assert nT == 1 or t_valid == nT * TT
    R = TT * SEQ_BLOCK
    n = proj.shape[0]
    full = lambda a: pl.BlockSpec(a.shape, lambda g, t, nd=a.ndim: (0,) * nd)
    st = pl.BlockSpec((SEQ_BLOCK, BR), lambda g, t: (g, 0))
    cvs = pl.BlockSpec((CARRY_ROWS, BR), lambda g, t: (g, 0))
    kern = functools.partial(_lru_kernel, TT=TT, n_steps=t_valid - (nT - 1) * TT, nT=nT)
    return pl.pallas_call(
        kern,
        grid=(G, nT),
        in_specs=[pl.BlockSpec((R, BR), lambda g, t: (g * nT + t, U_LRU_X // 4)),
                  pl.BlockSpec((R, BR), lambda g, t: (g * nT + t, U_LRU_Z // 4)),
                  full(cw), full(cb), full(wa), full(ba), full(wx), full(bx), full(lam), st, cvs],
        out_specs=[pl.BlockSpec((R, BR), lambda g, t: (g * nT + t, 0)), st, cvs],
        out_shape=[jax.ShapeDtypeStruct((n, BR), F32),
                   jax.ShapeDtypeStruct(h0.shape, F32),
                   jax.ShapeDtypeStruct(cv0.shape, F32)],
        scratch_shapes=[pltpu.VMEM((R + CARRY_ROWS, BR), F32), pltpu.VMEM((R, BR), F32),
                        pltpu.VMEM((R, BR), F32)],
        compiler_params=_cparams(("arbitrary", "arbitrary")),
        name="lru",
    )(proj, proj, cw, cb, wa, ba, wx, bx, lam, h0, cv0)


def _merge_kernel(yml_ref, ygd_ref, ys5_ref, ylru_ref, x_ref, p_ref, g_ref, wg_ref,
                  wbr_ref, wout_ref, wpg_ref, wple_ref, pg_ref, o_ref, *, x_btd, p_btd, out_btd):
    tm = yml_ref.shape[0]
    rows = lambda ref, btd, c: jnp.swapaxes(ref[...], 0, 1).reshape(tm, c) if btd else ref[...]
    p = rows(p_ref, p_btd, PLE_DIM)
    x = rows(x_ref, x_btd, D_MODEL)
    h = _rms(x, g_ref[...]).astype(BF16)
    merged = None
    for nb, y_ref in enumerate((yml_ref, ygd_ref, ys5_ref, ylru_ref)):
        pbr = jnp.dot(y_ref[...].astype(BF16), wbr_ref[nb], preferred_element_type=F32)
        gates = _dot_nt(h, wg_ref[nb * D_MODEL:(nb + 1) * D_MODEL, :])
        term = _sigmoid(gates) * pbr
        merged = term if merged is None else merged + term
    out = jnp.dot(merged.astype(BF16), wout_ref[...], preferred_element_type=F32)
    r = x + _rms(out, pg_ref[...])
    gate = _sigmoid(jnp.dot(r.astype(BF16), wpg_ref[...], preferred_element_type=F32))
    res = r + gate * jnp.dot(p.astype(BF16), wple_ref[...], preferred_element_type=F32)
    if out_btd:
        o_ref[...] = jnp.swapaxes(res.reshape(tm // SEQ_BLOCK, SEQ_BLOCK, D_MODEL), 0, 1)
    else:
        o_ref[...] = res


def _merge(ys, x, p, g, wg_all, li, wbr, wout, wpg, wple, pg, tm, *, p_layer=None, out_btd=False):
    x_btd = x.ndim == 3
    n = ys[0].shape[0]
    tt = tm // SEQ_BLOCK
    row = lambda c: pl.BlockSpec((tm, c), lambda i: (i, 0))
    x_spec = pl.BlockSpec((SEQ_BLOCK, tt, D_MODEL), lambda i: (0, i, 0)) if x_btd else row(D_MODEL)
    full = lambda a: pl.BlockSpec(a.shape, lambda i, nd=a.ndim: (0,) * nd, pipeline_mode=pl.Buffered(1))
    p_spec = row(PLE_DIM) if p_layer is None else pl.BlockSpec((None, SEQ_BLOCK, tt, PLE_DIM),
                                                               lambda i: (p_layer, 0, i, 0))
    if out_btd:
        out_spec = pl.BlockSpec((SEQ_BLOCK, tt, D_MODEL), lambda i: (0, i, 0))
        out_shape = jax.ShapeDtypeStruct((SEQ_BLOCK, n // SEQ_BLOCK, D_MODEL), F32)
    else:
        out_spec, out_shape = row(D_MODEL), jax.ShapeDtypeStruct((n, D_MODEL), F32)
    return pl.pallas_call(
        functools.partial(_merge_kernel, x_btd=x_btd, p_btd=p_layer is not None, out_btd=out_btd),
        grid=(n // tm,),
        in_specs=[row(BR), row(BR), row(BR), row(BR), x_spec, p_spec, full(g),
                  pl.BlockSpec((None, GATE_COLS, D_MODEL), lambda i: (li, 0, 0), pipeline_mode=pl.Buffered(1)),
                  full(wbr), full(wout), full(wpg), full(wple), full(pg)],
        out_specs=out_spec,
        out_shape=out_shape,
        compiler_params=_cparams(("parallel",)),
        name="merge",
    )(*ys, x, p, g, wg_all, wbr, wout, wpg, wple, pg)


def _w_relayout_kernel(tbl_ref, win_ref, o_ref):
    o_ref[...] = win_ref[0].astype(BF16)


def _w_relayout(wt, runs, n_chunks):
    depth, d_in, d_model = wt.shape
    tbl = [None] * n_chunks
    for c0, src, count in runs:
        for j in range(count):
            off = src + j * W_CHUNK
            assert off % SEQ_BLOCK == 0 and off + W_CHUNK <= d_in
            tbl[c0 + j] = (off // SEQ_BLOCK, 0)
    assert all(t is not None for t in tbl)
    grid_spec = pltpu.PrefetchScalarGridSpec(
        num_scalar_prefetch=1,
        grid=(depth, n_chunks),
        in_specs=[pl.BlockSpec((pl.Element(1), pl.Element(W_CHUNK), pl.Element(d_model)),
                               lambda l, c, t: (l, t[c, 0] * SEQ_BLOCK, 0))],
        out_specs=pl.BlockSpec((None, W_CHUNK, d_model), lambda l, c, t: (l, c, 0)),
    )
    return pl.pallas_call(
        _w_relayout_kernel,
        grid_spec=grid_spec,
        out_shape=jax.ShapeDtypeStruct((depth, n_chunks * W_CHUNK, d_model), BF16),
        compiler_params=_cparams(("arbitrary", "arbitrary")),
        name="w_relayout",
    )(jnp.asarray(tbl, jnp.int32), wt)


def _permute_w_in(w_in):
    sizes = (BR, BR, BR, HEADS, HEADS, BR, BR, 3 * BR, HEADS, HEADS, BR, BR, BR, BR, BR, GATE_COLS)
    offs = [0]
    for s in sizes:
        offs.append(offs[-1] + s)
    (ml_q, _, _, ml_i, _, ml_o, _, gd_qkv, gd_a, _, gd_z, _, _, _, _, gates) = offs[:-1]
    wt = jnp.swapaxes(w_in, 1, 2)
    upc = W_CHUNK // LANE
    w_main = _w_relayout(wt, [(U_GD_QKV // upc, gd_qkv, 3), (U_ML // upc, ml_q, 3), (U_ML // upc + 3, ml_o, 2),
                              (U_GD_Z // upc, gd_z, 5)], N_UNITS // upc)
    w_gates = _w_relayout(wt, [(0, gates, GATE_COLS // W_CHUNK)], GATE_COLS // W_CHUNK)
    unit = lambda off: jnp.pad(wt[:, off:off + 2 * HEADS, :], ((0, 0), (0, LANE - 2 * HEADS), (0, 0)))
    w_scalar = jnp.concatenate([unit(ml_i), unit(gd_a)], axis=1)
    return w_main, w_gates, w_scalar


def _pad_lanes(*vecs):
    v = jnp.concatenate(vecs)
    return jnp.pad(v, (0, LANE - v.shape[0])).reshape(1, LANE)


def _block_diag(blocks):
    n, a, b = blocks.shape
    eye = jnp.eye(n, dtype=blocks.dtype)
    return jnp.einsum('ij,iab->iajb', eye, blocks).reshape(n * a, n * b)


def _to_rows(x, G):
    B, T, C = x.shape
    return x.reshape(G, SEQ_BLOCK, T, C).transpose(0, 2, 1, 3).reshape(G * T * SEQ_BLOCK, C)


def _from_rows(y, G, T):
    C = y.shape[-1]
    return y.reshape(G, T, SEQ_BLOCK, C).transpose(0, 2, 1, 3).reshape(G * SEQ_BLOCK, T, C)


def _group(x, p, states, params, *, T, t_valid, TT, L, tm_proj, tm_merge, gps=1):
    B = x.shape[0]
    G = B // SEQ_BLOCK
    nT = T // TT
    depth = p.shape[0]
    xr = x if G == 1 else _to_rows(x, G)
    tm_proj = min(tm_proj, B * T)
    tm_merge = min(tm_merge, B * T)
    new_states = []
    c_all = s_all = None
    gps = math.gcd(gps, G)
    mat = dict(n_layers=depth, B=B, G=G, nT=nT, TT=TT, L=L, t_valid=t_valid, gps=gps)
    for li in range(depth):
        if li > 0:
            mat['gps'] = math.gcd(2 * gps, G) if nT == 1 else gps
        wide = GDN_TILE_FACTOR if nT % GDN_TILE_FACTOR == 0 else 1
        gdn_cfg = dict(mat, L=min(L, GDN_CHUNK), TT=TT * wide, nT=nT // wide)
        lp = {k: v[li] for k, v in params.items() if not k.startswith('w_in_')}
        if states is None:
            ml_init = s_init = None
            gcv0 = jnp.zeros((G * CARRY_ROWS, 3 * BR), F32)
            x0r = jnp.zeros((B, S5_STATE), F32)
            x0i = jnp.zeros((B, S5_STATE), F32)
            h0 = jnp.zeros((B, BR), F32)
            lcv0 = jnp.zeros((G * CARRY_ROWS, BR), F32)
        else:
            (n0, m0, gcv0, x0r, x0i, h0, lcv0) = [states[j][li] for j in (1, 2, 4, 5, 6, 7, 8)]
            n0 = n0.reshape(G, SEQ_BLOCK, HEADS, DH).transpose(0, 2, 1, 3).reshape(B * HEADS, DH)
            ml_init = (states[0], n0, jnp.pad(m0, ((0, 0), (0, LANE - HEADS))))
            s_init = states[3]
            gcv0 = _to_rows(gcv0, G)
            x0r = x0r.reshape(B, S5_STATE)
            x0i = x0i.reshape(B, S5_STATE)
            lcv0 = _to_rows(lcv0, G)
        in_place = G == 1

        proj, projg = _in_proj(xr, lp['prenorm_g'], params['w_in_main'], params['w_in_scalar'], li, tm_proj)

        y_ml, c_all, n1, m1 = _mlstm(proj, projg, lp['ml_ifb'], lp['ml_norm_g'], ml_init, c_all, li=li, **mat)
        y_gd, s_all, cvq, cvk, cvv = _gdn(proj, projg, lp['gd_conv_w'], lp['gd_alog'], lp['gd_dtb'], lp['gd_norm_g'],
                                          gcv0, s_init, s_all, li=li, **gdn_cfg)
        gcv1 = jnp.concatenate([cvq, cvk, cvv], axis=-1)
        y_s5, x1r, x1i = _s5(proj, lp['s5_abr'], lp['s5_abi'], lp['s5_wbr'], lp['s5_wbi'], lp['s5_wcr'],
                             lp['s5_wci'], lp['s5_d'], lp['s5_glu_w'], lp['s5_glu_b'], x0r, x0i,
                             G=G, nT=nT, TT=TT, t_valid=t_valid)
        y_lru, h1, lcv1 = _lru(proj, lp['lru_conv_w'], lp['lru_conv_b'], lp['lru_wa'], lp['lru_ba'],
                               lp['lru_wx'], lp['lru_bx'], lp['lru_lam'], h0, lcv0,
                               G=G, nT=nT, TT=TT, t_valid=t_valid)
        xr = _merge((y_ml, y_gd, y_s5, y_lru), xr, p if in_place else _to_rows(p[li], G),
                    lp['prenorm_g'], params['w_in_gates'], li, lp['w_branch'], lp['w_out'], lp['w_ple_gate'], lp['w_ple'], lp['postnorm_g'], tm_merge,
                    p_layer=li if in_place else None, out_btd=in_place and li == depth - 1)

        n1 = n1.reshape(G, HEADS, SEQ_BLOCK, DH).transpose(0, 2, 1, 3)
        new_states.append((n1.reshape(B, HEADS, DH), m1[:, :HEADS],
                           _from_rows(gcv1, G, CONV_W - 1), x1r.reshape(B, S5_G, S5_P),
                           x1i.reshape(B, S5_G, S5_P), h1, _from_rows(lcv1, G, CONV_W - 1)))
    y = xr if G == 1 else _from_rows(xr, G, T)
    n_s, m_s, gcv_s, xr_s, xi_s, h_s, lcv_s = (jnp.stack([ns[j] for ns in new_states]) for j in range(7))
    return y, (c_all, n_s, m_s, s_all, gcv_s, xr_s, xi_s, h_s, lcv_s)


def _prepare_params(prenorm_g, postnorm_g, w_in, ml_bi, ml_bf, ml_norm_g, gd_conv_w, gd_a_log, gd_dt_bias,
                    gd_norm_g, s5_a_re, s5_a_im, s5_log_dt, s5_b_re, s5_b_im, s5_c_re, s5_c_im, s5_d,
                    s5_glu_w, s5_glu_b, lru_conv_w, lru_conv_b, lru_wa, lru_ba, lru_wx, lru_bx, lru_lam,
                    w_branch, w_out, w_ple, w_ple_gate):
    depth = w_in.shape[0]
    row = lambda a: a.reshape(depth, 1, -1)
    per_layer = lambda f, *a: jnp.stack([f(*[x[i] for x in a]) for i in range(depth)])
    prm = dict(
        prenorm_g=row(prenorm_g), postnorm_g=row(postnorm_g),
        ml_ifb=per_layer(_pad_lanes, ml_bi, ml_bf), ml_norm_g=row(ml_norm_g),
        gd_conv_w=gd_conv_w, gd_norm_g=row(gd_norm_g),
        gd_alog=per_layer(lambda a: _pad_lanes(a), gd_a_log),
        gd_dtb=per_layer(lambda a: _pad_lanes(a), gd_dt_bias),
        s5_d=row(s5_d), s5_glu_w=s5_glu_w.astype(BF16), s5_glu_b=row(s5_glu_b),
        lru_conv_w=lru_conv_w, lru_conv_b=row(lru_conv_b),
        lru_wa=per_layer(_block_diag, lru_wa).astype(BF16), lru_ba=row(lru_ba),
        lru_wx=per_layer(_block_diag, lru_wx).astype(BF16), lru_bx=row(lru_bx), lru_lam=row(lru_lam),
        w_branch=w_branch.astype(BF16), w_out=w_out.astype(BF16), w_ple=w_ple.astype(BF16),
        w_ple_gate=w_ple_gate.astype(BF16),
    )
    abr, abi, wbr, wbi, wcr, wci = [], [], [], [], [], []
    nb = BR // LANE
    gpb = S5_G // nb
    for i in range(depth):
        a_r, a_i, bbr, bbi = _s5_prep(s5_a_re[i], s5_a_im[i], s5_log_dt[i], s5_b_re[i], s5_b_im[i])
        abr.append(a_r)
        abi.append(a_i)
        in_blocks = lambda bb: jnp.stack([_block_diag(bb.reshape(S5_N, S5_G, S5_P).transpose(1, 0, 2)
                                                      [k * gpb:(k + 1) * gpb]) for k in range(nb)])
        out_blocks = lambda c: jnp.stack([_block_diag(jnp.transpose(c, (0, 2, 1))[k * gpb:(k + 1) * gpb])
                                          for k in range(nb)])
        wbr.append(in_blocks(bbr).astype(BF16))
        wbi.append(in_blocks(bbi).astype(BF16))
        wcr.append(out_blocks(s5_c_re[i]).astype(BF16))
        wci.append(out_blocks(s5_c_im[i]).astype(BF16))
    prm.update(s5_abr=jnp.stack(abr), s5_abi=jnp.stack(abi), s5_wbr=jnp.stack(wbr), s5_wbi=jnp.stack(wbi),
               s5_wcr=jnp.stack(wcr), s5_wci=jnp.stack(wci))
    prm['w_in_main'], prm['w_in_gates'], prm['w_in_scalar'] = _permute_w_in(w_in)
    return prm


def kernel(x_prompt, x_sample, state_mlstm_c, state_mlstm_n, state_mlstm_m, state_gdn_s, state_gdn_conv, state_s5_re, state_s5_im, state_lru_h, state_lru_conv, p_prompt, p_sample, prenorm_g, postnorm_g, w_in, ml_bi, ml_bf, ml_norm_g, gd_conv_w, gd_a_log, gd_dt_bias, gd_norm_g, s5_a_re, s5_a_im, s5_log_dt, s5_b_re, s5_b_im, s5_c_re, s5_c_im, s5_d, s5_glu_w, s5_glu_b, lru_conv_w, lru_conv_b, lru_wa, lru_ba, lru_wx, lru_bx, lru_lam, w_branch, w_out, w_ple, w_ple_gate):
    prm = _prepare_params(prenorm_g, postnorm_g, w_in, ml_bi, ml_bf, ml_norm_g, gd_conv_w, gd_a_log,
                          gd_dt_bias, gd_norm_g, s5_a_re, s5_a_im, s5_log_dt, s5_b_re, s5_b_im, s5_c_re,
                          s5_c_im, s5_d, s5_glu_w, s5_glu_b, lru_conv_w, lru_conv_b, lru_wa, lru_ba, lru_wx,
                          lru_bx, lru_lam, w_branch, w_out, w_ple, w_ple_gate)

    t_p = x_prompt.shape[1]
    tt_p = math.gcd(t_p, PROMPT_CHUNK)
    y_prompt, pr = _group(x_prompt, p_prompt, None, prm, T=t_p, t_valid=t_p, TT=tt_p, L=tt_p,
                          tm_proj=TM_PROJ, tm_merge=TM_MERGE)

    t_s = x_sample.shape[1]
    t_pad = -(-t_s // SEQ_BLOCK) * SEQ_BLOCK
    pad_t = lambda a, ax: jnp.pad(a, [(0, t_pad - t_s) if i == ax else (0, 0) for i in range(a.ndim)])
    sample_states = (state_mlstm_c, state_mlstm_n, state_mlstm_m, state_gdn_s, state_gdn_conv,
                     state_s5_re, state_s5_im, state_lru_h, state_lru_conv)
    y_s, sa = _group(pad_t(x_sample, 1), pad_t(p_sample, 2), sample_states, prm, T=t_pad, t_valid=t_s,
                     TT=t_pad, L=t_pad, tm_proj=TM_PROJ, tm_merge=TM_MERGE, gps=SAMPLE_GROUPS_PER_STEP)
    y_sample = y_s[:, :t_s]
    return (y_prompt, y_sample) + pr + sa
```
